```python
import jax, jax.numpy as jnp
from jax import lax
import numpy as np

D_MODEL = 1024
BATCH = 8
SEQ = 16384
DEPTH = 2

N_MEM = 256
POOL_WIDTH = D_MODEL
POOL_WINDOWS = (2, 4, 8, 16)
POOL_GROUPS = len(POOL_WINDOWS)
POOL_GROUP_DIM = POOL_WIDTH // POOL_GROUPS
LRU_WIDTH = D_MODEL
LRU_HEADS = 8
LRU_HEAD_DIM = LRU_WIDTH // LRU_HEADS
LRU_CONV = 4
LRU_C = 8.0
SCONV_WIDTH = D_MODEL
SCONV_K = 3
N_BRANCH = 3
SPLIT_POINTS = (POOL_WIDTH,
                POOL_WIDTH + LRU_WIDTH,
                POOL_WIDTH + LRU_WIDTH + SCONV_WIDTH,
                POOL_WIDTH + LRU_WIDTH + 2 * SCONV_WIDTH,
                POOL_WIDTH + LRU_WIDTH + 3 * SCONV_WIDTH)
IN_COLS = POOL_WIDTH + LRU_WIDTH + 3 * SCONV_WIDTH + N_BRANCH * D_MODEL
X_HEADS = 4
X_HEAD_DIM = D_MODEL // X_HEADS
D_FF = -(-8 * D_MODEL // (3 * 256)) * 256
ALPHA = (2 * DEPTH) ** 0.25
BETA = (8 * DEPTH) ** -0.25
LN_EPS = 1e-5

kernel_name = "hybrid_pool_rglru_shortconv_deepnorm"


def layer_norm(x, g, b):
    xf = x.astype(jnp.float32)
    mu = jnp.mean(xf, axis=-1, keepdims=True)
    var = jnp.mean(jnp.square(xf - mu), axis=-1, keepdims=True)
    y = (xf - mu) * lax.rsqrt(var + LN_EPS)
    return (y * g.astype(jnp.float32) + b.astype(jnp.float32)).astype(x.dtype)


def causal_depthwise_conv(u, w):
    k, c = w.shape
    return lax.conv_general_dilated(
        u, w[:, None, :].astype(u.dtype), window_strides=(1,),
        padding=((k - 1, 0),), dimension_numbers=("NWC", "WIO", "NWC"),
        feature_group_count=c)


def multiscale_pool(u, pool_w, pool_scale):
    bsz, s, _ = u.shape
    uf = u.astype(jnp.float32)
    cs = jnp.cumsum(uf, axis=1)
    count = jnp.arange(1, s + 1, dtype=jnp.float32)[None, :, None]
    outs = []
    for g, win in enumerate(POOL_WINDOWS):
        sl = slice(g * POOL_GROUP_DIM, (g + 1) * POOL_GROUP_DIM)
        c = cs[..., sl]
        lag = jnp.pad(c, ((0, 0), (win, 0), (0, 0)))[:, :s]
        outs.append((c - lag) / jnp.minimum(count, float(win)) - uf[..., sl])
    p = jnp.stack(outs, axis=2).astype(u.dtype)
    y = jnp.einsum("bsgc,gcd->bsgd", p, pool_w).reshape(bsz, s, POOL_WIDTH)
    return y * pool_scale


def rg_lru(u, conv_w, conv_b, w_r, b_r, w_i, b_i, lam):
    v = causal_depthwise_conv(u, conv_w) + conv_b
    bsz, s, _ = v.shape
    vh = v.reshape(bsz, s, LRU_HEADS, LRU_HEAD_DIM)
    r = jax.nn.sigmoid(jnp.einsum("bshc,hcd->bshd", vh, w_r).reshape(bsz, s, LRU_WIDTH) + b_r)
    i = jax.nn.sigmoid(jnp.einsum("bshc,hcd->bshd", vh, w_i).reshape(bsz, s, LRU_WIDTH) + b_i)
    log_a = -LRU_C * r.astype(jnp.float32) * jax.nn.softplus(-lam.astype(jnp.float32))
    a = jnp.exp(log_a)
    bterm = jnp.sqrt(-jnp.expm1(2.0 * log_a)) * (i * v).astype(jnp.float32)

    def combine(left, right):
        a1, b1 = left
        a2, b2 = right
        return a1 * a2, a2 * b1 + b2

    _, h = lax.associative_scan(combine, (a, bterm), axis=1)
    return h.astype(u.dtype)


def gated_short_conv(b_gate, c_gate, h, conv_w, w_out):
    return (b_gate * causal_depthwise_conv(c_gate * h, conv_w)) @ w_out


def hybrid_mixer(x, w_in, b_in, pool_w, pool_scale, lru_conv_w, lru_conv_b,
                 lru_w_r, lru_b_r, lru_w_i, lru_b_i, lru_lambda, lru_w_out,
                 sconv_w, sconv_w_out, w_mix_out):
    bsz, s, _ = x.shape
    z = x @ w_in + b_in
    z_pool, z_lru, z_b, z_c, z_h, z_gate = jnp.split(z, SPLIT_POINTS, axis=-1)
    y_pool = multiscale_pool(z_pool, pool_w, pool_scale)
    y_lru = rg_lru(z_lru, lru_conv_w, lru_conv_b, lru_w_r, lru_b_r,
                   lru_w_i, lru_b_i, lru_lambda) @ lru_w_out
    y_conv = gated_short_conv(z_b, z_c, z_h, sconv_w, sconv_w_out)
    gates = jax.nn.sigmoid(z_gate).reshape(bsz, s, N_BRANCH, D_MODEL)
    merged = (gates[:, :, 0] * y_pool + gates[:, :, 1] * y_lru
              + gates[:, :, 2] * y_conv)
    return merged @ w_mix_out


def memory_cross_attention(x, mem, w_q, w_k, w_v, w_o):
    bsz, s, _ = x.shape
    m = mem.shape[1]
    q = (x @ w_q).reshape(bsz, s, X_HEADS, X_HEAD_DIM)
    k = (mem @ w_k).reshape(bsz, m, X_HEADS, X_HEAD_DIM)
    v = (mem @ w_v).reshape(bsz, m, X_HEADS, X_HEAD_DIM)
    scores = jnp.einsum("bshd,bmhd->bhsm", q, k).astype(jnp.float32) * (X_HEAD_DIM ** -0.5)
    p = jax.nn.softmax(scores, axis=-1).astype(x.dtype)
    o = jnp.einsum("bhsm,bmhd->bshd", p, v).reshape(bsz, s, D_MODEL)
    return o @ w_o


def swiglu(x, w_gate, w_up, w_down):
    return (jax.nn.silu(x @ w_gate) * (x @ w_up)) @ w_down


def _fwd_setup_inputs(seed: int = 0) -> dict:
    key = jax.random.key(seed)
    ks = jax.random.split(key, 28)

    def nrm(k, shape, scale):
        return jax.random.normal(k, shape, jnp.float32) * scale

    a0 = jax.random.uniform(ks[12], (DEPTH, LRU_WIDTH), jnp.float32, 0.9, 0.999)
    sig = a0 ** (1.0 / LRU_C)
    lru_lambda = jnp.log(sig) - jnp.log1p(-sig)
    return {
        "x": nrm(ks[0], (BATCH, SEQ, D_MODEL), 1.0),
        "mem": nrm(ks[1], (BATCH, N_MEM, D_MODEL), 1.0),
        "w_in": nrm(ks[2], (DEPTH, D_MODEL, IN_COLS), D_MODEL ** -0.5),
        "b_in": nrm(ks[3], (DEPTH, IN_COLS), 0.02),
        "pool_w": nrm(ks[4], (DEPTH, POOL_GROUPS, POOL_GROUP_DIM, POOL_GROUP_DIM), POOL_GROUP_DIM ** -0.5),
        "pool_scale": 1.0 + nrm(ks[5], (DEPTH, POOL_WIDTH), 0.1),
        "lru_conv_w": nrm(ks[6], (DEPTH, LRU_CONV, LRU_WIDTH), LRU_CONV ** -0.5),
        "lru_conv_b": nrm(ks[7], (DEPTH, LRU_WIDTH), 0.02),
        "lru_w_r": nrm(ks[8], (DEPTH, LRU_HEADS, LRU_HEAD_DIM, LRU_HEAD_DIM), LRU_HEAD_DIM ** -0.5),
        "lru_b_r": nrm(ks[9], (DEPTH, LRU_WIDTH), 0.02),
        "lru_w_i": nrm(ks[10], (DEPTH, LRU_HEADS, LRU_HEAD_DIM, LRU_HEAD_DIM), LRU_HEAD_DIM ** -0.5),
        "lru_b_i": nrm(ks[11], (DEPTH, LRU_WIDTH), 0.02),
        "lru_lambda": lru_lambda,
        "lru_w_out": nrm(ks[13], (DEPTH, LRU_WIDTH, D_MODEL), LRU_WIDTH ** -0.5),
        "sconv_w": nrm(ks[14], (DEPTH, SCONV_K, SCONV_WIDTH), SCONV_K ** -0.5),
        "sconv_w_out": nrm(ks[15], (DEPTH, SCONV_WIDTH, D_MODEL), SCONV_WIDTH ** -0.5),
        "w_mix_out": nrm(ks[16], (DEPTH, D_MODEL, D_MODEL), BETA * D_MODEL ** -0.5),
        "xa_w_q": nrm(ks[17], (DEPTH, D_MODEL, D_MODEL), D_MODEL ** -0.5),
        "xa_w_k": nrm(ks[18], (DEPTH, D_MODEL, D_MODEL), D_MODEL ** -0.5),
        "xa_w_v": nrm(ks[19], (DEPTH, D_MODEL, D_MODEL), BETA * D_MODEL ** -0.5),
        "xa_w_o": nrm(ks[20], (DEPTH, D_MODEL, D_MODEL), BETA * D_MODEL ** -0.5),
        "ffn_w_gate": nrm(ks[21], (DEPTH, D_MODEL, D_FF), D_MODEL ** -0.5),
        "ffn_w_up": nrm(ks[22], (DEPTH, D_MODEL, D_FF), D_MODEL ** -0.5),
        "ffn_w_down": nrm(ks[23], (DEPTH, D_FF, D_MODEL), BETA * D_FF ** -0.5),
        "ln_g": 1.0 + nrm(ks[24], (DEPTH, 3, D_MODEL), 0.05),
        "ln_b": nrm(ks[25], (DEPTH, 3, D_MODEL), 0.02),
    }


def _fwd_reference(x, mem, w_in, b_in, pool_w, pool_scale, lru_conv_w, lru_conv_b,
              lru_w_r, lru_b_r, lru_w_i, lru_b_i, lru_lambda, lru_w_out,
              sconv_w, sconv_w_out, w_mix_out, xa_w_q, xa_w_k, xa_w_v, xa_w_o,
              ffn_w_gate, ffn_w_up, ffn_w_down, ln_g, ln_b):
    for l in range(DEPTH):
        mix = hybrid_mixer(x, w_in[l], b_in[l], pool_w[l], pool_scale[l],
                           lru_conv_w[l], lru_conv_b[l], lru_w_r[l], lru_b_r[l],
                           lru_w_i[l], lru_b_i[l], lru_lambda[l], lru_w_out[l],
                           sconv_w[l], sconv_w_out[l], w_mix_out[l])
        x = layer_norm(ALPHA * x + mix, ln_g[l, 0], ln_b[l, 0])
        xa = memory_cross_attention(x, mem, xa_w_q[l], xa_w_k[l], xa_w_v[l], xa_w_o[l])
        x = layer_norm(ALPHA * x + xa, ln_g[l, 1], ln_b[l, 1])
        ff = swiglu(x, ffn_w_gate[l], ffn_w_up[l], ffn_w_down[l])
        x = layer_norm(ALPHA * x + ff, ln_g[l, 2], ln_b[l, 2])
    return x


import jax as _jax
import jax.numpy as _jnp

TWIN_FORMAT = 'train_step'
FWD_PARAMS = ['x', 'mem', 'w_in', 'b_in', 'pool_w', 'pool_scale', 'lru_conv_w', 'lru_conv_b', 'lru_w_r', 'lru_b_r', 'lru_w_i', 'lru_b_i', 'lru_lambda', 'lru_w_out', 'sconv_w', 'sconv_w_out', 'w_mix_out', 'xa_w_q', 'xa_w_k', 'xa_w_v', 'xa_w_o', 'ffn_w_gate', 'ffn_w_up', 'ffn_w_down', 'ln_g', 'ln_b']
TWIN_WEIGHTS = ['w_in', 'b_in', 'pool_w', 'pool_scale', 'lru_conv_w', 'lru_conv_b', 'lru_w_r', 'lru_b_r', 'lru_w_i', 'lru_b_i', 'lru_lambda', 'lru_w_out', 'sconv_w', 'sconv_w_out', 'w_mix_out', 'xa_w_q', 'xa_w_k', 'xa_w_v', 'xa_w_o', 'ffn_w_gate', 'ffn_w_up', 'ffn_w_down', 'ln_g', 'ln_b']
TWIN_DIFF_INPUT = 'x'
TWIN_INPUTS = ['x', 'mem', 'w_in', 'b_in', 'pool_w', 'pool_scale', 'lru_conv_w', 'lru_conv_b', 'lru_w_r', 'lru_b_r', 'lru_w_i', 'lru_b_i', 'lru_lambda', 'lru_w_out', 'sconv_w', 'sconv_w_out', 'w_mix_out', 'xa_w_q', 'xa_w_k', 'xa_w_v', 'xa_w_o', 'ffn_w_gate', 'ffn_w_up', 'ffn_w_down', 'ln_g', 'ln_b', 'loss_target', 'm_w_in', 'm_b_in', 'm_pool_w', 'm_pool_scale', 'm_lru_conv_w', 'm_lru_conv_b', 'm_lru_w_r', 'm_lru_b_r', 'm_lru_w_i', 'm_lru_b_i', 'm_lru_lambda', 'm_lru_w_out', 'm_sconv_w', 'm_sconv_w_out', 'm_w_mix_out', 'm_xa_w_q', 'm_xa_w_k', 'm_xa_w_v', 'm_xa_w_o', 'm_ffn_w_gate', 'm_ffn_w_up', 'm_ffn_w_down', 'm_ln_g', 'm_ln_b', 'v_w_in', 'v_b_in', 'v_pool_w', 'v_pool_scale', 'v_lru_conv_w', 'v_lru_conv_b', 'v_lru_w_r', 'v_lru_b_r', 'v_lru_w_i', 'v_lru_b_i', 'v_lru_lambda', 'v_lru_w_out', 'v_sconv_w', 'v_sconv_w_out', 'v_w_mix_out', 'v_xa_w_q', 'v_xa_w_k', 'v_xa_w_v', 'v_xa_w_o', 'v_ffn_w_gate', 'v_ffn_w_up', 'v_ffn_w_down', 'v_ln_g', 'v_ln_b']
TWIN_OUTPUTS = ['loss', 'grad_x', 'grad_w_in', 'grad_b_in', 'grad_pool_w', 'grad_pool_scale', 'grad_lru_conv_w', 'grad_lru_conv_b', 'grad_lru_w_r', 'grad_lru_b_r', 'grad_lru_w_i', 'grad_lru_b_i', 'grad_lru_lambda', 'grad_lru_w_out', 'grad_sconv_w', 'grad_sconv_w_out', 'grad_w_mix_out', 'grad_xa_w_q', 'grad_xa_w_k', 'grad_xa_w_v', 'grad_xa_w_o', 'grad_ffn_w_gate', 'grad_ffn_w_up', 'grad_ffn_w_down', 'grad_ln_g', 'grad_ln_b', 'delta_w_in', 'delta_b_in', 'delta_pool_w', 'delta_pool_scale', 'delta_lru_conv_w', 'delta_lru_conv_b', 'delta_lru_w_r', 'delta_lru_b_r', 'delta_lru_w_i', 'delta_lru_b_i', 'delta_lru_lambda', 'delta_lru_w_out', 'delta_sconv_w', 'delta_sconv_w_out', 'delta_w_mix_out', 'delta_xa_w_q', 'delta_xa_w_k', 'delta_xa_w_v', 'delta_xa_w_o', 'delta_ffn_w_gate', 'delta_ffn_w_up', 'delta_ffn_w_down', 'delta_ln_g', 'delta_ln_b', 'new_m_w_in', 'new_m_b_in', 'new_m_pool_w', 'new_m_pool_scale', 'new_m_lru_conv_w', 'new_m_lru_conv_b', 'new_m_lru_w_r', 'new_m_lru_b_r', 'new_m_lru_w_i', 'new_m_lru_b_i', 'new_m_lru_lambda', 'new_m_lru_w_out', 'new_m_sconv_w', 'new_m_sconv_w_out', 'new_m_w_mix_out', 'new_m_xa_w_q', 'new_m_xa_w_k', 'new_m_xa_w_v', 'new_m_xa_w_o', 'new_m_ffn_w_gate', 'new_m_ffn_w_up', 'new_m_ffn_w_down', 'new_m_ln_g', 'new_m_ln_b', 'new_v_w_in', 'new_v_b_in', 'new_v_pool_w', 'new_v_pool_scale', 'new_v_lru_conv_w', 'new_v_lru_conv_b', 'new_v_lru_w_r', 'new_v_lru_b_r', 'new_v_lru_w_i', 'new_v_lru_b_i', 'new_v_lru_lambda', 'new_v_lru_w_out', 'new_v_sconv_w', 'new_v_sconv_w_out', 'new_v_w_mix_out', 'new_v_xa_w_q', 'new_v_xa_w_k', 'new_v_xa_w_v', 'new_v_xa_w_o', 'new_v_ffn_w_gate', 'new_v_ffn_w_up', 'new_v_ffn_w_down', 'new_v_ln_g', 'new_v_ln_b']
TWIN_LEAF_KINDS = {'loss': 'loss', 'grad_x': 'grad_x', 'grad_w_in': 'grad_w', 'grad_b_in': 'grad_w', 'grad_pool_w': 'grad_w', 'grad_pool_scale': 'grad_w', 'grad_lru_conv_w': 'grad_w', 'grad_lru_conv_b': 'grad_w', 'grad_lru_w_r': 'grad_w', 'grad_lru_b_r': 'grad_w', 'grad_lru_w_i': 'grad_w', 'grad_lru_b_i': 'grad_w', 'grad_lru_lambda': 'grad_w', 'grad_lru_w_out': 'grad_w', 'grad_sconv_w': 'grad_w', 'grad_sconv_w_out': 'grad_w', 'grad_w_mix_out': 'grad_w', 'grad_xa_w_q': 'grad_w', 'grad_xa_w_k': 'grad_w', 'grad_xa_w_v': 'grad_w', 'grad_xa_w_o': 'grad_w', 'grad_ffn_w_gate': 'grad_w', 'grad_ffn_w_up': 'grad_w', 'grad_ffn_w_down': 'grad_w', 'grad_ln_g': 'grad_w', 'grad_ln_b': 'grad_w', 'delta_w_in': 'delta_w', 'delta_b_in': 'delta_w', 'delta_pool_w': 'delta_w', 'delta_pool_scale': 'delta_w', 'delta_lru_conv_w': 'delta_w', 'delta_lru_conv_b': 'delta_w', 'delta_lru_w_r': 'delta_w', 'delta_lru_b_r': 'delta_w', 'delta_lru_w_i': 'delta_w', 'delta_lru_b_i': 'delta_w', 'delta_lru_lambda': 'delta_w', 'delta_lru_w_out': 'delta_w', 'delta_sconv_w': 'delta_w', 'delta_sconv_w_out': 'delta_w', 'delta_w_mix_out': 'delta_w', 'delta_xa_w_q': 'delta_w', 'delta_xa_w_k': 'delta_w', 'delta_xa_w_v': 'delta_w', 'delta_xa_w_o': 'delta_w', 'delta_ffn_w_gate': 'delta_w', 'delta_ffn_w_up': 'delta_w', 'delta_ffn_w_down': 'delta_w', 'delta_ln_g': 'delta_w', 'delta_ln_b': 'delta_w', 'new_m_w_in': 'new_m', 'new_m_b_in': 'new_m', 'new_m_pool_w': 'new_m', 'new_m_pool_scale': 'new_m', 'new_m_lru_conv_w': 'new_m', 'new_m_lru_conv_b': 'new_m', 'new_m_lru_w_r': 'new_m', 'new_m_lru_b_r': 'new_m', 'new_m_lru_w_i': 'new_m', 'new_m_lru_b_i': 'new_m', 'new_m_lru_lambda': 'new_m', 'new_m_lru_w_out': 'new_m', 'new_m_sconv_w': 'new_m', 'new_m_sconv_w_out': 'new_m', 'new_m_w_mix_out': 'new_m', 'new_m_xa_w_q': 'new_m', 'new_m_xa_w_k': 'new_m', 'new_m_xa_w_v': 'new_m', 'new_m_xa_w_o': 'new_m', 'new_m_ffn_w_gate': 'new_m', 'new_m_ffn_w_up': 'new_m', 'new_m_ffn_w_down': 'new_m', 'new_m_ln_g': 'new_m', 'new_m_ln_b': 'new_m', 'new_v_w_in': 'new_v', 'new_v_b_in': 'new_v', 'new_v_pool_w': 'new_v', 'new_v_pool_scale': 'new_v', 'new_v_lru_conv_w': 'new_v', 'new_v_lru_conv_b': 'new_v', 'new_v_lru_w_r': 'new_v', 'new_v_lru_b_r': 'new_v', 'new_v_lru_w_i': 'new_v', 'new_v_lru_b_i': 'new_v', 'new_v_lru_lambda': 'new_v', 'new_v_lru_w_out': 'new_v', 'new_v_sconv_w': 'new_v', 'new_v_sconv_w_out': 'new_v', 'new_v_w_mix_out': 'new_v', 'new_v_xa_w_q': 'new_v', 'new_v_xa_w_k': 'new_v', 'new_v_xa_w_v': 'new_v', 'new_v_xa_w_o': 'new_v', 'new_v_ffn_w_gate': 'new_v', 'new_v_ffn_w_up': 'new_v', 'new_v_ffn_w_down': 'new_v', 'new_v_ln_g': 'new_v', 'new_v_ln_b': 'new_v'}


def _forward(args):
    return _fwd_reference(*[args[k] for k in FWD_PARAMS])


def _output_shape():
    def fwd():
        inp = _fwd_setup_inputs(0)
        return _fwd_reference(*[inp[k] for k in FWD_PARAMS])
    out = _jax.eval_shape(fwd)
    return out.shape, out.dtype

N_MICROBATCH = 1
ADAM_LR = 0.001
ADAM_B1 = 0.9
ADAM_B2 = 0.999
ADAM_EPS = 1e-08
ADAM_WD = 0.01
ADAM_STEP = 10
PER_EXAMPLE_BATCH_AXIS = {'x': 0, 'mem': 0, 'loss_target': 0}
SHARED_INPUTS = []
_WEIGHT_DTYPES = {'w_in': _jnp.float32, 'b_in': _jnp.float32, 'pool_w': _jnp.float32, 'pool_scale': _jnp.float32, 'lru_conv_w': _jnp.float32, 'lru_conv_b': _jnp.float32, 'lru_w_r': _jnp.float32, 'lru_b_r': _jnp.float32, 'lru_w_i': _jnp.float32, 'lru_b_i': _jnp.float32, 'lru_lambda': _jnp.float32, 'lru_w_out': _jnp.float32, 'sconv_w': _jnp.float32, 'sconv_w_out': _jnp.float32, 'w_mix_out': _jnp.float32, 'xa_w_q': _jnp.float32, 'xa_w_k': _jnp.float32, 'xa_w_v': _jnp.float32, 'xa_w_o': _jnp.float32, 'ffn_w_gate': _jnp.float32, 'ffn_w_up': _jnp.float32, 'ffn_w_down': _jnp.float32, 'ln_g': _jnp.float32, 'ln_b': _jnp.float32}
MOMENT_SCALE = {'w_in': 5.641207e-02, 'b_in': 9.970456e-01, 'pool_w': 6.378414e-02, 'pool_scale': 6.526249e-02, 'lru_conv_w': 1.398296e-01, 'lru_conv_b': 2.680901e+00, 'lru_w_r': 6.937999e-02, 'lru_b_r': 5.512854e-02, 'lru_w_i': 1.327498e-01, 'lru_b_i': 5.041733e-02, 'lru_lambda': 8.830308e-02, 'lru_w_out': 1.868849e-01, 'sconv_w': 6.988126e-02, 'sconv_w_out': 6.888700e-02, 'w_mix_out': 3.023320e-01, 'xa_w_q': 7.020212e-03, 'xa_w_k': 7.008248e-03, 'xa_w_v': 1.955679e-02, 'xa_w_o': 1.963843e-02, 'ffn_w_gate': 4.716255e-02, 'ffn_w_up': 4.652908e-02, 'ffn_w_down': 1.546663e-01, 'ln_g': 5.428162e+01, 'ln_b': 5.968030e+00}


def _to_microbatches(a, axis):
    t = _jnp.moveaxis(a, axis, 0)
    t = t.reshape((N_MICROBATCH, t.shape[0] // N_MICROBATCH) + t.shape[1:])
    return _jnp.moveaxis(t, 1, axis + 1)


def setup_inputs(seed: int = 0) -> dict:
    inp = _fwd_setup_inputs(seed)
    key = _jax.random.fold_in(_jax.random.key(seed), 7919)
    shape, _ = _output_shape()
    out = dict(inp)
    out["loss_target"] = _jax.random.normal(_jax.random.fold_in(key, 0), shape, _jnp.float32)
    for i, name in enumerate(TWIN_WEIGHTS):
        w = inp[name].astype(_jnp.float32)
        if MOMENT_SCALE is None:
            s = _jnp.sqrt(_jnp.mean(_jnp.square(w)) + 1e-30)
        else:
            s = MOMENT_SCALE[name]
        km, kv = _jax.random.split(_jax.random.fold_in(key, i + 1))
        out[name] = w
        out["m_" + name] = s * _jax.random.normal(km, w.shape, _jnp.float32)
        out["v_" + name] = (s * s) * _jax.random.uniform(kv, w.shape, _jnp.float32, 0.5, 1.5)
    if N_MICROBATCH > 1:
        for name, axis in PER_EXAMPLE_BATCH_AXIS.items():
            out[name] = _to_microbatches(out[name], axis)
    return {'x': out['x'], 'mem': out['mem'], 'w_in': out['w_in'], 'b_in': out['b_in'], 'pool_w': out['pool_w'], 'pool_scale': out['pool_scale'], 'lru_conv_w': out['lru_conv_w'], 'lru_conv_b': out['lru_conv_b'], 'lru_w_r': out['lru_w_r'], 'lru_b_r': out['lru_b_r'], 'lru_w_i': out['lru_w_i'], 'lru_b_i': out['lru_b_i'], 'lru_lambda': out['lru_lambda'], 'lru_w_out': out['lru_w_out'], 'sconv_w': out['sconv_w'], 'sconv_w_out': out['sconv_w_out'], 'w_mix_out': out['w_mix_out'], 'xa_w_q': out['xa_w_q'], 'xa_w_k': out['xa_w_k'], 'xa_w_v': out['xa_w_v'], 'xa_w_o': out['xa_w_o'], 'ffn_w_gate': out['ffn_w_gate'], 'ffn_w_up': out['ffn_w_up'], 'ffn_w_down': out['ffn_w_down'], 'ln_g': out['ln_g'], 'ln_b': out['ln_b'], 'loss_target': out['loss_target'], 'm_w_in': out['m_w_in'], 'm_b_in': out['m_b_in'], 'm_pool_w': out['m_pool_w'], 'm_pool_scale': out['m_pool_scale'], 'm_lru_conv_w': out['m_lru_conv_w'], 'm_lru_conv_b': out['m_lru_conv_b'], 'm_lru_w_r': out['m_lru_w_r'], 'm_lru_b_r': out['m_lru_b_r'], 'm_lru_w_i': out['m_lru_w_i'], 'm_lru_b_i': out['m_lru_b_i'], 'm_lru_lambda': out['m_lru_lambda'], 'm_lru_w_out': out['m_lru_w_out'], 'm_sconv_w': out['m_sconv_w'], 'm_sconv_w_out': out['m_sconv_w_out'], 'm_w_mix_out': out['m_w_mix_out'], 'm_xa_w_q': out['m_xa_w_q'], 'm_xa_w_k': out['m_xa_w_k'], 'm_xa_w_v': out['m_xa_w_v'], 'm_xa_w_o': out['m_xa_w_o'], 'm_ffn_w_gate': out['m_ffn_w_gate'], 'm_ffn_w_up': out['m_ffn_w_up'], 'm_ffn_w_down': out['m_ffn_w_down'], 'm_ln_g': out['m_ln_g'], 'm_ln_b': out['m_ln_b'], 'v_w_in': out['v_w_in'], 'v_b_in': out['v_b_in'], 'v_pool_w': out['v_pool_w'], 'v_pool_scale': out['v_pool_scale'], 'v_lru_conv_w': out['v_lru_conv_w'], 'v_lru_conv_b': out['v_lru_conv_b'], 'v_lru_w_r': out['v_lru_w_r'], 'v_lru_b_r': out['v_lru_b_r'], 'v_lru_w_i': out['v_lru_w_i'], 'v_lru_b_i': out['v_lru_b_i'], 'v_lru_lambda': out['v_lru_lambda'], 'v_lru_w_out': out['v_lru_w_out'], 'v_sconv_w': out['v_sconv_w'], 'v_sconv_w_out': out['v_sconv_w_out'], 'v_w_mix_out': out['v_w_mix_out'], 'v_xa_w_q': out['v_xa_w_q'], 'v_xa_w_k': out['v_xa_w_k'], 'v_xa_w_v': out['v_xa_w_v'], 'v_xa_w_o': out['v_xa_w_o'], 'v_ffn_w_gate': out['v_ffn_w_gate'], 'v_ffn_w_up': out['v_ffn_w_up'], 'v_ffn_w_down': out['v_ffn_w_down'], 'v_ln_g': out['v_ln_g'], 'v_ln_b': out['v_ln_b']}


def _loss(weights, diff, rest, loss_target):
    with _jax.named_scope("forward"):
        args = {**rest, TWIN_DIFF_INPUT: diff, **{k: w.astype(_WEIGHT_DTYPES[k]) for k, w in weights.items()}}
        y = _forward(args)
    with _jax.named_scope("loss_head"):
        err = _jnp.square(y.astype(_jnp.float32) - loss_target)
        return 0.5 * _jnp.sum(_jnp.mean(err, axis=-1)) if err.ndim else 0.5 * err


def _adamw(w, g, m, v):
    m = ADAM_B1 * m + (1.0 - ADAM_B1) * g
    v = ADAM_B2 * v + (1.0 - ADAM_B2) * _jnp.square(g)
    m_hat = m / (1.0 - ADAM_B1 ** ADAM_STEP)
    v_hat = v / (1.0 - ADAM_B2 ** ADAM_STEP)
    delta = -ADAM_LR * (m_hat / (_jnp.sqrt(v_hat) + ADAM_EPS) + ADAM_WD * w)
    return delta, m, v


def reference(x, mem, w_in, b_in, pool_w, pool_scale, lru_conv_w, lru_conv_b, lru_w_r, lru_b_r, lru_w_i, lru_b_i, lru_lambda, lru_w_out, sconv_w, sconv_w_out, w_mix_out, xa_w_q, xa_w_k, xa_w_v, xa_w_o, ffn_w_gate, ffn_w_up, ffn_w_down, ln_g, ln_b, loss_target, m_w_in, m_b_in, m_pool_w, m_pool_scale, m_lru_conv_w, m_lru_conv_b, m_lru_w_r, m_lru_b_r, m_lru_w_i, m_lru_b_i, m_lru_lambda, m_lru_w_out, m_sconv_w, m_sconv_w_out, m_w_mix_out, m_xa_w_q, m_xa_w_k, m_xa_w_v, m_xa_w_o, m_ffn_w_gate, m_ffn_w_up, m_ffn_w_down, m_ln_g, m_ln_b, v_w_in, v_b_in, v_pool_w, v_pool_scale, v_lru_conv_w, v_lru_conv_b, v_lru_w_r, v_lru_b_r, v_lru_w_i, v_lru_b_i, v_lru_lambda, v_lru_w_out, v_sconv_w, v_sconv_w_out, v_w_mix_out, v_xa_w_q, v_xa_w_k, v_xa_w_v, v_xa_w_o, v_ffn_w_gate, v_ffn_w_up, v_ffn_w_down, v_ln_g, v_ln_b):
    given = dict(x=x, mem=mem, w_in=w_in, b_in=b_in, pool_w=pool_w, pool_scale=pool_scale, lru_conv_w=lru_conv_w, lru_conv_b=lru_conv_b, lru_w_r=lru_w_r, lru_b_r=lru_b_r, lru_w_i=lru_w_i, lru_b_i=lru_b_i, lru_lambda=lru_lambda, lru_w_out=lru_w_out, sconv_w=sconv_w, sconv_w_out=sconv_w_out, w_mix_out=w_mix_out, xa_w_q=xa_w_q, xa_w_k=xa_w_k, xa_w_v=xa_w_v, xa_w_o=xa_w_o, ffn_w_gate=ffn_w_gate, ffn_w_up=ffn_w_up, ffn_w_down=ffn_w_down, ln_g=ln_g, ln_b=ln_b, loss_target=loss_target, m_w_in=m_w_in, m_b_in=m_b_in, m_pool_w=m_pool_w, m_pool_scale=m_pool_scale, m_lru_conv_w=m_lru_conv_w, m_lru_conv_b=m_lru_conv_b, m_lru_w_r=m_lru_w_r, m_lru_b_r=m_lru_b_r, m_lru_w_i=m_lru_w_i, m_lru_b_i=m_lru_b_i, m_lru_lambda=m_lru_lambda, m_lru_w_out=m_lru_w_out, m_sconv_w=m_sconv_w, m_sconv_w_out=m_sconv_w_out, m_w_mix_out=m_w_mix_out, m_xa_w_q=m_xa_w_q, m_xa_w_k=m_xa_w_k, m_xa_w_v=m_xa_w_v, m_xa_w_o=m_xa_w_o, m_ffn_w_gate=m_ffn_w_gate, m_ffn_w_up=m_ffn_w_up, m_ffn_w_down=m_ffn_w_down, m_ln_g=m_ln_g, m_ln_b=m_ln_b, v_w_in=v_w_in, v_b_in=v_b_in, v_pool_w=v_pool_w, v_pool_scale=v_pool_scale, v_lru_conv_w=v_lru_conv_w, v_lru_conv_b=v_lru_conv_b, v_lru_w_r=v_lru_w_r, v_lru_b_r=v_lru_b_r, v_lru_w_i=v_lru_w_i, v_lru_b_i=v_lru_b_i, v_lru_lambda=v_lru_lambda, v_lru_w_out=v_lru_w_out, v_sconv_w=v_sconv_w, v_sconv_w_out=v_sconv_w_out, v_w_mix_out=v_w_mix_out, v_xa_w_q=v_xa_w_q, v_xa_w_k=v_xa_w_k, v_xa_w_v=v_xa_w_v, v_xa_w_o=v_xa_w_o, v_ffn_w_gate=v_ffn_w_gate, v_ffn_w_up=v_ffn_w_up, v_ffn_w_down=v_ffn_w_down, v_ln_g=v_ln_g, v_ln_b=v_ln_b)
    weights = {n: given[n] for n in TWIN_WEIGHTS}
    shared = {n: given[n] for n in SHARED_INPUTS}
    per_example = {n: given[n] for n in ['x', 'mem']}
    grad_fn = _jax.value_and_grad(_loss, argnums=(0, 1))

    def one_microbatch(ex, loss_target):
        ex = dict(ex)
        diff = ex.pop(TWIN_DIFF_INPUT)
        return grad_fn(weights, diff, {**shared, **ex}, loss_target)

    if N_MICROBATCH == 1:
        loss, (grad_w, grad_x) = one_microbatch(per_example, given["loss_target"])
    else:
        def body(carry, xs):
            loss_sum, grad_sum = carry
            l_k, (gw_k, gx_k) = one_microbatch(xs[0], xs[1])
            with _jax.named_scope("update"):
                return (loss_sum + l_k, _jax.tree.map(_jnp.add, grad_sum, gw_k)), gx_k

        init = (_jnp.zeros((), _jnp.float32), _jax.tree.map(_jnp.zeros_like, weights))
        (loss, grad_w), grad_x = _jax.lax.scan(body, init, (per_example, given["loss_target"]))
    with _jax.named_scope("update"):
        delta_w, new_m, new_v = {}, {}, {}
        for n in TWIN_WEIGHTS:
            delta_w[n], new_m[n], new_v[n] = _adamw(weights[n], grad_w[n], given["m_" + n], given["v_" + n])
    return (loss, grad_x, *[grad_w[n] for n in TWIN_WEIGHTS], *[delta_w[n] for n in TWIN_WEIGHTS],
            *[new_m[n] for n in TWIN_WEIGHTS], *[new_v[n] for n in TWIN_WEIGHTS])
```

```python
import functools
import math

import jax
import jax.numpy as jnp
from jax import lax
from jax.experimental import pallas as pl
from jax.experimental.pallas import tpu as pltpu

F32 = jnp.float32
BF16 = jnp.bfloat16
MESH = pl.DeviceIdType.MESH

N_DEV = 8
LRU_HEADS = 8
LRU_CONV = 4
LRU_C = 8.0
SCONV_K = 3
POOL_WINDOWS = (2, 4, 8, 16)
X_HEADS = 4
DEPTH = 2
ALPHA = (2 * DEPTH) ** 0.25
LN_EPS = 1e-5
ADAM_LR = 0.001
ADAM_B1 = 0.9
ADAM_B2 = 0.999
ADAM_EPS = 1e-08
ADAM_WD = 0.01
ADAM_STEP = 10

HALO = 16
SUBLANES = 8
VMEM_LIMIT = 56 * 1024 * 1024
TS_MIXER = 128
TS_ATTN = 512
TS_FFN = 256
TS_MM = 512
TK_MM = 2048
TR_ADAM = 256

V_PSCALE, V_CW, V_CB, V_BR, V_BI, V_LAM, V_SW, V_G, V_B = 0, 1, 5, 6, 7, 8, 9, 12, 15
V_ROWS = 24
A_PSCALE, A_CW, A_CB, A_BR, A_BI, A_SP, A_SW, A_G, A_B = 0, 1, 5, 6, 7, 8, 9, 12, 13
A_ROWS = 16


def _cparams(sem):
    return pltpu.CompilerParams(dimension_semantics=sem, vmem_limit_bytes=VMEM_LIMIT)


def _tile(n, pref):
    if n <= pref:
        return n
    assert n % pref == 0, (n, pref)
    return pref


def _const_spec(shape):
    nd = len(shape)
    return pl.BlockSpec(shape, lambda *_: (0,) * nd)


def _dot(a, b):
    return jnp.dot(a.astype(BF16), b.astype(BF16), preferred_element_type=F32)


def _dot_tn(a, b):
    return lax.dot_general(a.astype(BF16), b.astype(BF16), (((0,), (0,)), ((), ())),
                           preferred_element_type=F32)


def _sigmoid(x):
    return 1.0 / (1.0 + jnp.exp(-x))


def _expm1(x):
    small = x * (1.0 + x * (0.5 + x * (1.0 / 6.0 + x * (1.0 / 24.0 + x * (1.0 / 120.0)))))
    return jnp.where(jnp.abs(x) < 0.05, small, jnp.exp(x) - 1.0)


def _softplus(y):
    e = jnp.exp(-jnp.abs(y))
    log1p = jnp.where(e < 1e-4, e * (1.0 - e * (0.5 - e * (1.0 / 3.0))), jnp.log(1.0 + e))
    return jnp.maximum(y, 0.0) + log1p


def _ln_fwd(r, g, b):
    mu = jnp.mean(r, axis=-1, keepdims=True)
    xc = r - mu
    var = jnp.mean(xc * xc, axis=-1, keepdims=True)
    return xc * lax.rsqrt(var + LN_EPS) * g + b


def _ln_bwd(dy, r, g):
    mu = jnp.mean(r, axis=-1, keepdims=True)
    xc = r - mu
    var = jnp.mean(xc * xc, axis=-1, keepdims=True)
    rstd = lax.rsqrt(var + LN_EPS)
    yhat = xc * rstd
    dyh = dy * g
    m1 = jnp.mean(dyh, axis=-1, keepdims=True)
    m2 = jnp.mean(dyh * yhat, axis=-1, keepdims=True)
    return rstd * (dyh - m1 - yhat * m2), dy * yhat


def _colsum(a):
    return jnp.sum(a, axis=0, keepdims=True)


def _all_gather(xs, name):
    rows, width = xs.shape

    def body(x_ref, out_ref, send_sems, recv_sems, local_sem):
        x, y, c = lax.axis_index("x"), lax.axis_index("y"), lax.axis_index("c")
        me, sibling = (x, y, c), (x, y, 1 - c)
        chips = [(1 - x, y), (x, 1 - y), (1 - x, 1 - y)]

        def slot(px, py, pc):
            return out_ref.at[4 * px + 2 * py + pc]

        def copy(k, block, to, src=None):
            return pltpu.make_async_remote_copy(
                src_ref=slot(*block) if src is None else src, dst_ref=slot(*block),
                send_sem=send_sems.at[k], recv_sem=recv_sems.at[k], device_id=to, device_id_type=MESH)

        mine = pltpu.make_async_copy(x_ref, slot(*me), local_sem)
        mine.start()
        first = [copy(0, me, sibling, src=x_ref)]
        first += [copy(1 + j, me, (*chip, c), src=x_ref) for j, chip in enumerate(chips)]
        for cp in first:
            cp.start()
        passed = [copy(4 + j, (*chip, c), sibling) for j, chip in enumerate(chips)]
        for j, chip in enumerate(chips):
            copy(1 + j, (*chip, c), me).wait_recv()
            passed[j].start()
        copy(0, sibling, me).wait_recv()
        for j, chip in enumerate(chips):
            copy(4 + j, (*chip, 1 - c), me).wait_recv()
        for cp in first + passed:
            cp.wait_send()
        mine.wait()

    return pl.pallas_call(
        body, name=name,
        out_shape=jax.ShapeDtypeStruct((N_DEV, rows, width), xs.dtype),
        in_specs=[pl.BlockSpec(memory_space=pl.ANY)],
        out_specs=pl.BlockSpec(memory_space=pl.ANY),
        scratch_shapes=[pltpu.SemaphoreType.DMA((7,)), pltpu.SemaphoreType.DMA((7,)), pltpu.SemaphoreType.DMA],
    )(xs)


def _all_to_all(g, name):
    _, rows, width = g.shape

    def body(g_ref, out_ref, send_sems, recv_sems, local_sem):
        x, y, c = lax.axis_index("x"), lax.axis_index("y"), lax.axis_index("c")
        me = 4 * x + 2 * y + c
        mine = pltpu.make_async_copy(g_ref.at[me], out_ref.at[me], local_sem)
        mine.start()
        copies = []
        for k in range(1, N_DEV):
            px = 1 - x if k & 4 else x
            py = 1 - y if k & 2 else y
            pc = 1 - c if k & 1 else c
            peer = 4 * px + 2 * py + pc
            copies.append(pltpu.make_async_remote_copy(
                src_ref=g_ref.at[peer], dst_ref=out_ref.at[me],
                send_sem=send_sems.at[k - 1], recv_sem=recv_sems.at[k - 1],
                device_id=(px, py, pc), device_id_type=MESH))
        for cp in copies:
            cp.start()
        for cp in copies:
            cp.wait_recv()
        for cp in copies:
            cp.wait_send()
        mine.wait()

    return pl.pallas_call(
        body, name=name,
        out_shape=jax.ShapeDtypeStruct(g.shape, g.dtype),
        in_specs=[pl.BlockSpec(memory_space=pl.ANY)],
        out_specs=pl.BlockSpec(memory_space=pl.ANY),
        scratch_shapes=[pltpu.SemaphoreType.DMA((7,)), pltpu.SemaphoreType.DMA((7,)), pltpu.SemaphoreType.DMA],
    )(g)


def _adamw_sum(parts, w, m, v, name):
    _, rows, width = parts.shape
    tr = _tile(rows, TR_ADAM)
    c1 = 1.0 - ADAM_B1 ** ADAM_STEP
    c2 = 1.0 - ADAM_B2 ** ADAM_STEP

    def body(p_ref, w_ref, m_ref, v_ref, g_ref, d_ref, nm_ref, nv_ref):
        g = p_ref[0]
        for k in range(1, N_DEV):
            g = g + p_ref[k]
        nm = ADAM_B1 * m_ref[...] + (1.0 - ADAM_B1) * g
        nv = ADAM_B2 * v_ref[...] + (1.0 - ADAM_B2) * (g * g)
        m_hat = nm / c1
        v_hat = nv / c2
        g_ref[...] = g
        d_ref[...] = -ADAM_LR * (m_hat / (jnp.sqrt(v_hat) + ADAM_EPS) + ADAM_WD * w_ref[...])
        nm_ref[...] = nm
        nv_ref[...] = nv

    spec = pl.BlockSpec((tr, width), lambda i: (i, 0))
    out = jax.ShapeDtypeStruct((rows, width), F32)
    return pl.pallas_call(
        body, name=name, grid=(rows // tr,),
        in_specs=[pl.BlockSpec((N_DEV, tr, width), lambda i: (0, i, 0)), spec, spec, spec],
        out_specs=[spec, spec, spec, spec], out_shape=[out, out, out, out],
        compiler_params=_cparams(("parallel",)),
    )(parts, w, m, v)


def _mm(a, wb, name, bias=None, add=None, add_scale=1.0, out_dtype=F32):
    m, k = a.shape
    nb, k2, tn = wb.shape
    assert k == k2
    tm = _tile(m, TS_MM)
    tk = _tile(k, TK_MM)
    nk = k // tk

    def body(*refs):
        a_ref, w_ref = refs[0], refs[1]
        pos = 2
        b_ref = add_ref = None
        if bias is not None:
            b_ref = refs[pos]
            pos += 1
        if add is not None:
            add_ref = refs[pos]
            pos += 1
        o_ref, acc_ref = refs[pos], refs[pos + 1]
        kk = pl.program_id(2)

        @pl.when(kk == 0)
        def _():
            acc_ref[...] = jnp.zeros_like(acc_ref)

        acc_ref[...] += _dot(a_ref[...], w_ref[...])

        @pl.when(kk == nk - 1)
        def _():
            r = acc_ref[...]
            if b_ref is not None:
                r = r + b_ref[...]
            if add_ref is not None:
                r = r + add_scale * add_ref[...]
            o_ref[...] = r.astype(o_ref.dtype)

    in_specs = [pl.BlockSpec((tm, tk), lambda j, i, kk: (i, kk)),
                pl.BlockSpec((None, tk, tn), lambda j, i, kk: (j, kk, 0))]
    args = [a, wb]
    if bias is not None:
        in_specs.append(pl.BlockSpec((1, tn), lambda j, i, kk: (0, j)))
        args.append(bias)
    if add is not None:
        in_specs.append(pl.BlockSpec((tm, tn), lambda j, i, kk: (i, j)))
        args.append(add)
    return pl.pallas_call(
        body, name=name, grid=(nb, m // tm, nk),
        in_specs=in_specs,
        out_specs=pl.BlockSpec((tm, tn), lambda j, i, kk: (i, j)),
        out_shape=jax.ShapeDtypeStruct((m, nb * tn), out_dtype),
        scratch_shapes=[pltpu.VMEM((tm, tn), F32)],
        compiler_params=_cparams(("parallel", "parallel", "arbitrary")),
    )(*args)


def _mm_tn(a, b, tn, name):
    s, k = a.shape
    s2, n = b.shape
    assert s == s2 and n % tn == 0
    nb = n // tn
    ts = _tile(s, TS_MM)

    def body(a_ref, b_ref, o_ref):
        @pl.when(pl.program_id(1) == 0)
        def _():
            o_ref[...] = jnp.zeros_like(o_ref)

        o_ref[...] += _dot_tn(a_ref[...], b_ref[...])

    return pl.pallas_call(
        body, name=name, grid=(nb, s // ts),
        in_specs=[pl.BlockSpec((ts, k), lambda j, i: (i, 0)),
                  pl.BlockSpec((ts, tn), lambda j, i: (i, j))],
        out_specs=pl.BlockSpec((None, k, tn), lambda j, i: (j, 0, 0)),
        out_shape=jax.ShapeDtypeStruct((nb, k, tn), F32),
        compiler_params=_cparams(("parallel", "arbitrary")),
    )(a, b)


def _scan_fwd(a_ref, b_ref, h_ref, carry_ref, ts):
    rowid = lax.broadcasted_iota(jnp.int32, (SUBLANES, 1), 0)

    def group(gi, hprev):
        r0 = pl.multiple_of(gi * SUBLANES, SUBLANES)
        a = a_ref[pl.ds(r0, SUBLANES), :]
        b = b_ref[pl.ds(r0, SUBLANES), :]
        for d in (1, 2, 4):
            a_sh = jnp.where(rowid >= d, pltpu.roll(a, d, 0), 1.0)
            b_sh = jnp.where(rowid >= d, pltpu.roll(b, d, 0), 0.0)
            b = a * b_sh + b
            a = a * a_sh
        hh = a * hprev + b
        h_ref[pl.ds(r0, SUBLANES), :] = hh
        return hh[SUBLANES - 1:SUBLANES, :]

    last = lax.fori_loop(0, ts // SUBLANES, group, carry_ref[0:1, :])
    carry_ref[0:1, :] = last


def _scan_rev(c_ref, b_ref, g_ref, carry_ref, ts):
    rowid = lax.broadcasted_iota(jnp.int32, (SUBLANES, 1), 0)
    ng = ts // SUBLANES

    def group(gi, gnext):
        r0 = pl.multiple_of((ng - 1 - gi) * SUBLANES, SUBLANES)
        c = c_ref[pl.ds(r0, SUBLANES), :]
        b = b_ref[pl.ds(r0, SUBLANES), :]
        for d in (1, 2, 4):
            keep = rowid < SUBLANES - d
            c_sh = jnp.where(keep, pltpu.roll(c, SUBLANES - d, 0), 1.0)
            b_sh = jnp.where(keep, pltpu.roll(b, SUBLANES - d, 0), 0.0)
            b = c * b_sh + b
            c = c * c_sh
        gg = c * gnext + b
        g_ref[pl.ds(r0, SUBLANES), :] = gg
        return gg[0:1, :]

    first = lax.fori_loop(0, ng, group, carry_ref[0:1, :])
    carry_ref[0:1, :] = first


def _heads_dot(v, w_ref, heads):
    hd = v.shape[1] // heads
    return jnp.concatenate([_dot(v[:, h * hd:(h + 1) * hd], w_ref[h]) for h in range(heads)], axis=1)


def _mixer_recompute(i, ts, d, z_ref, zh_ref, vec_ref, wr_ref, wi_ref, pext, lext, qext):
    first = i == 0

    def zc(k):
        return z_ref[:, k * d:(k + 1) * d]

    def zhalo(k):
        return jnp.where(first, 0.0, zh_ref[:, k * d:(k + 1) * d])

    tglob = i * ts + lax.broadcasted_iota(jnp.int32, (ts, 1), 0)
    pext[0:HALO, :] = zhalo(0)
    pext[HALO:HALO + ts, :] = zc(0)
    dg = d // len(POOL_WINDOWS)
    ps = []
    for g, win in enumerate(POOL_WINDOWS):
        cs = slice(g * dg, (g + 1) * dg)
        u = pext[HALO:HALO + ts, cs]
        s = u
        for j in range(1, win):
            s = s + pext[HALO - j:HALO - j + ts, cs]
        cnt = jnp.minimum(tglob + 1, win).astype(F32)
        ps.append(s / cnt - u)
    p = jnp.concatenate(ps, axis=1)
    lext[0:HALO, :] = zhalo(1)
    lext[HALO:HALO + ts, :] = zc(1)
    v = vec_ref[V_CB:V_CB + 1, :]
    for j in range(LRU_CONV):
        sh = LRU_CONV - 1 - j
        v = v + vec_ref[V_CW + j:V_CW + j + 1, :] * lext[HALO - sh:HALO - sh + ts, :]
    r = _sigmoid(_heads_dot(v, wr_ref, LRU_HEADS) + vec_ref[V_BR:V_BR + 1, :])
    ig = _sigmoid(_heads_dot(v, wi_ref, LRU_HEADS) + vec_ref[V_BI:V_BI + 1, :])
    sp = _softplus(-vec_ref[V_LAM:V_LAM + 1, :])
    log_a = -LRU_C * r * sp
    a = jnp.exp(log_a)
    mult = jnp.sqrt(-_expm1(2.0 * log_a))
    qext[0:HALO, :] = zhalo(3) * zhalo(4)
    qext[HALO:HALO + ts, :] = zc(3) * zc(4)
    cq = jnp.zeros((ts, d), F32)
    for j in range(SCONV_K):
        sh = SCONV_K - 1 - j
        cq = cq + vec_ref[V_SW + j:V_SW + j + 1, :] * qext[HALO - sh:HALO - sh + ts, :]
    return dict(p=p, v=v, r=r, ig=ig, sp=sp, a=a, mult=mult, cq=cq, tglob=tglob)


def _halo_index(ts):
    blocks = ts // HALO
    return lambda t: (jnp.maximum(t * blocks - 1, 0), 0)


def _mixer_fwd(x, z, pw, wr, wi, wlo, wsc, wmix, vec, name):
    s, d = x.shape
    ts = _tile(s, TS_MIXER)
    nt = s // ts
    dg = d // len(POOL_WINDOWS)

    def body(x_ref, z_ref, zh_ref, pw_ref, wr_ref, wi_ref, wlo_ref, wsc_ref, wmix_ref, vec_ref,
             x1_ref, rpre_ref, h_ref, yp_ref, yl_ref, yc_ref, mg_ref, e_ref,
             pext, lext, qext, a_scr, b_scr, hcarry):
        i = pl.program_id(0)

        @pl.when(i == 0)
        def _():
            hcarry[...] = jnp.zeros_like(hcarry)

        f = _mixer_recompute(i, ts, d, z_ref, zh_ref, vec_ref, wr_ref, wi_ref, pext, lext, qext)
        scale = vec_ref[V_PSCALE:V_PSCALE + 1, :]
        yp = jnp.concatenate([_dot(f["p"][:, g * dg:(g + 1) * dg], pw_ref[g]) for g in range(len(POOL_WINDOWS))],
                             axis=1) * scale
        a_scr[...] = f["a"]
        b_scr[...] = f["mult"] * (f["ig"] * f["v"])
        _scan_fwd(a_scr, b_scr, h_ref, hcarry, ts)
        yl = _dot(h_ref[...], wlo_ref[...])
        e = z_ref[:, 2 * d:3 * d] * f["cq"]
        yc = _dot(e, wsc_ref[...])
        merged = (_sigmoid(z_ref[:, 5 * d:6 * d]) * yp + _sigmoid(z_ref[:, 6 * d:7 * d]) * yl
                  + _sigmoid(z_ref[:, 7 * d:8 * d]) * yc)
        rpre = ALPHA * x_ref[...] + _dot(merged, wmix_ref[...])
        x1_ref[...] = _ln_fwd(rpre, vec_ref[V_G:V_G + 1, :], vec_ref[V_B:V_B + 1, :])
        rpre_ref[...] = rpre
        yp_ref[...] = yp
        yl_ref[...] = yl
        yc_ref[...] = yc
        mg_ref[...] = merged.astype(BF16)
        e_ref[...] = e.astype(BF16)

    tile = pl.BlockSpec((ts, d), lambda t: (t, 0))
    f32o = jax.ShapeDtypeStruct((s, d), F32)
    bfo = jax.ShapeDtypeStruct((s, d), BF16)
    return pl.pallas_call(
        body, name=name, grid=(nt,),
        in_specs=[tile, pl.BlockSpec((ts, 8 * d), lambda t: (t, 0)), pl.BlockSpec((HALO, 8 * d), _halo_index(ts)),
                  _const_spec(pw.shape), _const_spec(wr.shape), _const_spec(wi.shape), _const_spec(wlo.shape),
                  _const_spec(wsc.shape), _const_spec(wmix.shape), _const_spec(vec.shape)],
        out_specs=[tile] * 8,
        out_shape=[f32o, f32o, f32o, f32o, f32o, f32o, bfo, bfo],
        scratch_shapes=[pltpu.VMEM((HALO + ts, d), F32)] * 3 + [pltpu.VMEM((ts, d), F32)] * 2
        + [pltpu.VMEM((SUBLANES, d), F32)],
        compiler_params=_cparams(("arbitrary",)),
    )(x, z, z, pw, wr, wi, wlo, wsc, wmix, vec)


def _mixer_bwd(dx1, rpre, z, h, yp, yl, yc, pw, pwt, wr, wi, wrt, wit, wlot, wsct, wmixt, vec, name):
    s, d = dx1.shape
    ts = _tile(s, TS_MIXER)
    nt = s // ts
    dg = d // len(POOL_WINDOWS)
    ng = len(POOL_WINDOWS)

    def body(dx1_ref, rpre_ref, z_ref, zh_ref, h_ref, hh_ref, yp_ref, yl_ref, yc_ref,
             pw_ref, pwt_ref, wr_ref, wi_ref, wrt_ref, wit_ref, wlot_ref, wsct_ref, wmixt_ref, vec_ref,
             dz_ref, dr_ref, dyl_ref, dyc_ref, p_ref, dyps_ref, v_ref, dprer_ref, dprei_ref, acc_ref, dbin_ref,
             pext, lext, qext, fext, c_scr, b_scr, g_scr, gcarry, acarry, dcq_c, dv_c, m_c):
        i = pl.program_id(0)
        t = nt - 1 - i

        @pl.when(i == 0)
        def _():
            for ref in (gcarry, acarry, dcq_c, dv_c, m_c, acc_ref, dbin_ref):
                ref[...] = jnp.zeros_like(ref)

        f = _mixer_recompute(t, ts, d, z_ref, zh_ref, vec_ref, wr_ref, wi_ref, pext, lext, qext)

        def vrow(k):
            return vec_ref[k:k + 1, :]

        def zc(k):
            return z_ref[:, k * d:(k + 1) * d]

        def acc(row, val):
            acc_ref[row:row + 1, :] += _colsum(val)

        def future(tile_val, carry_ref):
            fext[0:ts, :] = tile_val
            fext[ts:ts + HALO, :] = carry_ref[...]
            carry_ref[...] = tile_val[0:HALO, :]

        dx1v = dx1_ref[...]
        dr, dyy = _ln_bwd(dx1v, rpre_ref[...], vrow(V_G))
        acc(A_G, dyy)
        acc(A_B, dx1v)
        dr_ref[...] = dr
        dmg = _dot(dr, wmixt_ref[...])
        dzs = [None] * 8
        gates = []
        for k, y_ref in enumerate((yp_ref, yl_ref, yc_ref)):
            gk = _sigmoid(zc(5 + k))
            dzs[5 + k] = dmg * y_ref[...] * gk * (1.0 - gk)
            gates.append(gk)
        dyp = dmg * gates[0]
        dyl = dmg * gates[1]
        dyc = dmg * gates[2]
        dyl_ref[...] = dyl.astype(BF16)
        dyc_ref[...] = dyc.astype(BF16)

        de = _dot(dyc, wsct_ref[...])
        dzs[2] = de * f["cq"]
        dcq = de * zc(2)
        for j in range(SCONV_K):
            sh = SCONV_K - 1 - j
            acc(A_SW + j, dcq * qext[HALO - sh:HALO - sh + ts, :])
        future(dcq, dcq_c)
        dq = jnp.zeros((ts, d), F32)
        for j in range(SCONV_K):
            sh = SCONV_K - 1 - j
            dq = dq + vrow(V_SW + j) * fext[sh:sh + ts, :]
        dzs[3] = dq * zc(4)
        dzs[4] = dq * zc(3)

        a, mult, r, ig, v, sp = f["a"], f["mult"], f["r"], f["ig"], f["v"], f["sp"]
        fext[0:ts, :] = a
        fext[ts:ts + SUBLANES, :] = acarry[...]
        acarry[...] = a[0:SUBLANES, :]
        c_scr[...] = fext[1:1 + ts, :]
        b_scr[...] = _dot(dyl, wlot_ref[...])
        _scan_rev(c_scr, b_scr, g_scr, gcarry, ts)
        gs = g_scr[...]
        fext[0:HALO, :] = jnp.where(t == 0, 0.0, hh_ref[...])
        fext[HALO:HALO + ts, :] = h_ref[...]
        hprev = fext[HALO - 1:HALO - 1 + ts, :]
        iv = ig * v
        dlog_a = gs * hprev * a + gs * iv * (-(a * a) / mult)
        div = gs * mult
        acc(A_SP, dlog_a * (-LRU_C) * r)
        dpre_r = dlog_a * (-LRU_C) * sp * r * (1.0 - r)
        dpre_i = div * v * ig * (1.0 - ig)
        acc(A_BR, dpre_r)
        acc(A_BI, dpre_i)
        dv = div * ig + _heads_dot(dpre_r, wrt_ref, LRU_HEADS) + _heads_dot(dpre_i, wit_ref, LRU_HEADS)
        v_ref[...] = v.astype(BF16)
        dprer_ref[...] = dpre_r.astype(BF16)
        dprei_ref[...] = dpre_i.astype(BF16)
        acc(A_CB, dv)
        for j in range(LRU_CONV):
            sh = LRU_CONV - 1 - j
            acc(A_CW + j, dv * lext[HALO - sh:HALO - sh + ts, :])
        future(dv, dv_c)
        dzl = jnp.zeros((ts, d), F32)
        for j in range(LRU_CONV):
            sh = LRU_CONV - 1 - j
            dzl = dzl + vrow(V_CW + j) * fext[sh:sh + ts, :]
        dzs[1] = dzl

        p = f["p"]
        ypre = jnp.concatenate([_dot(p[:, g * dg:(g + 1) * dg], pw_ref[g]) for g in range(ng)], axis=1)
        acc(A_PSCALE, dyp * ypre)
        dyps = dyp * vrow(V_PSCALE)
        p_ref[...] = p.astype(BF16)
        dyps_ref[...] = dyps.astype(BF16)
        dp = jnp.concatenate([_dot(dyps[:, g * dg:(g + 1) * dg], pwt_ref[g]) for g in range(ng)], axis=1)
        cnts = [jnp.minimum(f["tglob"] + 1, win).astype(F32) for win in POOL_WINDOWS]
        mm = jnp.concatenate([dp[:, g * dg:(g + 1) * dg] / cnts[g] for g in range(ng)], axis=1)
        future(mm, m_c)
        dzps = []
        for g, win in enumerate(POOL_WINDOWS):
            cs = slice(g * dg, (g + 1) * dg)
            sm = fext[0:ts, cs]
            for j in range(1, win):
                sm = sm + fext[j:j + ts, cs]
            dzps.append(sm - dp[:, cs])
        dzs[0] = jnp.concatenate(dzps, axis=1)

        for k in range(8):
            dz_ref[:, k * d:(k + 1) * d] = dzs[k].astype(BF16)
            dbin_ref[:, k * d:(k + 1) * d] += _colsum(dzs[k])

    def rev(tt):
        return (nt - 1 - tt, 0)

    halo = _halo_index(ts)
    tile = pl.BlockSpec((ts, d), rev)
    hspec = pl.BlockSpec((HALO, d), lambda tt: halo(nt - 1 - tt))
    f32o = jax.ShapeDtypeStruct((s, d), F32)
    bfo = jax.ShapeDtypeStruct((s, d), BF16)
    consts = (pw, pwt, wr, wi, wrt, wit, wlot, wsct, wmixt, vec)
    return pl.pallas_call(
        body, name=name, grid=(nt,),
        in_specs=[tile, tile, pl.BlockSpec((ts, 8 * d), rev),
                  pl.BlockSpec((HALO, 8 * d), lambda tt: halo(nt - 1 - tt)), tile, hspec, tile, tile, tile]
        + [_const_spec(c.shape) for c in consts],
        out_specs=[pl.BlockSpec((ts, 8 * d), rev), tile, tile, tile, tile, tile, tile, tile, tile,
                   _const_spec((A_ROWS, d)), _const_spec((1, 8 * d))],
        out_shape=[jax.ShapeDtypeStruct((s, 8 * d), BF16), f32o, bfo, bfo, bfo, bfo, bfo, bfo, bfo,
                   jax.ShapeDtypeStruct((A_ROWS, d), F32), jax.ShapeDtypeStruct((1, 8 * d), F32)],
        scratch_shapes=[pltpu.VMEM((HALO + ts, d), F32)] * 4 + [pltpu.VMEM((ts, d), F32)] * 3
        + [pltpu.VMEM((SUBLANES, d), F32)] * 2 + [pltpu.VMEM((HALO, d), F32)] * 3,
        compiler_params=_cparams(("arbitrary",)),
    )(dx1, rpre, z, z, h, h, yp, yl, yc, *consts)


def _softmax_rows(sc):
    mx = jnp.max(sc, axis=-1, keepdims=True)
    ex = jnp.exp(sc - mx)
    return ex / jnp.sum(ex, axis=-1, keepdims=True)


def _attn_fwd(x1, wq, wo, kt, vv, vec, name):
    s, d = x1.shape
    ts = _tile(s, TS_ATTN)
    hd = d // X_HEADS
    scale = hd ** -0.5

    def body(x_ref, wq_ref, wo_ref, kt_ref, v_ref, vec_ref, x2_ref, rpre_ref, q_ref, o_ref):
        xv = x_ref[...]
        q = _dot(xv, wq_ref[...]).astype(BF16)
        q_ref[...] = q
        for hh in range(X_HEADS):
            cs = slice(hh * hd, (hh + 1) * hd)
            p = _softmax_rows(_dot(q[:, cs], kt_ref[cs, :]) * scale)
            o_ref[:, cs] = _dot(p, v_ref[:, cs]).astype(BF16)
        rpre = ALPHA * xv + _dot(o_ref[...], wo_ref[...])
        rpre_ref[...] = rpre
        x2_ref[...] = _ln_fwd(rpre, vec_ref[V_G + 1:V_G + 2, :], vec_ref[V_B + 1:V_B + 2, :])

    tile = pl.BlockSpec((ts, d), lambda t: (t, 0))
    f32o = jax.ShapeDtypeStruct((s, d), F32)
    bfo = jax.ShapeDtypeStruct((s, d), BF16)
    consts = (wq, wo, kt, vv, vec)
    return pl.pallas_call(
        body, name=name, grid=(s // ts,),
        in_specs=[tile] + [_const_spec(c.shape) for c in consts],
        out_specs=[tile] * 4, out_shape=[f32o, f32o, bfo, bfo],
        compiler_params=_cparams(("parallel",)),
    )(x1, *consts)


def _attn_bwd(dx2, rpre, q, wqt, wot, kk, kt, vt, vec, name):
    s, d = dx2.shape
    ts = _tile(s, TS_ATTN)
    nm = kk.shape[0]
    hd = d // X_HEADS
    scale = hd ** -0.5

    def body(dx2_ref, rpre_ref, q_ref, wqt_ref, wot_ref, k_ref, kt_ref, vt_ref, vec_ref,
             dx1_ref, dq_ref, dr_ref, dk_ref, dv_ref, ln_ref):
        @pl.when(pl.program_id(0) == 0)
        def _():
            for ref in (dk_ref, dv_ref, ln_ref):
                ref[...] = jnp.zeros_like(ref)

        dyv = dx2_ref[...]
        dr, dyy = _ln_bwd(dyv, rpre_ref[...], vec_ref[V_G + 1:V_G + 2, :])
        ln_ref[0:1, :] += _colsum(dyy)
        ln_ref[1:2, :] += _colsum(dyv)
        dr_ref[...] = dr.astype(BF16)
        do = _dot(dr, wot_ref[...])
        q = q_ref[...]
        for hh in range(X_HEADS):
            cs = slice(hh * hd, (hh + 1) * hd)
            p = _softmax_rows(_dot(q[:, cs], kt_ref[cs, :]) * scale)
            dp = _dot(do[:, cs], vt_ref[cs, :])
            ds = p * (dp - jnp.sum(dp * p, axis=-1, keepdims=True)) * scale
            dq_ref[:, cs] = _dot(ds, k_ref[:, cs]).astype(BF16)
            dk_ref[:, cs] += _dot_tn(ds, q[:, cs])
            dv_ref[:, cs] += _dot_tn(p, do[:, cs])
        dx1_ref[...] = ALPHA * dr + _dot(dq_ref[...], wqt_ref[...])

    tile = pl.BlockSpec((ts, d), lambda t: (t, 0))
    consts = (wqt, wot, kk, kt, vt, vec)
    return pl.pallas_call(
        body, name=name, grid=(s // ts,),
        in_specs=[tile, tile, tile] + [_const_spec(c.shape) for c in consts],
        out_specs=[tile, tile, tile, _const_spec((nm, d)), _const_spec((nm, d)), _const_spec((2, d))],
        out_shape=[jax.ShapeDtypeStruct((s, d), F32), jax.ShapeDtypeStruct((s, d), BF16),
                   jax.ShapeDtypeStruct((s, d), BF16), jax.ShapeDtypeStruct((nm, d), F32),
                   jax.ShapeDtypeStruct((nm, d), F32), jax.ShapeDtypeStruct((2, d), F32)],
        compiler_params=_cparams(("arbitrary",)),
    )(dx2, rpre, q, *consts)


def _ffn_out(x2, hgu, wd, vec, name):
    s, d = x2.shape
    ff = wd.shape[0]
    ts = _tile(s, TS_FFN)

    def body(x_ref, hgu_ref, wd_ref, vec_ref, x3_ref, rpre_ref, act_ref):
        hg = hgu_ref[:, 0:ff]
        act = hg * _sigmoid(hg) * hgu_ref[:, ff:2 * ff]
        act_ref[...] = act.astype(BF16)
        rpre = ALPHA * x_ref[...] + _dot(act, wd_ref[...])
        rpre_ref[...] = rpre
        x3_ref[...] = _ln_fwd(rpre, vec_ref[V_G + 2:V_G + 3, :], vec_ref[V_B + 2:V_B + 3, :])

    tile = pl.BlockSpec((ts, d), lambda t: (t, 0))
    return pl.pallas_call(
        body, name=name, grid=(s // ts,),
        in_specs=[tile, pl.BlockSpec((ts, 2 * ff), lambda t: (t, 0)), _const_spec(wd.shape), _const_spec(vec.shape)],
        out_specs=[tile, tile, pl.BlockSpec((ts, ff), lambda t: (t, 0))],
        out_shape=[jax.ShapeDtypeStruct((s, d), F32), jax.ShapeDtypeStruct((s, d), F32),
                   jax.ShapeDtypeStruct((s, ff), BF16)],
        compiler_params=_cparams(("parallel",)),
    )(x2, hgu, wd, vec)


def _ffn_bwd(dy, rpre, hgu, wdt, wgt, wut, vec, name):
    s, d = dy.shape
    ff = wgt.shape[0]
    ts = _tile(s, TS_FFN)

    def body(dy_ref, rpre_ref, hgu_ref, wdt_ref, wgt_ref, wut_ref, vec_ref, dx_ref, dr_ref, dhgu_ref, ln_ref):
        @pl.when(pl.program_id(0) == 0)
        def _():
            ln_ref[...] = jnp.zeros_like(ln_ref)

        dyv = dy_ref[...]
        dr, dyy = _ln_bwd(dyv, rpre_ref[...], vec_ref[V_G + 2:V_G + 3, :])
        ln_ref[0:1, :] += _colsum(dyy)
        ln_ref[1:2, :] += _colsum(dyv)
        dr_ref[...] = dr.astype(BF16)
        dact = _dot(dr, wdt_ref[...])
        hg = hgu_ref[:, 0:ff]
        hu = hgu_ref[:, ff:2 * ff]
        sg = _sigmoid(hg)
        dhg = dact * hu * (sg * (1.0 + hg * (1.0 - sg)))
        dhu = dact * hg * sg
        dhgu_ref[:, 0:ff] = dhg.astype(BF16)
        dhgu_ref[:, ff:2 * ff] = dhu.astype(BF16)
        dx_ref[...] = ALPHA * dr + _dot(dhg, wgt_ref[...]) + _dot(dhu, wut_ref[...])

    tile = pl.BlockSpec((ts, d), lambda t: (t, 0))
    wide = pl.BlockSpec((ts, 2 * ff), lambda t: (t, 0))
    consts = (wdt, wgt, wut, vec)
    return pl.pallas_call(
        body, name=name, grid=(s // ts,),
        in_specs=[tile, tile, wide] + [_const_spec(c.shape) for c in consts],
        out_specs=[tile, tile, wide, _const_spec((2, d))],
        out_shape=[jax.ShapeDtypeStruct((s, d), F32), jax.ShapeDtypeStruct((s, d), BF16),
                   jax.ShapeDtypeStruct((s, 2 * ff), BF16), jax.ShapeDtypeStruct((2, d), F32)],
        compiler_params=_cparams(("arbitrary",)),
    )(dy, rpre, hgu, *consts)


def _loss_head(y, target, name):
    s, d = y.shape
    ts = _tile(s, TS_MM)

    def body(y_ref, t_ref, loss_ref, dy_ref):
        @pl.when(pl.program_id(0) == 0)
        def _():
            loss_ref[...] = jnp.zeros_like(loss_ref)

        err = y_ref[...] - t_ref[...]
        dy_ref[...] = err / d
        per_token = jnp.mean(err * err, axis=-1, keepdims=True)
        loss_ref[...] += 0.5 * jnp.sum(per_token, axis=0, keepdims=True)

    tile = pl.BlockSpec((ts, d), lambda t: (t, 0))
    return pl.pallas_call(
        body, name=name, grid=(s // ts,),
        in_specs=[tile, tile],
        out_specs=[_const_spec((1, 1)), tile],
        out_shape=[jax.ShapeDtypeStruct((1, 1), F32), jax.ShapeDtypeStruct((s, d), F32)],
        compiler_params=_cparams(("arbitrary",)),
    )(y, target)


SHARDED = (("w_in", 2), ("pool_w", 2), ("lru_w_out", 1), ("sconv_w_out", 1), ("w_mix_out", 1),
           ("xa_w_q", 1), ("xa_w_k", 1), ("xa_w_v", 1), ("xa_w_o", 1),
           ("ffn_w_gate", 2), ("ffn_w_up", 2), ("ffn_w_down", 1))
SHARDED_SMALL = (("lru_conv_w", 2), ("sconv_w", 2), ("ln_g", 2), ("ln_b", 2))
REPLICATED = ("b_in", "pool_scale", "lru_conv_b", "lru_w_r", "lru_b_r", "lru_w_i", "lru_b_i", "lru_lambda")
WEIGHTS = ("w_in", "b_in", "pool_w", "pool_scale", "lru_conv_w", "lru_conv_b", "lru_w_r", "lru_b_r", "lru_w_i",
           "lru_b_i", "lru_lambda", "lru_w_out", "sconv_w", "sconv_w_out", "w_mix_out", "xa_w_q", "xa_w_k",
           "xa_w_v", "xa_w_o", "ffn_w_gate", "ffn_w_up", "ffn_w_down", "ln_g", "ln_b")


def _pack(arrs, width, lead=0, row_multiple=16):
    head = arrs[0].shape[:lead]
    flat = jnp.concatenate([a.reshape(head + (-1,)) for a in arrs], axis=lead)
    n = flat.shape[-1]
    chunk = width * row_multiple
    total = -(-n // chunk) * chunk
    if total != n:
        flat = jnp.pad(flat, [(0, 0)] * lead + [(0, total - n)])
    return flat.reshape(head + (total // width, width))


def _unpack(buf, shapes, lead=0):
    head = buf.shape[:lead]
    flat = buf.reshape(head + (-1,))
    out, off = [], 0
    for shp in shapes:
        n = math.prod(shp)
        out.append(flat[..., off:off + n].reshape(head + tuple(shp)))
        off += n
    return out


def _split8(a, axis):
    shp = a.shape
    a = a.reshape(shp[:axis] + (N_DEV, shp[axis] // N_DEV) + shp[axis + 1:])
    return jnp.moveaxis(a, axis, 0)


def _join8(a, axis):
    a = jnp.moveaxis(a, 0, axis)
    shp = a.shape
    return a.reshape(shp[:axis] + (shp[axis] * shp[axis + 1],) + shp[axis + 2:])


def _t(a):
    return jnp.swapaxes(a, -1, -2)


def kernel(x, mem, w_in, b_in, pool_w, pool_scale, lru_conv_w, lru_conv_b, lru_w_r, lru_b_r, lru_w_i, lru_b_i, lru_lambda, lru_w_out, sconv_w, sconv_w_out, w_mix_out, xa_w_q, xa_w_k, xa_w_v, xa_w_o, ffn_w_gate, ffn_w_up, ffn_w_down, ln_g, ln_b, loss_target, m_w_in, m_b_in, m_pool_w, m_pool_scale, m_lru_conv_w, m_lru_conv_b, m_lru_w_r, m_lru_b_r, m_lru_w_i, m_lru_b_i, m_lru_lambda, m_lru_w_out, m_sconv_w, m_sconv_w_out, m_w_mix_out, m_xa_w_q, m_xa_w_k, m_xa_w_v, m_xa_w_o, m_ffn_w_gate, m_ffn_w_up, m_ffn_w_down, m_ln_g, m_ln_b, v_w_in, v_b_in, v_pool_w, v_pool_scale, v_lru_conv_w, v_lru_conv_b, v_lru_w_r, v_lru_b_r, v_lru_w_i, v_lru_b_i, v_lru_lambda, v_lru_w_out, v_sconv_w, v_sconv_w_out, v_w_mix_out, v_xa_w_q, v_xa_w_k, v_xa_w_v, v_xa_w_o, v_ffn_w_gate, v_ffn_w_up, v_ffn_w_down, v_ln_g, v_ln_b):
    args = dict(locals())
    w = {n: args[n] for n in WEIGHTS}
    mom_m = {n: args["m_" + n] for n in WEIGHTS}
    mom_v = {n: args["v_" + n] for n in WEIGHTS}
    depth = w_in.shape[0]
    s, d = x.shape[1], x.shape[2]
    nm = mem.shape[1]
    ff = ffn_w_gate.shape[2] * N_DEV
    dg = d // len(POOL_WINDOWS)
    xs = x.reshape(s, d)
    mems = mem.reshape(nm, d)
    target = loss_target.reshape(s, d)

    big_names = [n for n, _ in SHARDED]
    small_names = [n for n, _ in SHARDED_SMALL]
    axis_of = dict(SHARDED + SHARDED_SMALL)
    gathered = _all_gather(_pack([w[n].astype(BF16) for n in big_names], d), "gather_weights")
    gathered_small = _all_gather(_pack([w[n] for n in small_names], d, row_multiple=SUBLANES), "gather_vectors")
    full = {}
    for names, buf in ((big_names, gathered), (small_names, gathered_small)):
        for n, piece in zip(names, _unpack(buf, [w[n].shape for n in names], lead=1)):
            full[n] = piece
    blocks_in = full["w_in"]
    fw = {n: _join8(full[n], axis_of[n]) for n in big_names + small_names if n != "w_in"}

    layers = []
    for l in range(depth):
        vec = jnp.zeros((V_ROWS, d), F32)
        vec = vec.at[V_PSCALE].set(pool_scale[l]).at[V_CW:V_CW + LRU_CONV].set(fw["lru_conv_w"][l])
        vec = vec.at[V_CB].set(lru_conv_b[l]).at[V_BR].set(lru_b_r[l]).at[V_BI].set(lru_b_i[l])
        vec = vec.at[V_LAM].set(lru_lambda[l]).at[V_SW:V_SW + SCONV_K].set(fw["sconv_w"][l])
        vec = vec.at[V_G:V_G + 3].set(fw["ln_g"][l]).at[V_B:V_B + 3].set(fw["ln_b"][l])
        win_b = blocks_in[:, l]
        layers.append(dict(
            vec=vec, win=win_b, wint=_t(win_b).reshape(1, 8 * d, d), bin=b_in[l].reshape(1, 8 * d),
            pw=fw["pool_w"][l], pwt=_t(fw["pool_w"][l]),
            wr=lru_w_r[l].astype(BF16), wi=lru_w_i[l].astype(BF16),
            wrt=_t(lru_w_r[l]).astype(BF16), wit=_t(lru_w_i[l]).astype(BF16),
            wlo=fw["lru_w_out"][l], wlot=_t(fw["lru_w_out"][l]),
            wsc=fw["sconv_w_out"][l], wsct=_t(fw["sconv_w_out"][l]),
            wmix=fw["w_mix_out"][l], wmixt=_t(fw["w_mix_out"][l]),
            wq=fw["xa_w_q"][l], wqt=_t(fw["xa_w_q"][l]), wo=fw["xa_w_o"][l], wot=_t(fw["xa_w_o"][l]),
            wkv=jnp.stack([fw["xa_w_k"][l], fw["xa_w_v"][l]]),
            wgu=jnp.stack([fw["ffn_w_gate"][l], fw["ffn_w_up"][l]]),
            wgt=_t(fw["ffn_w_gate"][l]), wut=_t(fw["ffn_w_up"][l]),
            wd=fw["ffn_w_down"][l], wdt=_t(fw["ffn_w_down"][l])))

    saved = []
    cur = xs
    for l, p in enumerate(layers):
        z = _mm(cur, p["win"], f"z_in_{l}", bias=p["bin"])
        x1, rpre1, h, yp, yl, yc, merged, e = _mixer_fwd(
            cur, z, p["pw"], p["wr"], p["wi"], p["wlo"], p["wsc"], p["wmix"], p["vec"], f"mixer_fwd_{l}")
        kv = _mm(mems, p["wkv"], f"kv_{l}")
        kk = kv[:, :d].astype(BF16)
        vv = kv[:, d:].astype(BF16)
        x2, rpre2, q, o = _attn_fwd(x1, p["wq"], p["wo"], _t(kk), vv, p["vec"], f"attn_fwd_{l}")
        hgu = _mm(x2, p["wgu"], f"ffn_in_{l}")
        x3, rpre3, act = _ffn_out(x2, hgu, p["wd"], p["vec"], f"ffn_out_{l}")
        saved.append(dict(x0=cur, z=z, x1=x1, rpre1=rpre1, h=h, yp=yp, yl=yl, yc=yc, merged=merged, e=e,
                          kk=kk, vv=vv, x2=x2, rpre2=rpre2, q=q, o=o, hgu=hgu, rpre3=rpre3, act=act))
        cur = x3

    loss_part, dcur = _loss_head(cur, target, "loss_head")
    loss = lax.psum(loss_part[0, 0], ("x", "y", "c"))

    grads = [None] * depth
    for l in reversed(range(depth)):
        p, sv = layers[l], saved[l]
        g = {}
        dx2, dr3, dhgu, ln3 = _ffn_bwd(dcur, sv["rpre3"], sv["hgu"], p["wdt"], p["wgt"], p["wut"], p["vec"],
                                       f"ffn_bwd_{l}")
        g["ffn_w_down"] = _mm_tn(sv["act"], dr3, d, f"g_wd_{l}")[0]
        dwgu = _mm_tn(sv["x2"], dhgu, ff, f"g_wgu_{l}")
        g["ffn_w_gate"], g["ffn_w_up"] = dwgu[0], dwgu[1]
        dx1, dq, dr2, dk, dv, ln2 = _attn_bwd(dx2, sv["rpre2"], sv["q"], p["wqt"], p["wot"], sv["kk"], _t(sv["kk"]),
                                              _t(sv["vv"]), p["vec"], f"attn_bwd_{l}")
        g["xa_w_o"] = _mm_tn(sv["o"], dr2, d, f"g_wo_{l}")[0]
        g["xa_w_q"] = _mm_tn(sv["x1"], dq, d, f"g_wq_{l}")[0]
        dwkv = _mm_tn(mems, jnp.concatenate([dk, dv], axis=1), d, f"g_wkv_{l}")
        g["xa_w_k"], g["xa_w_v"] = dwkv[0], dwkv[1]
        (dz, dr1, dyl, dyc, pp, dyps, vb, dprer, dprei, accs, dbin) = _mixer_bwd(
            dx1, sv["rpre1"], sv["z"], sv["h"], sv["yp"], sv["yl"], sv["yc"], p["pw"], p["pwt"], p["wr"], p["wi"],
            p["wrt"], p["wit"], p["wlot"], p["wsct"], p["wmixt"], p["vec"], f"mixer_bwd_{l}")
        g["w_mix_out"] = _mm_tn(sv["merged"], dr1, d, f"g_wmix_{l}")[0]
        g["lru_w_out"] = _mm_tn(sv["h"], dyl, d, f"g_wlo_{l}")[0]
        g["sconv_w_out"] = _mm_tn(sv["e"], dyc, d, f"g_wsc_{l}")[0]
        hd = d // LRU_HEADS
        g["pool_w"] = jnp.stack([_mm_tn(pp[:, k * dg:(k + 1) * dg], dyps[:, k * dg:(k + 1) * dg], dg,
                                        f"g_pw_{l}_{k}")[0] for k in range(len(POOL_WINDOWS))])
        g["lru_w_r"] = jnp.stack([_mm_tn(vb[:, k * hd:(k + 1) * hd], dprer[:, k * hd:(k + 1) * hd], hd,
                                         f"g_wr_{l}_{k}")[0] for k in range(LRU_HEADS)])
        g["lru_w_i"] = jnp.stack([_mm_tn(vb[:, k * hd:(k + 1) * hd], dprei[:, k * hd:(k + 1) * hd], hd,
                                         f"g_wi_{l}_{k}")[0] for k in range(LRU_HEADS)])
        g["w_in_blocks"] = _mm_tn(sv["x0"], dz, d, f"g_win_{l}")
        g["b_in"] = dbin[0]
        g["pool_scale"] = accs[A_PSCALE]
        g["lru_conv_w"] = accs[A_CW:A_CW + LRU_CONV]
        g["lru_conv_b"] = accs[A_CB]
        g["lru_b_r"] = accs[A_BR]
        g["lru_b_i"] = accs[A_BI]
        g["lru_lambda"] = accs[A_SP] * (-_sigmoid(-lru_lambda[l]))
        g["sconv_w"] = accs[A_SW:A_SW + SCONV_K]
        g["ln_g"] = jnp.stack([accs[A_G], ln2[0], ln3[0]])
        g["ln_b"] = jnp.stack([accs[A_B], ln2[1], ln3[1]])
        grads[l] = g
        dcur = _mm(dz, p["wint"], f"dx_{l}", add=dr1, add_scale=ALPHA)
    grad_x = dcur.reshape(x.shape)

    sharded = SHARDED + SHARDED_SMALL
    slots = []
    for n, axis in sharded:
        if n == "w_in":
            slots.append(jnp.stack([grads[l]["w_in_blocks"] for l in range(depth)], axis=1))
        else:
            slots.append(_split8(jnp.stack([grads[l][n] for l in range(depth)]), axis))
    shard_names = [n for n, _ in sharded]
    shard_shapes = [w[n].shape for n in shard_names]
    received = _all_to_all(_pack(slots, d, lead=1, row_multiple=TR_ADAM), "scatter_grads")
    outs = _adamw_sum(received, *[_pack([t[n] for n in shard_names], d, row_multiple=TR_ADAM)
                                  for t in (w, mom_m, mom_v)], "adamw_sharded")
    res = {n: parts for n, *parts in zip(shard_names, *[_unpack(o, shard_shapes) for o in outs])}

    rep_shapes = [w[n].shape for n in REPLICATED]
    partial = _pack([jnp.stack([grads[l][n] for l in range(depth)]) for n in REPLICATED], d, row_multiple=TR_ADAM)
    outs = _adamw_sum(_all_gather(partial, "gather_small_grads"),
                      *[_pack([t[n] for n in REPLICATED], d, row_multiple=TR_ADAM) for t in (w, mom_m, mom_v)],
                      "adamw_replicated")
    res.update({n: parts for n, *parts in zip(REPLICATED, *[_unpack(o, rep_shapes) for o in outs])})

    return (loss, grad_x, *[res[n][0] for n in WEIGHTS], *[res[n][1] for n in WEIGHTS],
            *[res[n][2] for n in WEIGHTS], *[res[n][3] for n in WEIGHTS])
```

```python
import functools
import math

import jax
import jax.numpy as jnp
from jax import lax
from jax.experimental import pallas as pl
from jax.experimental.pallas import tpu as pltpu

F32 = jnp.float32
BF16 = jnp.bfloat16
MESH = pl.DeviceIdType.MESH

N_DEV = 8
LRU_HEADS = 8
LRU_CONV = 4
LRU_C = 8.0
SCONV_K = 3
POOL_WINDOWS = (2, 4, 8, 16)
X_HEADS = 4
DEPTH = 2
ALPHA = (2 * DEPTH) ** 0.25
LN_EPS = 1e-5
ADAM_LR = 0.001
ADAM_B1 = 0.9
ADAM_B2 = 0.999
ADAM_EPS = 1e-08
ADAM_WD = 0.01
ADAM_STEP = 10

HALO = 16
SUBLANES = 8
VMEM_LIMIT = 56 * 1024 * 1024
TS_MIXER = 128
TS_ATTN = 512
TS_FFN = 256
TS_MM = 512
TK_MM = 2048
TR_ADAM = 256

V_PSCALE, V_CW, V_CB, V_BR, V_BI, V_LAM, V_SW, V_G, V_B = 0, 1, 5, 6, 7, 8, 9, 12, 15
V_ROWS = 24
A_PSCALE, A_CW, A_CB, A_BR, A_BI, A_SP, A_SW, A_G, A_B = 0, 1, 5, 6, 7, 8, 9, 12, 13
A_ROWS = 16


def _cparams(sem):
    return pltpu.CompilerParams(dimension_semantics=sem, vmem_limit_bytes=VMEM_LIMIT)


def _tile(n, pref):
    if n <= pref:
        return n
    assert n % pref == 0, (n, pref)
    return pref


def _const_spec(shape):
    nd = len(shape)
    return pl.BlockSpec(shape, lambda *_: (0,) * nd)


def _dot(a, b):
    return jnp.dot(a.astype(BF16), b.astype(BF16), preferred_element_type=F32)


def _dot_tn(a, b):
    return lax.dot_general(a.astype(BF16), b.astype(BF16), (((0,), (0,)), ((), ())),
                           preferred_element_type=F32)


def _sigmoid(x):
    return 1.0 / (1.0 + jnp.exp(-x))


def _expm1(x):
    small = x * (1.0 + x * (0.5 + x * (1.0 / 6.0 + x * (1.0 / 24.0 + x * (1.0 / 120.0)))))
    return jnp.where(jnp.abs(x) < 0.05, small, jnp.exp(x) - 1.0)


def _softplus(y):
    e = jnp.exp(-jnp.abs(y))
    log1p = jnp.where(e < 1e-4, e * (1.0 - e * (0.5 - e * (1.0 / 3.0))), jnp.log(1.0 + e))
    return jnp.maximum(y, 0.0) + log1p


def _ln_fwd(r, g, b):
    mu = jnp.mean(r, axis=-1, keepdims=True)
    xc = r - mu
    var = jnp.mean(xc * xc, axis=-1, keepdims=True)
    return xc * lax.rsqrt(var + LN_EPS) * g + b


def _ln_bwd(dy, r, g):
    mu = jnp.mean(r, axis=-1, keepdims=True)
    xc = r - mu
    var = jnp.mean(xc * xc, axis=-1, keepdims=True)
    rstd = lax.rsqrt(var + LN_EPS)
    yhat = xc * rstd
    dyh = dy * g
    m1 = jnp.mean(dyh, axis=-1, keepdims=True)
    m2 = jnp.mean(dyh * yhat, axis=-1, keepdims=True)
    return rstd * (dyh - m1 - yhat * m2), dy * yhat


def _colsum(a):
    return jnp.sum(a, axis=0, keepdims=True)


def _all_gather(xs, name):
    rows, width = xs.shape

    def body(x_ref, out_ref, send_sems, recv_sems, local_sem):
        x, y, c = lax.axis_index("x"), lax.axis_index("y"), lax.axis_index("c")
        me, sibling = (x, y, c), (x, y, 1 - c)
        chips = [(1 - x, y), (x, 1 - y), (1 - x, 1 - y)]

        def slot(px, py, pc):
            return out_ref.at[4 * px + 2 * py + pc]

        def copy(k, block, to, src=None):
            return pltpu.make_async_remote_copy(
                src_ref=slot(*block) if src is None else src, dst_ref=slot(*block),
                send_sem=send_sems.at[k], recv_sem=recv_sems.at[k], device_id=to, device_id_type=MESH)

        mine = pltpu.make_async_copy(x_ref, slot(*me), local_sem)
        mine.start()
        first = [copy(0, me, sibling, src=x_ref)]
        first += [copy(1 + j, me, (*chip, c), src=x_ref) for j, chip in enumerate(chips)]
        for cp in first:
            cp.start()
        passed = [copy(4 + j, (*chip, c), sibling) for j, chip in enumerate(chips)]
        for j, chip in enumerate(chips):
            copy(1 + j, (*chip, c), me).wait_recv()
            passed[j].start()
        copy(0, sibling, me).wait_recv()
        for j, chip in enumerate(chips):
            copy(4 + j, (*chip, 1 - c), me).wait_recv()
        for cp in first + passed:
            cp.wait_send()
        mine.wait()

    return pl.pallas_call(
        body, name=name,
        out_shape=jax.ShapeDtypeStruct((N_DEV, rows, width), xs.dtype),
        in_specs=[pl.BlockSpec(memory_space=pl.ANY)],
        out_specs=pl.BlockSpec(memory_space=pl.ANY),
        scratch_shapes=[pltpu.SemaphoreType.DMA((7,)), pltpu.SemaphoreType.DMA((7,)), pltpu.SemaphoreType.DMA],
    )(xs)


def _all_to_all(g, name):
    _, rows, width = g.shape

    def body(g_ref, out_ref, send_sems, recv_sems, local_sem):
        x, y, c = lax.axis_index("x"), lax.axis_index("y"), lax.axis_index("c")
        me = 4 * x + 2 * y + c
        mine = pltpu.make_async_copy(g_ref.at[me], out_ref.at[me], local_sem)
        mine.start()
        copies = []
        for k in range(1, N_DEV):
            px = 1 - x if k & 4 else x
            py = 1 - y if k & 2 else y
            pc = 1 - c if k & 1 else c
            peer = 4 * px + 2 * py + pc
            copies.append(pltpu.make_async_remote_copy(
                src_ref=g_ref.at[peer], dst_ref=out_ref.at[me],
                send_sem=send_sems.at[k - 1], recv_sem=recv_sems.at[k - 1],
                device_id=(px, py, pc), device_id_type=MESH))
        for cp in copies:
            cp.start()
        for cp in copies:
            cp.wait_recv()
        for cp in copies:
            cp.wait_send()
        mine.wait()

    return pl.pallas_call(
        body, name=name,
        out_shape=jax.ShapeDtypeStruct(g.shape, g.dtype),
        in_specs=[pl.BlockSpec(memory_space=pl.ANY)],
        out_specs=pl.BlockSpec(memory_space=pl.ANY),
        scratch_shapes=[pltpu.SemaphoreType.DMA((7,)), pltpu.SemaphoreType.DMA((7,)), pltpu.SemaphoreType.DMA],
    )(g)


def _adamw_sum(parts, w, m, v, name):
    _, rows, width = parts.shape
    tr = _tile(rows, TR_ADAM)
    c1 = 1.0 - ADAM_B1 ** ADAM_STEP
    c2 = 1.0 - ADAM_B2 ** ADAM_STEP

    def body(p_ref, w_ref, m_ref, v_ref, g_ref, d_ref, nm_ref, nv_ref):
        g = p_ref[0]
        for k in range(1, N_DEV):
            g = g + p_ref[k]
        nm = ADAM_B1 * m_ref[...] + (1.0 - ADAM_B1) * g
        nv = ADAM_B2 * v_ref[...] + (1.0 - ADAM_B2) * (g * g)
        m_hat = nm / c1
        v_hat = nv / c2
        g_ref[...] = g
        d_ref[...] = -ADAM_LR * (m_hat / (jnp.sqrt(v_hat) + ADAM_EPS) + ADAM_WD * w_ref[...])
        nm_ref[...] = nm
        nv_ref[...] = nv

    spec = pl.BlockSpec((tr, width), lambda i: (i, 0))
    out = jax.ShapeDtypeStruct((rows, width), F32)
    return pl.pallas_call(
        body, name=name, grid=(rows // tr,),
        in_specs=[pl.BlockSpec((N_DEV, tr, width), lambda i: (0, i, 0)), spec, spec, spec],
        out_specs=[spec, spec, spec, spec], out_shape=[out, out, out, out],
        compiler_params=_cparams(("parallel",)),
    )(parts, w, m, v)


def _mm(a, wb, name, bias=None, add=None, add_scale=1.0, out_dtype=F32):
    m, k = a.shape
    nb, k2, tn = wb.shape
    assert k == k2
    tm = _tile(m, TS_MM)
    tk = _tile(k, TK_MM)
    nk = k // tk

    def body(*refs):
        a_ref, w_ref = refs[0], refs[1]
        pos = 2
        b_ref = add_ref = None
        if bias is not None:
            b_ref = refs[pos]
            pos += 1
        if add is not None:
            add_ref = refs[pos]
            pos += 1
        o_ref, acc_ref = refs[pos], refs[pos + 1]
        kk = pl.program_id(2)

        @pl.when(kk == 0)
        def _():
            acc_ref[...] = jnp.zeros_like(acc_ref)

        acc_ref[...] += _dot(a_ref[...], w_ref[...])

        @pl.when(kk == nk - 1)
        def _():
            r = acc_ref[...]
            if b_ref is not None:
                r = r + b_ref[...]
            if add_ref is not None:
                r = r + add_scale * add_ref[...]
            o_ref[...] = r.astype(o_ref.dtype)

    in_specs = [pl.BlockSpec((tm, tk), lambda j, i, kk: (i, kk)),
                pl.BlockSpec((None, tk, tn), lambda j, i, kk: (j, kk, 0))]
    args = [a, wb]
    if bias is not None:
        in_specs.append(pl.BlockSpec((1, tn), lambda j, i, kk: (0, j)))
        args.append(bias)
    if add is not None:
        in_specs.append(pl.BlockSpec((tm, tn), lambda j, i, kk: (i, j)))
        args.append(add)
    return pl.pallas_call(
        body, name=name, grid=(nb, m // tm, nk),
        in_specs=in_specs,
        out_specs=pl.BlockSpec((tm, tn), lambda j, i, kk: (i, j)),
        out_shape=jax.ShapeDtypeStruct((m, nb * tn), out_dtype),
        scratch_shapes=[pltpu.VMEM((tm, tn), F32)],
        compiler_params=_cparams(("parallel", "parallel", "arbitrary")),
    )(*args)


def _mm_tn(a, b, tn, name, tk=None):
    s, k = a.shape
    s2, n = b.shape
    assert s == s2 and n % tn == 0
    nb = n // tn
    ts = _tile(s, TS_MM)
    tk = k if tk is None else tk
    assert k % tk == 0

    def body(a_ref, b_ref, o_ref):
        @pl.when(pl.program_id(2) == 0)
        def _():
            o_ref[...] = jnp.zeros_like(o_ref)

        o_ref[...] += _dot_tn(a_ref[...], b_ref[...])

    return pl.pallas_call(
        body, name=name, grid=(nb, k // tk, s // ts),
        in_specs=[pl.BlockSpec((ts, tk), lambda j, kb, i: (i, kb)),
                  pl.BlockSpec((ts, tn), lambda j, kb, i: (i, j))],
        out_specs=pl.BlockSpec((None, tk, tn), lambda j, kb, i: (j, kb, 0)),
        out_shape=jax.ShapeDtypeStruct((nb, k, tn), F32),
        compiler_params=_cparams(("parallel", "parallel", "arbitrary")),
    )(a, b)


def _scan_fwd(a_ref, b_ref, h_ref, carry_ref, ts):
    rowid = lax.broadcasted_iota(jnp.int32, (SUBLANES, 1), 0)

    def group(gi, hprev):
        r0 = pl.multiple_of(gi * SUBLANES, SUBLANES)
        a = a_ref[pl.ds(r0, SUBLANES), :]
        b = b_ref[pl.ds(r0, SUBLANES), :]
        for d in (1, 2, 4):
            a_sh = jnp.where(rowid >= d, pltpu.roll(a, d, 0), 1.0)
            b_sh = jnp.where(rowid >= d, pltpu.roll(b, d, 0), 0.0)
            b = a * b_sh + b
            a = a * a_sh
        hh = a * hprev + b
        h_ref[pl.ds(r0, SUBLANES), :] = hh
        return hh[SUBLANES - 1:SUBLANES, :]

    last = lax.fori_loop(0, ts // SUBLANES, group, carry_ref[0:1, :])
    carry_ref[0:1, :] = last


def _scan_rev(c_ref, b_ref, g_ref, carry_ref, ts):
    rowid = lax.broadcasted_iota(jnp.int32, (SUBLANES, 1), 0)
    ng = ts // SUBLANES

    def group(gi, gnext):
        r0 = pl.multiple_of((ng - 1 - gi) * SUBLANES, SUBLANES)
        c = c_ref[pl.ds(r0, SUBLANES), :]
        b = b_ref[pl.ds(r0, SUBLANES), :]
        for d in (1, 2, 4):
            keep = rowid < SUBLANES - d
            c_sh = jnp.where(keep, pltpu.roll(c, SUBLANES - d, 0), 1.0)
            b_sh = jnp.where(keep, pltpu.roll(b, SUBLANES - d, 0), 0.0)
            b = c * b_sh + b
            c = c * c_sh
        gg = c * gnext + b
        g_ref[pl.ds(r0, SUBLANES), :] = gg
        return gg[0:1, :]

    first = lax.fori_loop(0, ng, group, carry_ref[0:1, :])
    carry_ref[0:1, :] = first


def _heads_dot(v, w_ref, heads):
    hd = v.shape[1] // heads
    return jnp.concatenate([_dot(v[:, h * hd:(h + 1) * hd], w_ref[h]) for h in range(heads)], axis=1)


def _mixer_recompute(i, ts, d, z_ref, zh_ref, vec_ref, wr_ref, wi_ref, pext, lext, qext):
    first = i == 0

    def zc(k):
        return z_ref[:, k * d:(k + 1) * d]

    def zhalo(k):
        return jnp.where(first, 0.0, zh_ref[:, k * d:(k + 1) * d])

    tglob = i * ts + lax.broadcasted_iota(jnp.int32, (ts, 1), 0)
    pext[0:HALO, :] = zhalo(0)
    pext[HALO:HALO + ts, :] = zc(0)
    dg = d // len(POOL_WINDOWS)
    ps = []
    for g, win in enumerate(POOL_WINDOWS):
        cs = slice(g * dg, (g + 1) * dg)
        u = pext[HALO:HALO + ts, cs]
        s = u
        for j in range(1, win):
            s = s + pext[HALO - j:HALO - j + ts, cs]
        cnt = jnp.minimum(tglob + 1, win).astype(F32)
        ps.append(s / cnt - u)
    p = jnp.concatenate(ps, axis=1)
    lext[0:HALO, :] = zhalo(1)
    lext[HALO:HALO + ts, :] = zc(1)
    v = vec_ref[V_CB:V_CB + 1, :]
    for j in range(LRU_CONV):
        sh = LRU_CONV - 1 - j
        v = v + vec_ref[V_CW + j:V_CW + j + 1, :] * lext[HALO - sh:HALO - sh + ts, :]
    r = _sigmoid(_heads_dot(v, wr_ref, LRU_HEADS) + vec_ref[V_BR:V_BR + 1, :])
    ig = _sigmoid(_heads_dot(v, wi_ref, LRU_HEADS) + vec_ref[V_BI:V_BI + 1, :])
    sp = _softplus(-vec_ref[V_LAM:V_LAM + 1, :])
    log_a = -LRU_C * r * sp
    a = jnp.exp(log_a)
    mult = jnp.sqrt(-_expm1(2.0 * log_a))
    qext[0:HALO, :] = zhalo(3) * zhalo(4)
    qext[HALO:HALO + ts, :] = zc(3) * zc(4)
    cq = jnp.zeros((ts, d), F32)
    for j in range(SCONV_K):
        sh = SCONV_K - 1 - j
        cq = cq + vec_ref[V_SW + j:V_SW + j + 1, :] * qext[HALO - sh:HALO - sh + ts, :]
    return dict(p=p, v=v, r=r, ig=ig, sp=sp, a=a, mult=mult, cq=cq, tglob=tglob)


def _halo_index(ts):
    blocks = ts // HALO
    return lambda t: (jnp.maximum(t * blocks - 1, 0), 0)


def _mixer_fwd(x, z, pw, wr, wi, wlo, wsc, wmix, vec, name):
    s, d = x.shape
    ts = _tile(s, TS_MIXER)
    nt = s // ts
    dg = d // len(POOL_WINDOWS)

    def body(x_ref, z_ref, zh_ref, pw_ref, wr_ref, wi_ref, wlo_ref, wsc_ref, wmix_ref, vec_ref,
             x1_ref, rpre_ref, h_ref, yp_ref, yl_ref, yc_ref, mg_ref, e_ref,
             pext, lext, qext, a_scr, b_scr, hcarry):
        i = pl.program_id(0)

        @pl.when(i == 0)
        def _():
            hcarry[...] = jnp.zeros_like(hcarry)

        f = _mixer_recompute(i, ts, d, z_ref, zh_ref, vec_ref, wr_ref, wi_ref, pext, lext, qext)
        scale = vec_ref[V_PSCALE:V_PSCALE + 1, :]
        yp = jnp.concatenate([_dot(f["p"][:, g * dg:(g + 1) * dg], pw_ref[g]) for g in range(len(POOL_WINDOWS))],
                             axis=1) * scale
        a_scr[...] = f["a"]
        b_scr[...] = f["mult"] * (f["ig"] * f["v"])
        _scan_fwd(a_scr, b_scr, h_ref, hcarry, ts)
        yl = _dot(h_ref[...], wlo_ref[...])
        e = z_ref[:, 2 * d:3 * d] * f["cq"]
        yc = _dot(e, wsc_ref[...])
        merged = (_sigmoid(z_ref[:, 5 * d:6 * d]) * yp + _sigmoid(z_ref[:, 6 * d:7 * d]) * yl
                  + _sigmoid(z_ref[:, 7 * d:8 * d]) * yc)
        rpre = ALPHA * x_ref[...] + _dot(merged, wmix_ref[...])
        x1_ref[...] = _ln_fwd(rpre, vec_ref[V_G:V_G + 1, :], vec_ref[V_B:V_B + 1, :])
        rpre_ref[...] = rpre
        yp_ref[...] = yp
        yl_ref[...] = yl
        yc_ref[...] = yc
        mg_ref[...] = merged.astype(BF16)
        e_ref[...] = e.astype(BF16)

    tile = pl.BlockSpec((ts, d), lambda t: (t, 0))
    f32o = jax.ShapeDtypeStruct((s, d), F32)
    bfo = jax.ShapeDtypeStruct((s, d), BF16)
    return pl.pallas_call(
        body, name=name, grid=(nt,),
        in_specs=[tile, pl.BlockSpec((ts, 8 * d), lambda t: (t, 0)), pl.BlockSpec((HALO, 8 * d), _halo_index(ts)),
                  _const_spec(pw.shape), _const_spec(wr.shape), _const_spec(wi.shape), _const_spec(wlo.shape),
                  _const_spec(wsc.shape), _const_spec(wmix.shape), _const_spec(vec.shape)],
        out_specs=[tile] * 8,
        out_shape=[f32o, f32o, f32o, f32o, f32o, f32o, bfo, bfo],
        scratch_shapes=[pltpu.VMEM((HALO + ts, d), F32)] * 3 + [pltpu.VMEM((ts, d), F32)] * 2
        + [pltpu.VMEM((SUBLANES, d), F32)],
        compiler_params=_cparams(("arbitrary",)),
    )(x, z, z, pw, wr, wi, wlo, wsc, wmix, vec)


def _mixer_bwd(dx1, rpre, z, h, yp, yl, yc, pw, pwt, wr, wi, wrt, wit, wlot, wsct, wmixt, vec, name):
    s, d = dx1.shape
    ts = _tile(s, TS_MIXER)
    nt = s // ts
    dg = d // len(POOL_WINDOWS)
    ng = len(POOL_WINDOWS)

    def body(dx1_ref, rpre_ref, z_ref, zh_ref, h_ref, hh_ref, yp_ref, yl_ref, yc_ref,
             pw_ref, pwt_ref, wr_ref, wi_ref, wrt_ref, wit_ref, wlot_ref, wsct_ref, wmixt_ref, vec_ref,
             dz_ref, dr_ref, dyl_ref, dyc_ref, acc_ref, dbin_ref, dpw_ref, dwr_ref, dwi_ref,
             pext, lext, qext, fext, c_scr, b_scr, g_scr, gcarry, acarry, dcq_c, dv_c, m_c):
        i = pl.program_id(0)
        t = nt - 1 - i
        hd = d // LRU_HEADS

        @pl.when(i == 0)
        def _():
            for ref in (gcarry, acarry, dcq_c, dv_c, m_c, acc_ref, dbin_ref, dpw_ref, dwr_ref, dwi_ref):
                ref[...] = jnp.zeros_like(ref)

        f = _mixer_recompute(t, ts, d, z_ref, zh_ref, vec_ref, wr_ref, wi_ref, pext, lext, qext)

        def vrow(k):
            return vec_ref[k:k + 1, :]

        def zc(k):
            return z_ref[:, k * d:(k + 1) * d]

        def acc(row, val):
            acc_ref[row:row + 1, :] += _colsum(val)

        def future(tile_val, carry_ref):
            fext[0:ts, :] = tile_val
            fext[ts:ts + HALO, :] = carry_ref[...]
            carry_ref[...] = tile_val[0:HALO, :]

        dx1v = dx1_ref[...]
        dr, dyy = _ln_bwd(dx1v, rpre_ref[...], vrow(V_G))
        acc(A_G, dyy)
        acc(A_B, dx1v)
        dr_ref[...] = dr
        dmg = _dot(dr, wmixt_ref[...])
        dzs = [None] * 8
        gates = []
        for k, y_ref in enumerate((yp_ref, yl_ref, yc_ref)):
            gk = _sigmoid(zc(5 + k))
            dzs[5 + k] = dmg * y_ref[...] * gk * (1.0 - gk)
            gates.append(gk)
        dyp = dmg * gates[0]
        dyl = dmg * gates[1]
        dyc = dmg * gates[2]
        dyl_ref[...] = dyl.astype(BF16)
        dyc_ref[...] = dyc.astype(BF16)

        de = _dot(dyc, wsct_ref[...])
        dzs[2] = de * f["cq"]
        dcq = de * zc(2)
        for j in range(SCONV_K):
            sh = SCONV_K - 1 - j
            acc(A_SW + j, dcq * qext[HALO - sh:HALO - sh + ts, :])
        future(dcq, dcq_c)
        dq = jnp.zeros((ts, d), F32)
        for j in range(SCONV_K):
            sh = SCONV_K - 1 - j
            dq = dq + vrow(V_SW + j) * fext[sh:sh + ts, :]
        dzs[3] = dq * zc(4)
        dzs[4] = dq * zc(3)

        a, mult, r, ig, v, sp = f["a"], f["mult"], f["r"], f["ig"], f["v"], f["sp"]
        fext[0:ts, :] = a
        fext[ts:ts + SUBLANES, :] = acarry[...]
        acarry[...] = a[0:SUBLANES, :]
        c_scr[...] = fext[1:1 + ts, :]
        b_scr[...] = _dot(dyl, wlot_ref[...])
        _scan_rev(c_scr, b_scr, g_scr, gcarry, ts)
        gs = g_scr[...]
        fext[0:HALO, :] = jnp.where(t == 0, 0.0, hh_ref[...])
        fext[HALO:HALO + ts, :] = h_ref[...]
        hprev = fext[HALO - 1:HALO - 1 + ts, :]
        iv = ig * v
        dlog_a = gs * hprev * a + gs * iv * (-(a * a) / mult)
        div = gs * mult
        acc(A_SP, dlog_a * (-LRU_C) * r)
        dpre_r = dlog_a * (-LRU_C) * sp * r * (1.0 - r)
        dpre_i = div * v * ig * (1.0 - ig)
        acc(A_BR, dpre_r)
        acc(A_BI, dpre_i)
        dv = div * ig + _heads_dot(dpre_r, wrt_ref, LRU_HEADS) + _heads_dot(dpre_i, wit_ref, LRU_HEADS)
        for hh in range(LRU_HEADS):
            hs = slice(hh * hd, (hh + 1) * hd)
            dwr_ref[hh] += _dot_tn(v[:, hs], dpre_r[:, hs])
            dwi_ref[hh] += _dot_tn(v[:, hs], dpre_i[:, hs])
        acc(A_CB, dv)
        for j in range(LRU_CONV):
            sh = LRU_CONV - 1 - j
            acc(A_CW + j, dv * lext[HALO - sh:HALO - sh + ts, :])
        future(dv, dv_c)
        dzl = jnp.zeros((ts, d), F32)
        for j in range(LRU_CONV):
            sh = LRU_CONV - 1 - j
            dzl = dzl + vrow(V_CW + j) * fext[sh:sh + ts, :]
        dzs[1] = dzl

        p = f["p"]
        ypre = jnp.concatenate([_dot(p[:, g * dg:(g + 1) * dg], pw_ref[g]) for g in range(ng)], axis=1)
        acc(A_PSCALE, dyp * ypre)
        dyps = dyp * vrow(V_PSCALE)
        for g in range(ng):
            dpw_ref[g] += _dot_tn(p[:, g * dg:(g + 1) * dg], dyps[:, g * dg:(g + 1) * dg])
        dp = jnp.concatenate([_dot(dyps[:, g * dg:(g + 1) * dg], pwt_ref[g]) for g in range(ng)], axis=1)
        cnts = [jnp.minimum(f["tglob"] + 1, win).astype(F32) for win in POOL_WINDOWS]
        mm = jnp.concatenate([dp[:, g * dg:(g + 1) * dg] / cnts[g] for g in range(ng)], axis=1)
        future(mm, m_c)
        dzps = []
        for g, win in enumerate(POOL_WINDOWS):
            cs = slice(g * dg, (g + 1) * dg)
            sm = fext[0:ts, cs]
            for j in range(1, win):
                sm = sm + fext[j:j + ts, cs]
            dzps.append(sm - dp[:, cs])
        dzs[0] = jnp.concatenate(dzps, axis=1)

        for k in range(8):
            dz_ref[:, k * d:(k + 1) * d] = dzs[k].astype(BF16)
            dbin_ref[:, k * d:(k + 1) * d] += _colsum(dzs[k])

    def rev(tt):
        return (nt - 1 - tt, 0)

    halo = _halo_index(ts)
    tile = pl.BlockSpec((ts, d), rev)
    hspec = pl.BlockSpec((HALO, d), lambda tt: halo(nt - 1 - tt))
    f32o = jax.ShapeDtypeStruct((s, d), F32)
    bfo = jax.ShapeDtypeStruct((s, d), BF16)
    consts = (pw, pwt, wr, wi, wrt, wit, wlot, wsct, wmixt, vec)
    return pl.pallas_call(
        body, name=name, grid=(nt,),
        in_specs=[tile, tile, pl.BlockSpec((ts, 8 * d), rev),
                  pl.BlockSpec((HALO, 8 * d), lambda tt: halo(nt - 1 - tt)), tile, hspec, tile, tile, tile]
        + [_const_spec(c.shape) for c in consts],
        out_specs=[pl.BlockSpec((ts, 8 * d), rev), tile, tile, tile,
                   _const_spec((A_ROWS, d)), _const_spec((1, 8 * d)),
                   _const_spec(pw.shape), _const_spec(wr.shape), _const_spec(wi.shape)],
        out_shape=[jax.ShapeDtypeStruct((s, 8 * d), BF16), f32o, bfo, bfo,
                   jax.ShapeDtypeStruct((A_ROWS, d), F32), jax.ShapeDtypeStruct((1, 8 * d), F32),
                   jax.ShapeDtypeStruct(pw.shape, F32), jax.ShapeDtypeStruct(wr.shape, F32),
                   jax.ShapeDtypeStruct(wi.shape, F32)],
        scratch_shapes=[pltpu.VMEM((HALO + ts, d), F32)] * 4 + [pltpu.VMEM((ts, d), F32)] * 3
        + [pltpu.VMEM((SUBLANES, d), F32)] * 2 + [pltpu.VMEM((HALO, d), F32)] * 3,
        compiler_params=_cparams(("arbitrary",)),
    )(dx1, rpre, z, z, h, h, yp, yl, yc, *consts)


def _softmax_rows(sc):
    mx = jnp.max(sc, axis=-1, keepdims=True)
    ex = jnp.exp(sc - mx)
    return ex / jnp.sum(ex, axis=-1, keepdims=True)


def _attn_fwd(x1, wq, wo, kt, vv, vec, name):
    s, d = x1.shape
    ts = _tile(s, TS_ATTN)
    hd = d // X_HEADS
    scale = hd ** -0.5

    def body(x_ref, wq_ref, wo_ref, kt_ref, v_ref, vec_ref, x2_ref, rpre_ref, q_ref, o_ref):
        xv = x_ref[...]
        q = _dot(xv, wq_ref[...]).astype(BF16)
        q_ref[...] = q
        for hh in range(X_HEADS):
            cs = slice(hh * hd, (hh + 1) * hd)
            p = _softmax_rows(_dot(q[:, cs], kt_ref[cs, :]) * scale)
            o_ref[:, cs] = _dot(p, v_ref[:, cs]).astype(BF16)
        rpre = ALPHA * xv + _dot(o_ref[...], wo_ref[...])
        rpre_ref[...] = rpre
        x2_ref[...] = _ln_fwd(rpre, vec_ref[V_G + 1:V_G + 2, :], vec_ref[V_B + 1:V_B + 2, :])

    tile = pl.BlockSpec((ts, d), lambda t: (t, 0))
    f32o = jax.ShapeDtypeStruct((s, d), F32)
    bfo = jax.ShapeDtypeStruct((s, d), BF16)
    consts = (wq, wo, kt, vv, vec)
    return pl.pallas_call(
        body, name=name, grid=(s // ts,),
        in_specs=[tile] + [_const_spec(c.shape) for c in consts],
        out_specs=[tile] * 4, out_shape=[f32o, f32o, bfo, bfo],
        compiler_params=_cparams(("parallel",)),
    )(x1, *consts)


def _attn_bwd(dx2, rpre, q, wqt, wot, kk, kt, vt, vec, name):
    s, d = dx2.shape
    ts = _tile(s, TS_ATTN)
    nm = kk.shape[0]
    hd = d // X_HEADS
    scale = hd ** -0.5

    def body(dx2_ref, rpre_ref, q_ref, wqt_ref, wot_ref, k_ref, kt_ref, vt_ref, vec_ref,
             dx1_ref, dq_ref, dr_ref, dk_ref, dv_ref, ln_ref):
        @pl.when(pl.program_id(0) == 0)
        def _():
            for ref in (dk_ref, dv_ref, ln_ref):
                ref[...] = jnp.zeros_like(ref)

        dyv = dx2_ref[...]
        dr, dyy = _ln_bwd(dyv, rpre_ref[...], vec_ref[V_G + 1:V_G + 2, :])
        ln_ref[0:1, :] += _colsum(dyy)
        ln_ref[1:2, :] += _colsum(dyv)
        dr_ref[...] = dr.astype(BF16)
        do = _dot(dr, wot_ref[...])
        q = q_ref[...]
        for hh in range(X_HEADS):
            cs = slice(hh * hd, (hh + 1) * hd)
            p = _softmax_rows(_dot(q[:, cs], kt_ref[cs, :]) * scale)
            dp = _dot(do[:, cs], vt_ref[cs, :])
            ds = p * (dp - jnp.sum(dp * p, axis=-1, keepdims=True)) * scale
            dq_ref[:, cs] = _dot(ds, k_ref[:, cs]).astype(BF16)
            dk_ref[:, cs] += _dot_tn(ds, q[:, cs])
            dv_ref[:, cs] += _dot_tn(p, do[:, cs])
        dx1_ref[...] = ALPHA * dr + _dot(dq_ref[...], wqt_ref[...])

    tile = pl.BlockSpec((ts, d), lambda t: (t, 0))
    consts = (wqt, wot, kk, kt, vt, vec)
    return pl.pallas_call(
        body, name=name, grid=(s // ts,),
        in_specs=[tile, tile, tile] + [_const_spec(c.shape) for c in consts],
        out_specs=[tile, tile, tile, _const_spec((nm, d)), _const_spec((nm, d)), _const_spec((2, d))],
        out_shape=[jax.ShapeDtypeStruct((s, d), F32), jax.ShapeDtypeStruct((s, d), BF16),
                   jax.ShapeDtypeStruct((s, d), BF16), jax.ShapeDtypeStruct((nm, d), F32),
                   jax.ShapeDtypeStruct((nm, d), F32), jax.ShapeDtypeStruct((2, d), F32)],
        compiler_params=_cparams(("arbitrary",)),
    )(dx2, rpre, q, *consts)


def _ffn_out(x2, hgu, wd, vec, name):
    s, d = x2.shape
    ff = wd.shape[0]
    ts = _tile(s, TS_FFN)

    def body(x_ref, hgu_ref, wd_ref, vec_ref, x3_ref, rpre_ref, act_ref):
        hg = hgu_ref[:, 0:ff]
        act = hg * _sigmoid(hg) * hgu_ref[:, ff:2 * ff]
        act_ref[...] = act.astype(BF16)
        rpre = ALPHA * x_ref[...] + _dot(act, wd_ref[...])
        rpre_ref[...] = rpre
        x3_ref[...] = _ln_fwd(rpre, vec_ref[V_G + 2:V_G + 3, :], vec_ref[V_B + 2:V_B + 3, :])

    tile = pl.BlockSpec((ts, d), lambda t: (t, 0))
    return pl.pallas_call(
        body, name=name, grid=(s // ts,),
        in_specs=[tile, pl.BlockSpec((ts, 2 * ff), lambda t: (t, 0)), _const_spec(wd.shape), _const_spec(vec.shape)],
        out_specs=[tile, tile, pl.BlockSpec((ts, ff), lambda t: (t, 0))],
        out_shape=[jax.ShapeDtypeStruct((s, d), F32), jax.ShapeDtypeStruct((s, d), F32),
                   jax.ShapeDtypeStruct((s, ff), BF16)],
        compiler_params=_cparams(("parallel",)),
    )(x2, hgu, wd, vec)


def _ffn_bwd(dy, rpre, hgu, wdt, wgt, wut, vec, name):
    s, d = dy.shape
    ff = wgt.shape[0]
    ts = _tile(s, TS_FFN)

    def body(dy_ref, rpre_ref, hgu_ref, wdt_ref, wgt_ref, wut_ref, vec_ref, dx_ref, dr_ref, dhgu_ref, ln_ref):
        @pl.when(pl.program_id(0) == 0)
        def _():
            ln_ref[...] = jnp.zeros_like(ln_ref)

        dyv = dy_ref[...]
        dr, dyy = _ln_bwd(dyv, rpre_ref[...], vec_ref[V_G + 2:V_G + 3, :])
        ln_ref[0:1, :] += _colsum(dyy)
        ln_ref[1:2, :] += _colsum(dyv)
        dr_ref[...] = dr.astype(BF16)
        dact = _dot(dr, wdt_ref[...])
        hg = hgu_ref[:, 0:ff]
        hu = hgu_ref[:, ff:2 * ff]
        sg = _sigmoid(hg)
        dhg = dact * hu * (sg * (1.0 + hg * (1.0 - sg)))
        dhu = dact * hg * sg
        dhgu_ref[:, 0:ff] = dhg.astype(BF16)
        dhgu_ref[:, ff:2 * ff] = dhu.astype(BF16)
        dx_ref[...] = ALPHA * dr + _dot(dhg, wgt_ref[...]) + _dot(dhu, wut_ref[...])

    tile = pl.BlockSpec((ts, d), lambda t: (t, 0))
    wide = pl.BlockSpec((ts, 2 * ff), lambda t: (t, 0))
    consts = (wdt, wgt, wut, vec)
    return pl.pallas_call(
        body, name=name, grid=(s // ts,),
        in_specs=[tile, tile, wide] + [_const_spec(c.shape) for c in consts],
        out_specs=[tile, tile, wide, _const_spec((2, d))],
        out_shape=[jax.ShapeDtypeStruct((s, d), F32), jax.ShapeDtypeStruct((s, d), BF16),
                   jax.ShapeDtypeStruct((s, 2 * ff), BF16), jax.ShapeDtypeStruct((2, d), F32)],
        compiler_params=_cparams(("arbitrary",)),
    )(dy, rpre, hgu, *consts)


def _loss_head(y, target, name):
    s, d = y.shape
    ts = _tile(s, TS_MM)

    def body(y_ref, t_ref, loss_ref, dy_ref):
        @pl.when(pl.program_id(0) == 0)
        def _():
            loss_ref[...] = jnp.zeros_like(loss_ref)

        err = y_ref[...] - t_ref[...]
        dy_ref[...] = err / d
        per_token = jnp.mean(err * err, axis=-1, keepdims=True)
        loss_ref[...] += 0.5 * jnp.sum(per_token, axis=0, keepdims=True)

    tile = pl.BlockSpec((ts, d), lambda t: (t, 0))
    return pl.pallas_call(
        body, name=name, grid=(s // ts,),
        in_specs=[tile, tile],
        out_specs=[_const_spec((1, 1)), tile],
        out_shape=[jax.ShapeDtypeStruct((1, 1), F32), jax.ShapeDtypeStruct((s, d), F32)],
        compiler_params=_cparams(("arbitrary",)),
    )(y, target)


SHARDED = (("w_in", 2), ("pool_w", 2), ("lru_w_out", 1), ("sconv_w_out", 1), ("w_mix_out", 1),
           ("xa_w_q", 1), ("xa_w_k", 1), ("xa_w_v", 1), ("xa_w_o", 1),
           ("ffn_w_gate", 1), ("ffn_w_up", 1), ("ffn_w_down", 1))
STORED_TRANSPOSED = ("ffn_w_gate", "ffn_w_up")
SHARDED_SMALL = (("lru_conv_w", 2), ("sconv_w", 2), ("ln_g", 2), ("ln_b", 2))
REPLICATED = ("b_in", "pool_scale", "lru_conv_b", "lru_w_r", "lru_b_r", "lru_w_i", "lru_b_i", "lru_lambda")
WEIGHTS = ("w_in", "b_in", "pool_w", "pool_scale", "lru_conv_w", "lru_conv_b", "lru_w_r", "lru_b_r", "lru_w_i",
           "lru_b_i", "lru_lambda", "lru_w_out", "sconv_w", "sconv_w_out", "w_mix_out", "xa_w_q", "xa_w_k",
           "xa_w_v", "xa_w_o", "ffn_w_gate", "ffn_w_up", "ffn_w_down", "ln_g", "ln_b")


def _pack(arrs, width, lead=0, row_multiple=16):
    head = arrs[0].shape[:lead]
    flat = jnp.concatenate([a.reshape(head + (-1,)) for a in arrs], axis=lead)
    n = flat.shape[-1]
    chunk = width * row_multiple
    total = -(-n // chunk) * chunk
    if total != n:
        flat = jnp.pad(flat, [(0, 0)] * lead + [(0, total - n)])
    return flat.reshape(head + (total // width, width))


def _unpack(buf, shapes, lead=0):
    head = buf.shape[:lead]
    flat = buf.reshape(head + (-1,))
    out, off = [], 0
    for shp in shapes:
        n = math.prod(shp)
        out.append(flat[..., off:off + n].reshape(head + tuple(shp)))
        off += n
    return out


def _split8(a, axis):
    shp = a.shape
    a = a.reshape(shp[:axis] + (N_DEV, shp[axis] // N_DEV) + shp[axis + 1:])
    return jnp.moveaxis(a, axis, 0)


def _join8(a, axis):
    a = jnp.moveaxis(a, 0, axis)
    shp = a.shape
    return a.reshape(shp[:axis] + (shp[axis] * shp[axis + 1],) + shp[axis + 2:])


def _t(a):
    return jnp.swapaxes(a, -1, -2)


def _stored(name, a):
    return _t(a) if name in STORED_TRANSPOSED else a


def kernel(x, mem, w_in, b_in, pool_w, pool_scale, lru_conv_w, lru_conv_b, lru_w_r, lru_b_r, lru_w_i, lru_b_i, lru_lambda, lru_w_out, sconv_w, sconv_w_out, w_mix_out, xa_w_q, xa_w_k, xa_w_v, xa_w_o, ffn_w_gate, ffn_w_up, ffn_w_down, ln_g, ln_b, loss_target, m_w_in, m_b_in, m_pool_w, m_pool_scale, m_lru_conv_w, m_lru_conv_b, m_lru_w_r, m_lru_b_r, m_lru_w_i, m_lru_b_i, m_lru_lambda, m_lru_w_out, m_sconv_w, m_sconv_w_out, m_w_mix_out, m_xa_w_q, m_xa_w_k, m_xa_w_v, m_xa_w_o, m_ffn_w_gate, m_ffn_w_up, m_ffn_w_down, m_ln_g, m_ln_b, v_w_in, v_b_in, v_pool_w, v_pool_scale, v_lru_conv_w, v_lru_conv_b, v_lru_w_r, v_lru_b_r, v_lru_w_i, v_lru_b_i, v_lru_lambda, v_lru_w_out, v_sconv_w, v_sconv_w_out, v_w_mix_out, v_xa_w_q, v_xa_w_k, v_xa_w_v, v_xa_w_o, v_ffn_w_gate, v_ffn_w_up, v_ffn_w_down, v_ln_g, v_ln_b):
    args = dict(locals())
    w = {n: args[n] for n in WEIGHTS}
    mom_m = {n: args["m_" + n] for n in WEIGHTS}
    mom_v = {n: args["v_" + n] for n in WEIGHTS}
    depth = w_in.shape[0]
    s, d = x.shape[1], x.shape[2]
    nm = mem.shape[1]
    ff = ffn_w_gate.shape[2] * N_DEV
    dg = d // len(POOL_WINDOWS)
    xs = x.reshape(s, d)
    mems = mem.reshape(nm, d)
    target = loss_target.reshape(s, d)

    big_names = [n for n, _ in SHARDED]
    small_names = [n for n, _ in SHARDED_SMALL]
    axis_of = dict(SHARDED + SHARDED_SMALL)
    stored_shape = {n: _stored(n, w[n]).shape for n in big_names + small_names}
    gathered = _all_gather(_pack([_stored(n, w[n]).astype(BF16) for n in big_names], d), "gather_weights")
    gathered_small = _all_gather(_pack([w[n] for n in small_names], d, row_multiple=SUBLANES), "gather_vectors")
    full = {}
    for names, buf in ((big_names, gathered), (small_names, gathered_small)):
        for n, piece in zip(names, _unpack(buf, [stored_shape[n] for n in names], lead=1)):
            full[n] = piece
    blocks_in = full["w_in"]
    fw = {n: _join8(full[n], axis_of[n]) for n in big_names + small_names if n != "w_in"}

    layers = []
    for l in range(depth):
        vec = jnp.zeros((V_ROWS, d), F32)
        vec = vec.at[V_PSCALE].set(pool_scale[l]).at[V_CW:V_CW + LRU_CONV].set(fw["lru_conv_w"][l])
        vec = vec.at[V_CB].set(lru_conv_b[l]).at[V_BR].set(lru_b_r[l]).at[V_BI].set(lru_b_i[l])
        vec = vec.at[V_LAM].set(lru_lambda[l]).at[V_SW:V_SW + SCONV_K].set(fw["sconv_w"][l])
        vec = vec.at[V_G:V_G + 3].set(fw["ln_g"][l]).at[V_B:V_B + 3].set(fw["ln_b"][l])
        win_b = blocks_in[:, l]
        layers.append(dict(
            vec=vec, win=win_b, wint=_t(win_b).reshape(1, 8 * d, d), bin=b_in[l].reshape(1, 8 * d),
            pw=fw["pool_w"][l], pwt=_t(fw["pool_w"][l]),
            wr=lru_w_r[l].astype(BF16), wi=lru_w_i[l].astype(BF16),
            wrt=_t(lru_w_r[l]).astype(BF16), wit=_t(lru_w_i[l]).astype(BF16),
            wlo=fw["lru_w_out"][l], wlot=_t(fw["lru_w_out"][l]),
            wsc=fw["sconv_w_out"][l], wsct=_t(fw["sconv_w_out"][l]),
            wmix=fw["w_mix_out"][l], wmixt=_t(fw["w_mix_out"][l]),
            wq=fw["xa_w_q"][l], wqt=_t(fw["xa_w_q"][l]), wo=fw["xa_w_o"][l], wot=_t(fw["xa_w_o"][l]),
            wkv=jnp.stack([fw["xa_w_k"][l], fw["xa_w_v"][l]]),
            wgu=jnp.stack([_t(fw["ffn_w_gate"][l]), _t(fw["ffn_w_up"][l])]),
            wgt=fw["ffn_w_gate"][l], wut=fw["ffn_w_up"][l],
            wd=fw["ffn_w_down"][l], wdt=_t(fw["ffn_w_down"][l])))

    saved = []
    cur = xs
    for l, p in enumerate(layers):
        z = _mm(cur, p["win"], f"z_in_{l}", bias=p["bin"])
        x1, rpre1, h, yp, yl, yc, merged, e = _mixer_fwd(
            cur, z, p["pw"], p["wr"], p["wi"], p["wlo"], p["wsc"], p["wmix"], p["vec"], f"mixer_fwd_{l}")
        kv = _mm(mems, p["wkv"], f"kv_{l}")
        kk = kv[:, :d].astype(BF16)
        vv = kv[:, d:].astype(BF16)
        x2, rpre2, q, o = _attn_fwd(x1, p["wq"], p["wo"], _t(kk), vv, p["vec"], f"attn_fwd_{l}")
        hgu = _mm(x2, p["wgu"], f"ffn_in_{l}")
        x3, rpre3, act = _ffn_out(x2, hgu, p["wd"], p["vec"], f"ffn_out_{l}")
        saved.append(dict(x0=cur, z=z, x1=x1, rpre1=rpre1, h=h, yp=yp, yl=yl, yc=yc, merged=merged, e=e,
                          kk=kk, vv=vv, x2=x2, rpre2=rpre2, q=q, o=o, hgu=hgu, rpre3=rpre3, act=act))
        cur = x3

    loss_part, dcur = _loss_head(cur, target, "loss_head")
    loss = lax.psum(loss_part[0, 0], ("x", "y", "c"))

    grads = [None] * depth
    for l in reversed(range(depth)):
        p, sv = layers[l], saved[l]
        g = {}
        dx2, dr3, dhgu, ln3 = _ffn_bwd(dcur, sv["rpre3"], sv["hgu"], p["wdt"], p["wgt"], p["wut"], p["vec"],
                                       f"ffn_bwd_{l}")
        g["ffn_w_down"] = _mm_tn(sv["act"], dr3, d, f"g_wd_{l}")[0]
        dwgu = _mm_tn(dhgu, sv["x2"], d, f"g_wgu_{l}", tk=ff)[0]
        g["ffn_w_gate"], g["ffn_w_up"] = dwgu[:ff], dwgu[ff:]
        dx1, dq, dr2, dk, dv, ln2 = _attn_bwd(dx2, sv["rpre2"], sv["q"], p["wqt"], p["wot"], sv["kk"], _t(sv["kk"]),
                                              _t(sv["vv"]), p["vec"], f"attn_bwd_{l}")
        g["xa_w_o"] = _mm_tn(sv["o"], dr2, d, f"g_wo_{l}")[0]
        g["xa_w_q"] = _mm_tn(sv["x1"], dq, d, f"g_wq_{l}")[0]
        dwkv = _mm_tn(mems, jnp.concatenate([dk, dv], axis=1), d, f"g_wkv_{l}")
        g["xa_w_k"], g["xa_w_v"] = dwkv[0], dwkv[1]
        (dz, dr1, dyl, dyc, accs, dbin, g["pool_w"], g["lru_w_r"], g["lru_w_i"]) = _mixer_bwd(
            dx1, sv["rpre1"], sv["z"], sv["h"], sv["yp"], sv["yl"], sv["yc"], p["pw"], p["pwt"], p["wr"], p["wi"],
            p["wrt"], p["wit"], p["wlot"], p["wsct"], p["wmixt"], p["vec"], f"mixer_bwd_{l}")
        g["w_mix_out"] = _mm_tn(sv["merged"], dr1, d, f"g_wmix_{l}")[0]
        g["lru_w_out"] = _mm_tn(sv["h"], dyl, d, f"g_wlo_{l}")[0]
        g["sconv_w_out"] = _mm_tn(sv["e"], dyc, d, f"g_wsc_{l}")[0]
        g["w_in_blocks"] = _mm_tn(sv["x0"], dz, d, f"g_win_{l}")
        g["b_in"] = dbin[0]
        g["pool_scale"] = accs[A_PSCALE]
        g["lru_conv_w"] = accs[A_CW:A_CW + LRU_CONV]
        g["lru_conv_b"] = accs[A_CB]
        g["lru_b_r"] = accs[A_BR]
        g["lru_b_i"] = accs[A_BI]
        g["lru_lambda"] = accs[A_SP] * (-_sigmoid(-lru_lambda[l]))
        g["sconv_w"] = accs[A_SW:A_SW + SCONV_K]
        g["ln_g"] = jnp.stack([accs[A_G], ln2[0], ln3[0]])
        g["ln_b"] = jnp.stack([accs[A_B], ln2[1], ln3[1]])
        grads[l] = g
        dcur = _mm(dz, p["wint"], f"dx_{l}", add=dr1, add_scale=ALPHA)
    grad_x = dcur.reshape(x.shape)

    sharded = SHARDED + SHARDED_SMALL
    slots = []
    for n, axis in sharded:
        if n == "w_in":
            slots.append(jnp.stack([grads[l]["w_in_blocks"] for l in range(depth)], axis=1))
        else:
            slots.append(_split8(jnp.stack([grads[l][n] for l in range(depth)]), axis))
    shard_names = [n for n, _ in sharded]
    shard_shapes = [stored_shape[n] for n in shard_names]
    received = _all_to_all(_pack(slots, d, lead=1, row_multiple=TR_ADAM), "scatter_grads")
    outs = _adamw_sum(received, *[_pack([_stored(n, t[n]) for n in shard_names], d, row_multiple=TR_ADAM)
                                  for t in (w, mom_m, mom_v)], "adamw_sharded")
    res = {n: [_stored(n, a) for a in parts]
           for n, *parts in zip(shard_names, *[_unpack(o, shard_shapes) for o in outs])}

    rep_shapes = [w[n].shape for n in REPLICATED]
    partial = _pack([jnp.stack([grads[l][n] for l in range(depth)]) for n in REPLICATED], d, row_multiple=TR_ADAM)
    outs = _adamw_sum(_all_gather(partial, "gather_small_grads"),
                      *[_pack([t[n] for n in REPLICATED], d, row_multiple=TR_ADAM) for t in (w, mom_m, mom_v)],
                      "adamw_replicated")
    res.update({n: parts for n, *parts in zip(REPLICATED, *[_unpack(o, rep_shapes) for o in outs])})

    return (loss, grad_x, *[res[n][0] for n in WEIGHTS], *[res[n][1] for n in WEIGHTS],
            *[res[n][2] for n in WEIGHTS], *[res[n][3] for n in WEIGHTS])
```

```python
import functools
import math

import jax
import jax.numpy as jnp
from jax import lax
from jax.experimental import pallas as pl
from jax.experimental.pallas import tpu as pltpu

F32 = jnp.float32
BF16 = jnp.bfloat16
MESH = pl.DeviceIdType.MESH

N_DEV = 8
LRU_HEADS = 8
LRU_CONV = 4
LRU_C = 8.0
SCONV_K = 3
POOL_WINDOWS = (2, 4, 8, 16)
X_HEADS = 4
DEPTH = 2
ALPHA = (2 * DEPTH) ** 0.25
LN_EPS = 1e-5
ADAM_LR = 0.001
ADAM_B1 = 0.9
ADAM_B2 = 0.999
ADAM_EPS = 1e-08
ADAM_WD = 0.01
ADAM_STEP = 10

HALO = 16
SUBLANES = 8
VMEM_LIMIT = 56 * 1024 * 1024
TS_MIXER = 128
TS_ATTN = 512
TS_FFN = 256
TS_MM = 512
TK_MM = 2048
TR_ADAM = 256
ROW_PAD = 64
TS_MM_TN = 1024

V_PSCALE, V_CW, V_CB, V_BR, V_BI, V_LAM, V_SW, V_G, V_B = 0, 1, 5, 6, 7, 8, 9, 12, 15
V_ROWS = 24
A_PSCALE, A_CW, A_CB, A_BR, A_BI, A_SP, A_SW, A_G, A_B = 0, 1, 5, 6, 7, 8, 9, 12, 13
A_ROWS = 16


def _cparams(sem):
    return pltpu.CompilerParams(dimension_semantics=sem, vmem_limit_bytes=VMEM_LIMIT)


def _tile(n, pref):
    if n <= pref:
        return n
    assert n % pref == 0, (n, pref)
    return pref


def _const_spec(shape):
    nd = len(shape)
    return pl.BlockSpec(shape, lambda *_: (0,) * nd)


def _dot(a, b):
    return jnp.dot(a.astype(BF16), b.astype(BF16), preferred_element_type=F32)


def _dot_tn(a, b):
    return lax.dot_general(a.astype(BF16), b.astype(BF16), (((0,), (0,)), ((), ())),
                           preferred_element_type=F32)


def _sigmoid(x):
    return 1.0 / (1.0 + jnp.exp(-x))


def _softplus(y):
    e = jnp.exp(-jnp.abs(y))
    log1p = jnp.where(e < 1e-4, e * (1.0 - e * (0.5 - e * (1.0 / 3.0))), jnp.log(1.0 + e))
    return jnp.maximum(y, 0.0) + log1p


def _ln_fwd(r, g, b):
    mu = jnp.mean(r, axis=-1, keepdims=True)
    xc = r - mu
    var = jnp.mean(xc * xc, axis=-1, keepdims=True)
    return xc * lax.rsqrt(var + LN_EPS) * g + b


def _ln_bwd(dy, r, g):
    mu = jnp.mean(r, axis=-1, keepdims=True)
    xc = r - mu
    var = jnp.mean(xc * xc, axis=-1, keepdims=True)
    rstd = lax.rsqrt(var + LN_EPS)
    yhat = xc * rstd
    dyh = dy * g
    m1 = jnp.mean(dyh, axis=-1, keepdims=True)
    m2 = jnp.mean(dyh * yhat, axis=-1, keepdims=True)
    return rstd * (dyh - m1 - yhat * m2), dy * yhat


def _colsum(a):
    return jnp.sum(a, axis=0, keepdims=True)


def _position():
    return lax.axis_index("x"), lax.axis_index("y"), lax.axis_index("c")


def _gather_copies(x_ref, out_ref, send_sems, recv_sems, local_sem):
    x, y, c = _position()
    me, sibling = (x, y, c), (x, y, 1 - c)
    chips = [(1 - x, y), (x, 1 - y), (1 - x, 1 - y)]

    def slot(px, py, pc):
        return out_ref.at[4 * px + 2 * py + pc]

    def copy(k, block, to, src=None):
        return pltpu.make_async_remote_copy(
            src_ref=slot(*block) if src is None else src, dst_ref=slot(*block),
            send_sem=send_sems.at[k], recv_sem=recv_sems.at[k], device_id=to, device_id_type=MESH)

    mine = pltpu.make_async_copy(x_ref, slot(*me), local_sem)
    first = [copy(0, me, sibling, src=x_ref)]
    first += [copy(1 + j, me, (*chip, c), src=x_ref) for j, chip in enumerate(chips)]
    passed = [copy(4 + j, (*chip, c), sibling) for j, chip in enumerate(chips)]
    over_ici = [copy(1 + j, (*chip, c), me) for j, chip in enumerate(chips)]
    from_sibling = copy(0, sibling, me)
    forwarded = [copy(4 + j, (*chip, 1 - c), me) for j, chip in enumerate(chips)]
    return mine, first, passed, over_ici, from_sibling, forwarded


def _scatter_copies(g_ref, out_ref, send_sems, recv_sems, local_sem):
    x, y, c = _position()
    me = 4 * x + 2 * y + c
    mine = pltpu.make_async_copy(g_ref.at[me], out_ref.at[me], local_sem)
    copies = []
    for k in range(1, N_DEV):
        px = 1 - x if k & 4 else x
        py = 1 - y if k & 2 else y
        pc = 1 - c if k & 1 else c
        copies.append(pltpu.make_async_remote_copy(
            src_ref=g_ref.at[4 * px + 2 * py + pc], dst_ref=out_ref.at[me],
            send_sem=send_sems.at[k - 1], recv_sem=recv_sems.at[k - 1],
            device_id=(px, py, pc), device_id_type=MESH))
    return mine, copies


def _comm_start(kind, *refs):
    if kind == "gather":
        mine, first, _, _, _, _ = _gather_copies(*refs)
        mine.start()
        for cp in first:
            cp.start()
    else:
        mine, copies = _scatter_copies(*refs)
        mine.start()
        for cp in copies:
            cp.start()


def _comm_finish(kind, *refs):
    if kind == "gather":
        mine, first, passed, over_ici, from_sibling, forwarded = _gather_copies(*refs)
        for arrival, forward in zip(over_ici, passed):
            arrival.wait_recv()
            forward.start()
        from_sibling.wait_recv()
        for arrival in forwarded:
            arrival.wait_recv()
        for cp in first + passed:
            cp.wait_send()
        mine.wait()
    else:
        mine, copies = _scatter_copies(*refs)
        for cp in copies:
            cp.wait_recv()
        for cp in copies:
            cp.wait_send()
        mine.wait()


def _comm_out_shape(kind, arr):
    return jax.ShapeDtypeStruct((N_DEV,) + arr.shape if kind == "gather" else arr.shape, arr.dtype)


COMM_SEMAPHORES = [pltpu.SemaphoreType.DMA((7,)), pltpu.SemaphoreType.DMA((7,)), pltpu.SemaphoreType.DMA]


def _pallas(body, *, name, grid, in_specs, out_specs, out_shape, semantics, args, scratch_shapes=(), comm=()):
    in_specs, out_specs, out_shape = list(in_specs), list(out_specs), list(out_shape)
    scratch_shapes = list(scratch_shapes)
    n_in, n_out, n_scr, nc = len(in_specs), len(out_specs), len(scratch_shapes), len(comm)
    if not comm:
        return pl.pallas_call(body, name=name, grid=grid, in_specs=in_specs, out_specs=out_specs, out_shape=out_shape,
                              scratch_shapes=scratch_shapes, compiler_params=_cparams(semantics))(*args)
    kinds = [kind for kind, _ in comm]

    def carrying(*refs):
        ins, rest = refs[:n_in], refs[n_in:]
        cin, rest = rest[:nc], rest[nc:]
        outs, rest = rest[:n_out], rest[n_out:]
        cout, rest = rest[:nc], rest[nc:]
        scr, sems = rest[:n_scr], rest[n_scr:]
        ids = [pl.program_id(ax) for ax in range(len(grid))]
        first = functools.reduce(jnp.logical_and, [i == 0 for i in ids])
        last = functools.reduce(jnp.logical_and, [i == g - 1 for i, g in zip(ids, grid)])
        plans = [(kinds[k], cin[k], cout[k], *sems[3 * k:3 * k + 3]) for k in range(nc)]

        @pl.when(first)
        def _():
            for plan in plans:
                _comm_start(*plan)

        body(*ins, *outs, *scr)

        @pl.when(last)
        def _():
            for plan in plans:
                _comm_finish(*plan)

    hbm = pl.BlockSpec(memory_space=pl.ANY)
    return pl.pallas_call(
        carrying, name=name, grid=grid,
        in_specs=in_specs + [hbm] * nc, out_specs=out_specs + [hbm] * nc,
        out_shape=out_shape + [_comm_out_shape(kind, arr) for kind, arr in comm],
        scratch_shapes=scratch_shapes + COMM_SEMAPHORES * nc,
        compiler_params=_cparams(("arbitrary",) * len(grid)),
    )(*args, *[arr for _, arr in comm])


def _all_gather(xs, name):
    def body(x_ref, out_ref, send_sems, recv_sems, local_sem):
        _comm_start("gather", x_ref, out_ref, send_sems, recv_sems, local_sem)
        _comm_finish("gather", x_ref, out_ref, send_sems, recv_sems, local_sem)

    return pl.pallas_call(
        body, name=name, out_shape=_comm_out_shape("gather", xs),
        in_specs=[pl.BlockSpec(memory_space=pl.ANY)], out_specs=pl.BlockSpec(memory_space=pl.ANY),
        scratch_shapes=COMM_SEMAPHORES,
    )(xs)


def _adamw_sum(parts, w, m, v, name):
    _, rows, width = parts.shape
    tr = max(t for t in range(SUBLANES, min(rows, TR_ADAM) + 1, SUBLANES) if rows % t == 0)
    c1 = 1.0 - ADAM_B1 ** ADAM_STEP
    c2 = 1.0 - ADAM_B2 ** ADAM_STEP

    def body(p_ref, w_ref, m_ref, v_ref, g_ref, d_ref, nm_ref, nv_ref):
        g = p_ref[0]
        for k in range(1, N_DEV):
            g = g + p_ref[k]
        nm = ADAM_B1 * m_ref[...] + (1.0 - ADAM_B1) * g
        nv = ADAM_B2 * v_ref[...] + (1.0 - ADAM_B2) * (g * g)
        m_hat = nm / c1
        v_hat = nv / c2
        g_ref[...] = g
        d_ref[...] = -ADAM_LR * (m_hat / (jnp.sqrt(v_hat) + ADAM_EPS) + ADAM_WD * w_ref[...])
        nm_ref[...] = nm
        nv_ref[...] = nv

    spec = pl.BlockSpec((tr, width), lambda i: (i, 0))
    out = jax.ShapeDtypeStruct((rows, width), F32)
    return pl.pallas_call(
        body, name=name, grid=(rows // tr,),
        in_specs=[pl.BlockSpec((N_DEV, tr, width), lambda i: (0, i, 0)), spec, spec, spec],
        out_specs=[spec, spec, spec, spec], out_shape=[out, out, out, out],
        compiler_params=_cparams(("parallel",)),
    )(parts, w, m, v)


def _mm(a, wb, name, bias=None, add=None, add_scale=1.0, out_dtype=F32, comm=()):
    m, k = a.shape
    nb, k2, tn = wb.shape
    assert k == k2
    tm = _tile(m, TS_MM)
    tk = _tile(k, TK_MM)
    nk = k // tk

    def body(*refs):
        a_ref, w_ref = refs[0], refs[1]
        pos = 2
        b_ref = add_ref = None
        if bias is not None:
            b_ref = refs[pos]
            pos += 1
        if add is not None:
            add_ref = refs[pos]
            pos += 1
        o_ref, acc_ref = refs[pos], refs[pos + 1]
        kk = pl.program_id(2)

        @pl.when(kk == 0)
        def _():
            acc_ref[...] = jnp.zeros_like(acc_ref)

        acc_ref[...] += _dot(a_ref[...], w_ref[...])

        @pl.when(kk == nk - 1)
        def _():
            r = acc_ref[...]
            if b_ref is not None:
                r = r + b_ref[...]
            if add_ref is not None:
                r = r + add_scale * add_ref[...]
            o_ref[...] = r.astype(o_ref.dtype)

    in_specs = [pl.BlockSpec((tm, tk), lambda j, i, kk: (i, kk)),
                pl.BlockSpec((None, tk, tn), lambda j, i, kk: (j, kk, 0))]
    args = [a, wb]
    if bias is not None:
        in_specs.append(pl.BlockSpec((1, tn), lambda j, i, kk: (0, j)))
        args.append(bias)
    if add is not None:
        in_specs.append(pl.BlockSpec((tm, tn), lambda j, i, kk: (i, j)))
        args.append(add)
    return _pallas(
        body, name=name, grid=(nb, m // tm, nk),
        in_specs=in_specs,
        out_specs=[pl.BlockSpec((tm, tn), lambda j, i, kk: (i, j))],
        out_shape=[jax.ShapeDtypeStruct((m, nb * tn), out_dtype)],
        scratch_shapes=[pltpu.VMEM((tm, tn), F32)],
        semantics=("parallel", "parallel", "arbitrary"), args=args, comm=comm)


def _mm_tn(a, b, tn, name, tk=None, comm=()):
    s, k = a.shape
    s2, n = b.shape
    assert s == s2 and n % tn == 0
    nb = n // tn
    ts = _tile(s, TS_MM_TN)
    tk = k if tk is None else tk
    assert k % tk == 0

    def body(a_ref, b_ref, o_ref):
        @pl.when(pl.program_id(2) == 0)
        def _():
            o_ref[...] = jnp.zeros_like(o_ref)

        o_ref[...] += _dot_tn(a_ref[...], b_ref[...])

    return _pallas(
        body, name=name, grid=(nb, k // tk, s // ts),
        in_specs=[pl.BlockSpec((ts, tk), lambda j, kb, i: (i, kb)),
                  pl.BlockSpec((ts, tn), lambda j, kb, i: (i, j))],
        out_specs=[pl.BlockSpec((None, tk, tn), lambda j, kb, i: (j, kb, 0))],
        out_shape=[jax.ShapeDtypeStruct((nb, k, tn), F32)],
        semantics=("parallel", "parallel", "arbitrary"), args=(a, b), comm=comm)


def _scan_fwd(a_ref, b_ref, h_ref, carry_ref, ts):
    rowid = lax.broadcasted_iota(jnp.int32, (SUBLANES, 1), 0)

    def group(gi, hprev):
        r0 = pl.multiple_of(gi * SUBLANES, SUBLANES)
        a = a_ref[pl.ds(r0, SUBLANES), :]
        b = b_ref[pl.ds(r0, SUBLANES), :]
        for d in (1, 2, 4):
            a_sh = jnp.where(rowid >= d, pltpu.roll(a, d, 0), 1.0)
            b_sh = jnp.where(rowid >= d, pltpu.roll(b, d, 0), 0.0)
            b = a * b_sh + b
            a = a * a_sh
        hh = a * hprev + b
        h_ref[pl.ds(r0, SUBLANES), :] = hh
        return hh[SUBLANES - 1:SUBLANES, :]

    last = lax.fori_loop(0, ts // SUBLANES, group, carry_ref[0:1, :])
    carry_ref[0:1, :] = last


def _scan_rev(c_ref, b_ref, g_ref, carry_ref, ts):
    rowid = lax.broadcasted_iota(jnp.int32, (SUBLANES, 1), 0)
    ng = ts // SUBLANES

    def group(gi, gnext):
        r0 = pl.multiple_of((ng - 1 - gi) * SUBLANES, SUBLANES)
        c = c_ref[pl.ds(r0, SUBLANES), :]
        b = b_ref[pl.ds(r0, SUBLANES), :]
        for d in (1, 2, 4):
            keep = rowid < SUBLANES - d
            c_sh = jnp.where(keep, pltpu.roll(c, SUBLANES - d, 0), 1.0)
            b_sh = jnp.where(keep, pltpu.roll(b, SUBLANES - d, 0), 0.0)
            b = c * b_sh + b
            c = c * c_sh
        gg = c * gnext + b
        g_ref[pl.ds(r0, SUBLANES), :] = gg
        return gg[0:1, :]

    first = lax.fori_loop(0, ng, group, carry_ref[0:1, :])
    carry_ref[0:1, :] = first


def _heads_dot(v, w_ref, heads):
    hd = v.shape[1] // heads
    return jnp.concatenate([_dot(v[:, h * hd:(h + 1) * hd], w_ref[h]) for h in range(heads)], axis=1)


def _past(ext, sh, ts):
    if sh == 0:
        return ext[HALO:HALO + ts]
    return pltpu.roll(ext, sh, 0)[HALO:HALO + ts]


def _future(ext, sh, ts):
    if sh == 0:
        return ext[0:ts]
    return pltpu.roll(ext, ts + HALO - sh, 0)[0:ts]


def _one_minus_sq(a, log_a):
    x = 2.0 * log_a
    series = -x * (1.0 + x * (0.5 + x * (1.0 / 6.0 + x * (1.0 / 24.0))))
    return jnp.where(x > -0.02, series, 1.0 - a * a)


def _mixer_recompute(i, ts, d, z_ref, zh_ref, vec_ref, wr_ref, wi_ref):
    first = i == 0

    def zc(k):
        return z_ref[:, k * d:(k + 1) * d]

    def with_history(k):
        return jnp.concatenate([jnp.where(first, 0.0, zh_ref[:, k * d:(k + 1) * d]), zc(k)], axis=0)

    tglob = i * ts + lax.broadcasted_iota(jnp.int32, (ts, 1), 0)
    pext = with_history(0)
    dg = d // len(POOL_WINDOWS)
    ps = []
    for g, win in enumerate(POOL_WINDOWS):
        e = pext[:, g * dg:(g + 1) * dg]
        sm = e
        sh = 1
        while sh < win:
            sm = sm + pltpu.roll(sm, sh, 0)
            sh *= 2
        cnt = jnp.minimum(tglob + 1, win).astype(F32)
        ps.append(sm[HALO:HALO + ts] / cnt - e[HALO:HALO + ts])
    p = jnp.concatenate(ps, axis=1)
    lext = with_history(1)
    v = vec_ref[V_CB:V_CB + 1, :]
    for j in range(LRU_CONV):
        v = v + vec_ref[V_CW + j:V_CW + j + 1, :] * _past(lext, LRU_CONV - 1 - j, ts)
    r = _sigmoid(_heads_dot(v, wr_ref, LRU_HEADS) + vec_ref[V_BR:V_BR + 1, :])
    ig = _sigmoid(_heads_dot(v, wi_ref, LRU_HEADS) + vec_ref[V_BI:V_BI + 1, :])
    sp = _softplus(-vec_ref[V_LAM:V_LAM + 1, :])
    log_a = -LRU_C * r * sp
    a = jnp.exp(log_a)
    mult = jnp.sqrt(_one_minus_sq(a, log_a))
    qext = with_history(3) * with_history(4)
    cq = jnp.zeros((ts, d), F32)
    for j in range(SCONV_K):
        cq = cq + vec_ref[V_SW + j:V_SW + j + 1, :] * _past(qext, SCONV_K - 1 - j, ts)
    return dict(p=p, v=v, r=r, ig=ig, sp=sp, a=a, mult=mult, cq=cq, tglob=tglob)


def _halo_index(ts):
    blocks = ts // HALO
    return lambda t: (jnp.maximum(t * blocks - 1, 0), 0)


def _mixer_fwd(x, z, pw, wr, wi, wlo, wsc, wmix, vec, name, comm=()):
    s, d = x.shape
    ts = _tile(s, TS_MIXER)
    nt = s // ts
    dg = d // len(POOL_WINDOWS)

    def body(x_ref, z_ref, zh_ref, pw_ref, wr_ref, wi_ref, wlo_ref, wsc_ref, wmix_ref, vec_ref,
             x1_ref, rpre_ref, h_ref, yp_ref, yl_ref, yc_ref, mg_ref, e_ref,
             a_scr, b_scr, hcarry):
        i = pl.program_id(0)

        @pl.when(i == 0)
        def _():
            hcarry[...] = jnp.zeros_like(hcarry)

        f = _mixer_recompute(i, ts, d, z_ref, zh_ref, vec_ref, wr_ref, wi_ref)
        scale = vec_ref[V_PSCALE:V_PSCALE + 1, :]
        yp = jnp.concatenate([_dot(f["p"][:, g * dg:(g + 1) * dg], pw_ref[g]) for g in range(len(POOL_WINDOWS))],
                             axis=1) * scale
        a_scr[...] = f["a"]
        b_scr[...] = f["mult"] * (f["ig"] * f["v"])
        _scan_fwd(a_scr, b_scr, h_ref, hcarry, ts)
        yl = _dot(h_ref[...], wlo_ref[...])
        e = z_ref[:, 2 * d:3 * d] * f["cq"]
        yc = _dot(e, wsc_ref[...])
        merged = (_sigmoid(z_ref[:, 5 * d:6 * d]) * yp + _sigmoid(z_ref[:, 6 * d:7 * d]) * yl
                  + _sigmoid(z_ref[:, 7 * d:8 * d]) * yc)
        rpre = ALPHA * x_ref[...] + _dot(merged, wmix_ref[...])
        x1_ref[...] = _ln_fwd(rpre, vec_ref[V_G:V_G + 1, :], vec_ref[V_B:V_B + 1, :])
        rpre_ref[...] = rpre
        yp_ref[...] = yp
        yl_ref[...] = yl
        yc_ref[...] = yc
        mg_ref[...] = merged.astype(BF16)
        e_ref[...] = e.astype(BF16)

    tile = pl.BlockSpec((ts, d), lambda t: (t, 0))
    f32o = jax.ShapeDtypeStruct((s, d), F32)
    bfo = jax.ShapeDtypeStruct((s, d), BF16)
    return _pallas(
        body, name=name, grid=(nt,),
        in_specs=[tile, pl.BlockSpec((ts, 8 * d), lambda t: (t, 0)), pl.BlockSpec((HALO, 8 * d), _halo_index(ts)),
                  _const_spec(pw.shape), _const_spec(wr.shape), _const_spec(wi.shape), _const_spec(wlo.shape),
                  _const_spec(wsc.shape), _const_spec(wmix.shape), _const_spec(vec.shape)],
        out_specs=[tile] * 8,
        out_shape=[f32o, f32o, f32o, f32o, f32o, f32o, bfo, bfo],
        scratch_shapes=[pltpu.VMEM((ts, d), F32)] * 2 + [pltpu.VMEM((SUBLANES, d), F32)],
        semantics=("arbitrary",), args=(x, z, z, pw, wr, wi, wlo, wsc, wmix, vec), comm=comm)


def _mixer_bwd(dx1, rpre, z, h, yp, yl, yc, pw, pwt, wr, wi, wrt, wit, wlot, wsct, wmixt, vec, name, comm=()):
    s, d = dx1.shape
    ts = _tile(s, TS_MIXER)
    nt = s // ts
    dg = d // len(POOL_WINDOWS)
    ng = len(POOL_WINDOWS)

    def body(dx1_ref, rpre_ref, z_ref, zh_ref, h_ref, hh_ref, yp_ref, yl_ref, yc_ref,
             pw_ref, pwt_ref, wr_ref, wi_ref, wrt_ref, wit_ref, wlot_ref, wsct_ref, wmixt_ref, vec_ref,
             dz_ref, dr_ref, dyl_ref, dyc_ref, acc_ref, dbin_ref, dpw_ref, dwr_ref, dwi_ref,
             c_scr, b_scr, g_scr, gcarry, acarry, dcq_c, dv_c, m_c):
        i = pl.program_id(0)
        t = nt - 1 - i
        hd = d // LRU_HEADS

        @pl.when(i == 0)
        def _():
            for ref in (gcarry, acarry, dcq_c, dv_c, m_c, acc_ref, dbin_ref, dpw_ref, dwr_ref, dwi_ref):
                ref[...] = jnp.zeros_like(ref)

        f = _mixer_recompute(t, ts, d, z_ref, zh_ref, vec_ref, wr_ref, wi_ref)

        def vrow(k):
            return vec_ref[k:k + 1, :]

        def zc(k):
            return z_ref[:, k * d:(k + 1) * d]

        def acc(row, val):
            acc_ref[row:row + 1, :] += _colsum(val)

        def with_future(tile_val, carry_ref):
            ext = jnp.concatenate([tile_val, carry_ref[...]], axis=0)
            carry_ref[...] = tile_val[0:HALO, :]
            return ext

        dx1v = dx1_ref[...]
        dr, dyy = _ln_bwd(dx1v, rpre_ref[...], vrow(V_G))
        acc(A_G, dyy)
        acc(A_B, dx1v)
        dr_ref[...] = dr
        dmg = _dot(dr, wmixt_ref[...])
        dzs = [None] * 8
        gates = []
        for k, y_ref in enumerate((yp_ref, yl_ref, yc_ref)):
            gk = _sigmoid(zc(5 + k))
            dzs[5 + k] = dmg * y_ref[...] * gk * (1.0 - gk)
            gates.append(gk)
        dyp = dmg * gates[0]
        dyl = dmg * gates[1]
        dyc = dmg * gates[2]
        dyl_ref[...] = dyl.astype(BF16)
        dyc_ref[...] = dyc.astype(BF16)

        de = _dot(dyc, wsct_ref[...])
        dzs[2] = de * f["cq"]
        dcq = de * zc(2)
        dcq_ext = with_future(dcq, dcq_c)
        qv = zc(3) * zc(4)
        dq = jnp.zeros((ts, d), F32)
        for j in range(SCONV_K):
            adv = _future(dcq_ext, SCONV_K - 1 - j, ts)
            acc(A_SW + j, adv * qv)
            dq = dq + vrow(V_SW + j) * adv
        dzs[3] = dq * zc(4)
        dzs[4] = dq * zc(3)

        a, mult, r, ig, v, sp = f["a"], f["mult"], f["r"], f["ig"], f["v"], f["sp"]
        c_scr[...] = _future(with_future(a, acarry), 1, ts)
        b_scr[...] = _dot(dyl, wlot_ref[...])
        _scan_rev(c_scr, b_scr, g_scr, gcarry, ts)
        gs = g_scr[...]
        hprev = _past(jnp.concatenate([jnp.where(t == 0, 0.0, hh_ref[...]), h_ref[...]], axis=0), 1, ts)
        iv = ig * v
        dlog_a = gs * hprev * a + gs * iv * (-(a * a) / mult)
        div = gs * mult
        acc(A_SP, dlog_a * (-LRU_C) * r)
        dpre_r = dlog_a * (-LRU_C) * sp * r * (1.0 - r)
        dpre_i = div * v * ig * (1.0 - ig)
        acc(A_BR, dpre_r)
        acc(A_BI, dpre_i)
        dv = div * ig + _heads_dot(dpre_r, wrt_ref, LRU_HEADS) + _heads_dot(dpre_i, wit_ref, LRU_HEADS)
        for hh in range(LRU_HEADS):
            hs = slice(hh * hd, (hh + 1) * hd)
            dwr_ref[hh] += _dot_tn(v[:, hs], dpre_r[:, hs])
            dwi_ref[hh] += _dot_tn(v[:, hs], dpre_i[:, hs])
        acc(A_CB, dv)
        dv_ext = with_future(dv, dv_c)
        zl = zc(1)
        dzl = jnp.zeros((ts, d), F32)
        for j in range(LRU_CONV):
            adv = _future(dv_ext, LRU_CONV - 1 - j, ts)
            acc(A_CW + j, adv * zl)
            dzl = dzl + vrow(V_CW + j) * adv
        dzs[1] = dzl

        p = f["p"]
        ypre = jnp.concatenate([_dot(p[:, g * dg:(g + 1) * dg], pw_ref[g]) for g in range(ng)], axis=1)
        acc(A_PSCALE, dyp * ypre)
        dyps = dyp * vrow(V_PSCALE)
        for g in range(ng):
            dpw_ref[g] += _dot_tn(p[:, g * dg:(g + 1) * dg], dyps[:, g * dg:(g + 1) * dg])
        dp = jnp.concatenate([_dot(dyps[:, g * dg:(g + 1) * dg], pwt_ref[g]) for g in range(ng)], axis=1)
        cnts = [jnp.minimum(f["tglob"] + 1, win).astype(F32) for win in POOL_WINDOWS]
        mm = jnp.concatenate([dp[:, g * dg:(g + 1) * dg] / cnts[g] for g in range(ng)], axis=1)
        m_ext = with_future(mm, m_c)
        dzps = []
        for g, win in enumerate(POOL_WINDOWS):
            cs = slice(g * dg, (g + 1) * dg)
            sm = m_ext[:, cs]
            sh = 1
            while sh < win:
                sm = sm + pltpu.roll(sm, ts + HALO - sh, 0)
                sh *= 2
            dzps.append(sm[0:ts] - dp[:, cs])
        dzs[0] = jnp.concatenate(dzps, axis=1)

        for k in range(8):
            dz_ref[:, k * d:(k + 1) * d] = dzs[k].astype(BF16)
            dbin_ref[:, k * d:(k + 1) * d] += _colsum(dzs[k])

    def rev(tt):
        return (nt - 1 - tt, 0)

    halo = _halo_index(ts)
    tile = pl.BlockSpec((ts, d), rev)
    hspec = pl.BlockSpec((HALO, d), lambda tt: halo(nt - 1 - tt))
    f32o = jax.ShapeDtypeStruct((s, d), F32)
    bfo = jax.ShapeDtypeStruct((s, d), BF16)
    consts = (pw, pwt, wr, wi, wrt, wit, wlot, wsct, wmixt, vec)
    return _pallas(
        body, name=name, grid=(nt,),
        in_specs=[tile, tile, pl.BlockSpec((ts, 8 * d), rev),
                  pl.BlockSpec((HALO, 8 * d), lambda tt: halo(nt - 1 - tt)), tile, hspec, tile, tile, tile]
        + [_const_spec(c.shape) for c in consts],
        out_specs=[pl.BlockSpec((ts, 8 * d), rev), tile, tile, tile,
                   _const_spec((A_ROWS, d)), _const_spec((1, 8 * d)),
                   _const_spec(pw.shape), _const_spec(wr.shape), _const_spec(wi.shape)],
        out_shape=[jax.ShapeDtypeStruct((s, 8 * d), BF16), f32o, bfo, bfo,
                   jax.ShapeDtypeStruct((A_ROWS, d), F32), jax.ShapeDtypeStruct((1, 8 * d), F32),
                   jax.ShapeDtypeStruct(pw.shape, F32), jax.ShapeDtypeStruct(wr.shape, F32),
                   jax.ShapeDtypeStruct(wi.shape, F32)],
        scratch_shapes=[pltpu.VMEM((ts, d), F32)] * 3 + [pltpu.VMEM((SUBLANES, d), F32)]
        + [pltpu.VMEM((HALO, d), F32)] * 4,
        semantics=("arbitrary",), args=(dx1, rpre, z, z, h, h, yp, yl, yc, *consts), comm=comm)


def _softmax_rows(sc):
    mx = jnp.max(sc, axis=-1, keepdims=True)
    ex = jnp.exp(sc - mx)
    return ex / jnp.sum(ex, axis=-1, keepdims=True)


def _attn_fwd(x1, wq, wo, kt, vv, vec, name):
    s, d = x1.shape
    ts = _tile(s, TS_ATTN)
    hd = d // X_HEADS
    scale = hd ** -0.5

    def body(x_ref, wq_ref, wo_ref, kt_ref, v_ref, vec_ref, x2_ref, rpre_ref, q_ref, o_ref):
        xv = x_ref[...]
        q = _dot(xv, wq_ref[...]).astype(BF16)
        q_ref[...] = q
        for hh in range(X_HEADS):
            cs = slice(hh * hd, (hh + 1) * hd)
            p = _softmax_rows(_dot(q[:, cs], kt_ref[cs, :]) * scale)
            o_ref[:, cs] = _dot(p, v_ref[:, cs]).astype(BF16)
        rpre = ALPHA * xv + _dot(o_ref[...], wo_ref[...])
        rpre_ref[...] = rpre
        x2_ref[...] = _ln_fwd(rpre, vec_ref[V_G + 1:V_G + 2, :], vec_ref[V_B + 1:V_B + 2, :])

    tile = pl.BlockSpec((ts, d), lambda t: (t, 0))
    f32o = jax.ShapeDtypeStruct((s, d), F32)
    bfo = jax.ShapeDtypeStruct((s, d), BF16)
    consts = (wq, wo, kt, vv, vec)
    return pl.pallas_call(
        body, name=name, grid=(s // ts,),
        in_specs=[tile] + [_const_spec(c.shape) for c in consts],
        out_specs=[tile] * 4, out_shape=[f32o, f32o, bfo, bfo],
        compiler_params=_cparams(("parallel",)),
    )(x1, *consts)


def _attn_bwd(dx2, rpre, q, wqt, wot, kk, kt, vt, vec, name):
    s, d = dx2.shape
    ts = _tile(s, TS_ATTN)
    nm = kk.shape[0]
    hd = d // X_HEADS
    scale = hd ** -0.5

    def body(dx2_ref, rpre_ref, q_ref, wqt_ref, wot_ref, k_ref, kt_ref, vt_ref, vec_ref,
             dx1_ref, dq_ref, dr_ref, dk_ref, dv_ref, ln_ref):
        @pl.when(pl.program_id(0) == 0)
        def _():
            for ref in (dk_ref, dv_ref, ln_ref):
                ref[...] = jnp.zeros_like(ref)

        dyv = dx2_ref[...]
        dr, dyy = _ln_bwd(dyv, rpre_ref[...], vec_ref[V_G + 1:V_G + 2, :])
        ln_ref[0:1, :] += _colsum(dyy)
        ln_ref[1:2, :] += _colsum(dyv)
        dr_ref[...] = dr.astype(BF16)
        do = _dot(dr, wot_ref[...])
        q = q_ref[...]
        for hh in range(X_HEADS):
            cs = slice(hh * hd, (hh + 1) * hd)
            p = _softmax_rows(_dot(q[:, cs], kt_ref[cs, :]) * scale)
            dp = _dot(do[:, cs], vt_ref[cs, :])
            ds = p * (dp - jnp.sum(dp * p, axis=-1, keepdims=True)) * scale
            dq_ref[:, cs] = _dot(ds, k_ref[:, cs]).astype(BF16)
            dk_ref[:, cs] += _dot_tn(ds, q[:, cs])
            dv_ref[:, cs] += _dot_tn(p, do[:, cs])
        dx1_ref[...] = ALPHA * dr + _dot(dq_ref[...], wqt_ref[...])

    tile = pl.BlockSpec((ts, d), lambda t: (t, 0))
    consts = (wqt, wot, kk, kt, vt, vec)
    return pl.pallas_call(
        body, name=name, grid=(s // ts,),
        in_specs=[tile, tile, tile] + [_const_spec(c.shape) for c in consts],
        out_specs=[tile, tile, tile, _const_spec((nm, d)), _const_spec((nm, d)), _const_spec((2, d))],
        out_shape=[jax.ShapeDtypeStruct((s, d), F32), jax.ShapeDtypeStruct((s, d), BF16),
                   jax.ShapeDtypeStruct((s, d), BF16), jax.ShapeDtypeStruct((nm, d), F32),
                   jax.ShapeDtypeStruct((nm, d), F32), jax.ShapeDtypeStruct((2, d), F32)],
        compiler_params=_cparams(("arbitrary",)),
    )(dx2, rpre, q, *consts)


def _ffn_out(x2, hgu, wd, vec, name):
    s, d = x2.shape
    ff = wd.shape[0]
    ts = _tile(s, TS_FFN)

    def body(x_ref, hgu_ref, wd_ref, vec_ref, x3_ref, rpre_ref, act_ref):
        hg = hgu_ref[:, 0:ff]
        act = hg * _sigmoid(hg) * hgu_ref[:, ff:2 * ff]
        act_ref[...] = act.astype(BF16)
        rpre = ALPHA * x_ref[...] + _dot(act, wd_ref[...])
        rpre_ref[...] = rpre
        x3_ref[...] = _ln_fwd(rpre, vec_ref[V_G + 2:V_G + 3, :], vec_ref[V_B + 2:V_B + 3, :])

    tile = pl.BlockSpec((ts, d), lambda t: (t, 0))
    return pl.pallas_call(
        body, name=name, grid=(s // ts,),
        in_specs=[tile, pl.BlockSpec((ts, 2 * ff), lambda t: (t, 0)), _const_spec(wd.shape), _const_spec(vec.shape)],
        out_specs=[tile, tile, pl.BlockSpec((ts, ff), lambda t: (t, 0))],
        out_shape=[jax.ShapeDtypeStruct((s, d), F32), jax.ShapeDtypeStruct((s, d), F32),
                   jax.ShapeDtypeStruct((s, ff), BF16)],
        compiler_params=_cparams(("parallel",)),
    )(x2, hgu, wd, vec)


def _ffn_bwd(dy, rpre, hgu, wdt, wgt, wut, vec, name, comm=()):
    s, d = dy.shape
    ff = wgt.shape[0]
    ts = _tile(s, TS_FFN)

    def body(dy_ref, rpre_ref, hgu_ref, wdt_ref, wgt_ref, wut_ref, vec_ref, dx_ref, dr_ref, dhgu_ref, ln_ref):
        @pl.when(pl.program_id(0) == 0)
        def _():
            ln_ref[...] = jnp.zeros_like(ln_ref)

        dyv = dy_ref[...]
        dr, dyy = _ln_bwd(dyv, rpre_ref[...], vec_ref[V_G + 2:V_G + 3, :])
        ln_ref[0:1, :] += _colsum(dyy)
        ln_ref[1:2, :] += _colsum(dyv)
        dr_ref[...] = dr.astype(BF16)
        dact = _dot(dr, wdt_ref[...])
        hg = hgu_ref[:, 0:ff]
        hu = hgu_ref[:, ff:2 * ff]
        sg = _sigmoid(hg)
        dhg = dact * hu * (sg * (1.0 + hg * (1.0 - sg)))
        dhu = dact * hg * sg
        dhgu_ref[:, 0:ff] = dhg.astype(BF16)
        dhgu_ref[:, ff:2 * ff] = dhu.astype(BF16)
        dx_ref[...] = ALPHA * dr + _dot(dhg, wgt_ref[...]) + _dot(dhu, wut_ref[...])

    tile = pl.BlockSpec((ts, d), lambda t: (t, 0))
    wide = pl.BlockSpec((ts, 2 * ff), lambda t: (t, 0))
    consts = (wdt, wgt, wut, vec)
    return _pallas(
        body, name=name, grid=(s // ts,),
        in_specs=[tile, tile, wide] + [_const_spec(c.shape) for c in consts],
        out_specs=[tile, tile, wide, _const_spec((2, d))],
        out_shape=[jax.ShapeDtypeStruct((s, d), F32), jax.ShapeDtypeStruct((s, d), BF16),
                   jax.ShapeDtypeStruct((s, 2 * ff), BF16), jax.ShapeDtypeStruct((2, d), F32)],
        semantics=("arbitrary",), args=(dy, rpre, hgu, *consts), comm=comm)


def _loss_head(y, target, name):
    s, d = y.shape
    ts = _tile(s, TS_MM)

    def body(y_ref, t_ref, loss_ref, dy_ref):
        @pl.when(pl.program_id(0) == 0)
        def _():
            loss_ref[...] = jnp.zeros_like(loss_ref)

        err = y_ref[...] - t_ref[...]
        dy_ref[...] = err / d
        per_token = jnp.mean(err * err, axis=-1, keepdims=True)
        loss_ref[...] += 0.5 * jnp.sum(per_token, axis=0, keepdims=True)

    tile = pl.BlockSpec((ts, d), lambda t: (t, 0))
    return pl.pallas_call(
        body, name=name, grid=(s // ts,),
        in_specs=[tile, tile],
        out_specs=[_const_spec((1, 1)), tile],
        out_shape=[jax.ShapeDtypeStruct((1, 1), F32), jax.ShapeDtypeStruct((s, d), F32)],
        compiler_params=_cparams(("arbitrary",)),
    )(y, target)


SHARD_AXIS = {"w_in": 1, "pool_w": 1, "lru_w_out": 0, "sconv_w_out": 0, "w_mix_out": 0,
              "xa_w_q": 0, "xa_w_k": 0, "xa_w_v": 0, "xa_w_o": 0,
              "ffn_w_gate": 0, "ffn_w_up": 0, "ffn_w_down": 0,
              "lru_conv_w": 1, "sconv_w": 1, "ln_g": 1, "ln_b": 1}
STORED_TRANSPOSED = ("ffn_w_gate", "ffn_w_up")
GROUP_IN = ("w_in",)
GROUP_MIXER = ("pool_w", "lru_w_out", "sconv_w_out", "w_mix_out")
GROUP_ATTN = ("xa_w_q", "xa_w_k", "xa_w_v", "xa_w_o")
GROUP_FFN = ("ffn_w_gate", "ffn_w_up", "ffn_w_down")
GROUP_VECTORS = ("lru_conv_w", "sconv_w", "ln_g", "ln_b")
REPLICATED = ("b_in", "pool_scale", "lru_conv_b", "lru_w_r", "lru_b_r", "lru_w_i", "lru_b_i", "lru_lambda")
WEIGHTS = ("w_in", "b_in", "pool_w", "pool_scale", "lru_conv_w", "lru_conv_b", "lru_w_r", "lru_b_r", "lru_w_i",
           "lru_b_i", "lru_lambda", "lru_w_out", "sconv_w", "sconv_w_out", "w_mix_out", "xa_w_q", "xa_w_k",
           "xa_w_v", "xa_w_o", "ffn_w_gate", "ffn_w_up", "ffn_w_down", "ln_g", "ln_b")


def _pack(arrs, width, lead=0, row_multiple=ROW_PAD):
    head = arrs[0].shape[:lead]
    flat = jnp.concatenate([a.reshape(head + (-1,)) for a in arrs], axis=lead)
    n = flat.shape[-1]
    chunk = width * row_multiple
    total = -(-n // chunk) * chunk
    if total != n:
        flat = jnp.pad(flat, [(0, 0)] * lead + [(0, total - n)])
    return flat.reshape(head + (total // width, width))


def _unpack(buf, shapes, lead=0):
    head = buf.shape[:lead]
    flat = buf.reshape(head + (-1,))
    out, off = [], 0
    for shp in shapes:
        n = math.prod(shp)
        out.append(flat[..., off:off + n].reshape(head + tuple(shp)))
        off += n
    return out


def _split8(a, axis):
    shp = a.shape
    a = a.reshape(shp[:axis] + (N_DEV, shp[axis] // N_DEV) + shp[axis + 1:])
    return jnp.moveaxis(a, axis, 0)


def _join8(a, axis):
    a = jnp.moveaxis(a, 0, axis)
    shp = a.shape
    return a.reshape(shp[:axis] + (shp[axis] * shp[axis + 1],) + shp[axis + 2:])


def _t(a):
    return jnp.swapaxes(a, -1, -2)


def _stored(name, a):
    return _t(a) if name in STORED_TRANSPOSED else a


def kernel(x, mem, w_in, b_in, pool_w, pool_scale, lru_conv_w, lru_conv_b, lru_w_r, lru_b_r, lru_w_i, lru_b_i, lru_lambda, lru_w_out, sconv_w, sconv_w_out, w_mix_out, xa_w_q, xa_w_k, xa_w_v, xa_w_o, ffn_w_gate, ffn_w_up, ffn_w_down, ln_g, ln_b, loss_target, m_w_in, m_b_in, m_pool_w, m_pool_scale, m_lru_conv_w, m_lru_conv_b, m_lru_w_r, m_lru_b_r, m_lru_w_i, m_lru_b_i, m_lru_lambda, m_lru_w_out, m_sconv_w, m_sconv_w_out, m_w_mix_out, m_xa_w_q, m_xa_w_k, m_xa_w_v, m_xa_w_o, m_ffn_w_gate, m_ffn_w_up, m_ffn_w_down, m_ln_g, m_ln_b, v_w_in, v_b_in, v_pool_w, v_pool_scale, v_lru_conv_w, v_lru_conv_b, v_lru_w_r, v_lru_b_r, v_lru_w_i, v_lru_b_i, v_lru_lambda, v_lru_w_out, v_sconv_w, v_sconv_w_out, v_w_mix_out, v_xa_w_q, v_xa_w_k, v_xa_w_v, v_xa_w_o, v_ffn_w_gate, v_ffn_w_up, v_ffn_w_down, v_ln_g, v_ln_b):
    args = dict(locals())
    w = {n: args[n] for n in WEIGHTS}
    mom_m = {n: args["m_" + n] for n in WEIGHTS}
    mom_v = {n: args["v_" + n] for n in WEIGHTS}
    depth = w_in.shape[0]
    s, d = x.shape[1], x.shape[2]
    nm = mem.shape[1]
    ff = ffn_w_gate.shape[2] * N_DEV
    xs = x.reshape(s, d)
    mems = mem.reshape(nm, d)
    target = loss_target.reshape(s, d)

    def shard(t, n, l):
        return _stored(n, t[n][l])

    def pack_shards(t, names, l, dtype=None):
        arrs = [shard(t, n, l) for n in names]
        return _pack([a if dtype is None else a.astype(dtype) for a in arrs], d)

    def unpack_gathered(buf, names):
        pieces = _unpack(buf, [shard(w, n, 0).shape for n in names], lead=1)
        return {n: (p if n == "w_in" else _join8(p, SHARD_AXIS[n])) for n, p in zip(names, pieces)}

    def layer_params(l, fw):
        vec = jnp.zeros((V_ROWS, d), F32)
        vec = vec.at[V_PSCALE].set(pool_scale[l]).at[V_CW:V_CW + LRU_CONV].set(fw["lru_conv_w"])
        vec = vec.at[V_CB].set(lru_conv_b[l]).at[V_BR].set(lru_b_r[l]).at[V_BI].set(lru_b_i[l])
        vec = vec.at[V_LAM].set(lru_lambda[l]).at[V_SW:V_SW + SCONV_K].set(fw["sconv_w"])
        vec = vec.at[V_G:V_G + 3].set(fw["ln_g"]).at[V_B:V_B + 3].set(fw["ln_b"])
        return dict(
            vec=vec, wint=_t(fw["w_in"]).reshape(1, 8 * d, d),
            pw=fw["pool_w"], pwt=_t(fw["pool_w"]),
            wr=lru_w_r[l].astype(BF16), wi=lru_w_i[l].astype(BF16),
            wrt=_t(lru_w_r[l]).astype(BF16), wit=_t(lru_w_i[l]).astype(BF16),
            wlo=fw["lru_w_out"], wlot=_t(fw["lru_w_out"]),
            wsc=fw["sconv_w_out"], wsct=_t(fw["sconv_w_out"]),
            wmix=fw["w_mix_out"], wmixt=_t(fw["w_mix_out"]),
            wq=fw["xa_w_q"], wqt=_t(fw["xa_w_q"]), wo=fw["xa_w_o"], wot=_t(fw["xa_w_o"]),
            wkv=jnp.stack([fw["xa_w_k"], fw["xa_w_v"]]),
            wgu=jnp.stack([_t(fw["ffn_w_gate"]), _t(fw["ffn_w_up"])]),
            wgt=fw["ffn_w_gate"], wut=fw["ffn_w_up"],
            wd=fw["ffn_w_down"], wdt=_t(fw["ffn_w_down"]))

    rest = GROUP_MIXER + GROUP_ATTN + GROUP_FFN
    fw0 = unpack_gathered(_all_gather(pack_shards(w, GROUP_IN, 0, BF16), "gather_w_in_0"), GROUP_IN)
    vectors = _all_gather(_pack([shard(w, n, l) for l in range(depth) for n in GROUP_VECTORS], d), "gather_vectors")
    vec_pieces = _unpack(vectors, [shard(w, n, l).shape for l in range(depth) for n in GROUP_VECTORS], lead=1)
    fvec = [{n: _join8(vec_pieces[l * len(GROUP_VECTORS) + k], SHARD_AXIS[n]) for k, n in enumerate(GROUP_VECTORS)}
            for l in range(depth)]

    layers, saved = [], []
    cur = xs
    fw_next = None
    for l in range(depth):
        fw = dict(fw0 if l == 0 else fw_next)
        fw.update(fvec[l])
        comm = [("gather", pack_shards(w, rest, 0, BF16))] if l == 0 else ()
        z, *got = _mm(cur, fw["w_in"], f"z_in_{l}", bias=b_in[l].reshape(1, 8 * d), comm=comm)
        if l == 0:
            fw.update(unpack_gathered(got[0], rest))
        p = layer_params(l, fw)
        comm = [("gather", pack_shards(w, GROUP_IN + rest, l + 1, BF16))] if l + 1 < depth else ()
        x1, rpre1, h, yp, yl, yc, merged, e, *got = _mixer_fwd(
            cur, z, p["pw"], p["wr"], p["wi"], p["wlo"], p["wsc"], p["wmix"], p["vec"], f"mixer_fwd_{l}", comm=comm)
        if l + 1 < depth:
            fw_next = unpack_gathered(got[0], GROUP_IN + rest)
        kv = _mm(mems, p["wkv"], f"kv_{l}")[0]
        kk = kv[:, :d].astype(BF16)
        vv = kv[:, d:].astype(BF16)
        x2, rpre2, q, o = _attn_fwd(x1, p["wq"], p["wo"], _t(kk), vv, p["vec"], f"attn_fwd_{l}")
        hgu = _mm(x2, p["wgu"], f"ffn_in_{l}")[0]
        x3, rpre3, act = _ffn_out(x2, hgu, p["wd"], p["vec"], f"ffn_out_{l}")
        layers.append(p)
        saved.append(dict(x0=cur, z=z, x1=x1, rpre1=rpre1, h=h, yp=yp, yl=yl, yc=yc, merged=merged, e=e,
                          kk=kk, vv=vv, x2=x2, rpre2=rpre2, q=q, o=o, hgu=hgu, rpre3=rpre3, act=act))
        cur = x3

    loss_part, dcur = _loss_head(cur, target, "loss_head")
    loss = lax.psum(loss_part[0, 0], ("x", "y", "c"))

    res = {}

    def slots_of(g, names):
        return _pack([g[n] if n == "w_in" else _split8(g[n], SHARD_AXIS[n]) for n in names], d, lead=1)

    def update(received, names, l, tag):
        outs = _adamw_sum(received, *[pack_shards(t, names, l) for t in (w, mom_m, mom_v)], f"adamw_{tag}_{l}")
        shapes = [shard(w, n, l).shape for n in names]
        for n, *parts in zip(names, *[_unpack(o, shapes) for o in outs]):
            res[(n, l)] = [_stored(n, a) for a in parts]

    def settle(exchanges, got):
        for (names, l, tag, _), received in zip(exchanges, got):
            update(received, names, l, tag)

    grads = [None] * depth
    for l in reversed(range(depth)):
        p, sv = layers[l], saved[l]
        g = {}
        dx2, dr3, dhgu, ln3 = _ffn_bwd(dcur, sv["rpre3"], sv["hgu"], p["wdt"], p["wgt"], p["wut"], p["vec"],
                                       f"ffn_bwd_{l}")
        g["ffn_w_down"] = _mm_tn(sv["act"], dr3, d, f"g_wd_{l}")[0][0]
        dwgu = _mm_tn(dhgu, sv["x2"], d, f"g_wgu_{l}", tk=ff)[0][0]
        g["ffn_w_gate"], g["ffn_w_up"] = dwgu[:ff], dwgu[ff:]
        dx1, dq, dr2, dk, dv, ln2 = _attn_bwd(dx2, sv["rpre2"], sv["q"], p["wqt"], p["wot"], sv["kk"], _t(sv["kk"]),
                                              _t(sv["vv"]), p["vec"], f"attn_bwd_{l}")
        g["xa_w_o"] = _mm_tn(sv["o"], dr2, d, f"g_wo_{l}")[0][0]
        g["xa_w_q"] = _mm_tn(sv["x1"], dq, d, f"g_wq_{l}")[0][0]
        dwkv = _mm_tn(mems, jnp.concatenate([dk, dv], axis=1), d, f"g_wkv_{l}")[0]
        g["xa_w_k"], g["xa_w_v"] = dwkv[0], dwkv[1]
        ffn_slots = slots_of(g, GROUP_FFN)
        (dz, dr1, dyl, dyc, accs, dbin, g["pool_w"], g["lru_w_r"], g["lru_w_i"], received) = _mixer_bwd(
            dx1, sv["rpre1"], sv["z"], sv["h"], sv["yp"], sv["yl"], sv["yc"], p["pw"], p["pwt"], p["wr"], p["wi"],
            p["wrt"], p["wit"], p["wlot"], p["wsct"], p["wmixt"], p["vec"], f"mixer_bwd_{l}",
            comm=[("scatter", ffn_slots)])
        update(received, GROUP_FFN, l, "ffn")
        g["w_mix_out"] = _mm_tn(sv["merged"], dr1, d, f"g_wmix_{l}")[0][0]
        g["lru_w_out"] = _mm_tn(sv["h"], dyl, d, f"g_wlo_{l}")[0][0]
        g["sconv_w_out"] = _mm_tn(sv["e"], dyc, d, f"g_wsc_{l}")[0][0]
        g["b_in"] = dbin[0]
        g["pool_scale"] = accs[A_PSCALE]
        g["lru_conv_w"] = accs[A_CW:A_CW + LRU_CONV]
        g["lru_conv_b"] = accs[A_CB]
        g["lru_b_r"] = accs[A_BR]
        g["lru_b_i"] = accs[A_BI]
        g["lru_lambda"] = accs[A_SP] * (-_sigmoid(-lru_lambda[l]))
        g["sconv_w"] = accs[A_SW:A_SW + SCONV_K]
        g["ln_g"] = jnp.stack([accs[A_G], ln2[0], ln3[0]])
        g["ln_b"] = jnp.stack([accs[A_B], ln2[1], ln3[1]])
        grads[l] = g
        behind_win = [(GROUP_ATTN, l, "attn", slots_of(g, GROUP_ATTN)),
                      (GROUP_MIXER + GROUP_VECTORS, l, "mixer", slots_of(g, GROUP_MIXER + GROUP_VECTORS))]
        g["w_in"], *got = _mm_tn(sv["x0"], dz, d, f"g_win_{l}", comm=[("scatter", t[3]) for t in behind_win])
        settle(behind_win, got)
        behind_dx = [(GROUP_IN, l, "w_in", slots_of(g, GROUP_IN))]
        comm = [("scatter", behind_dx[0][3])]
        if l == 0:
            comm.append(("gather", _pack([jnp.stack([grads[k][n] for k in range(depth)]) for n in REPLICATED], d)))
        dcur, *got = _mm(dz, p["wint"], f"dx_{l}", add=dr1, add_scale=ALPHA, comm=comm)
        settle(behind_dx, got[:1])
        if l == 0:
            outs = _adamw_sum(got[1], *[_pack([t[n] for n in REPLICATED], d) for t in (w, mom_m, mom_v)],
                              "adamw_replicated")
            rep_shapes = [w[n].shape for n in REPLICATED]
            final = {n: parts for n, *parts in zip(REPLICATED, *[_unpack(o, rep_shapes) for o in outs])}
    grad_x = dcur.reshape(x.shape)

    for n in WEIGHTS:
        if n not in final:
            final[n] = [jnp.stack([res[(n, l)][k] for l in range(depth)]) for k in range(4)]
    return (loss, grad_x, *[final[n][0] for n in WEIGHTS], *[final[n][1] for n in WEIGHTS],
            *[final[n][2] for n in WEIGHTS], *[final[n][3] for n in WEIGHTS])
```

```python
import functools
import math

import jax
import jax.numpy as jnp
from jax import lax
from jax.experimental import pallas as pl
from jax.experimental.pallas import tpu as pltpu

F32 = jnp.float32
BF16 = jnp.bfloat16
MESH = pl.DeviceIdType.MESH

N_DEV = 8
LRU_HEADS = 8
LRU_CONV = 4
LRU_C = 8.0
SCONV_K = 3
POOL_WINDOWS = (2, 4, 8, 16)
X_HEADS = 4
DEPTH = 2
ALPHA = (2 * DEPTH) ** 0.25
LN_EPS = 1e-5
ADAM_LR = 0.001
ADAM_B1 = 0.9
ADAM_B2 = 0.999
ADAM_EPS = 1e-08
ADAM_WD = 0.01
ADAM_STEP = 10

HALO = 16
SUBLANES = 8
VMEM_LIMIT = 56 * 1024 * 1024
TS_MIXER = 128
TS_ATTN = 512
TS_FFN = 256
TS_MM = 1024
TK_MM = 2048
TR_ADAM = 256
ROW_PAD = 64
TS_MM_TN = 1024

V_PSCALE, V_CW, V_CB, V_BR, V_BI, V_LAM, V_SW, V_G, V_B = 0, 1, 5, 6, 7, 8, 9, 12, 15
V_ROWS = 24
A_PSCALE, A_CW, A_CB, A_BR, A_BI, A_SP, A_SW, A_G, A_B = 0, 1, 5, 6, 7, 8, 9, 12, 13
A_ROWS = 16


def _cparams(sem):
    return pltpu.CompilerParams(dimension_semantics=sem, vmem_limit_bytes=VMEM_LIMIT)


def _tile(n, pref):
    if n <= pref:
        return n
    assert n % pref == 0, (n, pref)
    return pref


def _const_spec(shape):
    nd = len(shape)
    return pl.BlockSpec(shape, lambda *_: (0,) * nd, pipeline_mode=pl.Buffered(1))


def _acc_spec(shape):
    nd = len(shape)
    return pl.BlockSpec(shape, lambda *_: (0,) * nd)


def _dot(a, b):
    return jnp.dot(a.astype(BF16), b.astype(BF16), preferred_element_type=F32)


def _dot_tn(a, b):
    return lax.dot_general(a.astype(BF16), b.astype(BF16), (((0,), (0,)), ((), ())),
                           preferred_element_type=F32)


def _sigmoid(x):
    return 0.5 * jnp.tanh(0.5 * x) + 0.5


def _softplus(y):
    e = jnp.exp(-jnp.abs(y))
    log1p = jnp.where(e < 1e-4, e * (1.0 - e * (0.5 - e * (1.0 / 3.0))), jnp.log(1.0 + e))
    return jnp.maximum(y, 0.0) + log1p


def _ln_fwd(r, g, b):
    mu = jnp.mean(r, axis=-1, keepdims=True)
    xc = r - mu
    var = jnp.mean(xc * xc, axis=-1, keepdims=True)
    return xc * lax.rsqrt(var + LN_EPS) * g + b


def _ln_bwd(dy, r, g):
    mu = jnp.mean(r, axis=-1, keepdims=True)
    xc = r - mu
    var = jnp.mean(xc * xc, axis=-1, keepdims=True)
    rstd = lax.rsqrt(var + LN_EPS)
    yhat = xc * rstd
    dyh = dy * g
    m1 = jnp.mean(dyh, axis=-1, keepdims=True)
    m2 = jnp.mean(dyh * yhat, axis=-1, keepdims=True)
    return rstd * (dyh - m1 - yhat * m2), dy * yhat


def _colsum(a):
    return jnp.sum(a, axis=0, keepdims=True)


def _position():
    return lax.axis_index("x"), lax.axis_index("y"), lax.axis_index("c")


def _gather_copies(x_ref, out_ref, send_sems, recv_sems, local_sem):
    x, y, c = _position()
    me, sibling = (x, y, c), (x, y, 1 - c)
    chips = [(1 - x, y), (x, 1 - y), (1 - x, 1 - y)]

    def slot(px, py, pc):
        return out_ref.at[4 * px + 2 * py + pc]

    def copy(k, block, to, src=None):
        return pltpu.make_async_remote_copy(
            src_ref=slot(*block) if src is None else src, dst_ref=slot(*block),
            send_sem=send_sems.at[k], recv_sem=recv_sems.at[k], device_id=to, device_id_type=MESH)

    mine = pltpu.make_async_copy(x_ref, slot(*me), local_sem)
    first = [copy(0, me, sibling, src=x_ref)]
    first += [copy(1 + j, me, (*chip, c), src=x_ref) for j, chip in enumerate(chips)]
    passed = [copy(4 + j, (*chip, c), sibling) for j, chip in enumerate(chips)]
    over_ici = [copy(1 + j, (*chip, c), me) for j, chip in enumerate(chips)]
    from_sibling = copy(0, sibling, me)
    forwarded = [copy(4 + j, (*chip, 1 - c), me) for j, chip in enumerate(chips)]
    return mine, first, passed, over_ici, from_sibling, forwarded


def _scatter_copies(g_ref, out_ref, send_sems, recv_sems, local_sem):
    x, y, c = _position()
    me = 4 * x + 2 * y + c
    mine = pltpu.make_async_copy(g_ref.at[me], out_ref.at[me], local_sem)
    copies = []
    for k in range(1, N_DEV):
        px = 1 - x if k & 4 else x
        py = 1 - y if k & 2 else y
        pc = 1 - c if k & 1 else c
        copies.append(pltpu.make_async_remote_copy(
            src_ref=g_ref.at[4 * px + 2 * py + pc], dst_ref=out_ref.at[me],
            send_sem=send_sems.at[k - 1], recv_sem=recv_sems.at[k - 1],
            device_id=(px, py, pc), device_id_type=MESH))
    return mine, copies


def _comm_start(kind, *refs):
    if kind == "gather":
        mine, first, _, _, _, _ = _gather_copies(*refs)
        mine.start()
        for cp in first:
            cp.start()
    else:
        mine, copies = _scatter_copies(*refs)
        mine.start()
        for cp in copies:
            cp.start()


def _comm_finish(kind, *refs):
    if kind == "gather":
        mine, first, passed, over_ici, from_sibling, forwarded = _gather_copies(*refs)
        for arrival, forward in zip(over_ici, passed):
            arrival.wait_recv()
            forward.start()
        from_sibling.wait_recv()
        for arrival in forwarded:
            arrival.wait_recv()
        for cp in first + passed:
            cp.wait_send()
        mine.wait()
    else:
        mine, copies = _scatter_copies(*refs)
        for cp in copies:
            cp.wait_recv()
        for cp in copies:
            cp.wait_send()
        mine.wait()


def _comm_out_shape(kind, arr):
    return jax.ShapeDtypeStruct((N_DEV,) + arr.shape if kind == "gather" else arr.shape, arr.dtype)


COMM_SEMAPHORES = [pltpu.SemaphoreType.DMA((7,)), pltpu.SemaphoreType.DMA((7,)), pltpu.SemaphoreType.DMA]


def _pallas(body, *, name, grid, in_specs, out_specs, out_shape, semantics, args, scratch_shapes=(), comm=()):
    in_specs, out_specs, out_shape = list(in_specs), list(out_specs), list(out_shape)
    scratch_shapes = list(scratch_shapes)
    n_in, n_out, n_scr, nc = len(in_specs), len(out_specs), len(scratch_shapes), len(comm)
    if not comm:
        return pl.pallas_call(body, name=name, grid=grid, in_specs=in_specs, out_specs=out_specs, out_shape=out_shape,
                              scratch_shapes=scratch_shapes, compiler_params=_cparams(semantics))(*args)
    kinds = [kind for kind, _ in comm]

    def carrying(*refs):
        ins, rest = refs[:n_in], refs[n_in:]
        cin, rest = rest[:nc], rest[nc:]
        outs, rest = rest[:n_out], rest[n_out:]
        cout, rest = rest[:nc], rest[nc:]
        scr, sems = rest[:n_scr], rest[n_scr:]
        ids = [pl.program_id(ax) for ax in range(len(grid))]
        first = functools.reduce(jnp.logical_and, [i == 0 for i in ids])
        last = functools.reduce(jnp.logical_and, [i == g - 1 for i, g in zip(ids, grid)])
        plans = [(kinds[k], cin[k], cout[k], *sems[3 * k:3 * k + 3]) for k in range(nc)]

        @pl.when(first)
        def _():
            for plan in plans:
                _comm_start(*plan)

        body(*ins, *outs, *scr)

        @pl.when(last)
        def _():
            for plan in plans:
                _comm_finish(*plan)

    hbm = pl.BlockSpec(memory_space=pl.ANY)
    return pl.pallas_call(
        carrying, name=name, grid=grid,
        in_specs=in_specs + [hbm] * nc, out_specs=out_specs + [hbm] * nc,
        out_shape=out_shape + [_comm_out_shape(kind, arr) for kind, arr in comm],
        scratch_shapes=scratch_shapes + COMM_SEMAPHORES * nc,
        compiler_params=_cparams(("arbitrary",) * len(grid)),
    )(*args, *[arr for _, arr in comm])


def _all_gather(xs, name):
    def body(x_ref, out_ref, send_sems, recv_sems, local_sem):
        _comm_start("gather", x_ref, out_ref, send_sems, recv_sems, local_sem)
        _comm_finish("gather", x_ref, out_ref, send_sems, recv_sems, local_sem)

    return pl.pallas_call(
        body, name=name, out_shape=_comm_out_shape("gather", xs),
        in_specs=[pl.BlockSpec(memory_space=pl.ANY)], out_specs=pl.BlockSpec(memory_space=pl.ANY),
        scratch_shapes=COMM_SEMAPHORES,
    )(xs)


def _adamw_sum(parts, w, m, v, name):
    _, rows, width = parts.shape
    tr = max(t for t in range(SUBLANES, min(rows, TR_ADAM) + 1, SUBLANES) if rows % t == 0)
    c1 = 1.0 - ADAM_B1 ** ADAM_STEP
    c2 = 1.0 - ADAM_B2 ** ADAM_STEP

    def body(p_ref, w_ref, m_ref, v_ref, g_ref, d_ref, nm_ref, nv_ref):
        g = p_ref[0]
        for k in range(1, N_DEV):
            g = g + p_ref[k]
        nm = ADAM_B1 * m_ref[...] + (1.0 - ADAM_B1) * g
        nv = ADAM_B2 * v_ref[...] + (1.0 - ADAM_B2) * (g * g)
        m_hat = nm / c1
        v_hat = nv / c2
        g_ref[...] = g
        d_ref[...] = -ADAM_LR * (m_hat / (jnp.sqrt(v_hat) + ADAM_EPS) + ADAM_WD * w_ref[...])
        nm_ref[...] = nm
        nv_ref[...] = nv

    spec = pl.BlockSpec((tr, width), lambda i: (i, 0))
    out = jax.ShapeDtypeStruct((rows, width), F32)
    return pl.pallas_call(
        body, name=name, grid=(rows // tr,),
        in_specs=[pl.BlockSpec((N_DEV, tr, width), lambda i: (0, i, 0)), spec, spec, spec],
        out_specs=[spec, spec, spec, spec], out_shape=[out, out, out, out],
        compiler_params=_cparams(("parallel",)),
    )(parts, w, m, v)


def _mm(a, wb, name, bias=None, add=None, add_scale=1.0, out_dtype=F32, comm=()):
    m, k = a.shape
    nb, k2, tn = wb.shape
    assert k == k2
    tm = _tile(m, TS_MM)
    tk = _tile(k, TK_MM)
    nk = k // tk

    def body(*refs):
        a_ref, w_ref = refs[0], refs[1]
        pos = 2
        b_ref = add_ref = None
        if bias is not None:
            b_ref = refs[pos]
            pos += 1
        if add is not None:
            add_ref = refs[pos]
            pos += 1
        o_ref = refs[pos]

        def finish(r):
            if b_ref is not None:
                r = r + b_ref[...]
            if add_ref is not None:
                r = r + add_scale * add_ref[...]
            o_ref[...] = r.astype(o_ref.dtype)

        if nk == 1:
            finish(_dot(a_ref[...], w_ref[...]))
            return
        acc_ref = refs[pos + 1]
        kk = pl.program_id(2)

        @pl.when(kk == 0)
        def _():
            acc_ref[...] = jnp.zeros_like(acc_ref)

        acc_ref[...] += _dot(a_ref[...], w_ref[...])

        @pl.when(kk == nk - 1)
        def _():
            finish(acc_ref[...])

    in_specs = [pl.BlockSpec((tm, tk), lambda j, i, kk: (i, kk)),
                pl.BlockSpec((None, tk, tn), lambda j, i, kk: (j, kk, 0))]
    args = [a, wb]
    if bias is not None:
        in_specs.append(pl.BlockSpec((1, tn), lambda j, i, kk: (0, j)))
        args.append(bias)
    if add is not None:
        in_specs.append(pl.BlockSpec((tm, tn), lambda j, i, kk: (i, j)))
        args.append(add)
    return _pallas(
        body, name=name, grid=(nb, m // tm, nk),
        in_specs=in_specs,
        out_specs=[pl.BlockSpec((tm, tn), lambda j, i, kk: (i, j))],
        out_shape=[jax.ShapeDtypeStruct((m, nb * tn), out_dtype)],
        scratch_shapes=[pltpu.VMEM((tm, tn), F32)] if nk > 1 else [],
        semantics=("parallel", "parallel", "arbitrary"), args=args, comm=comm)


def _mm_tn(a, b, tn, name, tk=None, comm=()):
    s, k = a.shape
    s2, n = b.shape
    assert s == s2 and n % tn == 0
    nb = n // tn
    ts = _tile(s, TS_MM_TN)
    tk = k if tk is None else tk
    assert k % tk == 0

    def body(a_ref, b_ref, o_ref):
        @pl.when(pl.program_id(2) == 0)
        def _():
            o_ref[...] = jnp.zeros_like(o_ref)

        o_ref[...] += _dot_tn(a_ref[...], b_ref[...])

    return _pallas(
        body, name=name, grid=(nb, k // tk, s // ts),
        in_specs=[pl.BlockSpec((ts, tk), lambda j, kb, i: (i, kb)),
                  pl.BlockSpec((ts, tn), lambda j, kb, i: (i, j))],
        out_specs=[pl.BlockSpec((None, tk, tn), lambda j, kb, i: (j, kb, 0))],
        out_shape=[jax.ShapeDtypeStruct((nb, k, tn), F32)],
        semantics=("parallel", "parallel", "arbitrary"), args=(a, b), comm=comm)


def _scan_fwd(a_ref, b_ref, h_ref, carry_ref, ts):
    rowid = lax.broadcasted_iota(jnp.int32, (SUBLANES, 1), 0)

    def group(gi, hprev):
        r0 = pl.multiple_of(gi * SUBLANES, SUBLANES)
        a = a_ref[pl.ds(r0, SUBLANES), :]
        b = b_ref[pl.ds(r0, SUBLANES), :]
        for d in (1, 2, 4):
            a_sh = jnp.where(rowid >= d, pltpu.roll(a, d, 0), 1.0)
            b_sh = jnp.where(rowid >= d, pltpu.roll(b, d, 0), 0.0)
            b = a * b_sh + b
            a = a * a_sh
        hh = a * hprev + b
        h_ref[pl.ds(r0, SUBLANES), :] = hh
        return hh[SUBLANES - 1:SUBLANES, :]

    last = lax.fori_loop(0, ts // SUBLANES, group, carry_ref[0:1, :])
    carry_ref[0:1, :] = last


def _scan_rev(c_ref, b_ref, g_ref, carry_ref, ts):
    rowid = lax.broadcasted_iota(jnp.int32, (SUBLANES, 1), 0)
    ng = ts // SUBLANES

    def group(gi, gnext):
        r0 = pl.multiple_of((ng - 1 - gi) * SUBLANES, SUBLANES)
        c = c_ref[pl.ds(r0, SUBLANES), :]
        b = b_ref[pl.ds(r0, SUBLANES), :]
        for d in (1, 2, 4):
            keep = rowid < SUBLANES - d
            c_sh = jnp.where(keep, pltpu.roll(c, SUBLANES - d, 0), 1.0)
            b_sh = jnp.where(keep, pltpu.roll(b, SUBLANES - d, 0), 0.0)
            b = c * b_sh + b
            c = c * c_sh
        gg = c * gnext + b
        g_ref[pl.ds(r0, SUBLANES), :] = gg
        return gg[0:1, :]

    first = lax.fori_loop(0, ng, group, carry_ref[0:1, :])
    carry_ref[0:1, :] = first


def _heads_dot(v, w_ref, heads):
    hd = v.shape[1] // heads
    return jnp.concatenate([_dot(v[:, h * hd:(h + 1) * hd], w_ref[h]) for h in range(heads)], axis=1)


def _past(ext, sh, ts):
    if sh == 0:
        return ext[HALO:HALO + ts]
    return pltpu.roll(ext, sh, 0)[HALO:HALO + ts]


def _future(ext, sh, ts):
    if sh == 0:
        return ext[0:ts]
    return pltpu.roll(ext, ts + HALO - sh, 0)[0:ts]


def _one_minus_sq(a, log_a):
    x = 2.0 * log_a
    series = -x * (1.0 + x * (0.5 + x * (1.0 / 6.0 + x * (1.0 / 24.0))))
    return jnp.where(x > -0.02, series, 1.0 - a * a)


def _mixer_recompute(i, ts, d, z_ref, zh_ref, vec_ref, wr_ref, wi_ref, between=lambda: None):
    first = i == 0

    def zc(k):
        return z_ref[:, k * d:(k + 1) * d]

    def with_history(k):
        return jnp.concatenate([jnp.where(first, 0.0, zh_ref[:, k * d:(k + 1) * d]), zc(k)], axis=0)

    tglob = i * ts + lax.broadcasted_iota(jnp.int32, (ts, 1), 0)
    pext = with_history(0)
    dg = d // len(POOL_WINDOWS)
    ps = []
    for g, win in enumerate(POOL_WINDOWS):
        e = pext[:, g * dg:(g + 1) * dg]
        sm = e
        sh = 1
        while sh < win:
            sm = sm + pltpu.roll(sm, sh, 0)
            sh *= 2
        inv_cnt = 1.0 / jnp.minimum(tglob + 1, win).astype(F32)
        ps.append(sm[HALO:HALO + ts] * inv_cnt - e[HALO:HALO + ts])
        between()
    p = jnp.concatenate(ps, axis=1)
    lext = with_history(1)
    v = vec_ref[V_CB:V_CB + 1, :]
    for j in range(LRU_CONV):
        v = v + vec_ref[V_CW + j:V_CW + j + 1, :] * _past(lext, LRU_CONV - 1 - j, ts)
    between()
    r = _sigmoid(_heads_dot(v, wr_ref, LRU_HEADS) + vec_ref[V_BR:V_BR + 1, :])
    ig = _sigmoid(_heads_dot(v, wi_ref, LRU_HEADS) + vec_ref[V_BI:V_BI + 1, :])
    between()
    sp = _softplus(-vec_ref[V_LAM:V_LAM + 1, :])
    log_a = -LRU_C * r * sp
    a = jnp.exp(log_a)
    mult = jnp.sqrt(_one_minus_sq(a, log_a))
    between()
    qext = with_history(3) * with_history(4)
    cq = jnp.zeros((ts, d), F32)
    for j in range(SCONV_K):
        cq = cq + vec_ref[V_SW + j:V_SW + j + 1, :] * _past(qext, SCONV_K - 1 - j, ts)
    return dict(p=p, v=v, r=r, ig=ig, sp=sp, a=a, mult=mult, cq=cq, tglob=tglob)


def _halo_index(ts):
    blocks = ts // HALO
    return lambda t: (jnp.maximum(t * blocks - 1, 0), 0)


def _mixer_fwd(x, win, b_in, pw, wr, wi, wlo, wsc, wmix, vec, name, comm=()):
    s, d = x.shape
    ts = _tile(s, TS_MIXER)
    nt = s // ts
    dg = d // len(POOL_WINDOWS)
    nblk = win.shape[0]

    def body(x_ref, xn_ref, win_ref, bin_ref, pw_ref, wr_ref, wi_ref, wlo_ref, wsc_ref, wmix_ref, vec_ref,
             z_ref, x1_ref, rpre_ref, h_ref, yp_ref, yl_ref, yc_ref, mg_ref, e_ref,
             z_even, z_odd, zhist, a_scr, b_scr, hcarry):
        i = pl.program_id(0)

        def project_block(xb, dst, k):
            dst[:, k * d:(k + 1) * d] = _dot(xb, win_ref[k]) + bin_ref[:, k * d:(k + 1) * d]

        def project(tile_ref, dst):
            xb = tile_ref[...].astype(BF16)
            for k in range(nblk):
                project_block(xb, dst, k)

        @pl.when(i == 0)
        def _():
            hcarry[...] = jnp.zeros_like(hcarry)
            zhist[...] = jnp.zeros_like(zhist)
            project(x_ref, z_even)

        def step(zc, zn):
            xb = xn_ref[...].astype(BF16)
            todo = list(range(nblk))

            def next_block():
                if todo:
                    project_block(xb, zn, todo.pop(0))

            f = _mixer_recompute(i, ts, d, zc, zhist, vec_ref, wr_ref, wi_ref, between=next_block)
            scale = vec_ref[V_PSCALE:V_PSCALE + 1, :]
            yp = jnp.concatenate([_dot(f["p"][:, g * dg:(g + 1) * dg], pw_ref[g])
                                  for g in range(len(POOL_WINDOWS))], axis=1) * scale
            a_scr[...] = f["a"]
            b_scr[...] = f["mult"] * (f["ig"] * f["v"])
            _scan_fwd(a_scr, b_scr, h_ref, hcarry, ts)
            while todo:
                next_block()
            yl = _dot(h_ref[...], wlo_ref[...])
            e = zc[:, 2 * d:3 * d] * f["cq"]
            yc = _dot(e, wsc_ref[...])
            merged = (_sigmoid(zc[:, 5 * d:6 * d]) * yp + _sigmoid(zc[:, 6 * d:7 * d]) * yl
                      + _sigmoid(zc[:, 7 * d:8 * d]) * yc)
            rpre = ALPHA * x_ref[...] + _dot(merged, wmix_ref[...])
            x1_ref[...] = _ln_fwd(rpre, vec_ref[V_G:V_G + 1, :], vec_ref[V_B:V_B + 1, :])
            rpre_ref[...] = rpre
            yp_ref[...] = yp
            yl_ref[...] = yl
            yc_ref[...] = yc
            mg_ref[...] = merged.astype(BF16)
            e_ref[...] = e.astype(BF16)
            z_ref[...] = zc[...]
            zhist[...] = zc[ts - HALO:ts, :]

        parity = lax.rem(i, 2)

        @pl.when(parity == 0)
        def _():
            step(z_even, z_odd)

        @pl.when(parity == 1)
        def _():
            step(z_odd, z_even)

    tile = pl.BlockSpec((ts, d), lambda t: (t, 0))
    wide = pl.BlockSpec((ts, nblk * d), lambda t: (t, 0))
    f32o = jax.ShapeDtypeStruct((s, d), F32)
    bfo = jax.ShapeDtypeStruct((s, d), BF16)
    consts = (win, b_in, pw, wr, wi, wlo, wsc, wmix, vec)
    return _pallas(
        body, name=name, grid=(nt,),
        in_specs=[tile, pl.BlockSpec((ts, d), lambda t: (jnp.minimum(t + 1, nt - 1), 0))]
        + [_const_spec(c.shape) for c in consts],
        out_specs=[wide] + [tile] * 8,
        out_shape=[jax.ShapeDtypeStruct((s, nblk * d), F32), f32o, f32o, f32o, f32o, f32o, f32o, bfo, bfo],
        scratch_shapes=[pltpu.VMEM((ts, nblk * d), F32)] * 2 + [pltpu.VMEM((HALO, nblk * d), F32)]
        + [pltpu.VMEM((ts, d), F32)] * 2 + [pltpu.VMEM((SUBLANES, d), F32)],
        semantics=("arbitrary",), args=(x, x, *consts), comm=comm)


def _mixer_bwd(dx1, rpre, z, h, yp, yl, yc, pw, pwt, wr, wi, wrt, wit, wlot, wsct, wmixt, vec, name, comm=()):
    s, d = dx1.shape
    ts = _tile(s, TS_MIXER)
    nt = s // ts
    dg = d // len(POOL_WINDOWS)
    ng = len(POOL_WINDOWS)

    def body(dx1_ref, rpre_ref, z_ref, zh_ref, h_ref, hh_ref, yp_ref, yl_ref, yc_ref,
             pw_ref, pwt_ref, wr_ref, wi_ref, wrt_ref, wit_ref, wlot_ref, wsct_ref, wmixt_ref, vec_ref,
             dz_ref, dr_ref, dyl_ref, dyc_ref, acc_ref, dbin_ref, dpw_ref, dwr_ref, dwi_ref,
             c_scr, b_scr, g_scr, gcarry, acarry, dcq_c, dv_c, m_c):
        i = pl.program_id(0)
        t = nt - 1 - i
        hd = d // LRU_HEADS

        @pl.when(i == 0)
        def _():
            for ref in (gcarry, acarry, dcq_c, dv_c, m_c, acc_ref, dbin_ref, dpw_ref, dwr_ref, dwi_ref):
                ref[...] = jnp.zeros_like(ref)

        f = _mixer_recompute(t, ts, d, z_ref, zh_ref, vec_ref, wr_ref, wi_ref)

        def vrow(k):
            return vec_ref[k:k + 1, :]

        def zc(k):
            return z_ref[:, k * d:(k + 1) * d]

        def acc(row, val):
            acc_ref[row:row + 1, :] += _colsum(val)

        def with_future(tile_val, carry_ref):
            ext = jnp.concatenate([tile_val, carry_ref[...]], axis=0)
            carry_ref[...] = tile_val[0:HALO, :]
            return ext

        dx1v = dx1_ref[...]
        dr, dyy = _ln_bwd(dx1v, rpre_ref[...], vrow(V_G))
        acc(A_G, dyy)
        acc(A_B, dx1v)
        dr_ref[...] = dr
        dmg = _dot(dr, wmixt_ref[...])
        dzs = [None] * 8
        gates = []
        for k, y_ref in enumerate((yp_ref, yl_ref, yc_ref)):
            gk = _sigmoid(zc(5 + k))
            dzs[5 + k] = dmg * y_ref[...] * gk * (1.0 - gk)
            gates.append(gk)
        dyp = dmg * gates[0]
        dyl = dmg * gates[1]
        dyc = dmg * gates[2]
        dyl_ref[...] = dyl.astype(BF16)
        dyc_ref[...] = dyc.astype(BF16)

        de = _dot(dyc, wsct_ref[...])
        dzs[2] = de * f["cq"]
        dcq = de * zc(2)
        dcq_ext = with_future(dcq, dcq_c)
        qv = zc(3) * zc(4)
        dq = jnp.zeros((ts, d), F32)
        for j in range(SCONV_K):
            adv = _future(dcq_ext, SCONV_K - 1 - j, ts)
            acc(A_SW + j, adv * qv)
            dq = dq + vrow(V_SW + j) * adv
        dzs[3] = dq * zc(4)
        dzs[4] = dq * zc(3)

        a, mult, r, ig, v, sp = f["a"], f["mult"], f["r"], f["ig"], f["v"], f["sp"]
        c_scr[...] = _future(with_future(a, acarry), 1, ts)
        b_scr[...] = _dot(dyl, wlot_ref[...])
        _scan_rev(c_scr, b_scr, g_scr, gcarry, ts)
        gs = g_scr[...]
        hprev = _past(jnp.concatenate([jnp.where(t == 0, 0.0, hh_ref[...]), h_ref[...]], axis=0), 1, ts)
        iv = ig * v
        dlog_a = gs * hprev * a + gs * iv * (-(a * a) / mult)
        div = gs * mult
        acc(A_SP, dlog_a * (-LRU_C) * r)
        dpre_r = dlog_a * (-LRU_C) * sp * r * (1.0 - r)
        dpre_i = div * v * ig * (1.0 - ig)
        acc(A_BR, dpre_r)
        acc(A_BI, dpre_i)
        dv = div * ig + _heads_dot(dpre_r, wrt_ref, LRU_HEADS) + _heads_dot(dpre_i, wit_ref, LRU_HEADS)
        for hh in range(LRU_HEADS):
            hs = slice(hh * hd, (hh + 1) * hd)
            dwr_ref[hh] += _dot_tn(v[:, hs], dpre_r[:, hs])
            dwi_ref[hh] += _dot_tn(v[:, hs], dpre_i[:, hs])
        acc(A_CB, dv)
        dv_ext = with_future(dv, dv_c)
        zl = zc(1)
        dzl = jnp.zeros((ts, d), F32)
        for j in range(LRU_CONV):
            adv = _future(dv_ext, LRU_CONV - 1 - j, ts)
            acc(A_CW + j, adv * zl)
            dzl = dzl + vrow(V_CW + j) * adv
        dzs[1] = dzl

        p = f["p"]
        ypre = jnp.concatenate([_dot(p[:, g * dg:(g + 1) * dg], pw_ref[g]) for g in range(ng)], axis=1)
        acc(A_PSCALE, dyp * ypre)
        dyps = dyp * vrow(V_PSCALE)
        for g in range(ng):
            dpw_ref[g] += _dot_tn(p[:, g * dg:(g + 1) * dg], dyps[:, g * dg:(g + 1) * dg])
        dp = jnp.concatenate([_dot(dyps[:, g * dg:(g + 1) * dg], pwt_ref[g]) for g in range(ng)], axis=1)
        inv_cnts = [1.0 / jnp.minimum(f["tglob"] + 1, win).astype(F32) for win in POOL_WINDOWS]
        mm = jnp.concatenate([dp[:, g * dg:(g + 1) * dg] * inv_cnts[g] for g in range(ng)], axis=1)
        m_ext = with_future(mm, m_c)
        dzps = []
        for g, win in enumerate(POOL_WINDOWS):
            cs = slice(g * dg, (g + 1) * dg)
            sm = m_ext[:, cs]
            sh = 1
            while sh < win:
                sm = sm + pltpu.roll(sm, ts + HALO - sh, 0)
                sh *= 2
            dzps.append(sm[0:ts] - dp[:, cs])
        dzs[0] = jnp.concatenate(dzps, axis=1)

        for k in range(8):
            dz_ref[:, k * d:(k + 1) * d] = dzs[k].astype(BF16)
            dbin_ref[:, k * d:(k + 1) * d] += _colsum(dzs[k])

    def rev(tt):
        return (nt - 1 - tt, 0)

    halo = _halo_index(ts)
    tile = pl.BlockSpec((ts, d), rev)
    hspec = pl.BlockSpec((HALO, d), lambda tt: halo(nt - 1 - tt))
    f32o = jax.ShapeDtypeStruct((s, d), F32)
    bfo = jax.ShapeDtypeStruct((s, d), BF16)
    consts = (pw, pwt, wr, wi, wrt, wit, wlot, wsct, wmixt, vec)
    return _pallas(
        body, name=name, grid=(nt,),
        in_specs=[tile, tile, pl.BlockSpec((ts, 8 * d), rev),
                  pl.BlockSpec((HALO, 8 * d), lambda tt: halo(nt - 1 - tt)), tile, hspec, tile, tile, tile]
        + [_const_spec(c.shape) for c in consts],
        out_specs=[pl.BlockSpec((ts, 8 * d), rev), tile, tile, tile,
                   _acc_spec((A_ROWS, d)), _acc_spec((1, 8 * d)),
                   _acc_spec(pw.shape), _acc_spec(wr.shape), _acc_spec(wi.shape)],
        out_shape=[jax.ShapeDtypeStruct((s, 8 * d), BF16), f32o, bfo, bfo,
                   jax.ShapeDtypeStruct((A_ROWS, d), F32), jax.ShapeDtypeStruct((1, 8 * d), F32),
                   jax.ShapeDtypeStruct(pw.shape, F32), jax.ShapeDtypeStruct(wr.shape, F32),
                   jax.ShapeDtypeStruct(wi.shape, F32)],
        scratch_shapes=[pltpu.VMEM((ts, d), F32)] * 3 + [pltpu.VMEM((SUBLANES, d), F32)]
        + [pltpu.VMEM((HALO, d), F32)] * 4,
        semantics=("arbitrary",), args=(dx1, rpre, z, z, h, h, yp, yl, yc, *consts), comm=comm)


def _softmax_rows(sc):
    mx = jnp.max(sc, axis=-1, keepdims=True)
    ex = jnp.exp(sc - mx)
    return ex * (1.0 / jnp.sum(ex, axis=-1, keepdims=True))


def _attn_fwd(x1, wq, wo, kt, vv, vec, name):
    s, d = x1.shape
    ts = _tile(s, TS_ATTN)
    hd = d // X_HEADS
    scale = hd ** -0.5

    def body(x_ref, wq_ref, wo_ref, kt_ref, v_ref, vec_ref, x2_ref, rpre_ref, q_ref, o_ref):
        xv = x_ref[...]
        q = _dot(xv, wq_ref[...]).astype(BF16)
        q_ref[...] = q
        for hh in range(X_HEADS):
            cs = slice(hh * hd, (hh + 1) * hd)
            p = _softmax_rows(_dot(q[:, cs], kt_ref[cs, :]) * scale)
            o_ref[:, cs] = _dot(p, v_ref[:, cs]).astype(BF16)
        rpre = ALPHA * xv + _dot(o_ref[...], wo_ref[...])
        rpre_ref[...] = rpre
        x2_ref[...] = _ln_fwd(rpre, vec_ref[V_G + 1:V_G + 2, :], vec_ref[V_B + 1:V_B + 2, :])

    tile = pl.BlockSpec((ts, d), lambda t: (t, 0))
    f32o = jax.ShapeDtypeStruct((s, d), F32)
    bfo = jax.ShapeDtypeStruct((s, d), BF16)
    consts = (wq, wo, kt, vv, vec)
    return pl.pallas_call(
        body, name=name, grid=(s // ts,),
        in_specs=[tile] + [_const_spec(c.shape) for c in consts],
        out_specs=[tile] * 4, out_shape=[f32o, f32o, bfo, bfo],
        compiler_params=_cparams(("parallel",)),
    )(x1, *consts)


def _attn_bwd(dx2, rpre, q, wqt, wot, kk, kt, vt, vec, name):
    s, d = dx2.shape
    ts = _tile(s, TS_ATTN)
    nm = kk.shape[0]
    hd = d // X_HEADS
    scale = hd ** -0.5

    def body(dx2_ref, rpre_ref, q_ref, wqt_ref, wot_ref, k_ref, kt_ref, vt_ref, vec_ref,
             dx1_ref, dq_ref, dr_ref, dk_ref, dv_ref, ln_ref):
        @pl.when(pl.program_id(0) == 0)
        def _():
            for ref in (dk_ref, dv_ref, ln_ref):
                ref[...] = jnp.zeros_like(ref)

        dyv = dx2_ref[...]
        dr, dyy = _ln_bwd(dyv, rpre_ref[...], vec_ref[V_G + 1:V_G + 2, :])
        ln_ref[0:1, :] += _colsum(dyy)
        ln_ref[1:2, :] += _colsum(dyv)
        dr_ref[...] = dr.astype(BF16)
        do = _dot(dr, wot_ref[...])
        q = q_ref[...]
        for hh in range(X_HEADS):
            cs = slice(hh * hd, (hh + 1) * hd)
            p = _softmax_rows(_dot(q[:, cs], kt_ref[cs, :]) * scale)
            dp = _dot(do[:, cs], vt_ref[cs, :])
            ds = p * (dp - jnp.sum(dp * p, axis=-1, keepdims=True)) * scale
            dq_ref[:, cs] = _dot(ds, k_ref[:, cs]).astype(BF16)
            dk_ref[:, cs] += _dot_tn(ds, q[:, cs])
            dv_ref[:, cs] += _dot_tn(p, do[:, cs])
        dx1_ref[...] = ALPHA * dr + _dot(dq_ref[...], wqt_ref[...])

    tile = pl.BlockSpec((ts, d), lambda t: (t, 0))
    consts = (wqt, wot, kk, kt, vt, vec)
    return pl.pallas_call(
        body, name=name, grid=(s // ts,),
        in_specs=[tile, tile, tile] + [_const_spec(c.shape) for c in consts],
        out_specs=[tile, tile, tile, _acc_spec((nm, d)), _acc_spec((nm, d)), _acc_spec((2, d))],
        out_shape=[jax.ShapeDtypeStruct((s, d), F32), jax.ShapeDtypeStruct((s, d), BF16),
                   jax.ShapeDtypeStruct((s, d), BF16), jax.ShapeDtypeStruct((nm, d), F32),
                   jax.ShapeDtypeStruct((nm, d), F32), jax.ShapeDtypeStruct((2, d), F32)],
        compiler_params=_cparams(("arbitrary",)),
    )(dx2, rpre, q, *consts)


def _ffn_out(x2, hgu, wd, vec, name):
    s, d = x2.shape
    ff = wd.shape[0]
    ts = _tile(s, TS_FFN)

    def body(x_ref, hgu_ref, wd_ref, vec_ref, x3_ref, rpre_ref, act_ref):
        hg = hgu_ref[:, 0:ff]
        act = hg * _sigmoid(hg) * hgu_ref[:, ff:2 * ff]
        act_ref[...] = act.astype(BF16)
        rpre = ALPHA * x_ref[...] + _dot(act, wd_ref[...])
        rpre_ref[...] = rpre
        x3_ref[...] = _ln_fwd(rpre, vec_ref[V_G + 2:V_G + 3, :], vec_ref[V_B + 2:V_B + 3, :])

    tile = pl.BlockSpec((ts, d), lambda t: (t, 0))
    return pl.pallas_call(
        body, name=name, grid=(s // ts,),
        in_specs=[tile, pl.BlockSpec((ts, 2 * ff), lambda t: (t, 0)), _const_spec(wd.shape), _const_spec(vec.shape)],
        out_specs=[tile, tile, pl.BlockSpec((ts, ff), lambda t: (t, 0))],
        out_shape=[jax.ShapeDtypeStruct((s, d), F32), jax.ShapeDtypeStruct((s, d), F32),
                   jax.ShapeDtypeStruct((s, ff), BF16)],
        compiler_params=_cparams(("parallel",)),
    )(x2, hgu, wd, vec)


def _ffn_bwd(dy, rpre, hgu, wdt, wgt, wut, vec, name, comm=()):
    s, d = dy.shape
    ff = wgt.shape[0]
    ts = _tile(s, TS_FFN)

    def body(dy_ref, rpre_ref, hgu_ref, wdt_ref, wgt_ref, wut_ref, vec_ref, dx_ref, dr_ref, dhgu_ref, ln_ref):
        @pl.when(pl.program_id(0) == 0)
        def _():
            ln_ref[...] = jnp.zeros_like(ln_ref)

        dyv = dy_ref[...]
        dr, dyy = _ln_bwd(dyv, rpre_ref[...], vec_ref[V_G + 2:V_G + 3, :])
        ln_ref[0:1, :] += _colsum(dyy)
        ln_ref[1:2, :] += _colsum(dyv)
        dr_ref[...] = dr.astype(BF16)
        dact = _dot(dr, wdt_ref[...])
        hg = hgu_ref[:, 0:ff]
        hu = hgu_ref[:, ff:2 * ff]
        sg = _sigmoid(hg)
        dhg = dact * hu * (sg * (1.0 + hg * (1.0 - sg)))
        dhu = dact * hg * sg
        dhgu_ref[:, 0:ff] = dhg.astype(BF16)
        dhgu_ref[:, ff:2 * ff] = dhu.astype(BF16)
        dx_ref[...] = ALPHA * dr + _dot(dhg, wgt_ref[...]) + _dot(dhu, wut_ref[...])

    tile = pl.BlockSpec((ts, d), lambda t: (t, 0))
    wide = pl.BlockSpec((ts, 2 * ff), lambda t: (t, 0))
    consts = (wdt, wgt, wut, vec)
    return _pallas(
        body, name=name, grid=(s // ts,),
        in_specs=[tile, tile, wide] + [_const_spec(c.shape) for c in consts],
        out_specs=[tile, tile, wide, _acc_spec((2, d))],
        out_shape=[jax.ShapeDtypeStruct((s, d), F32), jax.ShapeDtypeStruct((s, d), BF16),
                   jax.ShapeDtypeStruct((s, 2 * ff), BF16), jax.ShapeDtypeStruct((2, d), F32)],
        semantics=("arbitrary",), args=(dy, rpre, hgu, *consts), comm=comm)


def _loss_head(y, target, name):
    s, d = y.shape
    ts = _tile(s, TS_MM)

    def body(y_ref, t_ref, loss_ref, dy_ref):
        @pl.when(pl.program_id(0) == 0)
        def _():
            loss_ref[...] = jnp.zeros_like(loss_ref)

        err = y_ref[...] - t_ref[...]
        dy_ref[...] = err / d
        per_token = jnp.mean(err * err, axis=-1, keepdims=True)
        loss_ref[...] += 0.5 * jnp.sum(per_token, axis=0, keepdims=True)

    tile = pl.BlockSpec((ts, d), lambda t: (t, 0))
    return pl.pallas_call(
        body, name=name, grid=(s // ts,),
        in_specs=[tile, tile],
        out_specs=[_acc_spec((1, 1)), tile],
        out_shape=[jax.ShapeDtypeStruct((1, 1), F32), jax.ShapeDtypeStruct((s, d), F32)],
        compiler_params=_cparams(("arbitrary",)),
    )(y, target)


SHARD_AXIS = {"w_in": 1, "pool_w": 1, "lru_w_out": 0, "sconv_w_out": 0, "w_mix_out": 0,
              "xa_w_q": 0, "xa_w_k": 0, "xa_w_v": 0, "xa_w_o": 0,
              "ffn_w_gate": 0, "ffn_w_up": 0, "ffn_w_down": 0,
              "lru_conv_w": 1, "sconv_w": 1, "ln_g": 1, "ln_b": 1}
STORED_TRANSPOSED = ("ffn_w_gate", "ffn_w_up")
GROUP_IN = ("w_in",)
GROUP_MIXER = ("pool_w", "lru_w_out", "sconv_w_out", "w_mix_out")
GROUP_ATTN = ("xa_w_q", "xa_w_k", "xa_w_v", "xa_w_o")
GROUP_FFN = ("ffn_w_gate", "ffn_w_up", "ffn_w_down")
GROUP_VECTORS = ("lru_conv_w", "sconv_w", "ln_g", "ln_b")
REPLICATED = ("b_in", "pool_scale", "lru_conv_b", "lru_w_r", "lru_b_r", "lru_w_i", "lru_b_i", "lru_lambda")
WEIGHTS = ("w_in", "b_in", "pool_w", "pool_scale", "lru_conv_w", "lru_conv_b", "lru_w_r", "lru_b_r", "lru_w_i",
           "lru_b_i", "lru_lambda", "lru_w_out", "sconv_w", "sconv_w_out", "w_mix_out", "xa_w_q", "xa_w_k",
           "xa_w_v", "xa_w_o", "ffn_w_gate", "ffn_w_up", "ffn_w_down", "ln_g", "ln_b")


def _pack(arrs, width, lead=0, row_multiple=ROW_PAD):
    head = arrs[0].shape[:lead]
    flat = jnp.concatenate([a.reshape(head + (-1,)) for a in arrs], axis=lead)
    n = flat.shape[-1]
    chunk = width * row_multiple
    total = -(-n // chunk) * chunk
    if total != n:
        flat = jnp.pad(flat, [(0, 0)] * lead + [(0, total - n)])
    return flat.reshape(head + (total // width, width))


def _unpack(buf, shapes, lead=0):
    head = buf.shape[:lead]
    flat = buf.reshape(head + (-1,))
    out, off = [], 0
    for shp in shapes:
        n = math.prod(shp)
        out.append(flat[..., off:off + n].reshape(head + tuple(shp)))
        off += n
    return out


def _split8(a, axis):
    shp = a.shape
    a = a.reshape(shp[:axis] + (N_DEV, shp[axis] // N_DEV) + shp[axis + 1:])
    return jnp.moveaxis(a, axis, 0)


def _join8(a, axis):
    a = jnp.moveaxis(a, 0, axis)
    shp = a.shape
    return a.reshape(shp[:axis] + (shp[axis] * shp[axis + 1],) + shp[axis + 2:])


def _t(a):
    return jnp.swapaxes(a, -1, -2)


def _stored(name, a):
    return _t(a) if name in STORED_TRANSPOSED else a


def kernel(x, mem, w_in, b_in, pool_w, pool_scale, lru_conv_w, lru_conv_b, lru_w_r, lru_b_r, lru_w_i, lru_b_i, lru_lambda, lru_w_out, sconv_w, sconv_w_out, w_mix_out, xa_w_q, xa_w_k, xa_w_v, xa_w_o, ffn_w_gate, ffn_w_up, ffn_w_down, ln_g, ln_b, loss_target, m_w_in, m_b_in, m_pool_w, m_pool_scale, m_lru_conv_w, m_lru_conv_b, m_lru_w_r, m_lru_b_r, m_lru_w_i, m_lru_b_i, m_lru_lambda, m_lru_w_out, m_sconv_w, m_sconv_w_out, m_w_mix_out, m_xa_w_q, m_xa_w_k, m_xa_w_v, m_xa_w_o, m_ffn_w_gate, m_ffn_w_up, m_ffn_w_down, m_ln_g, m_ln_b, v_w_in, v_b_in, v_pool_w, v_pool_scale, v_lru_conv_w, v_lru_conv_b, v_lru_w_r, v_lru_b_r, v_lru_w_i, v_lru_b_i, v_lru_lambda, v_lru_w_out, v_sconv_w, v_sconv_w_out, v_w_mix_out, v_xa_w_q, v_xa_w_k, v_xa_w_v, v_xa_w_o, v_ffn_w_gate, v_ffn_w_up, v_ffn_w_down, v_ln_g, v_ln_b):
    args = dict(locals())
    w = {n: args[n] for n in WEIGHTS}
    mom_m = {n: args["m_" + n] for n in WEIGHTS}
    mom_v = {n: args["v_" + n] for n in WEIGHTS}
    depth = w_in.shape[0]
    s, d = x.shape[1], x.shape[2]
    nm = mem.shape[1]
    ff = ffn_w_gate.shape[2] * N_DEV
    xs = x.reshape(s, d)
    mems = mem.reshape(nm, d)
    target = loss_target.reshape(s, d)

    def shard(t, n, l):
        return _stored(n, t[n][l])

    def pack_shards(t, names, l, dtype=None):
        arrs = [shard(t, n, l) for n in names]
        return _pack([a if dtype is None else a.astype(dtype) for a in arrs], d)

    def unpack_gathered(buf, names):
        pieces = _unpack(buf, [shard(w, n, 0).shape for n in names], lead=1)
        return {n: (p if n == "w_in" else _join8(p, SHARD_AXIS[n])) for n, p in zip(names, pieces)}

    def layer_vec(l, fw):
        vec = jnp.zeros((V_ROWS, d), F32)
        vec = vec.at[V_PSCALE].set(pool_scale[l]).at[V_CW:V_CW + LRU_CONV].set(fw["lru_conv_w"])
        vec = vec.at[V_CB].set(lru_conv_b[l]).at[V_BR].set(lru_b_r[l]).at[V_BI].set(lru_b_i[l])
        vec = vec.at[V_LAM].set(lru_lambda[l]).at[V_SW:V_SW + SCONV_K].set(fw["sconv_w"])
        return vec.at[V_G:V_G + 3].set(fw["ln_g"]).at[V_B:V_B + 3].set(fw["ln_b"])

    def layer_params(l, fw):
        return dict(
            vec=layer_vec(l, fw), wint=_t(fw["w_in"]).reshape(1, 8 * d, d),
            pw=fw["pool_w"], pwt=_t(fw["pool_w"]),
            wr=lru_w_r[l].astype(BF16), wi=lru_w_i[l].astype(BF16),
            wrt=_t(lru_w_r[l]).astype(BF16), wit=_t(lru_w_i[l]).astype(BF16),
            wlo=fw["lru_w_out"], wlot=_t(fw["lru_w_out"]),
            wsc=fw["sconv_w_out"], wsct=_t(fw["sconv_w_out"]),
            wmix=fw["w_mix_out"], wmixt=_t(fw["w_mix_out"]),
            wq=fw["xa_w_q"], wqt=_t(fw["xa_w_q"]), wo=fw["xa_w_o"], wot=_t(fw["xa_w_o"]),
            wkv=jnp.stack([fw["xa_w_k"], fw["xa_w_v"]]),
            wgu=jnp.concatenate([_t(fw["ffn_w_gate"]).reshape(d, 2, ff // 2),
                                 _t(fw["ffn_w_up"]).reshape(d, 2, ff // 2)], axis=1).transpose(1, 0, 2),
            wgt=fw["ffn_w_gate"], wut=fw["ffn_w_up"],
            wd=fw["ffn_w_down"], wdt=_t(fw["ffn_w_down"]))

    later = GROUP_ATTN + GROUP_FFN
    fw0 = unpack_gathered(_all_gather(pack_shards(w, GROUP_IN + GROUP_MIXER, 0, BF16), "gather_mixer_0"),
                          GROUP_IN + GROUP_MIXER)
    vectors = _all_gather(_pack([shard(w, n, l) for l in range(depth) for n in GROUP_VECTORS], d), "gather_vectors")
    vec_pieces = _unpack(vectors, [shard(w, n, l).shape for l in range(depth) for n in GROUP_VECTORS], lead=1)
    fvec = [{n: _join8(vec_pieces[l * len(GROUP_VECTORS) + k], SHARD_AXIS[n]) for k, n in enumerate(GROUP_VECTORS)}
            for l in range(depth)]

    layers, saved = [], []
    cur = xs
    fw_next = None
    for l in range(depth):
        fw = dict(fw0 if l == 0 else fw_next)
        fw.update(fvec[l])
        comm = []
        if l == 0:
            comm.append(("gather", pack_shards(w, later, 0, BF16)))
        if l + 1 < depth:
            comm.append(("gather", pack_shards(w, GROUP_IN + GROUP_MIXER + later, l + 1, BF16)))
        z, x1, rpre1, h, yp, yl, yc, merged, e, *got = _mixer_fwd(
            cur, fw["w_in"], b_in[l].reshape(1, 8 * d), fw["pool_w"], lru_w_r[l].astype(BF16),
            lru_w_i[l].astype(BF16), fw["lru_w_out"], fw["sconv_w_out"], fw["w_mix_out"],
            layer_vec(l, fw), f"mixer_fwd_{l}", comm=comm)
        if l == 0:
            fw.update(unpack_gathered(got.pop(0), later))
        if l + 1 < depth:
            fw_next = unpack_gathered(got.pop(0), GROUP_IN + GROUP_MIXER + later)
        p = layer_params(l, fw)
        kv = _mm(mems, p["wkv"], f"kv_{l}")[0]
        kk = kv[:, :d].astype(BF16)
        vv = kv[:, d:].astype(BF16)
        x2, rpre2, q, o = _attn_fwd(x1, p["wq"], p["wo"], _t(kk), vv, p["vec"], f"attn_fwd_{l}")
        hgu = _mm(x2, p["wgu"], f"ffn_in_{l}")[0]
        x3, rpre3, act = _ffn_out(x2, hgu, p["wd"], p["vec"], f"ffn_out_{l}")
        layers.append(p)
        saved.append(dict(x0=cur, z=z, x1=x1, rpre1=rpre1, h=h, yp=yp, yl=yl, yc=yc, merged=merged, e=e,
                          kk=kk, vv=vv, x2=x2, rpre2=rpre2, q=q, o=o, hgu=hgu, rpre3=rpre3, act=act))
        cur = x3

    loss_part, dcur = _loss_head(cur, target, "loss_head")
    loss = lax.psum(loss_part[0, 0], ("x", "y", "c"))

    res = {}

    def slots_of(g, names):
        return _pack([g[n] if n == "w_in" else _split8(g[n], SHARD_AXIS[n]) for n in names], d, lead=1)

    def update(received, names, l, tag):
        outs = _adamw_sum(received, *[pack_shards(t, names, l) for t in (w, mom_m, mom_v)], f"adamw_{tag}_{l}")
        shapes = [shard(w, n, l).shape for n in names]
        for n, *parts in zip(names, *[_unpack(o, shapes) for o in outs]):
            res[(n, l)] = [_stored(n, a) for a in parts]

    def settle(exchanges, got):
        for (names, l, tag, _), received in zip(exchanges, got):
            update(received, names, l, tag)

    grads = [None] * depth
    for l in reversed(range(depth)):
        p, sv = layers[l], saved[l]
        g = {}
        dx2, dr3, dhgu, ln3 = _ffn_bwd(dcur, sv["rpre3"], sv["hgu"], p["wdt"], p["wgt"], p["wut"], p["vec"],
                                       f"ffn_bwd_{l}")
        g["ffn_w_down"] = _mm_tn(sv["act"], dr3, d, f"g_wd_{l}")[0][0]
        dwgu = _mm_tn(dhgu, sv["x2"], d, f"g_wgu_{l}", tk=ff)[0][0]
        g["ffn_w_gate"], g["ffn_w_up"] = dwgu[:ff], dwgu[ff:]
        dx1, dq, dr2, dk, dv, ln2 = _attn_bwd(dx2, sv["rpre2"], sv["q"], p["wqt"], p["wot"], sv["kk"], _t(sv["kk"]),
                                              _t(sv["vv"]), p["vec"], f"attn_bwd_{l}")
        g["xa_w_o"] = _mm_tn(sv["o"], dr2, d, f"g_wo_{l}")[0][0]
        g["xa_w_q"] = _mm_tn(sv["x1"], dq, d, f"g_wq_{l}")[0][0]
        dwkv = _mm_tn(mems, jnp.concatenate([dk, dv], axis=1), d, f"g_wkv_{l}")[0]
        g["xa_w_k"], g["xa_w_v"] = dwkv[0], dwkv[1]
        ffn_slots = slots_of(g, GROUP_FFN)
        (dz, dr1, dyl, dyc, accs, dbin, g["pool_w"], g["lru_w_r"], g["lru_w_i"], received) = _mixer_bwd(
            dx1, sv["rpre1"], sv["z"], sv["h"], sv["yp"], sv["yl"], sv["yc"], p["pw"], p["pwt"], p["wr"], p["wi"],
            p["wrt"], p["wit"], p["wlot"], p["wsct"], p["wmixt"], p["vec"], f"mixer_bwd_{l}",
            comm=[("scatter", ffn_slots)])
        update(received, GROUP_FFN, l, "ffn")
        g["w_mix_out"] = _mm_tn(sv["merged"], dr1, d, f"g_wmix_{l}")[0][0]
        g["lru_w_out"] = _mm_tn(sv["h"], dyl, d, f"g_wlo_{l}")[0][0]
        g["sconv_w_out"] = _mm_tn(sv["e"], dyc, d, f"g_wsc_{l}")[0][0]
        g["b_in"] = dbin[0]
        g["pool_scale"] = accs[A_PSCALE]
        g["lru_conv_w"] = accs[A_CW:A_CW + LRU_CONV]
        g["lru_conv_b"] = accs[A_CB]
        g["lru_b_r"] = accs[A_BR]
        g["lru_b_i"] = accs[A_BI]
        g["lru_lambda"] = accs[A_SP] * (-_sigmoid(-lru_lambda[l]))
        g["sconv_w"] = accs[A_SW:A_SW + SCONV_K]
        g["ln_g"] = jnp.stack([accs[A_G], ln2[0], ln3[0]])
        g["ln_b"] = jnp.stack([accs[A_B], ln2[1], ln3[1]])
        grads[l] = g
        behind_win = [(GROUP_ATTN, l, "attn", slots_of(g, GROUP_ATTN)),
                      (GROUP_MIXER + GROUP_VECTORS, l, "mixer", slots_of(g, GROUP_MIXER + GROUP_VECTORS))]
        g["w_in"], *got = _mm_tn(sv["x0"], dz, d, f"g_win_{l}", comm=[("scatter", t[3]) for t in behind_win])
        settle(behind_win, got)
        behind_dx = [(GROUP_IN, l, "w_in", slots_of(g, GROUP_IN))]
        comm = [("scatter", behind_dx[0][3])]
        if l == 0:
            comm.append(("gather", _pack([jnp.stack([grads[k][n] for k in range(depth)]) for n in REPLICATED], d)))
        dcur, *got = _mm(dz, p["wint"], f"dx_{l}", add=dr1, add_scale=ALPHA, comm=comm)
        settle(behind_dx, got[:1])
        if l == 0:
            outs = _adamw_sum(got[1], *[_pack([t[n] for n in REPLICATED], d) for t in (w, mom_m, mom_v)],
                              "adamw_replicated")
            rep_shapes = [w[n].shape for n in REPLICATED]
            final = {n: parts for n, *parts in zip(REPLICATED, *[_unpack(o, rep_shapes) for o in outs])}
    grad_x = dcur.reshape(x.shape)

    for n in WEIGHTS:
        if n not in final:
            final[n] = [jnp.stack([res[(n, l)][k] for l in range(depth)]) for k in range(4)]
    return (loss, grad_x, *[final[n][0] for n in WEIGHTS], *[final[n][1] for n in WEIGHTS],
            *[final[n][2] for n in WEIGHTS], *[final[n][3] for n in WEIGHTS])
```

```python
import functools
import math

import jax
import jax.numpy as jnp
from jax import lax
from jax.experimental import pallas as pl
from jax.experimental.pallas import tpu as pltpu

F32 = jnp.float32
BF16 = jnp.bfloat16
MESH = pl.DeviceIdType.MESH

N_DEV = 8
LRU_HEADS = 8
LRU_CONV = 4
LRU_C = 8.0
SCONV_K = 3
POOL_WINDOWS = (2, 4, 8, 16)
X_HEADS = 4
DEPTH = 2
ALPHA = (2 * DEPTH) ** 0.25
LN_EPS = 1e-5
ADAM_LR = 0.001
ADAM_B1 = 0.9
ADAM_B2 = 0.999
ADAM_EPS = 1e-08
ADAM_WD = 0.01
ADAM_STEP = 10

HALO = 16
SUBLANES = 8
VMEM_LIMIT = 56 * 1024 * 1024
TS_MIXER = 128
TS_ATTN = 512
TS_FFN = 256
TS_MM = 1024
TK_MM = 2048
TR_ADAM = 256
ROW_PAD = 8
TS_MM_TN = 1024

V_PSCALE, V_CW, V_CB, V_BR, V_BI, V_LAM, V_SW, V_G, V_B = 0, 1, 5, 6, 7, 8, 9, 12, 15
V_ROWS = 24
A_PSCALE, A_CW, A_CB, A_BR, A_BI, A_SP, A_SW, A_G, A_B = 0, 1, 5, 6, 7, 8, 9, 12, 13
A_ROWS = 16


def _cparams(sem):
    return pltpu.CompilerParams(dimension_semantics=sem, vmem_limit_bytes=VMEM_LIMIT)


def _tile(n, pref):
    if n <= pref:
        return n
    assert n % pref == 0, (n, pref)
    return pref


def _const_spec(shape):
    nd = len(shape)
    return pl.BlockSpec(shape, lambda *_: (0,) * nd, pipeline_mode=pl.Buffered(1))


def _acc_spec(shape):
    nd = len(shape)
    return pl.BlockSpec(shape, lambda *_: (0,) * nd)


def _dot(a, b):
    return jnp.dot(a.astype(BF16), b.astype(BF16), preferred_element_type=F32)


def _dot_tn(a, b):
    return lax.dot_general(a.astype(BF16), b.astype(BF16), (((0,), (0,)), ((), ())),
                           preferred_element_type=F32)


def _sigmoid(x):
    return 0.5 * jnp.tanh(0.5 * x) + 0.5


def _softplus(y):
    e = jnp.exp(-jnp.abs(y))
    log1p = jnp.where(e < 1e-4, e * (1.0 - e * (0.5 - e * (1.0 / 3.0))), jnp.log(1.0 + e))
    return jnp.maximum(y, 0.0) + log1p


def _ln_fwd(r, g, b):
    mu = jnp.mean(r, axis=-1, keepdims=True)
    xc = r - mu
    var = jnp.mean(xc * xc, axis=-1, keepdims=True)
    return xc * lax.rsqrt(var + LN_EPS) * g + b


def _ln_bwd(dy, r, g):
    mu = jnp.mean(r, axis=-1, keepdims=True)
    xc = r - mu
    var = jnp.mean(xc * xc, axis=-1, keepdims=True)
    rstd = lax.rsqrt(var + LN_EPS)
    yhat = xc * rstd
    dyh = dy * g
    m1 = jnp.mean(dyh, axis=-1, keepdims=True)
    m2 = jnp.mean(dyh * yhat, axis=-1, keepdims=True)
    return rstd * (dyh - m1 - yhat * m2), dy * yhat


def _colsum(a):
    return jnp.sum(a, axis=0, keepdims=True)


def _position():
    return lax.axis_index("x"), lax.axis_index("y"), lax.axis_index("c")


def _gather_copies(x_ref, out_ref, send_sems, recv_sems, local_sem):
    x, y, c = _position()
    me, sibling = (x, y, c), (x, y, 1 - c)
    chips = [(1 - x, y), (x, 1 - y), (1 - x, 1 - y)]

    def slot(px, py, pc):
        return out_ref.at[4 * px + 2 * py + pc]

    def copy(k, block, to, src=None):
        return pltpu.make_async_remote_copy(
            src_ref=slot(*block) if src is None else src, dst_ref=slot(*block),
            send_sem=send_sems.at[k], recv_sem=recv_sems.at[k], device_id=to, device_id_type=MESH)

    mine = pltpu.make_async_copy(x_ref, slot(*me), local_sem)
    first = [copy(0, me, sibling, src=x_ref)]
    first += [copy(1 + j, me, (*chip, c), src=x_ref) for j, chip in enumerate(chips)]
    passed = [copy(4 + j, (*chip, c), sibling) for j, chip in enumerate(chips)]
    over_ici = [copy(1 + j, (*chip, c), me) for j, chip in enumerate(chips)]
    from_sibling = copy(0, sibling, me)
    forwarded = [copy(4 + j, (*chip, 1 - c), me) for j, chip in enumerate(chips)]
    return mine, first, passed, over_ici, from_sibling, forwarded


def _scatter_copies(g_ref, out_ref, send_sems, recv_sems, local_sem):
    x, y, c = _position()
    me = 4 * x + 2 * y + c
    mine = pltpu.make_async_copy(g_ref.at[me], out_ref.at[me], local_sem)
    copies = []
    for k in range(1, N_DEV):
        px = 1 - x if k & 4 else x
        py = 1 - y if k & 2 else y
        pc = 1 - c if k & 1 else c
        copies.append(pltpu.make_async_remote_copy(
            src_ref=g_ref.at[4 * px + 2 * py + pc], dst_ref=out_ref.at[me],
            send_sem=send_sems.at[k - 1], recv_sem=recv_sems.at[k - 1],
            device_id=(px, py, pc), device_id_type=MESH))
    return mine, copies


def _comm_start(kind, *refs):
    if kind == "gather":
        mine, first, _, _, _, _ = _gather_copies(*refs)
        mine.start()
        for cp in first:
            cp.start()
    else:
        mine, copies = _scatter_copies(*refs)
        mine.start()
        for cp in copies:
            cp.start()


def _comm_finish(kind, *refs):
    if kind == "gather":
        mine, first, passed, over_ici, from_sibling, forwarded = _gather_copies(*refs)
        for arrival, forward in zip(over_ici, passed):
            arrival.wait_recv()
            forward.start()
        from_sibling.wait_recv()
        for arrival in forwarded:
            arrival.wait_recv()
        for cp in first + passed:
            cp.wait_send()
        mine.wait()
    else:
        mine, copies = _scatter_copies(*refs)
        for cp in copies:
            cp.wait_recv()
        for cp in copies:
            cp.wait_send()
        mine.wait()


def _comm_out_shape(kind, arr):
    return jax.ShapeDtypeStruct((N_DEV,) + arr.shape if kind == "gather" else arr.shape, arr.dtype)


COMM_SEMAPHORES = [pltpu.SemaphoreType.DMA((7,)), pltpu.SemaphoreType.DMA((7,)), pltpu.SemaphoreType.DMA]


def _pallas(body, *, name, grid, in_specs, out_specs, out_shape, semantics, args, scratch_shapes=(), comm=()):
    in_specs, out_specs, out_shape = list(in_specs), list(out_specs), list(out_shape)
    scratch_shapes = list(scratch_shapes)
    n_in, n_out, n_scr, nc = len(in_specs), len(out_specs), len(scratch_shapes), len(comm)
    if not comm:
        return pl.pallas_call(body, name=name, grid=grid, in_specs=in_specs, out_specs=out_specs, out_shape=out_shape,
                              scratch_shapes=scratch_shapes, compiler_params=_cparams(semantics))(*args)
    kinds = [kind for kind, _ in comm]

    def carrying(*refs):
        ins, rest = refs[:n_in], refs[n_in:]
        cin, rest = rest[:nc], rest[nc:]
        outs, rest = rest[:n_out], rest[n_out:]
        cout, rest = rest[:nc], rest[nc:]
        scr, sems = rest[:n_scr], rest[n_scr:]
        ids = [pl.program_id(ax) for ax in range(len(grid))]
        first = functools.reduce(jnp.logical_and, [i == 0 for i in ids])
        last = functools.reduce(jnp.logical_and, [i == g - 1 for i, g in zip(ids, grid)])
        plans = [(kinds[k], cin[k], cout[k], *sems[3 * k:3 * k + 3]) for k in range(nc)]

        @pl.when(first)
        def _():
            for plan in plans:
                _comm_start(*plan)

        body(*ins, *outs, *scr)

        @pl.when(last)
        def _():
            for plan in plans:
                _comm_finish(*plan)

    hbm = pl.BlockSpec(memory_space=pl.ANY)
    return pl.pallas_call(
        carrying, name=name, grid=grid,
        in_specs=in_specs + [hbm] * nc, out_specs=out_specs + [hbm] * nc,
        out_shape=out_shape + [_comm_out_shape(kind, arr) for kind, arr in comm],
        scratch_shapes=scratch_shapes + COMM_SEMAPHORES * nc,
        compiler_params=_cparams(("arbitrary",) * len(grid)),
    )(*args, *[arr for _, arr in comm])


def _all_gather(xs, name):
    def body(x_ref, out_ref, send_sems, recv_sems, local_sem):
        _comm_start("gather", x_ref, out_ref, send_sems, recv_sems, local_sem)
        _comm_finish("gather", x_ref, out_ref, send_sems, recv_sems, local_sem)

    return pl.pallas_call(
        body, name=name, out_shape=_comm_out_shape("gather", xs),
        in_specs=[pl.BlockSpec(memory_space=pl.ANY)], out_specs=pl.BlockSpec(memory_space=pl.ANY),
        scratch_shapes=COMM_SEMAPHORES,
    )(xs)


def _adamw_sum(parts, w, m, v, name):
    _, rows, width = parts.shape
    tr = max(t for t in range(SUBLANES, min(rows, TR_ADAM) + 1, SUBLANES) if rows % t == 0)
    c1 = 1.0 - ADAM_B1 ** ADAM_STEP
    c2 = 1.0 - ADAM_B2 ** ADAM_STEP

    def body(p_ref, w_ref, m_ref, v_ref, g_ref, d_ref, nm_ref, nv_ref):
        g = p_ref[0]
        for k in range(1, N_DEV):
            g = g + p_ref[k]
        nm = ADAM_B1 * m_ref[...] + (1.0 - ADAM_B1) * g
        nv = ADAM_B2 * v_ref[...] + (1.0 - ADAM_B2) * (g * g)
        m_hat = nm / c1
        v_hat = nv / c2
        g_ref[...] = g
        d_ref[...] = -ADAM_LR * (m_hat / (jnp.sqrt(v_hat) + ADAM_EPS) + ADAM_WD * w_ref[...])
        nm_ref[...] = nm
        nv_ref[...] = nv

    spec = pl.BlockSpec((tr, width), lambda i: (i, 0))
    out = jax.ShapeDtypeStruct((rows, width), F32)
    return pl.pallas_call(
        body, name=name, grid=(rows // tr,),
        in_specs=[pl.BlockSpec((N_DEV, tr, width), lambda i: (0, i, 0)), spec, spec, spec],
        out_specs=[spec, spec, spec, spec], out_shape=[out, out, out, out],
        compiler_params=_cparams(("parallel",)),
    )(parts, w, m, v)


def _mm(a, wb, name, bias=None, add=None, add_scale=1.0, out_dtype=F32, tm=None, comm=()):
    m, k = a.shape
    nb, k2, tn = wb.shape
    assert k == k2
    tm = _tile(m, TS_MM if tm is None else tm)
    tk = _tile(k, TK_MM)
    nk = k // tk

    def body(*refs):
        a_ref, w_ref = refs[0], refs[1]
        pos = 2
        b_ref = add_ref = None
        if bias is not None:
            b_ref = refs[pos]
            pos += 1
        if add is not None:
            add_ref = refs[pos]
            pos += 1
        o_ref = refs[pos]

        def finish(r):
            if b_ref is not None:
                r = r + b_ref[...]
            if add_ref is not None:
                r = r + add_scale * add_ref[...]
            o_ref[...] = r.astype(o_ref.dtype)

        if nk == 1:
            finish(_dot(a_ref[...], w_ref[...]))
            return
        acc_ref = refs[pos + 1]
        kk = pl.program_id(2)

        @pl.when(kk == 0)
        def _():
            acc_ref[...] = jnp.zeros_like(acc_ref)

        acc_ref[...] += _dot(a_ref[...], w_ref[...])

        @pl.when(kk == nk - 1)
        def _():
            finish(acc_ref[...])

    in_specs = [pl.BlockSpec((tm, tk), lambda j, i, kk: (i, kk)),
                pl.BlockSpec((None, tk, tn), lambda j, i, kk: (j, kk, 0))]
    args = [a, wb]
    if bias is not None:
        in_specs.append(pl.BlockSpec((1, tn), lambda j, i, kk: (0, j)))
        args.append(bias)
    if add is not None:
        in_specs.append(pl.BlockSpec((tm, tn), lambda j, i, kk: (i, j)))
        args.append(add)
    return _pallas(
        body, name=name, grid=(nb, m // tm, nk),
        in_specs=in_specs,
        out_specs=[pl.BlockSpec((tm, tn), lambda j, i, kk: (i, j))],
        out_shape=[jax.ShapeDtypeStruct((m, nb * tn), out_dtype)],
        scratch_shapes=[pltpu.VMEM((tm, tn), F32)] if nk > 1 else [],
        semantics=("parallel", "parallel", "arbitrary"), args=args, comm=comm)


def _mm_tn(a, b, tn, name, tk=None, comm=()):
    s, k = a.shape
    s2, n = b.shape
    assert s == s2 and n % tn == 0
    nb = n // tn
    ts = _tile(s, TS_MM_TN)
    tk = k if tk is None else tk
    assert k % tk == 0

    def body(a_ref, b_ref, o_ref):
        @pl.when(pl.program_id(2) == 0)
        def _():
            o_ref[...] = jnp.zeros_like(o_ref)

        o_ref[...] += _dot_tn(a_ref[...], b_ref[...])

    return _pallas(
        body, name=name, grid=(nb, k // tk, s // ts),
        in_specs=[pl.BlockSpec((ts, tk), lambda j, kb, i: (i, kb)),
                  pl.BlockSpec((ts, tn), lambda j, kb, i: (i, j))],
        out_specs=[pl.BlockSpec((None, tk, tn), lambda j, kb, i: (j, kb, 0))],
        out_shape=[jax.ShapeDtypeStruct((nb, k, tn), F32)],
        semantics=("parallel", "parallel", "arbitrary"), args=(a, b), comm=comm)


def _scan_fwd(a_ref, b_ref, h_ref, carry_ref, ts):
    rowid = lax.broadcasted_iota(jnp.int32, (SUBLANES, 1), 0)

    def group(gi, hprev):
        r0 = pl.multiple_of(gi * SUBLANES, SUBLANES)
        a = a_ref[pl.ds(r0, SUBLANES), :]
        b = b_ref[pl.ds(r0, SUBLANES), :]
        for d in (1, 2, 4):
            a_sh = jnp.where(rowid >= d, pltpu.roll(a, d, 0), 1.0)
            b_sh = jnp.where(rowid >= d, pltpu.roll(b, d, 0), 0.0)
            b = a * b_sh + b
            a = a * a_sh
        hh = a * hprev + b
        h_ref[pl.ds(r0, SUBLANES), :] = hh
        return hh[SUBLANES - 1:SUBLANES, :]

    last = lax.fori_loop(0, ts // SUBLANES, group, carry_ref[0:1, :])
    carry_ref[0:1, :] = last


def _scan_rev(c_ref, b_ref, g_ref, carry_ref, ts):
    rowid = lax.broadcasted_iota(jnp.int32, (SUBLANES, 1), 0)
    ng = ts // SUBLANES

    def group(gi, gnext):
        r0 = pl.multiple_of((ng - 1 - gi) * SUBLANES, SUBLANES)
        c = c_ref[pl.ds(r0, SUBLANES), :]
        b = b_ref[pl.ds(r0, SUBLANES), :]
        for d in (1, 2, 4):
            keep = rowid < SUBLANES - d
            c_sh = jnp.where(keep, pltpu.roll(c, SUBLANES - d, 0), 1.0)
            b_sh = jnp.where(keep, pltpu.roll(b, SUBLANES - d, 0), 0.0)
            b = c * b_sh + b
            c = c * c_sh
        gg = c * gnext + b
        g_ref[pl.ds(r0, SUBLANES), :] = gg
        return gg[0:1, :]

    first = lax.fori_loop(0, ng, group, carry_ref[0:1, :])
    carry_ref[0:1, :] = first


def _heads_dot(v, w_ref, heads):
    hd = v.shape[1] // heads
    return jnp.concatenate([_dot(v[:, h * hd:(h + 1) * hd], w_ref[h]) for h in range(heads)], axis=1)


def _past(ext, sh, ts):
    if sh == 0:
        return ext[HALO:HALO + ts]
    return pltpu.roll(ext, sh, 0)[HALO:HALO + ts]


def _future(ext, sh, ts):
    if sh == 0:
        return ext[0:ts]
    return pltpu.roll(ext, ts + HALO - sh, 0)[0:ts]


def _one_minus_sq(a, log_a):
    x = 2.0 * log_a
    series = -x * (1.0 + x * (0.5 + x * (1.0 / 6.0 + x * (1.0 / 24.0))))
    return jnp.where(x > -0.02, series, 1.0 - a * a)


def _mixer_recompute(i, ts, d, z_ref, zh_ref, vec_ref, wr_ref, wi_ref, between=lambda: None):
    first = i == 0

    def zc(k):
        return z_ref[:, k * d:(k + 1) * d]

    def with_history(k):
        return jnp.concatenate([jnp.where(first, 0.0, zh_ref[:, k * d:(k + 1) * d]), zc(k)], axis=0)

    tglob = i * ts + lax.broadcasted_iota(jnp.int32, (ts, 1), 0)
    pext = with_history(0)
    dg = d // len(POOL_WINDOWS)
    ps = []
    for g, win in enumerate(POOL_WINDOWS):
        e = pext[:, g * dg:(g + 1) * dg]
        sm = e
        sh = 1
        while sh < win:
            sm = sm + pltpu.roll(sm, sh, 0)
            sh *= 2
        inv_cnt = 1.0 / jnp.minimum(tglob + 1, win).astype(F32)
        ps.append(sm[HALO:HALO + ts] * inv_cnt - e[HALO:HALO + ts])
        between()
    p = jnp.concatenate(ps, axis=1)
    lext = with_history(1)
    v = vec_ref[V_CB:V_CB + 1, :]
    for j in range(LRU_CONV):
        v = v + vec_ref[V_CW + j:V_CW + j + 1, :] * _past(lext, LRU_CONV - 1 - j, ts)
    between()
    r = _sigmoid(_heads_dot(v, wr_ref, LRU_HEADS) + vec_ref[V_BR:V_BR + 1, :])
    ig = _sigmoid(_heads_dot(v, wi_ref, LRU_HEADS) + vec_ref[V_BI:V_BI + 1, :])
    between()
    sp = _softplus(-vec_ref[V_LAM:V_LAM + 1, :])
    log_a = -LRU_C * r * sp
    a = jnp.exp(log_a)
    mult = jnp.sqrt(_one_minus_sq(a, log_a))
    between()
    qext = with_history(3) * with_history(4)
    cq = jnp.zeros((ts, d), F32)
    for j in range(SCONV_K):
        cq = cq + vec_ref[V_SW + j:V_SW + j + 1, :] * _past(qext, SCONV_K - 1 - j, ts)
    return dict(p=p, v=v, r=r, ig=ig, sp=sp, a=a, mult=mult, cq=cq, tglob=tglob)


def _halo_index(ts):
    blocks = ts // HALO
    return lambda t: (jnp.maximum(t * blocks - 1, 0), 0)


def _mixer_fwd(x, win, b_in, pw, wr, wi, wlo, wsc, wmix, vec, name, comm=()):
    s, d = x.shape
    ts = _tile(s, TS_MIXER)
    nt = s // ts
    dg = d // len(POOL_WINDOWS)
    nblk = win.shape[0]

    def body(x_ref, xn_ref, win_ref, bin_ref, pw_ref, wr_ref, wi_ref, wlo_ref, wsc_ref, wmix_ref, vec_ref,
             z_hbm, x1_ref, rpre_ref, h_ref, ypre_ref, yl_ref, yc_ref, mg_ref, e_ref, p_ref, v_ref, r_ref, ig_ref,
             cq_ref, z_even, z_odd, zhist, a_scr, b_scr, hcarry, z_sem):
        i = pl.program_id(0)

        def project_block(xb, dst, k):
            dst[:, k * d:(k + 1) * d] = _dot(xb, win_ref[k]) + bin_ref[:, k * d:(k + 1) * d]

        def project(tile_ref, dst):
            xb = tile_ref[...].astype(BF16)
            for k in range(nblk):
                project_block(xb, dst, k)

        @pl.when(i == 0)
        def _():
            hcarry[...] = jnp.zeros_like(hcarry)
            zhist[...] = jnp.zeros_like(zhist)
            project(x_ref, z_even)

        def step(zc, zn):
            z_out = pltpu.make_async_copy(zc, z_hbm.at[pl.ds(pl.multiple_of(i * ts, ts), ts), :], z_sem)
            z_out.start()
            xb = xn_ref[...].astype(BF16)
            todo = list(range(nblk))

            def next_block():
                if todo:
                    project_block(xb, zn, todo.pop(0))

            f = _mixer_recompute(i, ts, d, zc, zhist, vec_ref, wr_ref, wi_ref, between=next_block)
            ypre = jnp.concatenate([_dot(f["p"][:, g * dg:(g + 1) * dg], pw_ref[g])
                                    for g in range(len(POOL_WINDOWS))], axis=1)
            yp = ypre * vec_ref[V_PSCALE:V_PSCALE + 1, :]
            a_scr[...] = f["a"]
            b_scr[...] = f["mult"] * (f["ig"] * f["v"])
            _scan_fwd(a_scr, b_scr, h_ref, hcarry, ts)
            while todo:
                next_block()
            yl = _dot(h_ref[...], wlo_ref[...])
            e = zc[:, 2 * d:3 * d] * f["cq"]
            yc = _dot(e, wsc_ref[...])
            merged = (_sigmoid(zc[:, 5 * d:6 * d]) * yp + _sigmoid(zc[:, 6 * d:7 * d]) * yl
                      + _sigmoid(zc[:, 7 * d:8 * d]) * yc)
            rpre = ALPHA * x_ref[...] + _dot(merged, wmix_ref[...])
            x1_ref[...] = _ln_fwd(rpre, vec_ref[V_G:V_G + 1, :], vec_ref[V_B:V_B + 1, :])
            rpre_ref[...] = rpre
            ypre_ref[...] = ypre
            yl_ref[...] = yl
            yc_ref[...] = yc
            mg_ref[...] = merged.astype(BF16)
            e_ref[...] = e.astype(BF16)
            p_ref[...] = f["p"].astype(BF16)
            v_ref[...] = f["v"]
            r_ref[...] = f["r"]
            ig_ref[...] = f["ig"]
            cq_ref[...] = f["cq"]
            zhist[...] = zc[ts - HALO:ts, :]
            z_out.wait()

        parity = lax.rem(i, 2)

        @pl.when(parity == 0)
        def _():
            step(z_even, z_odd)

        @pl.when(parity == 1)
        def _():
            step(z_odd, z_even)

    tile = pl.BlockSpec((ts, d), lambda t: (t, 0))
    f32o = jax.ShapeDtypeStruct((s, d), F32)
    bfo = jax.ShapeDtypeStruct((s, d), BF16)
    consts = (win, b_in, pw, wr, wi, wlo, wsc, wmix, vec)
    return _pallas(
        body, name=name, grid=(nt,),
        in_specs=[tile, pl.BlockSpec((ts, d), lambda t: (jnp.minimum(t + 1, nt - 1), 0))]
        + [_const_spec(c.shape) for c in consts],
        out_specs=[pl.BlockSpec(memory_space=pl.ANY)] + [tile] * 13,
        out_shape=[jax.ShapeDtypeStruct((s, nblk * d), F32), f32o, f32o, f32o, f32o, f32o, f32o, bfo, bfo, bfo,
                   f32o, f32o, f32o, f32o],
        scratch_shapes=[pltpu.VMEM((ts, nblk * d), F32)] * 2 + [pltpu.VMEM((HALO, nblk * d), F32)]
        + [pltpu.VMEM((ts, d), F32)] * 2 + [pltpu.VMEM((SUBLANES, d), F32), pltpu.SemaphoreType.DMA],
        semantics=("arbitrary",), args=(x, x, *consts), comm=comm)


def _mixer_bwd(dx1, rpre, z, h, ypre, yl, yc, pp, vv, rr, ii, cq, pwt, wrt, wit, wlot, wsct, wmixt, vec, name,
               comm=()):
    s, d = dx1.shape
    ts = _tile(s, TS_MIXER)
    nt = s // ts
    dg = d // len(POOL_WINDOWS)
    ng = len(POOL_WINDOWS)

    def body(dx1_ref, rpre_ref, z_ref, h_ref, hh_ref, ypre_ref, yl_ref, yc_ref, p_ref, v_ref, r_ref, ig_ref, cq_ref,
             pwt_ref, wrt_ref, wit_ref, wlot_ref, wsct_ref, wmixt_ref, vec_ref,
             dz_ref, dr_ref, dyl_ref, dyc_ref, acc_ref, dbin_ref, dpw_ref, dwr_ref, dwi_ref,
             c_scr, b_scr, g_scr, gcarry, acarry, dcq_c, dv_c, m_c):
        i = pl.program_id(0)
        t = nt - 1 - i
        hd = d // LRU_HEADS

        @pl.when(i == 0)
        def _():
            for ref in (gcarry, acarry, dcq_c, dv_c, m_c, acc_ref, dbin_ref, dpw_ref, dwr_ref, dwi_ref):
                ref[...] = jnp.zeros_like(ref)

        def vrow(k):
            return vec_ref[k:k + 1, :]

        def zc(k):
            return z_ref[:, k * d:(k + 1) * d]

        def acc(row, val):
            acc_ref[row:row + 1, :] += _colsum(val)

        def with_future(tile_val, carry_ref):
            ext = jnp.concatenate([tile_val, carry_ref[...]], axis=0)
            carry_ref[...] = tile_val[0:HALO, :]
            return ext

        dx1v = dx1_ref[...]
        dr, dyy = _ln_bwd(dx1v, rpre_ref[...], vrow(V_G))
        acc(A_G, dyy)
        acc(A_B, dx1v)
        dr_ref[...] = dr
        dmg = _dot(dr, wmixt_ref[...])
        dzs = [None] * 8
        gates = []
        ypre = ypre_ref[...]
        for k, y in enumerate((ypre * vrow(V_PSCALE), yl_ref[...], yc_ref[...])):
            gk = _sigmoid(zc(5 + k))
            dzs[5 + k] = dmg * y * gk * (1.0 - gk)
            gates.append(gk)
        dyp = dmg * gates[0]
        dyl = dmg * gates[1]
        dyc = dmg * gates[2]
        dyl_ref[...] = dyl.astype(BF16)
        dyc_ref[...] = dyc.astype(BF16)

        de = _dot(dyc, wsct_ref[...])
        dzs[2] = de * cq_ref[...]
        dcq = de * zc(2)
        dcq_ext = with_future(dcq, dcq_c)
        qv = zc(3) * zc(4)
        dq = jnp.zeros((ts, d), F32)
        for j in range(SCONV_K):
            adv = _future(dcq_ext, SCONV_K - 1 - j, ts)
            acc(A_SW + j, adv * qv)
            dq = dq + vrow(V_SW + j) * adv
        dzs[3] = dq * zc(4)
        dzs[4] = dq * zc(3)

        r, ig, v = r_ref[...], ig_ref[...], v_ref[...]
        sp = _softplus(-vrow(V_LAM))
        log_a = -LRU_C * r * sp
        a = jnp.exp(log_a)
        mult = jnp.sqrt(_one_minus_sq(a, log_a))
        c_scr[...] = _future(with_future(a, acarry), 1, ts)
        b_scr[...] = _dot(dyl, wlot_ref[...])
        _scan_rev(c_scr, b_scr, g_scr, gcarry, ts)
        gs = g_scr[...]
        hprev = _past(jnp.concatenate([jnp.where(t == 0, 0.0, hh_ref[...]), h_ref[...]], axis=0), 1, ts)
        iv = ig * v
        dlog_a = gs * hprev * a + gs * iv * (-(a * a) / mult)
        div = gs * mult
        acc(A_SP, dlog_a * (-LRU_C) * r)
        dpre_r = dlog_a * (-LRU_C) * sp * r * (1.0 - r)
        dpre_i = div * v * ig * (1.0 - ig)
        acc(A_BR, dpre_r)
        acc(A_BI, dpre_i)
        dv = div * ig + _heads_dot(dpre_r, wrt_ref, LRU_HEADS) + _heads_dot(dpre_i, wit_ref, LRU_HEADS)
        for hh in range(LRU_HEADS):
            hs = slice(hh * hd, (hh + 1) * hd)
            dwr_ref[hh] += _dot_tn(v[:, hs], dpre_r[:, hs])
            dwi_ref[hh] += _dot_tn(v[:, hs], dpre_i[:, hs])
        acc(A_CB, dv)
        dv_ext = with_future(dv, dv_c)
        zl = zc(1)
        dzl = jnp.zeros((ts, d), F32)
        for j in range(LRU_CONV):
            adv = _future(dv_ext, LRU_CONV - 1 - j, ts)
            acc(A_CW + j, adv * zl)
            dzl = dzl + vrow(V_CW + j) * adv
        dzs[1] = dzl

        p = p_ref[...]
        acc(A_PSCALE, dyp * ypre)
        dyps = dyp * vrow(V_PSCALE)
        for g in range(ng):
            dpw_ref[g] += _dot_tn(p[:, g * dg:(g + 1) * dg], dyps[:, g * dg:(g + 1) * dg])
        dp = jnp.concatenate([_dot(dyps[:, g * dg:(g + 1) * dg], pwt_ref[g]) for g in range(ng)], axis=1)
        tglob = t * ts + lax.broadcasted_iota(jnp.int32, (ts, 1), 0)
        inv_cnts = [1.0 / jnp.minimum(tglob + 1, win).astype(F32) for win in POOL_WINDOWS]
        mm = jnp.concatenate([dp[:, g * dg:(g + 1) * dg] * inv_cnts[g] for g in range(ng)], axis=1)
        m_ext = with_future(mm, m_c)
        dzps = []
        for g, win in enumerate(POOL_WINDOWS):
            cs = slice(g * dg, (g + 1) * dg)
            sm = m_ext[:, cs]
            sh = 1
            while sh < win:
                sm = sm + pltpu.roll(sm, ts + HALO - sh, 0)
                sh *= 2
            dzps.append(sm[0:ts] - dp[:, cs])
        dzs[0] = jnp.concatenate(dzps, axis=1)

        for k in range(8):
            dz_ref[:, k * d:(k + 1) * d] = dzs[k].astype(BF16)
            dbin_ref[:, k * d:(k + 1) * d] += _colsum(dzs[k])

    def rev(tt):
        return (nt - 1 - tt, 0)

    halo = _halo_index(ts)
    tile = pl.BlockSpec((ts, d), rev)
    hspec = pl.BlockSpec((HALO, d), lambda tt: halo(nt - 1 - tt))
    f32o = jax.ShapeDtypeStruct((s, d), F32)
    bfo = jax.ShapeDtypeStruct((s, d), BF16)
    consts = (pwt, wrt, wit, wlot, wsct, wmixt, vec)
    return _pallas(
        body, name=name, grid=(nt,),
        in_specs=[tile, tile, pl.BlockSpec((ts, 8 * d), rev), tile, hspec] + [tile] * 8
        + [_const_spec(c.shape) for c in consts],
        out_specs=[pl.BlockSpec((ts, 8 * d), rev), tile, tile, tile,
                   _acc_spec((A_ROWS, d)), _acc_spec((1, 8 * d)),
                   _acc_spec(pwt.shape), _acc_spec(wrt.shape), _acc_spec(wit.shape)],
        out_shape=[jax.ShapeDtypeStruct((s, 8 * d), BF16), f32o, bfo, bfo,
                   jax.ShapeDtypeStruct((A_ROWS, d), F32), jax.ShapeDtypeStruct((1, 8 * d), F32),
                   jax.ShapeDtypeStruct(pwt.shape, F32), jax.ShapeDtypeStruct(wrt.shape, F32),
                   jax.ShapeDtypeStruct(wit.shape, F32)],
        scratch_shapes=[pltpu.VMEM((ts, d), F32)] * 3 + [pltpu.VMEM((SUBLANES, d), F32)]
        + [pltpu.VMEM((HALO, d), F32)] * 4,
        semantics=("arbitrary",), args=(dx1, rpre, z, h, h, ypre, yl, yc, pp, vv, rr, ii, cq, *consts), comm=comm)


def _softmax_rows(sc):
    mx = jnp.max(sc, axis=-1, keepdims=True)
    ex = jnp.exp(sc - mx)
    return ex * (1.0 / jnp.sum(ex, axis=-1, keepdims=True))


def _attn_fwd(x1, wq, wo, kt, vv, vec, name):
    s, d = x1.shape
    ts = _tile(s, TS_ATTN)
    hd = d // X_HEADS
    scale = hd ** -0.5

    def body(x_ref, wq_ref, wo_ref, kt_ref, v_ref, vec_ref, x2_ref, rpre_ref, q_ref, o_ref):
        xv = x_ref[...]
        q = _dot(xv, wq_ref[...]).astype(BF16)
        q_ref[...] = q
        for hh in range(X_HEADS):
            cs = slice(hh * hd, (hh + 1) * hd)
            p = _softmax_rows(_dot(q[:, cs], kt_ref[cs, :]) * scale)
            o_ref[:, cs] = _dot(p, v_ref[:, cs]).astype(BF16)
        rpre = ALPHA * xv + _dot(o_ref[...], wo_ref[...])
        rpre_ref[...] = rpre
        x2_ref[...] = _ln_fwd(rpre, vec_ref[V_G + 1:V_G + 2, :], vec_ref[V_B + 1:V_B + 2, :])

    tile = pl.BlockSpec((ts, d), lambda t: (t, 0))
    f32o = jax.ShapeDtypeStruct((s, d), F32)
    bfo = jax.ShapeDtypeStruct((s, d), BF16)
    consts = (wq, wo, kt, vv, vec)
    return pl.pallas_call(
        body, name=name, grid=(s // ts,),
        in_specs=[tile] + [_const_spec(c.shape) for c in consts],
        out_specs=[tile] * 4, out_shape=[f32o, f32o, bfo, bfo],
        compiler_params=_cparams(("parallel",)),
    )(x1, *consts)


def _attn_bwd(dx2, rpre, q, wqt, wot, kk, kt, vt, vec, name):
    s, d = dx2.shape
    ts = _tile(s, TS_ATTN)
    nm = kk.shape[0]
    hd = d // X_HEADS
    scale = hd ** -0.5

    def body(dx2_ref, rpre_ref, q_ref, wqt_ref, wot_ref, k_ref, kt_ref, vt_ref, vec_ref,
             dx1_ref, dq_ref, dr_ref, dk_ref, dv_ref, ln_ref):
        @pl.when(pl.program_id(0) == 0)
        def _():
            for ref in (dk_ref, dv_ref, ln_ref):
                ref[...] = jnp.zeros_like(ref)

        dyv = dx2_ref[...]
        dr, dyy = _ln_bwd(dyv, rpre_ref[...], vec_ref[V_G + 1:V_G + 2, :])
        ln_ref[0:1, :] += _colsum(dyy)
        ln_ref[1:2, :] += _colsum(dyv)
        dr_ref[...] = dr.astype(BF16)
        do = _dot(dr, wot_ref[...])
        q = q_ref[...]
        for hh in range(X_HEADS):
            cs = slice(hh * hd, (hh + 1) * hd)
            p = _softmax_rows(_dot(q[:, cs], kt_ref[cs, :]) * scale)
            dp = _dot(do[:, cs], vt_ref[cs, :])
            ds = p * (dp - jnp.sum(dp * p, axis=-1, keepdims=True)) * scale
            dq_ref[:, cs] = _dot(ds, k_ref[:, cs]).astype(BF16)
            dk_ref[:, cs] += _dot_tn(ds, q[:, cs])
            dv_ref[:, cs] += _dot_tn(p, do[:, cs])
        dx1_ref[...] = ALPHA * dr + _dot(dq_ref[...], wqt_ref[...])

    tile = pl.BlockSpec((ts, d), lambda t: (t, 0))
    consts = (wqt, wot, kk, kt, vt, vec)
    return pl.pallas_call(
        body, name=name, grid=(s // ts,),
        in_specs=[tile, tile, tile] + [_const_spec(c.shape) for c in consts],
        out_specs=[tile, tile, tile, _acc_spec((nm, d)), _acc_spec((nm, d)), _acc_spec((2, d))],
        out_shape=[jax.ShapeDtypeStruct((s, d), F32), jax.ShapeDtypeStruct((s, d), BF16),
                   jax.ShapeDtypeStruct((s, d), BF16), jax.ShapeDtypeStruct((nm, d), F32),
                   jax.ShapeDtypeStruct((nm, d), F32), jax.ShapeDtypeStruct((2, d), F32)],
        compiler_params=_cparams(("arbitrary",)),
    )(dx2, rpre, q, *consts)


def _ffn_out(x2, hgu, wd, vec, name):
    s, d = x2.shape
    ff = wd.shape[0]
    ts = _tile(s, TS_FFN)

    def body(x_ref, hgu_ref, wd_ref, vec_ref, x3_ref, rpre_ref, act_ref):
        hg = hgu_ref[:, 0:ff]
        act = hg * _sigmoid(hg) * hgu_ref[:, ff:2 * ff]
        act_ref[...] = act.astype(BF16)
        rpre = ALPHA * x_ref[...] + _dot(act, wd_ref[...])
        rpre_ref[...] = rpre
        x3_ref[...] = _ln_fwd(rpre, vec_ref[V_G + 2:V_G + 3, :], vec_ref[V_B + 2:V_B + 3, :])

    tile = pl.BlockSpec((ts, d), lambda t: (t, 0))
    return pl.pallas_call(
        body, name=name, grid=(s // ts,),
        in_specs=[tile, pl.BlockSpec((ts, 2 * ff), lambda t: (t, 0)), _const_spec(wd.shape), _const_spec(vec.shape)],
        out_specs=[tile, tile, pl.BlockSpec((ts, ff), lambda t: (t, 0))],
        out_shape=[jax.ShapeDtypeStruct((s, d), F32), jax.ShapeDtypeStruct((s, d), F32),
                   jax.ShapeDtypeStruct((s, ff), BF16)],
        compiler_params=_cparams(("parallel",)),
    )(x2, hgu, wd, vec)


def _ffn_bwd(dy, rpre, hgu, wdt, wgt, wut, vec, name, comm=()):
    s, d = dy.shape
    ff = wgt.shape[0]
    ts = _tile(s, TS_FFN)

    def body(dy_ref, rpre_ref, hgu_ref, wdt_ref, wgt_ref, wut_ref, vec_ref, dx_ref, dr_ref, dhgu_ref, ln_ref):
        @pl.when(pl.program_id(0) == 0)
        def _():
            ln_ref[...] = jnp.zeros_like(ln_ref)

        dyv = dy_ref[...]
        dr, dyy = _ln_bwd(dyv, rpre_ref[...], vec_ref[V_G + 2:V_G + 3, :])
        ln_ref[0:1, :] += _colsum(dyy)
        ln_ref[1:2, :] += _colsum(dyv)
        dr_ref[...] = dr.astype(BF16)
        dact = _dot(dr, wdt_ref[...])
        hg = hgu_ref[:, 0:ff]
        hu = hgu_ref[:, ff:2 * ff]
        sg = _sigmoid(hg)
        dhg = dact * hu * (sg * (1.0 + hg * (1.0 - sg)))
        dhu = dact * hg * sg
        dhgu_ref[:, 0:ff] = dhg.astype(BF16)
        dhgu_ref[:, ff:2 * ff] = dhu.astype(BF16)
        dx_ref[...] = ALPHA * dr + _dot(dhg, wgt_ref[...]) + _dot(dhu, wut_ref[...])

    tile = pl.BlockSpec((ts, d), lambda t: (t, 0))
    wide = pl.BlockSpec((ts, 2 * ff), lambda t: (t, 0))
    consts = (wdt, wgt, wut, vec)
    return _pallas(
        body, name=name, grid=(s // ts,),
        in_specs=[tile, tile, wide] + [_const_spec(c.shape) for c in consts],
        out_specs=[tile, tile, wide, _acc_spec((2, d))],
        out_shape=[jax.ShapeDtypeStruct((s, d), F32), jax.ShapeDtypeStruct((s, d), BF16),
                   jax.ShapeDtypeStruct((s, 2 * ff), BF16), jax.ShapeDtypeStruct((2, d), F32)],
        semantics=("arbitrary",), args=(dy, rpre, hgu, *consts), comm=comm)


def _loss_head(y, target, name):
    s, d = y.shape
    ts = _tile(s, TS_MM)

    def body(y_ref, t_ref, loss_ref, dy_ref):
        @pl.when(pl.program_id(0) == 0)
        def _():
            loss_ref[...] = jnp.zeros_like(loss_ref)

        err = y_ref[...] - t_ref[...]
        dy_ref[...] = err / d
        per_token = jnp.mean(err * err, axis=-1, keepdims=True)
        loss_ref[...] += 0.5 * jnp.sum(per_token, axis=0, keepdims=True)

    tile = pl.BlockSpec((ts, d), lambda t: (t, 0))
    return pl.pallas_call(
        body, name=name, grid=(s // ts,),
        in_specs=[tile, tile],
        out_specs=[_acc_spec((1, 1)), tile],
        out_shape=[jax.ShapeDtypeStruct((1, 1), F32), jax.ShapeDtypeStruct((s, d), F32)],
        compiler_params=_cparams(("arbitrary",)),
    )(y, target)


SHARD_AXIS = {"w_in": 1, "pool_w": 1, "lru_w_out": 0, "sconv_w_out": 0, "w_mix_out": 0,
              "xa_w_q": 0, "xa_w_k": 0, "xa_w_v": 0, "xa_w_o": 0,
              "ffn_w_gate": 0, "ffn_w_up": 0, "ffn_w_down": 0,
              "lru_conv_w": 1, "sconv_w": 1, "ln_g": 1, "ln_b": 1}
STORED_TRANSPOSED = ("ffn_w_gate", "ffn_w_up")
GROUP_IN = ("w_in",)
GROUP_MIXER = ("pool_w", "lru_w_out", "sconv_w_out", "w_mix_out")
GROUP_ATTN = ("xa_w_q", "xa_w_k", "xa_w_v", "xa_w_o")
GROUP_FFN = ("ffn_w_gate", "ffn_w_up", "ffn_w_down")
GROUP_VECTORS = ("lru_conv_w", "sconv_w", "ln_g", "ln_b")
REPLICATED = ("b_in", "pool_scale", "lru_conv_b", "lru_w_r", "lru_b_r", "lru_w_i", "lru_b_i", "lru_lambda")
WEIGHTS = ("w_in", "b_in", "pool_w", "pool_scale", "lru_conv_w", "lru_conv_b", "lru_w_r", "lru_b_r", "lru_w_i",
           "lru_b_i", "lru_lambda", "lru_w_out", "sconv_w", "sconv_w_out", "w_mix_out", "xa_w_q", "xa_w_k",
           "xa_w_v", "xa_w_o", "ffn_w_gate", "ffn_w_up", "ffn_w_down", "ln_g", "ln_b")


def _pack(arrs, width, lead=0, row_multiple=ROW_PAD):
    head = arrs[0].shape[:lead]
    flat = jnp.concatenate([a.reshape(head + (-1,)) for a in arrs], axis=lead)
    n = flat.shape[-1]
    chunk = width * row_multiple
    total = -(-n // chunk) * chunk
    if total != n:
        flat = jnp.pad(flat, [(0, 0)] * lead + [(0, total - n)])
    return flat.reshape(head + (total // width, width))


def _unpack(buf, shapes, lead=0):
    head = buf.shape[:lead]
    flat = buf.reshape(head + (-1,))
    out, off = [], 0
    for shp in shapes:
        n = math.prod(shp)
        out.append(flat[..., off:off + n].reshape(head + tuple(shp)))
        off += n
    return out


def _split8(a, axis):
    shp = a.shape
    a = a.reshape(shp[:axis] + (N_DEV, shp[axis] // N_DEV) + shp[axis + 1:])
    return jnp.moveaxis(a, axis, 0)


def _join8(a, axis):
    a = jnp.moveaxis(a, 0, axis)
    shp = a.shape
    return a.reshape(shp[:axis] + (shp[axis] * shp[axis + 1],) + shp[axis + 2:])


def _t(a):
    return jnp.swapaxes(a, -1, -2)


def _stored(name, a):
    return _t(a) if name in STORED_TRANSPOSED else a


def kernel(x, mem, w_in, b_in, pool_w, pool_scale, lru_conv_w, lru_conv_b, lru_w_r, lru_b_r, lru_w_i, lru_b_i, lru_lambda, lru_w_out, sconv_w, sconv_w_out, w_mix_out, xa_w_q, xa_w_k, xa_w_v, xa_w_o, ffn_w_gate, ffn_w_up, ffn_w_down, ln_g, ln_b, loss_target, m_w_in, m_b_in, m_pool_w, m_pool_scale, m_lru_conv_w, m_lru_conv_b, m_lru_w_r, m_lru_b_r, m_lru_w_i, m_lru_b_i, m_lru_lambda, m_lru_w_out, m_sconv_w, m_sconv_w_out, m_w_mix_out, m_xa_w_q, m_xa_w_k, m_xa_w_v, m_xa_w_o, m_ffn_w_gate, m_ffn_w_up, m_ffn_w_down, m_ln_g, m_ln_b, v_w_in, v_b_in, v_pool_w, v_pool_scale, v_lru_conv_w, v_lru_conv_b, v_lru_w_r, v_lru_b_r, v_lru_w_i, v_lru_b_i, v_lru_lambda, v_lru_w_out, v_sconv_w, v_sconv_w_out, v_w_mix_out, v_xa_w_q, v_xa_w_k, v_xa_w_v, v_xa_w_o, v_ffn_w_gate, v_ffn_w_up, v_ffn_w_down, v_ln_g, v_ln_b):
    args = dict(locals())
    w = {n: args[n] for n in WEIGHTS}
    mom_m = {n: args["m_" + n] for n in WEIGHTS}
    mom_v = {n: args["v_" + n] for n in WEIGHTS}
    depth = w_in.shape[0]
    s, d = x.shape[1], x.shape[2]
    nm = mem.shape[1]
    ff = ffn_w_gate.shape[2] * N_DEV
    xs = x.reshape(s, d)
    mems = mem.reshape(nm, d)
    target = loss_target.reshape(s, d)

    def shard(t, n, l):
        return _stored(n, t[n][l])

    def pack_shards(t, names, l, dtype=None):
        arrs = [shard(t, n, l) for n in names]
        return _pack([a if dtype is None else a.astype(dtype) for a in arrs], d)

    def unpack_gathered(buf, names):
        pieces = _unpack(buf, [shard(w, n, 0).shape for n in names], lead=1)
        return {n: (p if n == "w_in" else _join8(p, SHARD_AXIS[n])) for n, p in zip(names, pieces)}

    def layer_vec(l, fw):
        vec = jnp.zeros((V_ROWS, d), F32)
        vec = vec.at[V_PSCALE].set(pool_scale[l]).at[V_CW:V_CW + LRU_CONV].set(fw["lru_conv_w"])
        vec = vec.at[V_CB].set(lru_conv_b[l]).at[V_BR].set(lru_b_r[l]).at[V_BI].set(lru_b_i[l])
        vec = vec.at[V_LAM].set(lru_lambda[l]).at[V_SW:V_SW + SCONV_K].set(fw["sconv_w"])
        return vec.at[V_G:V_G + 3].set(fw["ln_g"]).at[V_B:V_B + 3].set(fw["ln_b"])

    def layer_params(l, fw):
        return dict(
            vec=layer_vec(l, fw), wint=_t(fw["w_in"]).reshape(1, 8 * d, d),
            pw=fw["pool_w"], pwt=_t(fw["pool_w"]),
            wr=lru_w_r[l].astype(BF16), wi=lru_w_i[l].astype(BF16),
            wrt=_t(lru_w_r[l]).astype(BF16), wit=_t(lru_w_i[l]).astype(BF16),
            wlo=fw["lru_w_out"], wlot=_t(fw["lru_w_out"]),
            wsc=fw["sconv_w_out"], wsct=_t(fw["sconv_w_out"]),
            wmix=fw["w_mix_out"], wmixt=_t(fw["w_mix_out"]),
            wq=fw["xa_w_q"], wqt=_t(fw["xa_w_q"]), wo=fw["xa_w_o"], wot=_t(fw["xa_w_o"]),
            wkv=jnp.stack([fw["xa_w_k"], fw["xa_w_v"]]),
            wgu=jnp.stack([_t(fw["ffn_w_gate"]), _t(fw["ffn_w_up"])]),
            wgt=fw["ffn_w_gate"], wut=fw["ffn_w_up"],
            wd=fw["ffn_w_down"], wdt=_t(fw["ffn_w_down"]))

    later = GROUP_ATTN + GROUP_FFN
    fw0 = unpack_gathered(_all_gather(pack_shards(w, GROUP_IN + GROUP_MIXER, 0, BF16), "gather_mixer_0"),
                          GROUP_IN + GROUP_MIXER)
    vectors = _all_gather(_pack([shard(w, n, l) for l in range(depth) for n in GROUP_VECTORS], d), "gather_vectors")
    vec_pieces = _unpack(vectors, [shard(w, n, l).shape for l in range(depth) for n in GROUP_VECTORS], lead=1)
    fvec = [{n: _join8(vec_pieces[l * len(GROUP_VECTORS) + k], SHARD_AXIS[n]) for k, n in enumerate(GROUP_VECTORS)}
            for l in range(depth)]

    layers, saved = [], []
    cur = xs
    fw_next = None
    for l in range(depth):
        fw = dict(fw0 if l == 0 else fw_next)
        fw.update(fvec[l])
        comm = []
        if l == 0:
            comm.append(("gather", pack_shards(w, later, 0, BF16)))
        if l + 1 < depth:
            comm.append(("gather", pack_shards(w, GROUP_IN + GROUP_MIXER + later, l + 1, BF16)))
        z, x1, rpre1, h, ypre, yl, yc, merged, e, pp, vb, rb, ib, cq, *got = _mixer_fwd(
            cur, fw["w_in"], b_in[l].reshape(1, 8 * d), fw["pool_w"], lru_w_r[l].astype(BF16),
            lru_w_i[l].astype(BF16), fw["lru_w_out"], fw["sconv_w_out"], fw["w_mix_out"],
            layer_vec(l, fw), f"mixer_fwd_{l}", comm=comm)
        if l == 0:
            fw.update(unpack_gathered(got.pop(0), later))
        if l + 1 < depth:
            fw_next = unpack_gathered(got.pop(0), GROUP_IN + GROUP_MIXER + later)
        p = layer_params(l, fw)
        kv = _mm(mems, p["wkv"], f"kv_{l}")[0]
        kk = kv[:, :d].astype(BF16)
        vv = kv[:, d:].astype(BF16)
        x2, rpre2, q, o = _attn_fwd(x1, p["wq"], p["wo"], _t(kk), vv, p["vec"], f"attn_fwd_{l}")
        hgu = _mm(x2, p["wgu"], f"ffn_in_{l}", tm=TS_MM // 2)[0]
        x3, rpre3, act = _ffn_out(x2, hgu, p["wd"], p["vec"], f"ffn_out_{l}")
        layers.append(p)
        saved.append(dict(x0=cur, z=z, x1=x1, rpre1=rpre1, h=h, ypre=ypre, yl=yl, yc=yc, merged=merged, e=e,
                          pp=pp, vb=vb, rb=rb, ib=ib, cq=cq,
                          kk=kk, vv=vv, x2=x2, rpre2=rpre2, q=q, o=o, hgu=hgu, rpre3=rpre3, act=act))
        cur = x3

    loss_part, dcur = _loss_head(cur, target, "loss_head")
    loss = lax.psum(loss_part[0, 0], ("x", "y", "c"))

    res = {}

    def slots_of(g, names):
        return _pack([g[n] if n == "w_in" else _split8(g[n], SHARD_AXIS[n]) for n in names], d, lead=1)

    def update(received, names, l, tag):
        outs = _adamw_sum(received, *[pack_shards(t, names, l) for t in (w, mom_m, mom_v)], f"adamw_{tag}_{l}")
        shapes = [shard(w, n, l).shape for n in names]
        for n, *parts in zip(names, *[_unpack(o, shapes) for o in outs]):
            res[(n, l)] = [_stored(n, a) for a in parts]

    def settle(exchanges, got):
        for (names, l, tag, _), received in zip(exchanges, got):
            update(received, names, l, tag)

    grads = [None] * depth
    for l in reversed(range(depth)):
        p, sv = layers[l], saved[l]
        g = {}
        dx2, dr3, dhgu, ln3 = _ffn_bwd(dcur, sv["rpre3"], sv["hgu"], p["wdt"], p["wgt"], p["wut"], p["vec"],
                                       f"ffn_bwd_{l}")
        g["ffn_w_down"] = _mm_tn(sv["act"], dr3, d, f"g_wd_{l}")[0][0]
        dwgu = _mm_tn(dhgu, sv["x2"], d, f"g_wgu_{l}", tk=ff)[0][0]
        g["ffn_w_gate"], g["ffn_w_up"] = dwgu[:ff], dwgu[ff:]
        dx1, dq, dr2, dk, dv, ln2 = _attn_bwd(dx2, sv["rpre2"], sv["q"], p["wqt"], p["wot"], sv["kk"], _t(sv["kk"]),
                                              _t(sv["vv"]), p["vec"], f"attn_bwd_{l}")
        g["xa_w_o"] = _mm_tn(sv["o"], dr2, d, f"g_wo_{l}")[0][0]
        g["xa_w_q"] = _mm_tn(sv["x1"], dq, d, f"g_wq_{l}")[0][0]
        dwkv = _mm_tn(mems, jnp.concatenate([dk, dv], axis=1), d, f"g_wkv_{l}")[0]
        g["xa_w_k"], g["xa_w_v"] = dwkv[0], dwkv[1]
        ffn_slots = slots_of(g, GROUP_FFN)
        (dz, dr1, dyl, dyc, accs, dbin, g["pool_w"], g["lru_w_r"], g["lru_w_i"], received) = _mixer_bwd(
            dx1, sv["rpre1"], sv["z"], sv["h"], sv["ypre"], sv["yl"], sv["yc"], sv["pp"], sv["vb"], sv["rb"],
            sv["ib"], sv["cq"], p["pwt"], p["wrt"], p["wit"], p["wlot"], p["wsct"], p["wmixt"], p["vec"],
            f"mixer_bwd_{l}",
            comm=[("scatter", ffn_slots)])
        update(received, GROUP_FFN, l, "ffn")
        g["w_mix_out"] = _mm_tn(sv["merged"], dr1, d, f"g_wmix_{l}")[0][0]
        g["lru_w_out"] = _mm_tn(sv["h"], dyl, d, f"g_wlo_{l}")[0][0]
        g["sconv_w_out"] = _mm_tn(sv["e"], dyc, d, f"g_wsc_{l}")[0][0]
        g["b_in"] = dbin[0]
        g["pool_scale"] = accs[A_PSCALE]
        g["lru_conv_w"] = accs[A_CW:A_CW + LRU_CONV]
        g["lru_conv_b"] = accs[A_CB]
        g["lru_b_r"] = accs[A_BR]
        g["lru_b_i"] = accs[A_BI]
        g["lru_lambda"] = accs[A_SP] * (-_sigmoid(-lru_lambda[l]))
        g["sconv_w"] = accs[A_SW:A_SW + SCONV_K]
        g["ln_g"] = jnp.stack([accs[A_G], ln2[0], ln3[0]])
        g["ln_b"] = jnp.stack([accs[A_B], ln2[1], ln3[1]])
        grads[l] = g
        behind_win = [(GROUP_ATTN, l, "attn", slots_of(g, GROUP_ATTN)),
                      (GROUP_MIXER + GROUP_VECTORS, l, "mixer", slots_of(g, GROUP_MIXER + GROUP_VECTORS))]
        g["w_in"], *got = _mm_tn(sv["x0"], dz, d, f"g_win_{l}", comm=[("scatter", t[3]) for t in behind_win])
        settle(behind_win, got)
        behind_dx = [(GROUP_IN, l, "w_in", slots_of(g, GROUP_IN))]
        comm = [("scatter", behind_dx[0][3])]
        if l == 0:
            comm.append(("gather", _pack([jnp.stack([grads[k][n] for k in range(depth)]) for n in REPLICATED], d)))
        dcur, *got = _mm(dz, p["wint"], f"dx_{l}", add=dr1, add_scale=ALPHA, comm=comm)
        settle(behind_dx, got[:1])
        if l == 0:
            outs = _adamw_sum(got[1], *[_pack([t[n] for n in REPLICATED], d) for t in (w, mom_m, mom_v)],
                              "adamw_replicated")
            rep_shapes = [w[n].shape for n in REPLICATED]
            final = {n: parts for n, *parts in zip(REPLICATED, *[_unpack(o, rep_shapes) for o in outs])}
    grad_x = dcur.reshape(x.shape)

    for n in WEIGHTS:
        if n not in final:
            final[n] = [jnp.stack([res[(n, l)][k] for l in range(depth)]) for k in range(4)]
    return (loss, grad_x, *[final[n][0] for n in WEIGHTS], *[final[n][1] for n in WEIGHTS],
            *[final[n][2] for n in WEIGHTS], *[final[n][3] for n in WEIGHTS])
```

```python
import functools
import math

import jax
import jax.numpy as jnp
from jax import lax
from jax.experimental import pallas as pl
from jax.experimental.pallas import tpu as pltpu

F32 = jnp.float32
BF16 = jnp.bfloat16
MESH = pl.DeviceIdType.MESH

N_DEV = 8
LRU_HEADS = 8
LRU_CONV = 4
LRU_C = 8.0
SCONV_K = 3
POOL_WINDOWS = (2, 4, 8, 16)
X_HEADS = 4
DEPTH = 2
ALPHA = (2 * DEPTH) ** 0.25
LN_EPS = 1e-5
ADAM_LR = 0.001
ADAM_B1 = 0.9
ADAM_B2 = 0.999
ADAM_EPS = 1e-08
ADAM_WD = 0.01
ADAM_STEP = 10

HALO = 16
SUBLANES = 8
VMEM_LIMIT = 56 * 1024 * 1024
TS_MIXER = 128
TS_ATTN = 512
TS_FFN = 256
TS_MM = 1024
TK_MM = 2048
TR_ADAM = 256
ROW_PAD = 8
TS_MM_TN = 1024

V_PSCALE, V_CW, V_CB, V_BR, V_BI, V_LAM, V_SW, V_G, V_B = 0, 1, 5, 6, 7, 8, 9, 12, 15
V_ROWS = 24
A_PSCALE, A_CW, A_CB, A_BR, A_BI, A_SP, A_SW, A_G, A_B = 0, 1, 5, 6, 7, 8, 9, 12, 13
A_ROWS = 16


def _cparams(sem):
    return pltpu.CompilerParams(dimension_semantics=sem, vmem_limit_bytes=VMEM_LIMIT)


def _tile(n, pref):
    if n <= pref:
        return n
    assert n % pref == 0, (n, pref)
    return pref


def _const_spec(shape):
    nd = len(shape)
    return pl.BlockSpec(shape, lambda *_: (0,) * nd, pipeline_mode=pl.Buffered(1))


def _acc_spec(shape):
    nd = len(shape)
    return pl.BlockSpec(shape, lambda *_: (0,) * nd)


def _dot(a, b):
    return jnp.dot(a.astype(BF16), b.astype(BF16), preferred_element_type=F32)


def _dot_tn(a, b):
    return lax.dot_general(a.astype(BF16), b.astype(BF16), (((0,), (0,)), ((), ())),
                           preferred_element_type=F32)


def _sigmoid(x):
    return 0.5 * jnp.tanh(0.5 * x) + 0.5


def _softplus(y):
    e = jnp.exp(-jnp.abs(y))
    log1p = jnp.where(e < 1e-4, e * (1.0 - e * (0.5 - e * (1.0 / 3.0))), jnp.log(1.0 + e))
    return jnp.maximum(y, 0.0) + log1p


def _ln_fwd(r, g, b):
    mu = jnp.mean(r, axis=-1, keepdims=True)
    xc = r - mu
    var = jnp.mean(xc * xc, axis=-1, keepdims=True)
    return xc * lax.rsqrt(var + LN_EPS) * g + b


def _ln_bwd(dy, r, g):
    mu = jnp.mean(r, axis=-1, keepdims=True)
    xc = r - mu
    var = jnp.mean(xc * xc, axis=-1, keepdims=True)
    rstd = lax.rsqrt(var + LN_EPS)
    yhat = xc * rstd
    dyh = dy * g
    m1 = jnp.mean(dyh, axis=-1, keepdims=True)
    m2 = jnp.mean(dyh * yhat, axis=-1, keepdims=True)
    return rstd * (dyh - m1 - yhat * m2), dy * yhat


def _colsum(a):
    return jnp.sum(a, axis=0, keepdims=True)


def _position():
    return lax.axis_index("x"), lax.axis_index("y"), lax.axis_index("c")


def _gather_copies(x_ref, out_ref, send_sems, recv_sems, local_sem):
    x, y, c = _position()
    me, sibling = (x, y, c), (x, y, 1 - c)
    chips = [(1 - x, y), (x, 1 - y), (1 - x, 1 - y)]

    def slot(px, py, pc):
        return out_ref.at[4 * px + 2 * py + pc]

    def copy(k, block, to, src=None):
        return pltpu.make_async_remote_copy(
            src_ref=slot(*block) if src is None else src, dst_ref=slot(*block),
            send_sem=send_sems.at[k], recv_sem=recv_sems.at[k], device_id=to, device_id_type=MESH)

    mine = pltpu.make_async_copy(x_ref, slot(*me), local_sem)
    first = [copy(0, me, sibling, src=x_ref)]
    first += [copy(1 + j, me, (*chip, c), src=x_ref) for j, chip in enumerate(chips)]
    passed = [copy(4 + j, (*chip, c), sibling) for j, chip in enumerate(chips)]
    over_ici = [copy(1 + j, (*chip, c), me) for j, chip in enumerate(chips)]
    from_sibling = copy(0, sibling, me)
    forwarded = [copy(4 + j, (*chip, 1 - c), me) for j, chip in enumerate(chips)]
    return mine, first, passed, over_ici, from_sibling, forwarded


def _scatter_copies(g_ref, out_ref, send_sems, recv_sems, local_sem):
    x, y, c = _position()
    me = 4 * x + 2 * y + c
    mine = pltpu.make_async_copy(g_ref.at[me], out_ref.at[me], local_sem)
    copies = []
    for k in range(1, N_DEV):
        px = 1 - x if k & 4 else x
        py = 1 - y if k & 2 else y
        pc = 1 - c if k & 1 else c
        copies.append(pltpu.make_async_remote_copy(
            src_ref=g_ref.at[4 * px + 2 * py + pc], dst_ref=out_ref.at[me],
            send_sem=send_sems.at[k - 1], recv_sem=recv_sems.at[k - 1],
            device_id=(px, py, pc), device_id_type=MESH))
    return mine, copies


def _comm_start(kind, *refs):
    if kind == "gather":
        mine, first, _, _, _, _ = _gather_copies(*refs)
        mine.start()
        for cp in first:
            cp.start()
    else:
        mine, copies = _scatter_copies(*refs)
        mine.start()
        for cp in copies:
            cp.start()


def _comm_finish(kind, *refs):
    if kind == "gather":
        mine, first, passed, over_ici, from_sibling, forwarded = _gather_copies(*refs)
        for arrival, forward in zip(over_ici, passed):
            arrival.wait_recv()
            forward.start()
        from_sibling.wait_recv()
        for arrival in forwarded:
            arrival.wait_recv()
        for cp in first + passed:
            cp.wait_send()
        mine.wait()
    else:
        mine, copies = _scatter_copies(*refs)
        for cp in copies:
            cp.wait_recv()
        for cp in copies:
            cp.wait_send()
        mine.wait()


def _comm_out_shape(kind, arr):
    return jax.ShapeDtypeStruct((N_DEV,) + arr.shape if kind == "gather" else arr.shape, arr.dtype)


COMM_SEMAPHORES = [pltpu.SemaphoreType.DMA((7,)), pltpu.SemaphoreType.DMA((7,)), pltpu.SemaphoreType.DMA]


def _pallas(body, *, name, grid, in_specs, out_specs, out_shape, semantics, args, scratch_shapes=(), comm=()):
    in_specs, out_specs, out_shape = list(in_specs), list(out_specs), list(out_shape)
    scratch_shapes = list(scratch_shapes)
    n_in, n_out, n_scr, nc = len(in_specs), len(out_specs), len(scratch_shapes), len(comm)
    if not comm:
        return pl.pallas_call(body, name=name, grid=grid, in_specs=in_specs, out_specs=out_specs, out_shape=out_shape,
                              scratch_shapes=scratch_shapes, compiler_params=_cparams(semantics))(*args)
    kinds = [kind for kind, _ in comm]

    def carrying(*refs):
        ins, rest = refs[:n_in], refs[n_in:]
        cin, rest = rest[:nc], rest[nc:]
        outs, rest = rest[:n_out], rest[n_out:]
        cout, rest = rest[:nc], rest[nc:]
        scr, sems = rest[:n_scr], rest[n_scr:]
        ids = [pl.program_id(ax) for ax in range(len(grid))]
        first = functools.reduce(jnp.logical_and, [i == 0 for i in ids])
        last = functools.reduce(jnp.logical_and, [i == g - 1 for i, g in zip(ids, grid)])
        plans = [(kinds[k], cin[k], cout[k], *sems[3 * k:3 * k + 3]) for k in range(nc)]

        @pl.when(first)
        def _():
            for plan in plans:
                _comm_start(*plan)

        body(*ins, *outs, *scr)

        @pl.when(last)
        def _():
            for plan in plans:
                _comm_finish(*plan)

    hbm = pl.BlockSpec(memory_space=pl.ANY)
    return pl.pallas_call(
        carrying, name=name, grid=grid,
        in_specs=in_specs + [hbm] * nc, out_specs=out_specs + [hbm] * nc,
        out_shape=out_shape + [_comm_out_shape(kind, arr) for kind, arr in comm],
        scratch_shapes=scratch_shapes + COMM_SEMAPHORES * nc,
        compiler_params=_cparams(("arbitrary",) * len(grid)),
    )(*args, *[arr for _, arr in comm])


def _all_gather(xs, name):
    def body(x_ref, out_ref, send_sems, recv_sems, local_sem):
        _comm_start("gather", x_ref, out_ref, send_sems, recv_sems, local_sem)
        _comm_finish("gather", x_ref, out_ref, send_sems, recv_sems, local_sem)

    return pl.pallas_call(
        body, name=name, out_shape=_comm_out_shape("gather", xs),
        in_specs=[pl.BlockSpec(memory_space=pl.ANY)], out_specs=pl.BlockSpec(memory_space=pl.ANY),
        scratch_shapes=COMM_SEMAPHORES,
    )(xs)


def _adamw_sum(parts, w, m, v, name):
    _, rows, width = parts.shape
    tr = max(t for t in range(SUBLANES, min(rows, TR_ADAM) + 1, SUBLANES) if rows % t == 0)
    c1 = 1.0 - ADAM_B1 ** ADAM_STEP
    c2 = 1.0 - ADAM_B2 ** ADAM_STEP

    def body(p_ref, w_ref, m_ref, v_ref, g_ref, d_ref, nm_ref, nv_ref):
        g = p_ref[0]
        for k in range(1, N_DEV):
            g = g + p_ref[k]
        nm = ADAM_B1 * m_ref[...] + (1.0 - ADAM_B1) * g
        nv = ADAM_B2 * v_ref[...] + (1.0 - ADAM_B2) * (g * g)
        m_hat = nm / c1
        v_hat = nv / c2
        g_ref[...] = g
        d_ref[...] = -ADAM_LR * (m_hat / (jnp.sqrt(v_hat) + ADAM_EPS) + ADAM_WD * w_ref[...])
        nm_ref[...] = nm
        nv_ref[...] = nv

    spec = pl.BlockSpec((tr, width), lambda i: (i, 0))
    out = jax.ShapeDtypeStruct((rows, width), F32)
    return pl.pallas_call(
        body, name=name, grid=(rows // tr,),
        in_specs=[pl.BlockSpec((N_DEV, tr, width), lambda i: (0, i, 0)), spec, spec, spec],
        out_specs=[spec, spec, spec, spec], out_shape=[out, out, out, out],
        compiler_params=_cparams(("parallel",)),
    )(parts, w, m, v)


def _mm(a, wb, name, bias=None, add=None, add_scale=1.0, out_dtype=F32, tm=None, comm=()):
    m, k = a.shape
    nb, k2, tn = wb.shape
    assert k == k2
    tm = _tile(m, TS_MM if tm is None else tm)
    tk = _tile(k, TK_MM)
    nk = k // tk

    def body(*refs):
        a_ref, w_ref = refs[0], refs[1]
        pos = 2
        b_ref = add_ref = None
        if bias is not None:
            b_ref = refs[pos]
            pos += 1
        if add is not None:
            add_ref = refs[pos]
            pos += 1
        o_ref = refs[pos]

        def finish(r):
            if b_ref is not None:
                r = r + b_ref[...]
            if add_ref is not None:
                r = r + add_scale * add_ref[...]
            o_ref[...] = r.astype(o_ref.dtype)

        if nk == 1:
            finish(_dot(a_ref[...], w_ref[...]))
            return
        acc_ref = refs[pos + 1]
        kk = pl.program_id(2)

        @pl.when(kk == 0)
        def _():
            acc_ref[...] = jnp.zeros_like(acc_ref)

        acc_ref[...] += _dot(a_ref[...], w_ref[...])

        @pl.when(kk == nk - 1)
        def _():
            finish(acc_ref[...])

    in_specs = [pl.BlockSpec((tm, tk), lambda j, i, kk: (i, kk)),
                pl.BlockSpec((None, tk, tn), lambda j, i, kk: (j, kk, 0))]
    args = [a, wb]
    if bias is not None:
        in_specs.append(pl.BlockSpec((1, tn), lambda j, i, kk: (0, j)))
        args.append(bias)
    if add is not None:
        in_specs.append(pl.BlockSpec((tm, tn), lambda j, i, kk: (i, j)))
        args.append(add)
    return _pallas(
        body, name=name, grid=(nb, m // tm, nk),
        in_specs=in_specs,
        out_specs=[pl.BlockSpec((tm, tn), lambda j, i, kk: (i, j))],
        out_shape=[jax.ShapeDtypeStruct((m, nb * tn), out_dtype)],
        scratch_shapes=[pltpu.VMEM((tm, tn), F32)] if nk > 1 else [],
        semantics=("parallel", "parallel", "arbitrary"), args=args, comm=comm)


def _mm_tn(a, b, tn, name, tk=None, comm=()):
    s, k = a.shape
    s2, n = b.shape
    assert s == s2 and n % tn == 0
    nb = n // tn
    ts = _tile(s, TS_MM_TN)
    tk = k if tk is None else tk
    assert k % tk == 0

    def body(a_ref, b_ref, o_ref):
        @pl.when(pl.program_id(2) == 0)
        def _():
            o_ref[...] = jnp.zeros_like(o_ref)

        o_ref[...] += _dot_tn(a_ref[...], b_ref[...])

    return _pallas(
        body, name=name, grid=(nb, k // tk, s // ts),
        in_specs=[pl.BlockSpec((ts, tk), lambda j, kb, i: (i, kb)),
                  pl.BlockSpec((ts, tn), lambda j, kb, i: (i, j))],
        out_specs=[pl.BlockSpec((None, tk, tn), lambda j, kb, i: (j, kb, 0))],
        out_shape=[jax.ShapeDtypeStruct((nb, k, tn), F32)],
        semantics=("parallel", "parallel", "arbitrary"), args=(a, b), comm=comm)


def _scan_fwd(a_ref, b_ref, h_ref, carry_ref, ts):
    rowid = lax.broadcasted_iota(jnp.int32, (SUBLANES, 1), 0)

    def group(gi, hprev):
        r0 = pl.multiple_of(gi * SUBLANES, SUBLANES)
        a = a_ref[pl.ds(r0, SUBLANES), :]
        b = b_ref[pl.ds(r0, SUBLANES), :]
        for d in (1, 2, 4):
            a_sh = jnp.where(rowid >= d, pltpu.roll(a, d, 0), 1.0)
            b_sh = jnp.where(rowid >= d, pltpu.roll(b, d, 0), 0.0)
            b = a * b_sh + b
            a = a * a_sh
        hh = a * hprev + b
        h_ref[pl.ds(r0, SUBLANES), :] = hh
        return hh[SUBLANES - 1:SUBLANES, :]

    last = lax.fori_loop(0, ts // SUBLANES, group, carry_ref[0:1, :])
    carry_ref[0:1, :] = last


def _scan_rev(c_ref, b_ref, g_ref, carry_ref, ts):
    rowid = lax.broadcasted_iota(jnp.int32, (SUBLANES, 1), 0)
    ng = ts // SUBLANES

    def group(gi, gnext):
        r0 = pl.multiple_of((ng - 1 - gi) * SUBLANES, SUBLANES)
        c = c_ref[pl.ds(r0, SUBLANES), :]
        b = b_ref[pl.ds(r0, SUBLANES), :]
        for d in (1, 2, 4):
            keep = rowid < SUBLANES - d
            c_sh = jnp.where(keep, pltpu.roll(c, SUBLANES - d, 0), 1.0)
            b_sh = jnp.where(keep, pltpu.roll(b, SUBLANES - d, 0), 0.0)
            b = c * b_sh + b
            c = c * c_sh
        gg = c * gnext + b
        g_ref[pl.ds(r0, SUBLANES), :] = gg
        return gg[0:1, :]

    first = lax.fori_loop(0, ng, group, carry_ref[0:1, :])
    carry_ref[0:1, :] = first


def _past(ext, sh, ts):
    if sh == 0:
        return ext[HALO:HALO + ts]
    return pltpu.roll(ext, sh, 0)[HALO:HALO + ts]


def _future(ext, sh, ts):
    if sh == 0:
        return ext[0:ts]
    return pltpu.roll(ext, ts + HALO - sh, 0)[0:ts]


def _one_minus_sq(a, log_a):
    x = 2.0 * log_a
    series = -x * (1.0 + x * (0.5 + x * (1.0 / 6.0 + x * (1.0 / 24.0))))
    return jnp.where(x > -0.02, series, 1.0 - a * a)


def _halo_index(ts):
    blocks = ts // HALO
    return lambda t: (jnp.maximum(t * blocks - 1, 0), 0)


def _head_columns(d):
    cw = d // LRU_HEADS
    return [slice(c * cw, (c + 1) * cw) for c in range(LRU_HEADS)]


def _shift(cs, off):
    return slice(cs.start + off, cs.stop + off)


def _mixer_fwd(x, win, b_in, pw, wr, wi, wlo, wsc, wmix, vec, name, comm=()):
    s, d = x.shape
    ts = _tile(s, TS_MIXER)
    nt = s // ts
    dg = d // len(POOL_WINDOWS)
    nblk = win.shape[0]
    cols = _head_columns(d)

    def body(x_ref, xn_ref, win_ref, bin_ref, pw_ref, wr_ref, wi_ref, wlo_ref, wsc_ref, wmix_ref, vec_ref,
             z_hbm, x1_ref, rpre_ref, h_ref, ypre_ref, yl_ref, yc_ref, mg_ref, e_ref, p_ref, v_ref, r_ref, ig_ref,
             cq_ref, z_even, z_odd, zhist, a_scr, b_scr, hcarry, z_sem):
        i = pl.program_id(0)
        first = i == 0

        def project_block(xb, dst, k):
            dst[:, k * d:(k + 1) * d] = _dot(xb, win_ref[k]) + bin_ref[:, k * d:(k + 1) * d]

        @pl.when(first)
        def _():
            hcarry[...] = jnp.zeros_like(hcarry)
            zhist[...] = jnp.zeros_like(zhist)
            xb = x_ref[...].astype(BF16)
            for k in range(nblk):
                project_block(xb, z_even, k)

        def vrow(k, cs):
            return vec_ref[k:k + 1, cs]

        def step(zc, zn):
            z_out = pltpu.make_async_copy(zc, z_hbm.at[pl.ds(pl.multiple_of(i * ts, ts), ts), :], z_sem)
            z_out.start()
            xb = xn_ref[...].astype(BF16)
            tglob = i * ts + lax.broadcasted_iota(jnp.int32, (ts, 1), 0)
            sp = _softplus(-vec_ref[V_LAM:V_LAM + 1, :])

            def with_history(k, cs):
                kc = _shift(cs, k * d)
                return jnp.concatenate([jnp.where(first, 0.0, zhist[:, kc]), zc[:, kc]], axis=0)

            for hh, cs in enumerate(cols):
                if hh < nblk:
                    project_block(xb, zn, hh)
                win_len = POOL_WINDOWS[cs.start // dg]
                ext = with_history(0, cs)
                sm = ext
                sh = 1
                while sh < win_len:
                    sm = sm + pltpu.roll(sm, sh, 0)
                    sh *= 2
                inv_cnt = 1.0 / jnp.minimum(tglob + 1, win_len).astype(F32)
                p_ref[:, cs] = (sm[HALO:HALO + ts] * inv_cnt - ext[HALO:HALO + ts]).astype(BF16)
                ext = with_history(1, cs)
                v = vrow(V_CB, cs)
                for j in range(LRU_CONV):
                    v = v + vrow(V_CW + j, cs) * _past(ext, LRU_CONV - 1 - j, ts)
                r = _sigmoid(_dot(v, wr_ref[hh]) + vrow(V_BR, cs))
                ig = _sigmoid(_dot(v, wi_ref[hh]) + vrow(V_BI, cs))
                log_a = -LRU_C * r * sp[:, cs]
                a = jnp.exp(log_a)
                a_scr[:, cs] = a
                b_scr[:, cs] = jnp.sqrt(_one_minus_sq(a, log_a)) * (ig * v)
                v_ref[:, cs] = v
                r_ref[:, cs] = r
                ig_ref[:, cs] = ig
                ext = with_history(3, cs) * with_history(4, cs)
                cq = jnp.zeros((ts, cs.stop - cs.start), F32)
                for j in range(SCONV_K):
                    cq = cq + vrow(V_SW + j, cs) * _past(ext, SCONV_K - 1 - j, ts)
                cq_ref[:, cs] = cq
                e_ref[:, cs] = (zc[:, _shift(cs, 2 * d)] * cq).astype(BF16)
            for k in range(len(cols), nblk):
                project_block(xb, zn, k)
            _scan_fwd(a_scr, b_scr, h_ref, hcarry, ts)
            ypre = jnp.concatenate([_dot(p_ref[:, g * dg:(g + 1) * dg], pw_ref[g])
                                    for g in range(len(POOL_WINDOWS))], axis=1)
            yl = _dot(h_ref[...], wlo_ref[...])
            yc = _dot(e_ref[...], wsc_ref[...])
            ypre_ref[...] = ypre
            yl_ref[...] = yl
            yc_ref[...] = yc
            for cs in cols:
                merged = (_sigmoid(zc[:, _shift(cs, 5 * d)]) * (ypre[:, cs] * vrow(V_PSCALE, cs))
                          + _sigmoid(zc[:, _shift(cs, 6 * d)]) * yl[:, cs]
                          + _sigmoid(zc[:, _shift(cs, 7 * d)]) * yc[:, cs])
                mg_ref[:, cs] = merged.astype(BF16)
            rpre = ALPHA * x_ref[...] + _dot(mg_ref[...], wmix_ref[...])
            x1_ref[...] = _ln_fwd(rpre, vec_ref[V_G:V_G + 1, :], vec_ref[V_B:V_B + 1, :])
            rpre_ref[...] = rpre
            zhist[...] = zc[ts - HALO:ts, :]
            z_out.wait()

        parity = lax.rem(i, 2)

        @pl.when(parity == 0)
        def _():
            step(z_even, z_odd)

        @pl.when(parity == 1)
        def _():
            step(z_odd, z_even)

    tile = pl.BlockSpec((ts, d), lambda t: (t, 0))
    f32o = jax.ShapeDtypeStruct((s, d), F32)
    bfo = jax.ShapeDtypeStruct((s, d), BF16)
    consts = (win, b_in, pw, wr, wi, wlo, wsc, wmix, vec)
    return _pallas(
        body, name=name, grid=(nt,),
        in_specs=[tile, pl.BlockSpec((ts, d), lambda t: (jnp.minimum(t + 1, nt - 1), 0))]
        + [_const_spec(c.shape) for c in consts],
        out_specs=[pl.BlockSpec(memory_space=pl.ANY)] + [tile] * 13,
        out_shape=[jax.ShapeDtypeStruct((s, nblk * d), F32), f32o, f32o, f32o, f32o, f32o, f32o, bfo, bfo, bfo,
                   f32o, f32o, f32o, f32o],
        scratch_shapes=[pltpu.VMEM((ts, nblk * d), F32)] * 2 + [pltpu.VMEM((HALO, nblk * d), F32)]
        + [pltpu.VMEM((ts, d), F32)] * 2 + [pltpu.VMEM((SUBLANES, d), F32), pltpu.SemaphoreType.DMA],
        semantics=("arbitrary",), args=(x, x, *consts), comm=comm)


def _mixer_bwd(dx1, rpre, z, h, ypre, yl, yc, pp, vv, rr, ii, cq, pwt, wrt, wit, wlot, wsct, wmixt, vec, name,
               comm=()):
    s, d = dx1.shape
    ts = _tile(s, TS_MIXER)
    nt = s // ts
    dg = d // len(POOL_WINDOWS)
    cols = _head_columns(d)

    def body(dx1_ref, rpre_ref, z_ref, h_ref, hh_ref, ypre_ref, yl_ref, yc_ref, p_ref, v_ref, r_ref, ig_ref, cq_ref,
             pwt_ref, wrt_ref, wit_ref, wlot_ref, wsct_ref, wmixt_ref, vec_ref,
             dz_ref, dr_ref, dyl_ref, dyc_ref, acc_ref, dbin_ref, dpw_ref, dwr_ref, dwi_ref,
             c_scr, b_scr, g_scr, dyps_scr, a_keep, m_keep, gcarry, acarry, dcq_c, dv_c, m_c):
        i = pl.program_id(0)
        t = nt - 1 - i

        @pl.when(i == 0)
        def _():
            for ref in (gcarry, acarry, dcq_c, dv_c, m_c, acc_ref, dbin_ref, dpw_ref, dwr_ref, dwi_ref):
                ref[...] = jnp.zeros_like(ref)

        def vrow(k, cs):
            return vec_ref[k:k + 1, cs]

        def zc(k, cs):
            return z_ref[:, _shift(cs, k * d)]

        def acc(row, cs, val):
            acc_ref[row:row + 1, cs] += _colsum(val)

        def emit_dz(k, cs, val):
            kc = _shift(cs, k * d)
            dz_ref[:, kc] = val.astype(BF16)
            dbin_ref[:, kc] += _colsum(val)

        def with_future(tile_val, carry_ref, cs):
            ext = jnp.concatenate([tile_val, carry_ref[:, cs]], axis=0)
            carry_ref[:, cs] = tile_val[0:HALO, :]
            return ext

        dx1v = dx1_ref[...]
        dr, dyy = _ln_bwd(dx1v, rpre_ref[...], vec_ref[V_G:V_G + 1, :])
        acc_ref[A_G:A_G + 1, :] += _colsum(dyy)
        acc_ref[A_B:A_B + 1, :] += _colsum(dx1v)
        dr_ref[...] = dr
        dmg = _dot(dr, wmixt_ref[...])
        for cs in cols:
            dm = dmg[:, cs]
            ypre = ypre_ref[:, cs]
            ys = (ypre * vrow(V_PSCALE, cs), yl_ref[:, cs], yc_ref[:, cs])
            dys = []
            for k in range(3):
                gk = _sigmoid(zc(5 + k, cs))
                emit_dz(5 + k, cs, dm * ys[k] * gk * (1.0 - gk))
                dys.append(dm * gk)
            acc(A_PSCALE, cs, dys[0] * ypre)
            dyps_scr[:, cs] = dys[0] * vrow(V_PSCALE, cs)
            dyl_ref[:, cs] = dys[1].astype(BF16)
            dyc_ref[:, cs] = dys[2].astype(BF16)
        de = _dot(dyc_ref[...], wsct_ref[...])
        dh = _dot(dyl_ref[...], wlot_ref[...])

        sp = _softplus(-vec_ref[V_LAM:V_LAM + 1, :])
        for cs in cols:
            dec = de[:, cs]
            emit_dz(2, cs, dec * cq_ref[:, cs])
            dcq_ext = with_future(dec * zc(2, cs), dcq_c, cs)
            zcc, zh = zc(3, cs), zc(4, cs)
            qv = zcc * zh
            dq = jnp.zeros_like(qv)
            for j in range(SCONV_K):
                adv = _future(dcq_ext, SCONV_K - 1 - j, ts)
                acc(A_SW + j, cs, adv * qv)
                dq = dq + vrow(V_SW + j, cs) * adv
            emit_dz(3, cs, dq * zh)
            emit_dz(4, cs, dq * zcc)
            log_a = -LRU_C * r_ref[:, cs] * sp[:, cs]
            a = jnp.exp(log_a)
            a_keep[:, cs] = a
            m_keep[:, cs] = jnp.sqrt(_one_minus_sq(a, log_a))
            c_scr[:, cs] = _future(with_future(a, acarry, cs), 1, ts)
            b_scr[:, cs] = dh[:, cs]
        _scan_rev(c_scr, b_scr, g_scr, gcarry, ts)

        for hh, cs in enumerate(cols):
            gs, a, mult = g_scr[:, cs], a_keep[:, cs], m_keep[:, cs]
            r, ig, v = r_ref[:, cs], ig_ref[:, cs], v_ref[:, cs]
            hprev = _past(jnp.concatenate([jnp.where(t == 0, 0.0, hh_ref[:, cs]), h_ref[:, cs]], axis=0), 1, ts)
            iv = ig * v
            dlog_a = gs * hprev * a + gs * iv * (-(a * a) / mult)
            div = gs * mult
            acc(A_SP, cs, dlog_a * (-LRU_C) * r)
            dpre_r = dlog_a * (-LRU_C) * sp[:, cs] * r * (1.0 - r)
            dpre_i = div * v * ig * (1.0 - ig)
            acc(A_BR, cs, dpre_r)
            acc(A_BI, cs, dpre_i)
            dv = div * ig + _dot(dpre_r, wrt_ref[hh]) + _dot(dpre_i, wit_ref[hh])
            dwr_ref[hh] += _dot_tn(v, dpre_r)
            dwi_ref[hh] += _dot_tn(v, dpre_i)
            acc(A_CB, cs, dv)
            dv_ext = with_future(dv, dv_c, cs)
            zl = zc(1, cs)
            dzl = jnp.zeros_like(zl)
            for j in range(LRU_CONV):
                adv = _future(dv_ext, LRU_CONV - 1 - j, ts)
                acc(A_CW + j, cs, adv * zl)
                dzl = dzl + vrow(V_CW + j, cs) * adv
            emit_dz(1, cs, dzl)

        tglob = t * ts + lax.broadcasted_iota(jnp.int32, (ts, 1), 0)
        for g, win_len in enumerate(POOL_WINDOWS):
            cs = slice(g * dg, (g + 1) * dg)
            dyps = dyps_scr[:, cs]
            dpw_ref[g] += _dot_tn(p_ref[:, cs], dyps)
            dp = _dot(dyps, pwt_ref[g])
            inv_cnt = 1.0 / jnp.minimum(tglob + 1, win_len).astype(F32)
            sm = with_future(dp * inv_cnt, m_c, cs)
            sh = 1
            while sh < win_len:
                sm = sm + pltpu.roll(sm, ts + HALO - sh, 0)
                sh *= 2
            emit_dz(0, cs, sm[0:ts] - dp)

    def rev(tt):
        return (nt - 1 - tt, 0)

    halo = _halo_index(ts)
    tile = pl.BlockSpec((ts, d), rev)
    hspec = pl.BlockSpec((HALO, d), lambda tt: halo(nt - 1 - tt))
    f32o = jax.ShapeDtypeStruct((s, d), F32)
    bfo = jax.ShapeDtypeStruct((s, d), BF16)
    consts = (pwt, wrt, wit, wlot, wsct, wmixt, vec)
    return _pallas(
        body, name=name, grid=(nt,),
        in_specs=[tile, tile, pl.BlockSpec((ts, 8 * d), rev), tile, hspec] + [tile] * 8
        + [_const_spec(c.shape) for c in consts],
        out_specs=[pl.BlockSpec((ts, 8 * d), rev), tile, tile, tile,
                   _acc_spec((A_ROWS, d)), _acc_spec((1, 8 * d)),
                   _acc_spec(pwt.shape), _acc_spec(wrt.shape), _acc_spec(wit.shape)],
        out_shape=[jax.ShapeDtypeStruct((s, 8 * d), BF16), f32o, bfo, bfo,
                   jax.ShapeDtypeStruct((A_ROWS, d), F32), jax.ShapeDtypeStruct((1, 8 * d), F32),
                   jax.ShapeDtypeStruct(pwt.shape, F32), jax.ShapeDtypeStruct(wrt.shape, F32),
                   jax.ShapeDtypeStruct(wit.shape, F32)],
        scratch_shapes=[pltpu.VMEM((ts, d), F32)] * 6 + [pltpu.VMEM((SUBLANES, d), F32)]
        + [pltpu.VMEM((HALO, d), F32)] * 4,
        semantics=("arbitrary",), args=(dx1, rpre, z, h, h, ypre, yl, yc, pp, vv, rr, ii, cq, *consts), comm=comm)


def _softmax_rows(sc):
    mx = jnp.max(sc, axis=-1, keepdims=True)
    ex = jnp.exp(sc - mx)
    return ex * (1.0 / jnp.sum(ex, axis=-1, keepdims=True))


def _attn_fwd(x1, wq, wo, kt, vv, vec, name):
    s, d = x1.shape
    ts = _tile(s, TS_ATTN)
    hd = d // X_HEADS
    scale = hd ** -0.5

    def body(x_ref, wq_ref, wo_ref, kt_ref, v_ref, vec_ref, x2_ref, rpre_ref, q_ref, o_ref):
        xv = x_ref[...]
        q = _dot(xv, wq_ref[...]).astype(BF16)
        q_ref[...] = q
        for hh in range(X_HEADS):
            cs = slice(hh * hd, (hh + 1) * hd)
            p = _softmax_rows(_dot(q[:, cs], kt_ref[cs, :]) * scale)
            o_ref[:, cs] = _dot(p, v_ref[:, cs]).astype(BF16)
        rpre = ALPHA * xv + _dot(o_ref[...], wo_ref[...])
        rpre_ref[...] = rpre
        x2_ref[...] = _ln_fwd(rpre, vec_ref[V_G + 1:V_G + 2, :], vec_ref[V_B + 1:V_B + 2, :])

    tile = pl.BlockSpec((ts, d), lambda t: (t, 0))
    f32o = jax.ShapeDtypeStruct((s, d), F32)
    bfo = jax.ShapeDtypeStruct((s, d), BF16)
    consts = (wq, wo, kt, vv, vec)
    return pl.pallas_call(
        body, name=name, grid=(s // ts,),
        in_specs=[tile] + [_const_spec(c.shape) for c in consts],
        out_specs=[tile] * 4, out_shape=[f32o, f32o, bfo, bfo],
        compiler_params=_cparams(("parallel",)),
    )(x1, *consts)


def _attn_bwd(dx2, rpre, q, wqt, wot, kk, kt, vt, vec, name):
    s, d = dx2.shape
    ts = _tile(s, TS_ATTN)
    nm = kk.shape[0]
    hd = d // X_HEADS
    scale = hd ** -0.5

    def body(dx2_ref, rpre_ref, q_ref, wqt_ref, wot_ref, k_ref, kt_ref, vt_ref, vec_ref,
             dx1_ref, dq_ref, dr_ref, dk_ref, dv_ref, ln_ref):
        @pl.when(pl.program_id(0) == 0)
        def _():
            for ref in (dk_ref, dv_ref, ln_ref):
                ref[...] = jnp.zeros_like(ref)

        dyv = dx2_ref[...]
        dr, dyy = _ln_bwd(dyv, rpre_ref[...], vec_ref[V_G + 1:V_G + 2, :])
        ln_ref[0:1, :] += _colsum(dyy)
        ln_ref[1:2, :] += _colsum(dyv)
        dr_ref[...] = dr.astype(BF16)
        do = _dot(dr, wot_ref[...])
        q = q_ref[...]
        for hh in range(X_HEADS):
            cs = slice(hh * hd, (hh + 1) * hd)
            p = _softmax_rows(_dot(q[:, cs], kt_ref[cs, :]) * scale)
            dp = _dot(do[:, cs], vt_ref[cs, :])
            ds = p * (dp - jnp.sum(dp * p, axis=-1, keepdims=True)) * scale
            dq_ref[:, cs] = _dot(ds, k_ref[:, cs]).astype(BF16)
            dk_ref[:, cs] += _dot_tn(ds, q[:, cs])
            dv_ref[:, cs] += _dot_tn(p, do[:, cs])
        dx1_ref[...] = ALPHA * dr + _dot(dq_ref[...], wqt_ref[...])

    tile = pl.BlockSpec((ts, d), lambda t: (t, 0))
    consts = (wqt, wot, kk, kt, vt, vec)
    return pl.pallas_call(
        body, name=name, grid=(s // ts,),
        in_specs=[tile, tile, tile] + [_const_spec(c.shape) for c in consts],
        out_specs=[tile, tile, tile, _acc_spec((nm, d)), _acc_spec((nm, d)), _acc_spec((2, d))],
        out_shape=[jax.ShapeDtypeStruct((s, d), F32), jax.ShapeDtypeStruct((s, d), BF16),
                   jax.ShapeDtypeStruct((s, d), BF16), jax.ShapeDtypeStruct((nm, d), F32),
                   jax.ShapeDtypeStruct((nm, d), F32), jax.ShapeDtypeStruct((2, d), F32)],
        compiler_params=_cparams(("arbitrary",)),
    )(dx2, rpre, q, *consts)


def _ffn_out(x2, hgu, wd, vec, name):
    s, d = x2.shape
    ff = wd.shape[0]
    ts = _tile(s, TS_FFN)

    def body(x_ref, hgu_ref, wd_ref, vec_ref, x3_ref, rpre_ref, act_ref):
        hg = hgu_ref[:, 0:ff]
        act = hg * _sigmoid(hg) * hgu_ref[:, ff:2 * ff]
        act_ref[...] = act.astype(BF16)
        rpre = ALPHA * x_ref[...] + _dot(act, wd_ref[...])
        rpre_ref[...] = rpre
        x3_ref[...] = _ln_fwd(rpre, vec_ref[V_G + 2:V_G + 3, :], vec_ref[V_B + 2:V_B + 3, :])

    tile = pl.BlockSpec((ts, d), lambda t: (t, 0))
    return pl.pallas_call(
        body, name=name, grid=(s // ts,),
        in_specs=[tile, pl.BlockSpec((ts, 2 * ff), lambda t: (t, 0)), _const_spec(wd.shape), _const_spec(vec.shape)],
        out_specs=[tile, tile, pl.BlockSpec((ts, ff), lambda t: (t, 0))],
        out_shape=[jax.ShapeDtypeStruct((s, d), F32), jax.ShapeDtypeStruct((s, d), F32),
                   jax.ShapeDtypeStruct((s, ff), BF16)],
        compiler_params=_cparams(("parallel",)),
    )(x2, hgu, wd, vec)


def _ffn_bwd(dy, rpre, hgu, wdt, wgt, wut, vec, name, comm=()):
    s, d = dy.shape
    ff = wgt.shape[0]
    ts = _tile(s, TS_FFN)

    def body(dy_ref, rpre_ref, hgu_ref, wdt_ref, wgt_ref, wut_ref, vec_ref, dx_ref, dr_ref, dhgu_ref, ln_ref):
        @pl.when(pl.program_id(0) == 0)
        def _():
            ln_ref[...] = jnp.zeros_like(ln_ref)

        dyv = dy_ref[...]
        dr, dyy = _ln_bwd(dyv, rpre_ref[...], vec_ref[V_G + 2:V_G + 3, :])
        ln_ref[0:1, :] += _colsum(dyy)
        ln_ref[1:2, :] += _colsum(dyv)
        dr_ref[...] = dr.astype(BF16)
        dact = _dot(dr, wdt_ref[...])
        hg = hgu_ref[:, 0:ff]
        hu = hgu_ref[:, ff:2 * ff]
        sg = _sigmoid(hg)
        dhg = dact * hu * (sg * (1.0 + hg * (1.0 - sg)))
        dhu = dact * hg * sg
        dhgu_ref[:, 0:ff] = dhg.astype(BF16)
        dhgu_ref[:, ff:2 * ff] = dhu.astype(BF16)
        dx_ref[...] = ALPHA * dr + _dot(dhg, wgt_ref[...]) + _dot(dhu, wut_ref[...])

    tile = pl.BlockSpec((ts, d), lambda t: (t, 0))
    wide = pl.BlockSpec((ts, 2 * ff), lambda t: (t, 0))
    consts = (wdt, wgt, wut, vec)
    return _pallas(
        body, name=name, grid=(s // ts,),
        in_specs=[tile, tile, wide] + [_const_spec(c.shape) for c in consts],
        out_specs=[tile, tile, wide, _acc_spec((2, d))],
        out_shape=[jax.ShapeDtypeStruct((s, d), F32), jax.ShapeDtypeStruct((s, d), BF16),
                   jax.ShapeDtypeStruct((s, 2 * ff), BF16), jax.ShapeDtypeStruct((2, d), F32)],
        semantics=("arbitrary",), args=(dy, rpre, hgu, *consts), comm=comm)


def _loss_head(y, target, name):
    s, d = y.shape
    ts = _tile(s, TS_MM)

    def body(y_ref, t_ref, loss_ref, dy_ref):
        @pl.when(pl.program_id(0) == 0)
        def _():
            loss_ref[...] = jnp.zeros_like(loss_ref)

        err = y_ref[...] - t_ref[...]
        dy_ref[...] = err / d
        per_token = jnp.mean(err * err, axis=-1, keepdims=True)
        loss_ref[...] += 0.5 * jnp.sum(per_token, axis=0, keepdims=True)

    tile = pl.BlockSpec((ts, d), lambda t: (t, 0))
    return pl.pallas_call(
        body, name=name, grid=(s // ts,),
        in_specs=[tile, tile],
        out_specs=[_acc_spec((1, 1)), tile],
        out_shape=[jax.ShapeDtypeStruct((1, 1), F32), jax.ShapeDtypeStruct((s, d), F32)],
        compiler_params=_cparams(("arbitrary",)),
    )(y, target)


SHARD_AXIS = {"w_in": 1, "pool_w": 1, "lru_w_out": 0, "sconv_w_out": 0, "w_mix_out": 0,
              "xa_w_q": 0, "xa_w_k": 0, "xa_w_v": 0, "xa_w_o": 0,
              "ffn_w_gate": 0, "ffn_w_up": 0, "ffn_w_down": 0,
              "lru_conv_w": 1, "sconv_w": 1, "ln_g": 1, "ln_b": 1}
STORED_TRANSPOSED = ("ffn_w_gate", "ffn_w_up")
GROUP_IN = ("w_in",)
GROUP_MIXER = ("pool_w", "lru_w_out", "sconv_w_out", "w_mix_out")
GROUP_ATTN = ("xa_w_q", "xa_w_k", "xa_w_v", "xa_w_o")
GROUP_FFN = ("ffn_w_gate", "ffn_w_up", "ffn_w_down")
GROUP_VECTORS = ("lru_conv_w", "sconv_w", "ln_g", "ln_b")
REPLICATED = ("b_in", "pool_scale", "lru_conv_b", "lru_w_r", "lru_b_r", "lru_w_i", "lru_b_i", "lru_lambda")
WEIGHTS = ("w_in", "b_in", "pool_w", "pool_scale", "lru_conv_w", "lru_conv_b", "lru_w_r", "lru_b_r", "lru_w_i",
           "lru_b_i", "lru_lambda", "lru_w_out", "sconv_w", "sconv_w_out", "w_mix_out", "xa_w_q", "xa_w_k",
           "xa_w_v", "xa_w_o", "ffn_w_gate", "ffn_w_up", "ffn_w_down", "ln_g", "ln_b")


def _pack(arrs, width, lead=0, row_multiple=ROW_PAD):
    head = arrs[0].shape[:lead]
    flat = jnp.concatenate([a.reshape(head + (-1,)) for a in arrs], axis=lead)
    n = flat.shape[-1]
    chunk = width * row_multiple
    total = -(-n // chunk) * chunk
    if total != n:
        flat = jnp.pad(flat, [(0, 0)] * lead + [(0, total - n)])
    return flat.reshape(head + (total // width, width))


def _unpack(buf, shapes, lead=0):
    head = buf.shape[:lead]
    flat = buf.reshape(head + (-1,))
    out, off = [], 0
    for shp in shapes:
        n = math.prod(shp)
        out.append(flat[..., off:off + n].reshape(head + tuple(shp)))
        off += n
    return out


def _split8(a, axis):
    shp = a.shape
    a = a.reshape(shp[:axis] + (N_DEV, shp[axis] // N_DEV) + shp[axis + 1:])
    return jnp.moveaxis(a, axis, 0)


def _join8(a, axis):
    a = jnp.moveaxis(a, 0, axis)
    shp = a.shape
    return a.reshape(shp[:axis] + (shp[axis] * shp[axis + 1],) + shp[axis + 2:])


def _t(a):
    return jnp.swapaxes(a, -1, -2)


def _stored(name, a):
    return _t(a) if name in STORED_TRANSPOSED else a


def kernel(x, mem, w_in, b_in, pool_w, pool_scale, lru_conv_w, lru_conv_b, lru_w_r, lru_b_r, lru_w_i, lru_b_i, lru_lambda, lru_w_out, sconv_w, sconv_w_out, w_mix_out, xa_w_q, xa_w_k, xa_w_v, xa_w_o, ffn_w_gate, ffn_w_up, ffn_w_down, ln_g, ln_b, loss_target, m_w_in, m_b_in, m_pool_w, m_pool_scale, m_lru_conv_w, m_lru_conv_b, m_lru_w_r, m_lru_b_r, m_lru_w_i, m_lru_b_i, m_lru_lambda, m_lru_w_out, m_sconv_w, m_sconv_w_out, m_w_mix_out, m_xa_w_q, m_xa_w_k, m_xa_w_v, m_xa_w_o, m_ffn_w_gate, m_ffn_w_up, m_ffn_w_down, m_ln_g, m_ln_b, v_w_in, v_b_in, v_pool_w, v_pool_scale, v_lru_conv_w, v_lru_conv_b, v_lru_w_r, v_lru_b_r, v_lru_w_i, v_lru_b_i, v_lru_lambda, v_lru_w_out, v_sconv_w, v_sconv_w_out, v_w_mix_out, v_xa_w_q, v_xa_w_k, v_xa_w_v, v_xa_w_o, v_ffn_w_gate, v_ffn_w_up, v_ffn_w_down, v_ln_g, v_ln_b):
    args = dict(locals())
    w = {n: args[n] for n in WEIGHTS}
    mom_m = {n: args["m_" + n] for n in WEIGHTS}
    mom_v = {n: args["v_" + n] for n in WEIGHTS}
    depth = w_in.shape[0]
    s, d = x.shape[1], x.shape[2]
    nm = mem.shape[1]
    ff = ffn_w_gate.shape[2] * N_DEV
    xs = x.reshape(s, d)
    mems = mem.reshape(nm, d)
    target = loss_target.reshape(s, d)

    def shard(t, n, l):
        return _stored(n, t[n][l])

    def pack_shards(t, names, l, dtype=None):
        arrs = [shard(t, n, l) for n in names]
        return _pack([a if dtype is None else a.astype(dtype) for a in arrs], d)

    def unpack_gathered(buf, names):
        pieces = _unpack(buf, [shard(w, n, 0).shape for n in names], lead=1)
        return {n: (p if n == "w_in" else _join8(p, SHARD_AXIS[n])) for n, p in zip(names, pieces)}

    def layer_vec(l, fw):
        vec = jnp.zeros((V_ROWS, d), F32)
        vec = vec.at[V_PSCALE].set(pool_scale[l]).at[V_CW:V_CW + LRU_CONV].set(fw["lru_conv_w"])
        vec = vec.at[V_CB].set(lru_conv_b[l]).at[V_BR].set(lru_b_r[l]).at[V_BI].set(lru_b_i[l])
        vec = vec.at[V_LAM].set(lru_lambda[l]).at[V_SW:V_SW + SCONV_K].set(fw["sconv_w"])
        return vec.at[V_G:V_G + 3].set(fw["ln_g"]).at[V_B:V_B + 3].set(fw["ln_b"])

    def layer_params(l, fw):
        return dict(
            vec=layer_vec(l, fw), wint=_t(fw["w_in"]).reshape(1, 8 * d, d),
            pw=fw["pool_w"], pwt=_t(fw["pool_w"]),
            wr=lru_w_r[l].astype(BF16), wi=lru_w_i[l].astype(BF16),
            wrt=_t(lru_w_r[l]).astype(BF16), wit=_t(lru_w_i[l]).astype(BF16),
            wlo=fw["lru_w_out"], wlot=_t(fw["lru_w_out"]),
            wsc=fw["sconv_w_out"], wsct=_t(fw["sconv_w_out"]),
            wmix=fw["w_mix_out"], wmixt=_t(fw["w_mix_out"]),
            wq=fw["xa_w_q"], wqt=_t(fw["xa_w_q"]), wo=fw["xa_w_o"], wot=_t(fw["xa_w_o"]),
            wkv=jnp.stack([fw["xa_w_k"], fw["xa_w_v"]]),
            wgu=jnp.stack([_t(fw["ffn_w_gate"]), _t(fw["ffn_w_up"])]),
            wgt=fw["ffn_w_gate"], wut=fw["ffn_w_up"],
            wd=fw["ffn_w_down"], wdt=_t(fw["ffn_w_down"]))

    later = GROUP_ATTN + GROUP_FFN
    fw0 = unpack_gathered(_all_gather(pack_shards(w, GROUP_IN + GROUP_MIXER, 0, BF16), "gather_mixer_0"),
                          GROUP_IN + GROUP_MIXER)
    vectors = _all_gather(_pack([shard(w, n, l) for l in range(depth) for n in GROUP_VECTORS], d), "gather_vectors")
    vec_pieces = _unpack(vectors, [shard(w, n, l).shape for l in range(depth) for n in GROUP_VECTORS], lead=1)
    fvec = [{n: _join8(vec_pieces[l * len(GROUP_VECTORS) + k], SHARD_AXIS[n]) for k, n in enumerate(GROUP_VECTORS)}
            for l in range(depth)]

    layers, saved = [], []
    cur = xs
    fw_next = None
    for l in range(depth):
        fw = dict(fw0 if l == 0 else fw_next)
        fw.update(fvec[l])
        comm = []
        if l == 0:
            comm.append(("gather", pack_shards(w, later, 0, BF16)))
        if l + 1 < depth:
            comm.append(("gather", pack_shards(w, GROUP_IN + GROUP_MIXER + later, l + 1, BF16)))
        z, x1, rpre1, h, ypre, yl, yc, merged, e, pp, vb, rb, ib, cq, *got = _mixer_fwd(
            cur, fw["w_in"], b_in[l].reshape(1, 8 * d), fw["pool_w"], lru_w_r[l].astype(BF16),
            lru_w_i[l].astype(BF16), fw["lru_w_out"], fw["sconv_w_out"], fw["w_mix_out"],
            layer_vec(l, fw), f"mixer_fwd_{l}", comm=comm)
        if l == 0:
            fw.update(unpack_gathered(got.pop(0), later))
        if l + 1 < depth:
            fw_next = unpack_gathered(got.pop(0), GROUP_IN + GROUP_MIXER + later)
        p = layer_params(l, fw)
        kv = _mm(mems, p["wkv"], f"kv_{l}")[0]
        kk = kv[:, :d].astype(BF16)
        vv = kv[:, d:].astype(BF16)
        x2, rpre2, q, o = _attn_fwd(x1, p["wq"], p["wo"], _t(kk), vv, p["vec"], f"attn_fwd_{l}")
        hgu = _mm(x2, p["wgu"], f"ffn_in_{l}", tm=TS_MM // 2)[0]
        x3, rpre3, act = _ffn_out(x2, hgu, p["wd"], p["vec"], f"ffn_out_{l}")
        layers.append(p)
        saved.append(dict(x0=cur, z=z, x1=x1, rpre1=rpre1, h=h, ypre=ypre, yl=yl, yc=yc, merged=merged, e=e,
                          pp=pp, vb=vb, rb=rb, ib=ib, cq=cq,
                          kk=kk, vv=vv, x2=x2, rpre2=rpre2, q=q, o=o, hgu=hgu, rpre3=rpre3, act=act))
        cur = x3

    loss_part, dcur = _loss_head(cur, target, "loss_head")
    loss = lax.psum(loss_part[0, 0], ("x", "y", "c"))

    res = {}

    def slots_of(g, names):
        return _pack([g[n] if n == "w_in" else _split8(g[n], SHARD_AXIS[n]) for n in names], d, lead=1)

    def update(received, names, l, tag):
        outs = _adamw_sum(received, *[pack_shards(t, names, l) for t in (w, mom_m, mom_v)], f"adamw_{tag}_{l}")
        shapes = [shard(w, n, l).shape for n in names]
        for n, *parts in zip(names, *[_unpack(o, shapes) for o in outs]):
            res[(n, l)] = [_stored(n, a) for a in parts]

    def settle(exchanges, got):
        for (names, l, tag, _), received in zip(exchanges, got):
            update(received, names, l, tag)

    grads = [None] * depth
    for l in reversed(range(depth)):
        p, sv = layers[l], saved[l]
        g = {}
        dx2, dr3, dhgu, ln3 = _ffn_bwd(dcur, sv["rpre3"], sv["hgu"], p["wdt"], p["wgt"], p["wut"], p["vec"],
                                       f"ffn_bwd_{l}")
        g["ffn_w_down"] = _mm_tn(sv["act"], dr3, d, f"g_wd_{l}")[0][0]
        dwgu = _mm_tn(dhgu, sv["x2"], d, f"g_wgu_{l}", tk=ff)[0][0]
        g["ffn_w_gate"], g["ffn_w_up"] = dwgu[:ff], dwgu[ff:]
        dx1, dq, dr2, dk, dv, ln2 = _attn_bwd(dx2, sv["rpre2"], sv["q"], p["wqt"], p["wot"], sv["kk"], _t(sv["kk"]),
                                              _t(sv["vv"]), p["vec"], f"attn_bwd_{l}")
        g["xa_w_o"] = _mm_tn(sv["o"], dr2, d, f"g_wo_{l}")[0][0]
        g["xa_w_q"] = _mm_tn(sv["x1"], dq, d, f"g_wq_{l}")[0][0]
        dwkv = _mm_tn(mems, jnp.concatenate([dk, dv], axis=1), d, f"g_wkv_{l}")[0]
        g["xa_w_k"], g["xa_w_v"] = dwkv[0], dwkv[1]
        ffn_slots = slots_of(g, GROUP_FFN)
        (dz, dr1, dyl, dyc, accs, dbin, g["pool_w"], g["lru_w_r"], g["lru_w_i"], received) = _mixer_bwd(
            dx1, sv["rpre1"], sv["z"], sv["h"], sv["ypre"], sv["yl"], sv["yc"], sv["pp"], sv["vb"], sv["rb"],
            sv["ib"], sv["cq"], p["pwt"], p["wrt"], p["wit"], p["wlot"], p["wsct"], p["wmixt"], p["vec"],
            f"mixer_bwd_{l}",
            comm=[("scatter", ffn_slots)])
        update(received, GROUP_FFN, l, "ffn")
        g["w_mix_out"] = _mm_tn(sv["merged"], dr1, d, f"g_wmix_{l}")[0][0]
        g["lru_w_out"] = _mm_tn(sv["h"], dyl, d, f"g_wlo_{l}")[0][0]
        g["sconv_w_out"] = _mm_tn(sv["e"], dyc, d, f"g_wsc_{l}")[0][0]
        g["b_in"] = dbin[0]
        g["pool_scale"] = accs[A_PSCALE]
        g["lru_conv_w"] = accs[A_CW:A_CW + LRU_CONV]
        g["lru_conv_b"] = accs[A_CB]
        g["lru_b_r"] = accs[A_BR]
        g["lru_b_i"] = accs[A_BI]
        g["lru_lambda"] = accs[A_SP] * (-_sigmoid(-lru_lambda[l]))
        g["sconv_w"] = accs[A_SW:A_SW + SCONV_K]
        g["ln_g"] = jnp.stack([accs[A_G], ln2[0], ln3[0]])
        g["ln_b"] = jnp.stack([accs[A_B], ln2[1], ln3[1]])
        grads[l] = g
        behind_win = [(GROUP_ATTN, l, "attn", slots_of(g, GROUP_ATTN)),
                      (GROUP_MIXER + GROUP_VECTORS, l, "mixer", slots_of(g, GROUP_MIXER + GROUP_VECTORS))]
        g["w_in"], *got = _mm_tn(sv["x0"], dz, d, f"g_win_{l}", comm=[("scatter", t[3]) for t in behind_win])
        settle(behind_win, got)
        behind_dx = [(GROUP_IN, l, "w_in", slots_of(g, GROUP_IN))]
        comm = [("scatter", behind_dx[0][3])]
        if l == 0:
            comm.append(("gather", _pack([jnp.stack([grads[k][n] for k in range(depth)]) for n in REPLICATED], d)))
        dcur, *got = _mm(dz, p["wint"], f"dx_{l}", add=dr1, add_scale=ALPHA, comm=comm)
        settle(behind_dx, got[:1])
        if l == 0:
            outs = _adamw_sum(got[1], *[_pack([t[n] for n in REPLICATED], d) for t in (w, mom_m, mom_v)],
                              "adamw_replicated")
            rep_shapes = [w[n].shape for n in REPLICATED]
            final = {n: parts for n, *parts in zip(REPLICATED, *[_unpack(o, rep_shapes) for o in outs])}
    grad_x = dcur.reshape(x.shape)

    for n in WEIGHTS:
        if n not in final:
            final[n] = [jnp.stack([res[(n, l)][k] for l in range(depth)]) for k in range(4)]
    return (loss, grad_x, *[final[n][0] for n in WEIGHTS], *[final[n][1] for n in WEIGHTS],
            *[final[n][2] for n in WEIGHTS], *[final[n][3] for n in WEIGHTS])
```

```python
import functools
import math

import jax
import jax.numpy as jnp
from jax import lax
from jax.experimental import pallas as pl
from jax.experimental.pallas import tpu as pltpu

F32 = jnp.float32
BF16 = jnp.bfloat16
MESH = pl.DeviceIdType.MESH

N_DEV = 8
LRU_HEADS = 8
LRU_CONV = 4
LRU_C = 8.0
SCONV_K = 3
POOL_WINDOWS = (2, 4, 8, 16)
X_HEADS = 4
DEPTH = 2
ALPHA = (2 * DEPTH) ** 0.25
LN_EPS = 1e-5
ADAM_LR = 0.001
ADAM_B1 = 0.9
ADAM_B2 = 0.999
ADAM_EPS = 1e-08
ADAM_WD = 0.01
ADAM_STEP = 10

HALO = 16
SUBLANES = 8
VMEM_LIMIT = 56 * 1024 * 1024
TS_MIXER = 128
TS_ATTN = 512
TS_FFN = 256
TS_MM = 1024
TK_MM = 2048
TR_ADAM = 256
ROW_PAD = 8
TS_MM_TN = 1024

V_PSCALE, V_CW, V_CB, V_BR, V_BI, V_LAM, V_SW, V_G, V_B = 0, 1, 5, 6, 7, 8, 9, 12, 15
V_ROWS = 24
A_PSCALE, A_CW, A_CB, A_BR, A_BI, A_SP, A_SW, A_G, A_B = 0, 1, 5, 6, 7, 8, 9, 12, 13
A_ROWS = 16


def _cparams(sem):
    return pltpu.CompilerParams(dimension_semantics=sem, vmem_limit_bytes=VMEM_LIMIT)


def _tile(n, pref):
    if n <= pref:
        return n
    assert n % pref == 0, (n, pref)
    return pref


def _const_spec(shape):
    nd = len(shape)
    return pl.BlockSpec(shape, lambda *_: (0,) * nd, pipeline_mode=pl.Buffered(1))


def _acc_spec(shape):
    nd = len(shape)
    return pl.BlockSpec(shape, lambda *_: (0,) * nd)


def _dot(a, b):
    return jnp.dot(a.astype(BF16), b.astype(BF16), preferred_element_type=F32)


def _dot_tn(a, b):
    return lax.dot_general(a.astype(BF16), b.astype(BF16), (((0,), (0,)), ((), ())),
                           preferred_element_type=F32)


def _sigmoid(x):
    return 0.5 * jnp.tanh(0.5 * x) + 0.5


def _softplus(y):
    e = jnp.exp(-jnp.abs(y))
    log1p = jnp.where(e < 1e-4, e * (1.0 - e * (0.5 - e * (1.0 / 3.0))), jnp.log(1.0 + e))
    return jnp.maximum(y, 0.0) + log1p


def _ln_fwd(r, g, b):
    mu = jnp.mean(r, axis=-1, keepdims=True)
    xc = r - mu
    var = jnp.mean(xc * xc, axis=-1, keepdims=True)
    return xc * lax.rsqrt(var + LN_EPS) * g + b


def _ln_bwd(dy, r, g):
    mu = jnp.mean(r, axis=-1, keepdims=True)
    xc = r - mu
    var = jnp.mean(xc * xc, axis=-1, keepdims=True)
    rstd = lax.rsqrt(var + LN_EPS)
    yhat = xc * rstd
    dyh = dy * g
    m1 = jnp.mean(dyh, axis=-1, keepdims=True)
    m2 = jnp.mean(dyh * yhat, axis=-1, keepdims=True)
    return rstd * (dyh - m1 - yhat * m2), dy * yhat


def _colsum(a):
    return jnp.sum(a, axis=0, keepdims=True)


def _position():
    return lax.axis_index("x"), lax.axis_index("y"), lax.axis_index("c")


def _gather_copies(x_ref, out_ref, send_sems, recv_sems, local_sem):
    x, y, c = _position()
    me, sibling = (x, y, c), (x, y, 1 - c)
    chips = [(1 - x, y), (x, 1 - y), (1 - x, 1 - y)]

    def slot(px, py, pc):
        return out_ref.at[4 * px + 2 * py + pc]

    def copy(k, block, to, src=None):
        return pltpu.make_async_remote_copy(
            src_ref=slot(*block) if src is None else src, dst_ref=slot(*block),
            send_sem=send_sems.at[k], recv_sem=recv_sems.at[k], device_id=to, device_id_type=MESH)

    mine = pltpu.make_async_copy(x_ref, slot(*me), local_sem)
    first = [copy(0, me, sibling, src=x_ref)]
    first += [copy(1 + j, me, (*chip, c), src=x_ref) for j, chip in enumerate(chips)]
    passed = [copy(4 + j, (*chip, c), sibling) for j, chip in enumerate(chips)]
    over_ici = [copy(1 + j, (*chip, c), me) for j, chip in enumerate(chips)]
    from_sibling = copy(0, sibling, me)
    forwarded = [copy(4 + j, (*chip, 1 - c), me) for j, chip in enumerate(chips)]
    return mine, first, passed, over_ici, from_sibling, forwarded


def _scatter_copies(g_refs, out_ref, send_sems, recv_sems, local_sem):
    x, y, c = _position()
    me = 4 * x + 2 * y + c
    offsets, rows = [], 0
    for g_ref in g_refs:
        offsets.append(rows)
        rows += g_ref.shape[1]

    def landing(g_ref, off):
        return out_ref.at[me, pl.ds(off, g_ref.shape[1])]

    mine = [pltpu.make_async_copy(g_ref.at[me], landing(g_ref, off), local_sem)
            for g_ref, off in zip(g_refs, offsets)]
    mine_all = pltpu.make_async_copy(out_ref.at[me], out_ref.at[me], local_sem)
    copies, waits = [], []
    for k in range(1, N_DEV):
        px = 1 - x if k & 4 else x
        py = 1 - y if k & 2 else y
        pc = 1 - c if k & 1 else c
        sems = dict(send_sem=send_sems.at[k - 1], recv_sem=recv_sems.at[k - 1],
                    device_id=(px, py, pc), device_id_type=MESH)
        copies += [pltpu.make_async_remote_copy(src_ref=g_ref.at[4 * px + 2 * py + pc], dst_ref=landing(g_ref, off),
                                                **sems) for g_ref, off in zip(g_refs, offsets)]
        waits.append(pltpu.make_async_remote_copy(src_ref=out_ref.at[me], dst_ref=out_ref.at[me], **sems))
    return mine, copies, mine_all, waits


def _comm_start(kind, srcs, *refs):
    if kind == "gather":
        mine, first, _, _, _, _ = _gather_copies(srcs[0], *refs)
        mine.start()
        for cp in first:
            cp.start()
    else:
        mine, copies, _, _ = _scatter_copies(srcs, *refs)
        for cp in mine + copies:
            cp.start()


def _comm_finish(kind, srcs, *refs):
    if kind == "gather":
        mine, first, passed, over_ici, from_sibling, forwarded = _gather_copies(srcs[0], *refs)
        for arrival, forward in zip(over_ici, passed):
            arrival.wait_recv()
            forward.start()
        from_sibling.wait_recv()
        for arrival in forwarded:
            arrival.wait_recv()
        for cp in first + passed:
            cp.wait_send()
        mine.wait()
    else:
        _, _, mine_all, waits = _scatter_copies(srcs, *refs)
        for cp in waits:
            cp.wait_recv()
        for cp in waits:
            cp.wait_send()
        mine_all.wait()


def _comm_sources(kind, payload):
    return [payload] if kind == "gather" else list(payload)


def _comm_out_shape(kind, payload):
    srcs = _comm_sources(kind, payload)
    if kind == "gather":
        return jax.ShapeDtypeStruct((N_DEV,) + srcs[0].shape, srcs[0].dtype)
    return jax.ShapeDtypeStruct((N_DEV, sum(a.shape[1] for a in srcs), srcs[0].shape[2]), srcs[0].dtype)


COMM_SEMAPHORES = [pltpu.SemaphoreType.DMA((7,)), pltpu.SemaphoreType.DMA((7,)), pltpu.SemaphoreType.DMA]


def _pallas(body, *, name, grid, in_specs, out_specs, out_shape, semantics, args, scratch_shapes=(), comm=()):
    in_specs, out_specs, out_shape = list(in_specs), list(out_specs), list(out_shape)
    scratch_shapes = list(scratch_shapes)
    n_in, n_out, n_scr, nc = len(in_specs), len(out_specs), len(scratch_shapes), len(comm)
    if not comm:
        return pl.pallas_call(body, name=name, grid=grid, in_specs=in_specs, out_specs=out_specs, out_shape=out_shape,
                              scratch_shapes=scratch_shapes, compiler_params=_cparams(semantics))(*args)
    kinds = [kind for kind, _ in comm]
    sources = [_comm_sources(kind, payload) for kind, payload in comm]
    n_src = sum(len(srcs) for srcs in sources)

    def carrying(*refs):
        ins, rest = refs[:n_in], refs[n_in:]
        cin, rest = list(rest[:n_src]), rest[n_src:]
        outs, rest = rest[:n_out], rest[n_out:]
        cout, rest = rest[:nc], rest[nc:]
        scr, sems = rest[:n_scr], rest[n_scr:]
        ids = [pl.program_id(ax) for ax in range(len(grid))]
        first = functools.reduce(jnp.logical_and, [i == 0 for i in ids])
        last = functools.reduce(jnp.logical_and, [i == g - 1 for i, g in zip(ids, grid)])
        plans = []
        for k in range(nc):
            mine, cin = cin[:len(sources[k])], cin[len(sources[k]):]
            plans.append((kinds[k], mine, cout[k], *sems[3 * k:3 * k + 3]))

        @pl.when(first)
        def _():
            for plan in plans:
                _comm_start(*plan)

        body(*ins, *outs, *scr)

        @pl.when(last)
        def _():
            for plan in plans:
                _comm_finish(*plan)

    hbm = pl.BlockSpec(memory_space=pl.ANY)
    return pl.pallas_call(
        carrying, name=name, grid=grid,
        in_specs=in_specs + [hbm] * n_src, out_specs=out_specs + [hbm] * nc,
        out_shape=out_shape + [_comm_out_shape(kind, payload) for kind, payload in comm],
        scratch_shapes=scratch_shapes + COMM_SEMAPHORES * nc,
        compiler_params=_cparams(("arbitrary",) * len(grid)),
    )(*args, *[a for srcs in sources for a in srcs])


def _all_gather(xs, name):
    def body(x_ref, out_ref, send_sems, recv_sems, local_sem):
        _comm_start("gather", [x_ref], out_ref, send_sems, recv_sems, local_sem)
        _comm_finish("gather", [x_ref], out_ref, send_sems, recv_sems, local_sem)

    return pl.pallas_call(
        body, name=name, out_shape=_comm_out_shape("gather", xs),
        in_specs=[pl.BlockSpec(memory_space=pl.ANY)], out_specs=pl.BlockSpec(memory_space=pl.ANY),
        scratch_shapes=COMM_SEMAPHORES,
    )(xs)


def _adamw_sum(parts, w, m, v, name):
    _, rows, width = parts.shape
    tr = max(t for t in range(SUBLANES, min(rows, TR_ADAM) + 1, SUBLANES) if rows % t == 0)
    c1 = 1.0 - ADAM_B1 ** ADAM_STEP
    c2 = 1.0 - ADAM_B2 ** ADAM_STEP

    def body(p_ref, w_ref, m_ref, v_ref, g_ref, d_ref, nm_ref, nv_ref):
        g = p_ref[0].astype(F32)
        for k in range(1, N_DEV):
            g = g + p_ref[k].astype(F32)
        nm = ADAM_B1 * m_ref[...] + (1.0 - ADAM_B1) * g
        nv = ADAM_B2 * v_ref[...] + (1.0 - ADAM_B2) * (g * g)
        m_hat = nm / c1
        v_hat = nv / c2
        g_ref[...] = g
        d_ref[...] = -ADAM_LR * (m_hat / (jnp.sqrt(v_hat) + ADAM_EPS) + ADAM_WD * w_ref[...])
        nm_ref[...] = nm
        nv_ref[...] = nv

    spec = pl.BlockSpec((tr, width), lambda i: (i, 0))
    out = jax.ShapeDtypeStruct((rows, width), F32)
    return pl.pallas_call(
        body, name=name, grid=(rows // tr,),
        in_specs=[pl.BlockSpec((N_DEV, tr, width), lambda i: (0, i, 0)), spec, spec, spec],
        out_specs=[spec, spec, spec, spec], out_shape=[out, out, out, out],
        compiler_params=_cparams(("parallel",)),
    )(parts, w, m, v)


def _mm(a, wb, name, bias=None, add=None, add_scale=1.0, out_dtype=F32, tm=None, comm=()):
    m, k = a.shape
    nb, k2, tn = wb.shape
    assert k == k2
    tm = _tile(m, TS_MM if tm is None else tm)
    tk = _tile(k, TK_MM)
    nk = k // tk

    def body(*refs):
        a_ref, w_ref = refs[0], refs[1]
        pos = 2
        b_ref = add_ref = None
        if bias is not None:
            b_ref = refs[pos]
            pos += 1
        if add is not None:
            add_ref = refs[pos]
            pos += 1
        o_ref = refs[pos]

        def finish(r):
            if b_ref is not None:
                r = r + b_ref[...]
            if add_ref is not None:
                r = r + add_scale * add_ref[...]
            o_ref[...] = r.astype(o_ref.dtype)

        if nk == 1:
            finish(_dot(a_ref[...], w_ref[...]))
            return
        acc_ref = refs[pos + 1]
        kk = pl.program_id(2)

        @pl.when(kk == 0)
        def _():
            acc_ref[...] = jnp.zeros_like(acc_ref)

        acc_ref[...] += _dot(a_ref[...], w_ref[...])

        @pl.when(kk == nk - 1)
        def _():
            finish(acc_ref[...])

    in_specs = [pl.BlockSpec((tm, tk), lambda j, i, kk: (i, kk)),
                pl.BlockSpec((None, tk, tn), lambda j, i, kk: (j, kk, 0))]
    args = [a, wb]
    if bias is not None:
        in_specs.append(pl.BlockSpec((1, tn), lambda j, i, kk: (0, j)))
        args.append(bias)
    if add is not None:
        in_specs.append(pl.BlockSpec((tm, tn), lambda j, i, kk: (i, j)))
        args.append(add)
    return _pallas(
        body, name=name, grid=(nb, m // tm, nk),
        in_specs=in_specs,
        out_specs=[pl.BlockSpec((tm, tn), lambda j, i, kk: (i, j))],
        out_shape=[jax.ShapeDtypeStruct((m, nb * tn), out_dtype)],
        scratch_shapes=[pltpu.VMEM((tm, tn), F32)] if nk > 1 else [],
        semantics=("parallel", "parallel", "arbitrary"), args=args, comm=comm)


def _mm_tn(a, b, tn, name, tk=None, comm=()):
    s, k = a.shape
    s2, n = b.shape
    assert s == s2 and n % tn == 0
    nb = n // tn
    ts = _tile(s, TS_MM_TN)
    tk = k if tk is None else tk
    assert k % tk == 0

    def body(a_ref, b_ref, o_ref):
        @pl.when(pl.program_id(2) == 0)
        def _():
            o_ref[...] = jnp.zeros_like(o_ref)

        o_ref[...] += _dot_tn(a_ref[...], b_ref[...])

    return _pallas(
        body, name=name, grid=(nb, k // tk, s // ts),
        in_specs=[pl.BlockSpec((ts, tk), lambda j, kb, i: (i, kb)),
                  pl.BlockSpec((ts, tn), lambda j, kb, i: (i, j))],
        out_specs=[pl.BlockSpec((None, tk, tn), lambda j, kb, i: (j, kb, 0))],
        out_shape=[jax.ShapeDtypeStruct((nb, k, tn), F32)],
        semantics=("parallel", "parallel", "arbitrary"), args=(a, b), comm=comm)


def _scan_fwd(a_ref, b_ref, h_ref, carry_ref, ts):
    rowid = lax.broadcasted_iota(jnp.int32, (SUBLANES, 1), 0)

    def group(gi, hprev):
        r0 = pl.multiple_of(gi * SUBLANES, SUBLANES)
        a = a_ref[pl.ds(r0, SUBLANES), :]
        b = b_ref[pl.ds(r0, SUBLANES), :]
        for d in (1, 2, 4):
            a_sh = jnp.where(rowid >= d, pltpu.roll(a, d, 0), 1.0)
            b_sh = jnp.where(rowid >= d, pltpu.roll(b, d, 0), 0.0)
            b = a * b_sh + b
            a = a * a_sh
        hh = a * hprev + b
        h_ref[pl.ds(r0, SUBLANES), :] = hh
        return hh[SUBLANES - 1:SUBLANES, :]

    last = lax.fori_loop(0, ts // SUBLANES, group, carry_ref[0:1, :])
    carry_ref[0:1, :] = last


def _scan_rev(c_ref, b_ref, g_ref, carry_ref, ts):
    rowid = lax.broadcasted_iota(jnp.int32, (SUBLANES, 1), 0)
    ng = ts // SUBLANES

    def group(gi, gnext):
        r0 = pl.multiple_of((ng - 1 - gi) * SUBLANES, SUBLANES)
        c = c_ref[pl.ds(r0, SUBLANES), :]
        b = b_ref[pl.ds(r0, SUBLANES), :]
        for d in (1, 2, 4):
            keep = rowid < SUBLANES - d
            c_sh = jnp.where(keep, pltpu.roll(c, SUBLANES - d, 0), 1.0)
            b_sh = jnp.where(keep, pltpu.roll(b, SUBLANES - d, 0), 0.0)
            b = c * b_sh + b
            c = c * c_sh
        gg = c * gnext + b
        g_ref[pl.ds(r0, SUBLANES), :] = gg
        return gg[0:1, :]

    first = lax.fori_loop(0, ng, group, carry_ref[0:1, :])
    carry_ref[0:1, :] = first


def _past(ext, sh, ts):
    if sh == 0:
        return ext[HALO:HALO + ts]
    return pltpu.roll(ext, sh, 0)[HALO:HALO + ts]


def _future(ext, sh, ts):
    if sh == 0:
        return ext[0:ts]
    return pltpu.roll(ext, ts + HALO - sh, 0)[0:ts]


def _one_minus_sq(a, log_a):
    x = 2.0 * log_a
    series = -x * (1.0 + x * (0.5 + x * (1.0 / 6.0 + x * (1.0 / 24.0))))
    return jnp.where(x > -0.02, series, 1.0 - a * a)


def _halo_index(ts):
    blocks = ts // HALO
    return lambda t: (jnp.maximum(t * blocks - 1, 0), 0)


def _head_columns(d):
    cw = d // LRU_HEADS
    return [slice(c * cw, (c + 1) * cw) for c in range(LRU_HEADS)]


def _shift(cs, off):
    return slice(cs.start + off, cs.stop + off)


def _mixer_fwd(x, win, b_in, pw, wr, wi, wlo, wsc, wmix, vec, name, comm=()):
    s, d = x.shape
    ts = _tile(s, TS_MIXER)
    nt = s // ts
    dg = d // len(POOL_WINDOWS)
    nblk = win.shape[0]
    cols = _head_columns(d)

    def body(x_ref, xn_ref, win_ref, bin_ref, pw_ref, wr_ref, wi_ref, wlo_ref, wsc_ref, wmix_ref, vec_ref,
             z_hbm, x1_ref, rpre_ref, h_ref, ypre_ref, yl_ref, yc_ref, mg_ref, e_ref, p_ref, v_ref, r_ref, ig_ref,
             cq_ref, z_even, z_odd, zhist, a_scr, b_scr, hcarry, z_sem):
        i = pl.program_id(0)
        first = i == 0

        def project_block(xb, dst, k):
            dst[:, k * d:(k + 1) * d] = _dot(xb, win_ref[k]) + bin_ref[:, k * d:(k + 1) * d]

        @pl.when(first)
        def _():
            hcarry[...] = jnp.zeros_like(hcarry)
            zhist[...] = jnp.zeros_like(zhist)
            xb = x_ref[...].astype(BF16)
            for k in range(nblk):
                project_block(xb, z_even, k)

        def vrow(k, cs):
            return vec_ref[k:k + 1, cs]

        def step(zc, zn):
            z_out = pltpu.make_async_copy(zc, z_hbm.at[pl.ds(pl.multiple_of(i * ts, ts), ts), :], z_sem)
            z_out.start()
            xb = xn_ref[...].astype(BF16)
            tglob = i * ts + lax.broadcasted_iota(jnp.int32, (ts, 1), 0)
            sp = _softplus(-vec_ref[V_LAM:V_LAM + 1, :])

            def with_history(k, cs):
                kc = _shift(cs, k * d)
                return jnp.concatenate([jnp.where(first, 0.0, zhist[:, kc]), zc[:, kc]], axis=0)

            for hh, cs in enumerate(cols):
                if hh < nblk:
                    project_block(xb, zn, hh)
                win_len = POOL_WINDOWS[cs.start // dg]
                ext = with_history(0, cs)
                sm = ext
                sh = 1
                while sh < win_len:
                    sm = sm + pltpu.roll(sm, sh, 0)
                    sh *= 2
                inv_cnt = 1.0 / jnp.minimum(tglob + 1, win_len).astype(F32)
                p_ref[:, cs] = (sm[HALO:HALO + ts] * inv_cnt - ext[HALO:HALO + ts]).astype(BF16)
                ext = with_history(1, cs)
                v = vrow(V_CB, cs)
                for j in range(LRU_CONV):
                    v = v + vrow(V_CW + j, cs) * _past(ext, LRU_CONV - 1 - j, ts)
                r = _sigmoid(_dot(v, wr_ref[hh]) + vrow(V_BR, cs))
                ig = _sigmoid(_dot(v, wi_ref[hh]) + vrow(V_BI, cs))
                log_a = -LRU_C * r * sp[:, cs]
                a = jnp.exp(log_a)
                a_scr[:, cs] = a
                b_scr[:, cs] = jnp.sqrt(_one_minus_sq(a, log_a)) * (ig * v)
                v_ref[:, cs] = v
                r_ref[:, cs] = r
                ig_ref[:, cs] = ig
                ext = with_history(3, cs) * with_history(4, cs)
                cq = jnp.zeros((ts, cs.stop - cs.start), F32)
                for j in range(SCONV_K):
                    cq = cq + vrow(V_SW + j, cs) * _past(ext, SCONV_K - 1 - j, ts)
                cq_ref[:, cs] = cq
                e_ref[:, cs] = (zc[:, _shift(cs, 2 * d)] * cq).astype(BF16)
            for k in range(len(cols), nblk):
                project_block(xb, zn, k)
            _scan_fwd(a_scr, b_scr, h_ref, hcarry, ts)
            ypre = jnp.concatenate([_dot(p_ref[:, g * dg:(g + 1) * dg], pw_ref[g])
                                    for g in range(len(POOL_WINDOWS))], axis=1)
            yl = _dot(h_ref[...], wlo_ref[...])
            yc = _dot(e_ref[...], wsc_ref[...])
            ypre_ref[...] = ypre
            yl_ref[...] = yl
            yc_ref[...] = yc
            for cs in cols:
                merged = (_sigmoid(zc[:, _shift(cs, 5 * d)]) * (ypre[:, cs] * vrow(V_PSCALE, cs))
                          + _sigmoid(zc[:, _shift(cs, 6 * d)]) * yl[:, cs]
                          + _sigmoid(zc[:, _shift(cs, 7 * d)]) * yc[:, cs])
                mg_ref[:, cs] = merged.astype(BF16)
            rpre = ALPHA * x_ref[...] + _dot(mg_ref[...], wmix_ref[...])
            x1_ref[...] = _ln_fwd(rpre, vec_ref[V_G:V_G + 1, :], vec_ref[V_B:V_B + 1, :])
            rpre_ref[...] = rpre
            zhist[...] = zc[ts - HALO:ts, :]
            z_out.wait()

        parity = lax.rem(i, 2)

        @pl.when(parity == 0)
        def _():
            step(z_even, z_odd)

        @pl.when(parity == 1)
        def _():
            step(z_odd, z_even)

    tile = pl.BlockSpec((ts, d), lambda t: (t, 0))
    f32o = jax.ShapeDtypeStruct((s, d), F32)
    bfo = jax.ShapeDtypeStruct((s, d), BF16)
    consts = (win, b_in, pw, wr, wi, wlo, wsc, wmix, vec)
    return _pallas(
        body, name=name, grid=(nt,),
        in_specs=[tile, pl.BlockSpec((ts, d), lambda t: (jnp.minimum(t + 1, nt - 1), 0))]
        + [_const_spec(c.shape) for c in consts],
        out_specs=[pl.BlockSpec(memory_space=pl.ANY)] + [tile] * 13,
        out_shape=[jax.ShapeDtypeStruct((s, nblk * d), F32), f32o, f32o, f32o, f32o, f32o, f32o, bfo, bfo, bfo,
                   f32o, f32o, f32o, f32o],
        scratch_shapes=[pltpu.VMEM((ts, nblk * d), F32)] * 2 + [pltpu.VMEM((HALO, nblk * d), F32)]
        + [pltpu.VMEM((ts, d), F32)] * 2 + [pltpu.VMEM((SUBLANES, d), F32), pltpu.SemaphoreType.DMA],
        semantics=("arbitrary",), args=(x, x, *consts), comm=comm)


def _mixer_bwd(dx1, rpre, z, h, ypre, yl, yc, pp, vv, rr, ii, cq, pwt, wrt, wit, wlot, wsct, wmixt, vec, name,
               comm=()):
    s, d = dx1.shape
    ts = _tile(s, TS_MIXER)
    nt = s // ts
    dg = d // len(POOL_WINDOWS)
    cols = _head_columns(d)

    def body(dx1_ref, rpre_ref, z_ref, h_ref, hh_ref, ypre_ref, yl_ref, yc_ref, p_ref, v_ref, r_ref, ig_ref, cq_ref,
             pwt_ref, wrt_ref, wit_ref, wlot_ref, wsct_ref, wmixt_ref, vec_ref,
             dz_ref, dr_ref, dyl_ref, dyc_ref, acc_ref, dbin_ref, dpw_ref, dwr_ref, dwi_ref,
             c_scr, b_scr, g_scr, dyps_scr, a_keep, m_keep, gcarry, acarry, dcq_c, dv_c, m_c):
        i = pl.program_id(0)
        t = nt - 1 - i

        @pl.when(i == 0)
        def _():
            for ref in (gcarry, acarry, dcq_c, dv_c, m_c, acc_ref, dbin_ref, dpw_ref, dwr_ref, dwi_ref):
                ref[...] = jnp.zeros_like(ref)

        def vrow(k, cs):
            return vec_ref[k:k + 1, cs]

        def zc(k, cs):
            return z_ref[:, _shift(cs, k * d)]

        def acc(row, cs, val):
            acc_ref[row:row + 1, cs] += _colsum(val)

        def emit_dz(k, cs, val):
            kc = _shift(cs, k * d)
            dz_ref[:, kc] = val.astype(BF16)
            dbin_ref[:, kc] += _colsum(val)

        def with_future(tile_val, carry_ref, cs):
            ext = jnp.concatenate([tile_val, carry_ref[:, cs]], axis=0)
            carry_ref[:, cs] = tile_val[0:HALO, :]
            return ext

        dx1v = dx1_ref[...]
        dr, dyy = _ln_bwd(dx1v, rpre_ref[...], vec_ref[V_G:V_G + 1, :])
        acc_ref[A_G:A_G + 1, :] += _colsum(dyy)
        acc_ref[A_B:A_B + 1, :] += _colsum(dx1v)
        dr_ref[...] = dr
        dmg = _dot(dr, wmixt_ref[...])
        for cs in cols:
            dm = dmg[:, cs]
            ypre = ypre_ref[:, cs]
            ys = (ypre * vrow(V_PSCALE, cs), yl_ref[:, cs], yc_ref[:, cs])
            dys = []
            for k in range(3):
                gk = _sigmoid(zc(5 + k, cs))
                emit_dz(5 + k, cs, dm * ys[k] * gk * (1.0 - gk))
                dys.append(dm * gk)
            acc(A_PSCALE, cs, dys[0] * ypre)
            dyps_scr[:, cs] = dys[0] * vrow(V_PSCALE, cs)
            dyl_ref[:, cs] = dys[1].astype(BF16)
            dyc_ref[:, cs] = dys[2].astype(BF16)
        de = _dot(dyc_ref[...], wsct_ref[...])
        dh = _dot(dyl_ref[...], wlot_ref[...])

        sp = _softplus(-vec_ref[V_LAM:V_LAM + 1, :])
        for cs in cols:
            dec = de[:, cs]
            emit_dz(2, cs, dec * cq_ref[:, cs])
            dcq_ext = with_future(dec * zc(2, cs), dcq_c, cs)
            zcc, zh = zc(3, cs), zc(4, cs)
            qv = zcc * zh
            dq = jnp.zeros_like(qv)
            for j in range(SCONV_K):
                adv = _future(dcq_ext, SCONV_K - 1 - j, ts)
                acc(A_SW + j, cs, adv * qv)
                dq = dq + vrow(V_SW + j, cs) * adv
            emit_dz(3, cs, dq * zh)
            emit_dz(4, cs, dq * zcc)
            log_a = -LRU_C * r_ref[:, cs] * sp[:, cs]
            a = jnp.exp(log_a)
            a_keep[:, cs] = a
            m_keep[:, cs] = jnp.sqrt(_one_minus_sq(a, log_a))
            c_scr[:, cs] = _future(with_future(a, acarry, cs), 1, ts)
            b_scr[:, cs] = dh[:, cs]
        _scan_rev(c_scr, b_scr, g_scr, gcarry, ts)

        for hh, cs in enumerate(cols):
            gs, a, mult = g_scr[:, cs], a_keep[:, cs], m_keep[:, cs]
            r, ig, v = r_ref[:, cs], ig_ref[:, cs], v_ref[:, cs]
            hprev = _past(jnp.concatenate([jnp.where(t == 0, 0.0, hh_ref[:, cs]), h_ref[:, cs]], axis=0), 1, ts)
            iv = ig * v
            dlog_a = gs * hprev * a + gs * iv * (-(a * a) / mult)
            div = gs * mult
            acc(A_SP, cs, dlog_a * (-LRU_C) * r)
            dpre_r = dlog_a * (-LRU_C) * sp[:, cs] * r * (1.0 - r)
            dpre_i = div * v * ig * (1.0 - ig)
            acc(A_BR, cs, dpre_r)
            acc(A_BI, cs, dpre_i)
            dv = div * ig + _dot(dpre_r, wrt_ref[hh]) + _dot(dpre_i, wit_ref[hh])
            dwr_ref[hh] += _dot_tn(v, dpre_r)
            dwi_ref[hh] += _dot_tn(v, dpre_i)
            acc(A_CB, cs, dv)
            dv_ext = with_future(dv, dv_c, cs)
            zl = zc(1, cs)
            dzl = jnp.zeros_like(zl)
            for j in range(LRU_CONV):
                adv = _future(dv_ext, LRU_CONV - 1 - j, ts)
                acc(A_CW + j, cs, adv * zl)
                dzl = dzl + vrow(V_CW + j, cs) * adv
            emit_dz(1, cs, dzl)

        tglob = t * ts + lax.broadcasted_iota(jnp.int32, (ts, 1), 0)
        for g, win_len in enumerate(POOL_WINDOWS):
            cs = slice(g * dg, (g + 1) * dg)
            dyps = dyps_scr[:, cs]
            dpw_ref[g] += _dot_tn(p_ref[:, cs], dyps)
            dp = _dot(dyps, pwt_ref[g])
            inv_cnt = 1.0 / jnp.minimum(tglob + 1, win_len).astype(F32)
            sm = with_future(dp * inv_cnt, m_c, cs)
            sh = 1
            while sh < win_len:
                sm = sm + pltpu.roll(sm, ts + HALO - sh, 0)
                sh *= 2
            emit_dz(0, cs, sm[0:ts] - dp)

    def rev(tt):
        return (nt - 1 - tt, 0)

    halo = _halo_index(ts)
    tile = pl.BlockSpec((ts, d), rev)
    hspec = pl.BlockSpec((HALO, d), lambda tt: halo(nt - 1 - tt))
    f32o = jax.ShapeDtypeStruct((s, d), F32)
    bfo = jax.ShapeDtypeStruct((s, d), BF16)
    consts = (pwt, wrt, wit, wlot, wsct, wmixt, vec)
    return _pallas(
        body, name=name, grid=(nt,),
        in_specs=[tile, tile, pl.BlockSpec((ts, 8 * d), rev), tile, hspec] + [tile] * 8
        + [_const_spec(c.shape) for c in consts],
        out_specs=[pl.BlockSpec((ts, 8 * d), rev), tile, tile, tile,
                   _acc_spec((A_ROWS, d)), _acc_spec((1, 8 * d)),
                   _acc_spec(pwt.shape), _acc_spec(wrt.shape), _acc_spec(wit.shape)],
        out_shape=[jax.ShapeDtypeStruct((s, 8 * d), BF16), f32o, bfo, bfo,
                   jax.ShapeDtypeStruct((A_ROWS, d), F32), jax.ShapeDtypeStruct((1, 8 * d), F32),
                   jax.ShapeDtypeStruct(pwt.shape, F32), jax.ShapeDtypeStruct(wrt.shape, F32),
                   jax.ShapeDtypeStruct(wit.shape, F32)],
        scratch_shapes=[pltpu.VMEM((ts, d), F32)] * 6 + [pltpu.VMEM((SUBLANES, d), F32)]
        + [pltpu.VMEM((HALO, d), F32)] * 4,
        semantics=("arbitrary",), args=(dx1, rpre, z, h, h, ypre, yl, yc, pp, vv, rr, ii, cq, *consts), comm=comm)


def _softmax_rows(sc):
    mx = jnp.max(sc, axis=-1, keepdims=True)
    ex = jnp.exp(sc - mx)
    return ex * (1.0 / jnp.sum(ex, axis=-1, keepdims=True))


def _attn_fwd(x1, wq, wo, kt, vv, vec, name):
    s, d = x1.shape
    ts = _tile(s, TS_ATTN)
    hd = d // X_HEADS
    scale = hd ** -0.5

    def body(x_ref, wq_ref, wo_ref, kt_ref, v_ref, vec_ref, x2_ref, rpre_ref, q_ref, o_ref):
        xv = x_ref[...]
        q = _dot(xv, wq_ref[...]).astype(BF16)
        q_ref[...] = q
        for hh in range(X_HEADS):
            cs = slice(hh * hd, (hh + 1) * hd)
            p = _softmax_rows(_dot(q[:, cs], kt_ref[cs, :]) * scale)
            o_ref[:, cs] = _dot(p, v_ref[:, cs]).astype(BF16)
        rpre = ALPHA * xv + _dot(o_ref[...], wo_ref[...])
        rpre_ref[...] = rpre
        x2_ref[...] = _ln_fwd(rpre, vec_ref[V_G + 1:V_G + 2, :], vec_ref[V_B + 1:V_B + 2, :])

    tile = pl.BlockSpec((ts, d), lambda t: (t, 0))
    f32o = jax.ShapeDtypeStruct((s, d), F32)
    bfo = jax.ShapeDtypeStruct((s, d), BF16)
    consts = (wq, wo, kt, vv, vec)
    return pl.pallas_call(
        body, name=name, grid=(s // ts,),
        in_specs=[tile] + [_const_spec(c.shape) for c in consts],
        out_specs=[tile] * 4, out_shape=[f32o, f32o, bfo, bfo],
        compiler_params=_cparams(("parallel",)),
    )(x1, *consts)


def _attn_bwd(dx2, rpre, q, wqt, wot, kk, kt, vt, vec, name):
    s, d = dx2.shape
    ts = _tile(s, TS_ATTN)
    nm = kk.shape[0]
    hd = d // X_HEADS
    scale = hd ** -0.5

    def body(dx2_ref, rpre_ref, q_ref, wqt_ref, wot_ref, k_ref, kt_ref, vt_ref, vec_ref,
             dx1_ref, dq_ref, dr_ref, dk_ref, dv_ref, ln_ref):
        @pl.when(pl.program_id(0) == 0)
        def _():
            for ref in (dk_ref, dv_ref, ln_ref):
                ref[...] = jnp.zeros_like(ref)

        dyv = dx2_ref[...]
        dr, dyy = _ln_bwd(dyv, rpre_ref[...], vec_ref[V_G + 1:V_G + 2, :])
        ln_ref[0:1, :] += _colsum(dyy)
        ln_ref[1:2, :] += _colsum(dyv)
        dr_ref[...] = dr.astype(BF16)
        do = _dot(dr, wot_ref[...])
        q = q_ref[...]
        for hh in range(X_HEADS):
            cs = slice(hh * hd, (hh + 1) * hd)
            p = _softmax_rows(_dot(q[:, cs], kt_ref[cs, :]) * scale)
            dp = _dot(do[:, cs], vt_ref[cs, :])
            ds = p * (dp - jnp.sum(dp * p, axis=-1, keepdims=True)) * scale
            dq_ref[:, cs] = _dot(ds, k_ref[:, cs]).astype(BF16)
            dk_ref[:, cs] += _dot_tn(ds, q[:, cs])
            dv_ref[:, cs] += _dot_tn(p, do[:, cs])
        dx1_ref[...] = ALPHA * dr + _dot(dq_ref[...], wqt_ref[...])

    tile = pl.BlockSpec((ts, d), lambda t: (t, 0))
    consts = (wqt, wot, kk, kt, vt, vec)
    return pl.pallas_call(
        body, name=name, grid=(s // ts,),
        in_specs=[tile, tile, tile] + [_const_spec(c.shape) for c in consts],
        out_specs=[tile, tile, tile, _acc_spec((nm, d)), _acc_spec((nm, d)), _acc_spec((2, d))],
        out_shape=[jax.ShapeDtypeStruct((s, d), F32), jax.ShapeDtypeStruct((s, d), BF16),
                   jax.ShapeDtypeStruct((s, d), BF16), jax.ShapeDtypeStruct((nm, d), F32),
                   jax.ShapeDtypeStruct((nm, d), F32), jax.ShapeDtypeStruct((2, d), F32)],
        compiler_params=_cparams(("arbitrary",)),
    )(dx2, rpre, q, *consts)


def _ffn_out(x2, hgu, wd, vec, name):
    s, d = x2.shape
    ff = wd.shape[0]
    ts = _tile(s, TS_FFN)

    def body(x_ref, hgu_ref, wd_ref, vec_ref, x3_ref, rpre_ref, act_ref):
        hg = hgu_ref[:, 0:ff]
        act = hg * _sigmoid(hg) * hgu_ref[:, ff:2 * ff]
        act_ref[...] = act.astype(BF16)
        rpre = ALPHA * x_ref[...] + _dot(act, wd_ref[...])
        rpre_ref[...] = rpre
        x3_ref[...] = _ln_fwd(rpre, vec_ref[V_G + 2:V_G + 3, :], vec_ref[V_B + 2:V_B + 3, :])

    tile = pl.BlockSpec((ts, d), lambda t: (t, 0))
    return pl.pallas_call(
        body, name=name, grid=(s // ts,),
        in_specs=[tile, pl.BlockSpec((ts, 2 * ff), lambda t: (t, 0)), _const_spec(wd.shape), _const_spec(vec.shape)],
        out_specs=[tile, tile, pl.BlockSpec((ts, ff), lambda t: (t, 0))],
        out_shape=[jax.ShapeDtypeStruct((s, d), F32), jax.ShapeDtypeStruct((s, d), F32),
                   jax.ShapeDtypeStruct((s, ff), BF16)],
        compiler_params=_cparams(("parallel",)),
    )(x2, hgu, wd, vec)


def _ffn_bwd(dy, rpre, hgu, wdt, wgt, wut, vec, name, comm=()):
    s, d = dy.shape
    ff = wgt.shape[0]
    ts = _tile(s, TS_FFN)

    def body(dy_ref, rpre_ref, hgu_ref, wdt_ref, wgt_ref, wut_ref, vec_ref, dx_ref, dr_ref, dhgu_ref, ln_ref):
        @pl.when(pl.program_id(0) == 0)
        def _():
            ln_ref[...] = jnp.zeros_like(ln_ref)

        dyv = dy_ref[...]
        dr, dyy = _ln_bwd(dyv, rpre_ref[...], vec_ref[V_G + 2:V_G + 3, :])
        ln_ref[0:1, :] += _colsum(dyy)
        ln_ref[1:2, :] += _colsum(dyv)
        dr_ref[...] = dr.astype(BF16)
        dact = _dot(dr, wdt_ref[...])
        hg = hgu_ref[:, 0:ff]
        hu = hgu_ref[:, ff:2 * ff]
        sg = _sigmoid(hg)
        dhg = dact * hu * (sg * (1.0 + hg * (1.0 - sg)))
        dhu = dact * hg * sg
        dhgu_ref[:, 0:ff] = dhg.astype(BF16)
        dhgu_ref[:, ff:2 * ff] = dhu.astype(BF16)
        dx_ref[...] = ALPHA * dr + _dot(dhg, wgt_ref[...]) + _dot(dhu, wut_ref[...])

    tile = pl.BlockSpec((ts, d), lambda t: (t, 0))
    wide = pl.BlockSpec((ts, 2 * ff), lambda t: (t, 0))
    consts = (wdt, wgt, wut, vec)
    return _pallas(
        body, name=name, grid=(s // ts,),
        in_specs=[tile, tile, wide] + [_const_spec(c.shape) for c in consts],
        out_specs=[tile, tile, wide, _acc_spec((2, d))],
        out_shape=[jax.ShapeDtypeStruct((s, d), F32), jax.ShapeDtypeStruct((s, d), BF16),
                   jax.ShapeDtypeStruct((s, 2 * ff), BF16), jax.ShapeDtypeStruct((2, d), F32)],
        semantics=("arbitrary",), args=(dy, rpre, hgu, *consts), comm=comm)


def _loss_head(y, target, name):
    s, d = y.shape
    ts = _tile(s, TS_MM)

    def body(y_ref, t_ref, loss_ref, dy_ref):
        @pl.when(pl.program_id(0) == 0)
        def _():
            loss_ref[...] = jnp.zeros_like(loss_ref)

        err = y_ref[...] - t_ref[...]
        dy_ref[...] = err / d
        per_token = jnp.mean(err * err, axis=-1, keepdims=True)
        loss_ref[...] += 0.5 * jnp.sum(per_token, axis=0, keepdims=True)

    tile = pl.BlockSpec((ts, d), lambda t: (t, 0))
    return pl.pallas_call(
        body, name=name, grid=(s // ts,),
        in_specs=[tile, tile],
        out_specs=[_acc_spec((1, 1)), tile],
        out_shape=[jax.ShapeDtypeStruct((1, 1), F32), jax.ShapeDtypeStruct((s, d), F32)],
        compiler_params=_cparams(("arbitrary",)),
    )(y, target)


SHARD_AXIS = {"w_in": 1, "pool_w": 1, "lru_w_out": 0, "sconv_w_out": 0, "w_mix_out": 0,
              "xa_w_q": 0, "xa_w_k": 0, "xa_w_v": 0, "xa_w_o": 0,
              "ffn_w_gate": 0, "ffn_w_up": 0, "ffn_w_down": 0,
              "lru_conv_w": 1, "sconv_w": 1, "ln_g": 1, "ln_b": 1}
STORED_TRANSPOSED = ("ffn_w_gate", "ffn_w_up")
GROUP_IN = ("w_in",)
GROUP_MIXER = ("pool_w", "lru_w_out", "sconv_w_out", "w_mix_out")
GROUP_ATTN = ("xa_w_q", "xa_w_k", "xa_w_v", "xa_w_o")
GROUP_FFN = ("ffn_w_gate", "ffn_w_up", "ffn_w_down")
GROUP_VECTORS = ("lru_conv_w", "sconv_w", "ln_g", "ln_b")
REPLICATED = ("b_in", "pool_scale", "lru_conv_b", "lru_w_r", "lru_b_r", "lru_w_i", "lru_b_i", "lru_lambda")
WEIGHTS = ("w_in", "b_in", "pool_w", "pool_scale", "lru_conv_w", "lru_conv_b", "lru_w_r", "lru_b_r", "lru_w_i",
           "lru_b_i", "lru_lambda", "lru_w_out", "sconv_w", "sconv_w_out", "w_mix_out", "xa_w_q", "xa_w_k",
           "xa_w_v", "xa_w_o", "ffn_w_gate", "ffn_w_up", "ffn_w_down", "ln_g", "ln_b")


def _pack(arrs, width, lead=0, row_multiple=ROW_PAD):
    head = arrs[0].shape[:lead]
    flat = jnp.concatenate([a.reshape(head + (-1,)) for a in arrs], axis=lead)
    n = flat.shape[-1]
    chunk = width * row_multiple
    total = -(-n // chunk) * chunk
    if total != n:
        flat = jnp.pad(flat, [(0, 0)] * lead + [(0, total - n)])
    return flat.reshape(head + (total // width, width))


def _unpack(buf, shapes, lead=0):
    head = buf.shape[:lead]
    flat = buf.reshape(head + (-1,))
    out, off = [], 0
    for shp in shapes:
        n = math.prod(shp)
        out.append(flat[..., off:off + n].reshape(head + tuple(shp)))
        off += n
    return out


def _split8(a, axis):
    shp = a.shape
    a = a.reshape(shp[:axis] + (N_DEV, shp[axis] // N_DEV) + shp[axis + 1:])
    return jnp.moveaxis(a, axis, 0)


def _join8(a, axis):
    a = jnp.moveaxis(a, 0, axis)
    shp = a.shape
    return a.reshape(shp[:axis] + (shp[axis] * shp[axis + 1],) + shp[axis + 2:])


def _t(a):
    return jnp.swapaxes(a, -1, -2)


def _stored(name, a):
    return _t(a) if name in STORED_TRANSPOSED else a


def kernel(x, mem, w_in, b_in, pool_w, pool_scale, lru_conv_w, lru_conv_b, lru_w_r, lru_b_r, lru_w_i, lru_b_i, lru_lambda, lru_w_out, sconv_w, sconv_w_out, w_mix_out, xa_w_q, xa_w_k, xa_w_v, xa_w_o, ffn_w_gate, ffn_w_up, ffn_w_down, ln_g, ln_b, loss_target, m_w_in, m_b_in, m_pool_w, m_pool_scale, m_lru_conv_w, m_lru_conv_b, m_lru_w_r, m_lru_b_r, m_lru_w_i, m_lru_b_i, m_lru_lambda, m_lru_w_out, m_sconv_w, m_sconv_w_out, m_w_mix_out, m_xa_w_q, m_xa_w_k, m_xa_w_v, m_xa_w_o, m_ffn_w_gate, m_ffn_w_up, m_ffn_w_down, m_ln_g, m_ln_b, v_w_in, v_b_in, v_pool_w, v_pool_scale, v_lru_conv_w, v_lru_conv_b, v_lru_w_r, v_lru_b_r, v_lru_w_i, v_lru_b_i, v_lru_lambda, v_lru_w_out, v_sconv_w, v_sconv_w_out, v_w_mix_out, v_xa_w_q, v_xa_w_k, v_xa_w_v, v_xa_w_o, v_ffn_w_gate, v_ffn_w_up, v_ffn_w_down, v_ln_g, v_ln_b):
    args = dict(locals())
    w = {n: args[n] for n in WEIGHTS}
    mom_m = {n: args["m_" + n] for n in WEIGHTS}
    mom_v = {n: args["v_" + n] for n in WEIGHTS}
    depth = w_in.shape[0]
    s, d = x.shape[1], x.shape[2]
    nm = mem.shape[1]
    ff = ffn_w_gate.shape[2] * N_DEV
    xs = x.reshape(s, d)
    mems = mem.reshape(nm, d)
    target = loss_target.reshape(s, d)

    def shard(t, n, l):
        return _stored(n, t[n][l])

    def pack_shards(t, names, l, dtype=None):
        arrs = [shard(t, n, l) for n in names]
        return _pack([a if dtype is None else a.astype(dtype) for a in arrs], d)

    def unpack_gathered(buf, names):
        pieces = _unpack(buf, [shard(w, n, 0).shape for n in names], lead=1)
        return {n: (p if n == "w_in" else _join8(p, SHARD_AXIS[n])) for n, p in zip(names, pieces)}

    def layer_vec(l, fw):
        vec = jnp.zeros((V_ROWS, d), F32)
        vec = vec.at[V_PSCALE].set(pool_scale[l]).at[V_CW:V_CW + LRU_CONV].set(fw["lru_conv_w"])
        vec = vec.at[V_CB].set(lru_conv_b[l]).at[V_BR].set(lru_b_r[l]).at[V_BI].set(lru_b_i[l])
        vec = vec.at[V_LAM].set(lru_lambda[l]).at[V_SW:V_SW + SCONV_K].set(fw["sconv_w"])
        return vec.at[V_G:V_G + 3].set(fw["ln_g"]).at[V_B:V_B + 3].set(fw["ln_b"])

    def layer_params(l, fw):
        return dict(
            vec=layer_vec(l, fw), wint=_t(fw["w_in"]).reshape(1, 8 * d, d),
            pw=fw["pool_w"], pwt=_t(fw["pool_w"]),
            wr=lru_w_r[l].astype(BF16), wi=lru_w_i[l].astype(BF16),
            wrt=_t(lru_w_r[l]).astype(BF16), wit=_t(lru_w_i[l]).astype(BF16),
            wlo=fw["lru_w_out"], wlot=_t(fw["lru_w_out"]),
            wsc=fw["sconv_w_out"], wsct=_t(fw["sconv_w_out"]),
            wmix=fw["w_mix_out"], wmixt=_t(fw["w_mix_out"]),
            wq=fw["xa_w_q"], wqt=_t(fw["xa_w_q"]), wo=fw["xa_w_o"], wot=_t(fw["xa_w_o"]),
            wkv=jnp.stack([fw["xa_w_k"], fw["xa_w_v"]]),
            wgu=jnp.stack([_t(fw["ffn_w_gate"]), _t(fw["ffn_w_up"])]),
            wgt=fw["ffn_w_gate"], wut=fw["ffn_w_up"],
            wd=fw["ffn_w_down"], wdt=_t(fw["ffn_w_down"]))

    later = GROUP_ATTN + GROUP_FFN
    fw0 = unpack_gathered(_all_gather(pack_shards(w, GROUP_IN + GROUP_MIXER, 0, BF16), "gather_mixer_0"),
                          GROUP_IN + GROUP_MIXER)
    vectors = _all_gather(_pack([shard(w, n, l) for l in range(depth) for n in GROUP_VECTORS], d), "gather_vectors")
    vec_pieces = _unpack(vectors, [shard(w, n, l).shape for l in range(depth) for n in GROUP_VECTORS], lead=1)
    fvec = [{n: _join8(vec_pieces[l * len(GROUP_VECTORS) + k], SHARD_AXIS[n]) for k, n in enumerate(GROUP_VECTORS)}
            for l in range(depth)]

    layers, saved = [], []
    cur = xs
    fw_next = None
    for l in range(depth):
        fw = dict(fw0 if l == 0 else fw_next)
        fw.update(fvec[l])
        comm = []
        if l == 0:
            comm.append(("gather", pack_shards(w, later, 0, BF16)))
        if l + 1 < depth:
            comm.append(("gather", pack_shards(w, GROUP_IN + GROUP_MIXER + later, l + 1, BF16)))
        z, x1, rpre1, h, ypre, yl, yc, merged, e, pp, vb, rb, ib, cq, *got = _mixer_fwd(
            cur, fw["w_in"], b_in[l].reshape(1, 8 * d), fw["pool_w"], lru_w_r[l].astype(BF16),
            lru_w_i[l].astype(BF16), fw["lru_w_out"], fw["sconv_w_out"], fw["w_mix_out"],
            layer_vec(l, fw), f"mixer_fwd_{l}", comm=comm)
        if l == 0:
            fw.update(unpack_gathered(got.pop(0), later))
        if l + 1 < depth:
            fw_next = unpack_gathered(got.pop(0), GROUP_IN + GROUP_MIXER + later)
        p = layer_params(l, fw)
        kv = _mm(mems, p["wkv"], f"kv_{l}")[0]
        kk = kv[:, :d].astype(BF16)
        vv = kv[:, d:].astype(BF16)
        x2, rpre2, q, o = _attn_fwd(x1, p["wq"], p["wo"], _t(kk), vv, p["vec"], f"attn_fwd_{l}")
        hgu = _mm(x2, p["wgu"], f"ffn_in_{l}", tm=TS_MM // 2)[0]
        x3, rpre3, act = _ffn_out(x2, hgu, p["wd"], p["vec"], f"ffn_out_{l}")
        layers.append(p)
        saved.append(dict(x0=cur, z=z, x1=x1, rpre1=rpre1, h=h, ypre=ypre, yl=yl, yc=yc, merged=merged, e=e,
                          pp=pp, vb=vb, rb=rb, ib=ib, cq=cq,
                          kk=kk, vv=vv, x2=x2, rpre2=rpre2, q=q, o=o, hgu=hgu, rpre3=rpre3, act=act))
        cur = x3

    loss_part, dcur = _loss_head(cur, target, "loss_head")
    loss = lax.psum(loss_part[0, 0], ("x", "y", "c"))

    res = {}

    def slots_of(g, names):
        whole = [n for n in names if n not in GROUP_VECTORS]
        parts = [(g[n] if n == "w_in" else _split8(g[n], SHARD_AXIS[n])).reshape(N_DEV, -1, d) for n in whole]
        if len(whole) < len(names):
            assert tuple(names[len(whole):]) == GROUP_VECTORS
            parts.append(_pack([_split8(g[n], SHARD_AXIS[n]) for n in GROUP_VECTORS], d, lead=1))
        return parts

    def update(received, names, l, tag):
        outs = _adamw_sum(received, *[pack_shards(t, names, l) for t in (w, mom_m, mom_v)], f"adamw_{tag}_{l}")
        shapes = [shard(w, n, l).shape for n in names]
        for n, *parts in zip(names, *[_unpack(o, shapes) for o in outs]):
            res[(n, l)] = [_stored(n, a) for a in parts]

    def settle(exchanges, got):
        for (names, l, tag, _), received in zip(exchanges, got):
            update(received, names, l, tag)

    grads = [None] * depth
    for l in reversed(range(depth)):
        p, sv = layers[l], saved[l]
        g = {}
        dx2, dr3, dhgu, ln3 = _ffn_bwd(dcur, sv["rpre3"], sv["hgu"], p["wdt"], p["wgt"], p["wut"], p["vec"],
                                       f"ffn_bwd_{l}")
        g["ffn_w_down"] = _mm_tn(sv["act"], dr3, d, f"g_wd_{l}")[0][0]
        dwgu = _mm_tn(dhgu, sv["x2"], d, f"g_wgu_{l}", tk=ff)[0][0]
        g["ffn_w_gate"], g["ffn_w_up"] = dwgu[:ff], dwgu[ff:]
        dx1, dq, dr2, dk, dv, ln2 = _attn_bwd(dx2, sv["rpre2"], sv["q"], p["wqt"], p["wot"], sv["kk"], _t(sv["kk"]),
                                              _t(sv["vv"]), p["vec"], f"attn_bwd_{l}")
        g["xa_w_o"] = _mm_tn(sv["o"], dr2, d, f"g_wo_{l}")[0][0]
        g["xa_w_q"] = _mm_tn(sv["x1"], dq, d, f"g_wq_{l}")[0][0]
        dwkv = _mm_tn(mems, jnp.concatenate([dk, dv], axis=1), d, f"g_wkv_{l}")[0]
        g["xa_w_k"], g["xa_w_v"] = dwkv[0], dwkv[1]
        ffn_slots = slots_of(g, GROUP_FFN)
        (dz, dr1, dyl, dyc, accs, dbin, g["pool_w"], g["lru_w_r"], g["lru_w_i"], received) = _mixer_bwd(
            dx1, sv["rpre1"], sv["z"], sv["h"], sv["ypre"], sv["yl"], sv["yc"], sv["pp"], sv["vb"], sv["rb"],
            sv["ib"], sv["cq"], p["pwt"], p["wrt"], p["wit"], p["wlot"], p["wsct"], p["wmixt"], p["vec"],
            f"mixer_bwd_{l}",
            comm=[("scatter", ffn_slots)])
        update(received, GROUP_FFN, l, "ffn")
        g["w_mix_out"] = _mm_tn(sv["merged"], dr1, d, f"g_wmix_{l}")[0][0]
        g["lru_w_out"] = _mm_tn(sv["h"], dyl, d, f"g_wlo_{l}")[0][0]
        g["sconv_w_out"] = _mm_tn(sv["e"], dyc, d, f"g_wsc_{l}")[0][0]
        g["b_in"] = dbin[0]
        g["pool_scale"] = accs[A_PSCALE]
        g["lru_conv_w"] = accs[A_CW:A_CW + LRU_CONV]
        g["lru_conv_b"] = accs[A_CB]
        g["lru_b_r"] = accs[A_BR]
        g["lru_b_i"] = accs[A_BI]
        g["lru_lambda"] = accs[A_SP] * (-_sigmoid(-lru_lambda[l]))
        g["sconv_w"] = accs[A_SW:A_SW + SCONV_K]
        g["ln_g"] = jnp.stack([accs[A_G], ln2[0], ln3[0]])
        g["ln_b"] = jnp.stack([accs[A_B], ln2[1], ln3[1]])
        grads[l] = g
        behind_win = [(GROUP_ATTN, l, "attn", slots_of(g, GROUP_ATTN)),
                      (GROUP_MIXER + GROUP_VECTORS, l, "mixer", slots_of(g, GROUP_MIXER + GROUP_VECTORS))]
        comm = [("scatter", t[3]) for t in behind_win]
        if l == 0:
            comm.append(("gather", _pack([jnp.stack([grads[k][n] for k in range(depth)]) for n in REPLICATED], d)))
        g["w_in"], *got = _mm_tn(sv["x0"], dz, d, f"g_win_{l}", comm=comm)
        settle(behind_win, got)
        if l == 0:
            outs = _adamw_sum(got[2], *[_pack([t[n] for n in REPLICATED], d) for t in (w, mom_m, mom_v)],
                              "adamw_replicated")
            rep_shapes = [w[n].shape for n in REPLICATED]
            final = {n: parts for n, *parts in zip(REPLICATED, *[_unpack(o, rep_shapes) for o in outs])}
        behind_dx = [(GROUP_IN, l, "w_in", [a.astype(BF16) for a in slots_of(g, GROUP_IN)])]
        dcur, *got = _mm(dz, p["wint"], f"dx_{l}", add=dr1, add_scale=ALPHA, comm=[("scatter", behind_dx[0][3])])
        settle(behind_dx, got)
    grad_x = dcur.reshape(x.shape)

    for n in WEIGHTS:
        if n not in final:
            final[n] = [jnp.stack([res[(n, l)][k] for l in range(depth)]) for k in range(4)]
    return (loss, grad_x, *[final[n][0] for n in WEIGHTS], *[final[n][1] for n in WEIGHTS],
            *[final[n][2] for n in WEIGHTS], *[final[n][3] for n in WEIGHTS])
```

```python
import functools
import math

import jax
import jax.numpy as jnp
from jax import lax
from jax.experimental import pallas as pl
from jax.experimental.pallas import tpu as pltpu

F32 = jnp.float32
BF16 = jnp.bfloat16
MESH = pl.DeviceIdType.MESH

N_DEV = 8
LRU_HEADS = 8
LRU_CONV = 4
LRU_C = 8.0
SCONV_K = 3
POOL_WINDOWS = (2, 4, 8, 16)
X_HEADS = 4
DEPTH = 2
ALPHA = (2 * DEPTH) ** 0.25
LN_EPS = 1e-5
ADAM_LR = 0.001
ADAM_B1 = 0.9
ADAM_B2 = 0.999
ADAM_EPS = 1e-08
ADAM_WD = 0.01
ADAM_STEP = 10

HALO = 16
SUBLANES = 8
VMEM_LIMIT = 56 * 1024 * 1024
TS_MIXER = 128
TS_ATTN = 512
TS_FFN = 256
TS_MM = 1024
TK_MM = 2048
TR_ADAM = 256
ROW_PAD = 8
TS_MM_TN = 1024

V_PSCALE, V_CW, V_CB, V_BR, V_BI, V_LAM, V_SW, V_G, V_B = 0, 1, 5, 6, 7, 8, 9, 12, 15
V_ROWS = 24
A_PSCALE, A_CW, A_CB, A_BR, A_BI, A_SP, A_SW, A_G, A_B = 0, 1, 5, 6, 7, 8, 9, 12, 13
A_ROWS = 16


def _cparams(sem):
    return pltpu.CompilerParams(dimension_semantics=sem, vmem_limit_bytes=VMEM_LIMIT)


def _tile(n, pref):
    if n <= pref:
        return n
    assert n % pref == 0, (n, pref)
    return pref


def _const_spec(shape):
    nd = len(shape)
    return pl.BlockSpec(shape, lambda *_: (0,) * nd, pipeline_mode=pl.Buffered(1))


def _acc_spec(shape):
    nd = len(shape)
    return pl.BlockSpec(shape, lambda *_: (0,) * nd)


def _dot(a, b):
    return jnp.dot(a.astype(BF16), b.astype(BF16), preferred_element_type=F32)


def _dot_tn(a, b):
    return lax.dot_general(a.astype(BF16), b.astype(BF16), (((0,), (0,)), ((), ())),
                           preferred_element_type=F32)


def _sigmoid(x):
    return 0.5 * jnp.tanh(0.5 * x) + 0.5


def _softplus(y):
    e = jnp.exp(-jnp.abs(y))
    log1p = jnp.where(e < 1e-4, e * (1.0 - e * (0.5 - e * (1.0 / 3.0))), jnp.log(1.0 + e))
    return jnp.maximum(y, 0.0) + log1p


def _ln_fwd(r, g, b):
    mu = jnp.mean(r, axis=-1, keepdims=True)
    xc = r - mu
    var = jnp.mean(xc * xc, axis=-1, keepdims=True)
    return xc * lax.rsqrt(var + LN_EPS) * g + b


def _ln_bwd(dy, r, g):
    mu = jnp.mean(r, axis=-1, keepdims=True)
    xc = r - mu
    var = jnp.mean(xc * xc, axis=-1, keepdims=True)
    rstd = lax.rsqrt(var + LN_EPS)
    yhat = xc * rstd
    dyh = dy * g
    m1 = jnp.mean(dyh, axis=-1, keepdims=True)
    m2 = jnp.mean(dyh * yhat, axis=-1, keepdims=True)
    return rstd * (dyh - m1 - yhat * m2), dy * yhat


def _colsum(a):
    return jnp.sum(a, axis=0, keepdims=True)


def _position():
    return lax.axis_index("x"), lax.axis_index("y"), lax.axis_index("c")


def _gather_copies(x_ref, out_ref, send_sems, recv_sems, local_sem):
    x, y, c = _position()
    me, sibling = (x, y, c), (x, y, 1 - c)
    chips = [(1 - x, y), (x, 1 - y), (1 - x, 1 - y)]

    def slot(px, py, pc):
        return out_ref.at[4 * px + 2 * py + pc]

    def copy(k, block, to, src=None):
        return pltpu.make_async_remote_copy(
            src_ref=slot(*block) if src is None else src, dst_ref=slot(*block),
            send_sem=send_sems.at[k], recv_sem=recv_sems.at[k], device_id=to, device_id_type=MESH)

    mine = pltpu.make_async_copy(x_ref, slot(*me), local_sem)
    first = [copy(0, me, sibling, src=x_ref)]
    first += [copy(1 + j, me, (*chip, c), src=x_ref) for j, chip in enumerate(chips)]
    passed = [copy(4 + j, (*chip, c), sibling) for j, chip in enumerate(chips)]
    over_ici = [copy(1 + j, (*chip, c), me) for j, chip in enumerate(chips)]
    from_sibling = copy(0, sibling, me)
    forwarded = [copy(4 + j, (*chip, 1 - c), me) for j, chip in enumerate(chips)]
    return mine, first, passed, over_ici, from_sibling, forwarded


def _scatter_copies(g_refs, out_ref, send_sems, recv_sems, local_sem):
    x, y, c = _position()
    me = 4 * x + 2 * y + c
    offsets, rows = [], 0
    for g_ref in g_refs:
        offsets.append(rows)
        rows += g_ref.shape[1]

    def landing(g_ref, off):
        return out_ref.at[me, pl.ds(off, g_ref.shape[1])]

    mine = [pltpu.make_async_copy(g_ref.at[me], landing(g_ref, off), local_sem)
            for g_ref, off in zip(g_refs, offsets)]
    mine_all = pltpu.make_async_copy(out_ref.at[me], out_ref.at[me], local_sem)
    copies, waits = [], []
    for k in range(1, N_DEV):
        px = 1 - x if k & 4 else x
        py = 1 - y if k & 2 else y
        pc = 1 - c if k & 1 else c
        sems = dict(send_sem=send_sems.at[k - 1], recv_sem=recv_sems.at[k - 1],
                    device_id=(px, py, pc), device_id_type=MESH)
        copies += [pltpu.make_async_remote_copy(src_ref=g_ref.at[4 * px + 2 * py + pc], dst_ref=landing(g_ref, off),
                                                **sems) for g_ref, off in zip(g_refs, offsets)]
        waits.append(pltpu.make_async_remote_copy(src_ref=out_ref.at[me], dst_ref=out_ref.at[me], **sems))
    return mine, copies, mine_all, waits


def _comm_start(kind, srcs, *refs):
    if kind == "gather":
        mine, first, _, _, _, _ = _gather_copies(srcs[0], *refs)
        mine.start()
        for cp in first:
            cp.start()
    else:
        mine, copies, _, _ = _scatter_copies(srcs, *refs)
        for cp in mine + copies:
            cp.start()


def _comm_finish(kind, srcs, *refs):
    if kind == "gather":
        mine, first, passed, over_ici, from_sibling, forwarded = _gather_copies(srcs[0], *refs)
        for arrival, forward in zip(over_ici, passed):
            arrival.wait_recv()
            forward.start()
        from_sibling.wait_recv()
        for arrival in forwarded:
            arrival.wait_recv()
        for cp in first + passed:
            cp.wait_send()
        mine.wait()
    else:
        _, _, mine_all, waits = _scatter_copies(srcs, *refs)
        for cp in waits:
            cp.wait_recv()
        for cp in waits:
            cp.wait_send()
        mine_all.wait()


def _comm_sources(kind, payload):
    return [payload] if kind == "gather" else list(payload)


def _comm_out_shape(kind, payload):
    srcs = _comm_sources(kind, payload)
    if kind == "gather":
        return jax.ShapeDtypeStruct((N_DEV,) + srcs[0].shape, srcs[0].dtype)
    return jax.ShapeDtypeStruct((N_DEV, sum(a.shape[1] for a in srcs), srcs[0].shape[2]), srcs[0].dtype)


COMM_SEMAPHORES = [pltpu.SemaphoreType.DMA((7,)), pltpu.SemaphoreType.DMA((7,)), pltpu.SemaphoreType.DMA]


def _pallas(body, *, name, grid, in_specs, out_specs, out_shape, semantics, args, scratch_shapes=(), comm=()):
    in_specs, out_specs, out_shape = list(in_specs), list(out_specs), list(out_shape)
    scratch_shapes = list(scratch_shapes)
    n_in, n_out, n_scr, nc = len(in_specs), len(out_specs), len(scratch_shapes), len(comm)
    if not comm:
        return pl.pallas_call(body, name=name, grid=grid, in_specs=in_specs, out_specs=out_specs, out_shape=out_shape,
                              scratch_shapes=scratch_shapes, compiler_params=_cparams(semantics))(*args)
    kinds = [kind for kind, _ in comm]
    sources = [_comm_sources(kind, payload) for kind, payload in comm]
    n_src = sum(len(srcs) for srcs in sources)

    def carrying(*refs):
        ins, rest = refs[:n_in], refs[n_in:]
        cin, rest = list(rest[:n_src]), rest[n_src:]
        outs, rest = rest[:n_out], rest[n_out:]
        cout, rest = rest[:nc], rest[nc:]
        scr, sems = rest[:n_scr], rest[n_scr:]
        ids = [pl.program_id(ax) for ax in range(len(grid))]
        first = functools.reduce(jnp.logical_and, [i == 0 for i in ids])
        last = functools.reduce(jnp.logical_and, [i == g - 1 for i, g in zip(ids, grid)])
        plans = []
        for k in range(nc):
            mine, cin = cin[:len(sources[k])], cin[len(sources[k]):]
            plans.append((kinds[k], mine, cout[k], *sems[3 * k:3 * k + 3]))

        @pl.when(first)
        def _():
            for plan in plans:
                _comm_start(*plan)

        body(*ins, *outs, *scr)

        @pl.when(last)
        def _():
            for plan in plans:
                _comm_finish(*plan)

    hbm = pl.BlockSpec(memory_space=pl.ANY)
    return pl.pallas_call(
        carrying, name=name, grid=grid,
        in_specs=in_specs + [hbm] * n_src, out_specs=out_specs + [hbm] * nc,
        out_shape=out_shape + [_comm_out_shape(kind, payload) for kind, payload in comm],
        scratch_shapes=scratch_shapes + COMM_SEMAPHORES * nc,
        compiler_params=_cparams(("arbitrary",) * len(grid)),
    )(*args, *[a for srcs in sources for a in srcs])


def _all_gather(xs, name):
    def body(x_ref, out_ref, send_sems, recv_sems, local_sem):
        _comm_start("gather", [x_ref], out_ref, send_sems, recv_sems, local_sem)
        _comm_finish("gather", [x_ref], out_ref, send_sems, recv_sems, local_sem)

    return pl.pallas_call(
        body, name=name, out_shape=_comm_out_shape("gather", xs),
        in_specs=[pl.BlockSpec(memory_space=pl.ANY)], out_specs=pl.BlockSpec(memory_space=pl.ANY),
        scratch_shapes=COMM_SEMAPHORES,
    )(xs)


def _adamw_sum(parts, w, m, v, name):
    _, rows, width = parts.shape
    tr = max(t for t in range(SUBLANES, min(rows, TR_ADAM) + 1, SUBLANES) if rows % t == 0)
    c1 = 1.0 - ADAM_B1 ** ADAM_STEP
    c2 = 1.0 - ADAM_B2 ** ADAM_STEP

    def body(p_ref, w_ref, m_ref, v_ref, g_ref, d_ref, nm_ref, nv_ref):
        g = p_ref[0].astype(F32)
        for k in range(1, N_DEV):
            g = g + p_ref[k].astype(F32)
        nm = ADAM_B1 * m_ref[...] + (1.0 - ADAM_B1) * g
        nv = ADAM_B2 * v_ref[...] + (1.0 - ADAM_B2) * (g * g)
        m_hat = nm / c1
        v_hat = nv / c2
        g_ref[...] = g
        d_ref[...] = -ADAM_LR * (m_hat / (jnp.sqrt(v_hat) + ADAM_EPS) + ADAM_WD * w_ref[...])
        nm_ref[...] = nm
        nv_ref[...] = nv

    spec = pl.BlockSpec((tr, width), lambda i: (i, 0))
    out = jax.ShapeDtypeStruct((rows, width), F32)
    return pl.pallas_call(
        body, name=name, grid=(rows // tr,),
        in_specs=[pl.BlockSpec((N_DEV, tr, width), lambda i: (0, i, 0)), spec, spec, spec],
        out_specs=[spec, spec, spec, spec], out_shape=[out, out, out, out],
        compiler_params=_cparams(("parallel",)),
    )(parts, w, m, v)


def _mm(a, wb, name, bias=None, add=None, add_scale=1.0, out_dtype=F32, tm=None, comm=()):
    m, k = a.shape
    nb, k2, tn = wb.shape
    assert k == k2
    tm = _tile(m, TS_MM if tm is None else tm)
    tk = _tile(k, TK_MM)
    nk = k // tk

    def body(*refs):
        a_ref, w_ref = refs[0], refs[1]
        pos = 2
        b_ref = add_ref = None
        if bias is not None:
            b_ref = refs[pos]
            pos += 1
        if add is not None:
            add_ref = refs[pos]
            pos += 1
        o_ref = refs[pos]

        def finish(r):
            if b_ref is not None:
                r = r + b_ref[...]
            if add_ref is not None:
                r = r + add_scale * add_ref[...]
            o_ref[...] = r.astype(o_ref.dtype)

        if nk == 1:
            finish(_dot(a_ref[...], w_ref[...]))
            return
        acc_ref = refs[pos + 1]
        kk = pl.program_id(2)

        @pl.when(kk == 0)
        def _():
            acc_ref[...] = jnp.zeros_like(acc_ref)

        acc_ref[...] += _dot(a_ref[...], w_ref[...])

        @pl.when(kk == nk - 1)
        def _():
            finish(acc_ref[...])

    in_specs = [pl.BlockSpec((tm, tk), lambda j, i, kk: (i, kk)),
                pl.BlockSpec((None, tk, tn), lambda j, i, kk: (j, kk, 0))]
    args = [a, wb]
    if bias is not None:
        in_specs.append(pl.BlockSpec((1, tn), lambda j, i, kk: (0, j)))
        args.append(bias)
    if add is not None:
        in_specs.append(pl.BlockSpec((tm, tn), lambda j, i, kk: (i, j)))
        args.append(add)
    return _pallas(
        body, name=name, grid=(nb, m // tm, nk),
        in_specs=in_specs,
        out_specs=[pl.BlockSpec((tm, tn), lambda j, i, kk: (i, j))],
        out_shape=[jax.ShapeDtypeStruct((m, nb * tn), out_dtype)],
        scratch_shapes=[pltpu.VMEM((tm, tn), F32)] if nk > 1 else [],
        semantics=("parallel", "parallel", "arbitrary"), args=args, comm=comm)


def _mm_tn(a, b, tn, name, tk=None, comm=()):
    s, k = a.shape
    s2, n = b.shape
    assert s == s2 and n % tn == 0
    nb = n // tn
    ts = _tile(s, TS_MM_TN)
    tk = k if tk is None else tk
    assert k % tk == 0

    def body(a_ref, b_ref, o_ref):
        @pl.when(pl.program_id(2) == 0)
        def _():
            o_ref[...] = jnp.zeros_like(o_ref)

        o_ref[...] += _dot_tn(a_ref[...], b_ref[...])

    return _pallas(
        body, name=name, grid=(nb, k // tk, s // ts),
        in_specs=[pl.BlockSpec((ts, tk), lambda j, kb, i: (i, kb)),
                  pl.BlockSpec((ts, tn), lambda j, kb, i: (i, j))],
        out_specs=[pl.BlockSpec((None, tk, tn), lambda j, kb, i: (j, kb, 0))],
        out_shape=[jax.ShapeDtypeStruct((nb, k, tn), F32)],
        semantics=("parallel", "parallel", "arbitrary"), args=(a, b), comm=comm)


def _scan_fwd(a_ref, b_ref, h_ref, carry_ref, ts):
    rowid = lax.broadcasted_iota(jnp.int32, (SUBLANES, 1), 0)

    def group(gi, hprev):
        r0 = pl.multiple_of(gi * SUBLANES, SUBLANES)
        a = a_ref[pl.ds(r0, SUBLANES), :]
        b = b_ref[pl.ds(r0, SUBLANES), :]
        for d in (1, 2, 4):
            a_sh = jnp.where(rowid >= d, pltpu.roll(a, d, 0), 1.0)
            b_sh = jnp.where(rowid >= d, pltpu.roll(b, d, 0), 0.0)
            b = a * b_sh + b
            a = a * a_sh
        hh = a * hprev + b
        h_ref[pl.ds(r0, SUBLANES), :] = hh
        return hh[SUBLANES - 1:SUBLANES, :]

    last = lax.fori_loop(0, ts // SUBLANES, group, carry_ref[0:1, :])
    carry_ref[0:1, :] = last


def _scan_rev(c_ref, b_ref, g_ref, carry_ref, ts):
    rowid = lax.broadcasted_iota(jnp.int32, (SUBLANES, 1), 0)
    ng = ts // SUBLANES

    def group(gi, gnext):
        r0 = pl.multiple_of((ng - 1 - gi) * SUBLANES, SUBLANES)
        c = c_ref[pl.ds(r0, SUBLANES), :]
        b = b_ref[pl.ds(r0, SUBLANES), :]
        for d in (1, 2, 4):
            keep = rowid < SUBLANES - d
            c_sh = jnp.where(keep, pltpu.roll(c, SUBLANES - d, 0), 1.0)
            b_sh = jnp.where(keep, pltpu.roll(b, SUBLANES - d, 0), 0.0)
            b = c * b_sh + b
            c = c * c_sh
        gg = c * gnext + b
        g_ref[pl.ds(r0, SUBLANES), :] = gg
        return gg[0:1, :]

    first = lax.fori_loop(0, ng, group, carry_ref[0:1, :])
    carry_ref[0:1, :] = first


def _past(ext, sh, ts):
    if sh == 0:
        return ext[HALO:HALO + ts]
    return pltpu.roll(ext, sh, 0)[HALO:HALO + ts]


def _future(ext, sh, ts):
    if sh == 0:
        return ext[0:ts]
    return pltpu.roll(ext, ts + HALO - sh, 0)[0:ts]


def _one_minus_sq(a, log_a):
    x = 2.0 * log_a
    series = -x * (1.0 + x * (0.5 + x * (1.0 / 6.0 + x * (1.0 / 24.0))))
    return jnp.where(x > -0.02, series, 1.0 - a * a)


def _halo_index(ts):
    blocks = ts // HALO
    return lambda t: (jnp.maximum(t * blocks - 1, 0), 0)


def _head_columns(d):
    cw = d // LRU_HEADS
    return [slice(c * cw, (c + 1) * cw) for c in range(LRU_HEADS)]


def _shift(cs, off):
    return slice(cs.start + off, cs.stop + off)


def _mixer_fwd(x, win, b_in, pw, wr, wi, wlo, wsc, wmix, vec, name, comm=()):
    s, d = x.shape
    ts = _tile(s, TS_MIXER)
    nt = s // ts
    dg = d // len(POOL_WINDOWS)
    nblk = win.shape[0]
    cols = _head_columns(d)

    def body(x_ref, xn_ref, win_ref, bin_ref, pw_ref, wr_ref, wi_ref, wlo_ref, wsc_ref, wmix_ref, vec_ref,
             z_hbm, x1_ref, rpre_ref, h_ref, ypre_ref, yl_ref, yc_ref, mg_ref, e_ref, p_ref, v_ref, r_ref, ig_ref,
             cq_ref, z_even, z_odd, zhist, a_scr, b_scr, hcarry, z_sem):
        i = pl.program_id(0)
        first = i == 0

        def project_block(xb, dst, k):
            dst[:, k * d:(k + 1) * d] = _dot(xb, win_ref[k]) + bin_ref[:, k * d:(k + 1) * d]

        @pl.when(first)
        def _():
            hcarry[...] = jnp.zeros_like(hcarry)
            zhist[...] = jnp.zeros_like(zhist)
            xb = x_ref[...].astype(BF16)
            for k in range(nblk):
                project_block(xb, z_even, k)

        def vrow(k, cs):
            return vec_ref[k:k + 1, cs]

        def step(zc, zn):
            z_out = pltpu.make_async_copy(zc, z_hbm.at[pl.ds(pl.multiple_of(i * ts, ts), ts), :], z_sem)
            z_out.start()
            xb = xn_ref[...].astype(BF16)
            tglob = i * ts + lax.broadcasted_iota(jnp.int32, (ts, 1), 0)
            sp = _softplus(-vec_ref[V_LAM:V_LAM + 1, :])

            def with_history(k, cs):
                kc = _shift(cs, k * d)
                return jnp.concatenate([jnp.where(first, 0.0, zhist[:, kc]), zc[:, kc]], axis=0)

            for hh, cs in enumerate(cols):
                if hh < nblk:
                    project_block(xb, zn, hh)
                win_len = POOL_WINDOWS[cs.start // dg]
                ext = with_history(0, cs)
                sm = ext
                sh = 1
                while sh < win_len:
                    sm = sm + pltpu.roll(sm, sh, 0)
                    sh *= 2
                inv_cnt = 1.0 / jnp.minimum(tglob + 1, win_len).astype(F32)
                p_ref[:, cs] = (sm[HALO:HALO + ts] * inv_cnt - ext[HALO:HALO + ts]).astype(BF16)
                ext = with_history(1, cs)
                v = vrow(V_CB, cs)
                for j in range(LRU_CONV):
                    v = v + vrow(V_CW + j, cs) * _past(ext, LRU_CONV - 1 - j, ts)
                r = _sigmoid(_dot(v, wr_ref[hh]) + vrow(V_BR, cs))
                ig = _sigmoid(_dot(v, wi_ref[hh]) + vrow(V_BI, cs))
                log_a = -LRU_C * r * sp[:, cs]
                a = jnp.exp(log_a)
                a_scr[:, cs] = a
                b_scr[:, cs] = jnp.sqrt(_one_minus_sq(a, log_a)) * (ig * v)
                v_ref[:, cs] = v
                r_ref[:, cs] = r
                ig_ref[:, cs] = ig
                ext = with_history(3, cs) * with_history(4, cs)
                cq = jnp.zeros((ts, cs.stop - cs.start), F32)
                for j in range(SCONV_K):
                    cq = cq + vrow(V_SW + j, cs) * _past(ext, SCONV_K - 1 - j, ts)
                cq_ref[:, cs] = cq
                e_ref[:, cs] = (zc[:, _shift(cs, 2 * d)] * cq).astype(BF16)
            for k in range(len(cols), nblk):
                project_block(xb, zn, k)
            _scan_fwd(a_scr, b_scr, h_ref, hcarry, ts)
            ypre = jnp.concatenate([_dot(p_ref[:, g * dg:(g + 1) * dg], pw_ref[g])
                                    for g in range(len(POOL_WINDOWS))], axis=1)
            yl = _dot(h_ref[...], wlo_ref[...])
            yc = _dot(e_ref[...], wsc_ref[...])
            ypre_ref[...] = ypre
            yl_ref[...] = yl
            yc_ref[...] = yc
            for cs in cols:
                merged = (_sigmoid(zc[:, _shift(cs, 5 * d)]) * (ypre[:, cs] * vrow(V_PSCALE, cs))
                          + _sigmoid(zc[:, _shift(cs, 6 * d)]) * yl[:, cs]
                          + _sigmoid(zc[:, _shift(cs, 7 * d)]) * yc[:, cs])
                mg_ref[:, cs] = merged.astype(BF16)
            rpre = ALPHA * x_ref[...] + _dot(mg_ref[...], wmix_ref[...])
            x1_ref[...] = _ln_fwd(rpre, vec_ref[V_G:V_G + 1, :], vec_ref[V_B:V_B + 1, :])
            rpre_ref[...] = rpre
            zhist[...] = zc[ts - HALO:ts, :]
            z_out.wait()

        parity = lax.rem(i, 2)

        @pl.when(parity == 0)
        def _():
            step(z_even, z_odd)

        @pl.when(parity == 1)
        def _():
            step(z_odd, z_even)

    tile = pl.BlockSpec((ts, d), lambda t: (t, 0))
    f32o = jax.ShapeDtypeStruct((s, d), F32)
    bfo = jax.ShapeDtypeStruct((s, d), BF16)
    consts = (win, b_in, pw, wr, wi, wlo, wsc, wmix, vec)
    return _pallas(
        body, name=name, grid=(nt,),
        in_specs=[tile, pl.BlockSpec((ts, d), lambda t: (jnp.minimum(t + 1, nt - 1), 0))]
        + [_const_spec(c.shape) for c in consts],
        out_specs=[pl.BlockSpec(memory_space=pl.ANY)] + [tile] * 13,
        out_shape=[jax.ShapeDtypeStruct((s, nblk * d), F32), f32o, f32o, f32o, f32o, f32o, f32o, bfo, bfo, bfo,
                   f32o, f32o, f32o, f32o],
        scratch_shapes=[pltpu.VMEM((ts, nblk * d), F32)] * 2 + [pltpu.VMEM((HALO, nblk * d), F32)]
        + [pltpu.VMEM((ts, d), F32)] * 2 + [pltpu.VMEM((SUBLANES, d), F32), pltpu.SemaphoreType.DMA],
        semantics=("arbitrary",), args=(x, x, *consts), comm=comm)


def _mixer_bwd(dx1, rpre, z, h, ypre, yl, yc, pp, vv, rr, ii, cq, pwt, wrt, wit, wlot, wsct, wmixt, vec, name,
               comm=()):
    s, d = dx1.shape
    ts = _tile(s, TS_MIXER)
    nt = s // ts
    dg = d // len(POOL_WINDOWS)
    cols = _head_columns(d)

    def body(dx1_ref, rpre_ref, z_ref, h_ref, hh_ref, ypre_ref, yl_ref, yc_ref, p_ref, v_ref, r_ref, ig_ref, cq_ref,
             pwt_ref, wrt_ref, wit_ref, wlot_ref, wsct_ref, wmixt_ref, vec_ref,
             dz_ref, dr_ref, dyl_ref, dyc_ref, acc_ref, dbin_ref, dpw_ref, dwr_ref, dwi_ref,
             c_scr, b_scr, g_scr, dyps_scr, a_keep, m_keep, gcarry, acarry, dcq_c, dv_c, m_c):
        i = pl.program_id(0)
        t = nt - 1 - i

        @pl.when(i == 0)
        def _():
            for ref in (gcarry, acarry, dcq_c, dv_c, m_c, acc_ref, dbin_ref, dpw_ref, dwr_ref, dwi_ref):
                ref[...] = jnp.zeros_like(ref)

        def vrow(k, cs):
            return vec_ref[k:k + 1, cs]

        def zc(k, cs):
            return z_ref[:, _shift(cs, k * d)]

        def acc(row, cs, val):
            acc_ref[row:row + 1, cs] += _colsum(val)

        def emit_dz(k, cs, val):
            kc = _shift(cs, k * d)
            dz_ref[:, kc] = val.astype(BF16)
            dbin_ref[:, kc] += _colsum(val)

        def with_future(tile_val, carry_ref, cs):
            ext = jnp.concatenate([tile_val, carry_ref[:, cs]], axis=0)
            carry_ref[:, cs] = tile_val[0:HALO, :]
            return ext

        dx1v = dx1_ref[...]
        dr, dyy = _ln_bwd(dx1v, rpre_ref[...], vec_ref[V_G:V_G + 1, :])
        acc_ref[A_G:A_G + 1, :] += _colsum(dyy)
        acc_ref[A_B:A_B + 1, :] += _colsum(dx1v)
        dr_ref[...] = dr
        dmg = _dot(dr, wmixt_ref[...])
        for cs in cols:
            dm = dmg[:, cs]
            ypre = ypre_ref[:, cs]
            ys = (ypre * vrow(V_PSCALE, cs), yl_ref[:, cs], yc_ref[:, cs])
            dys = []
            for k in range(3):
                gk = _sigmoid(zc(5 + k, cs))
                emit_dz(5 + k, cs, dm * ys[k] * gk * (1.0 - gk))
                dys.append(dm * gk)
            acc(A_PSCALE, cs, dys[0] * ypre)
            dyps_scr[:, cs] = dys[0] * vrow(V_PSCALE, cs)
            dyl_ref[:, cs] = dys[1].astype(BF16)
            dyc_ref[:, cs] = dys[2].astype(BF16)
        de = _dot(dyc_ref[...], wsct_ref[...])
        dh = _dot(dyl_ref[...], wlot_ref[...])

        sp = _softplus(-vec_ref[V_LAM:V_LAM + 1, :])
        for cs in cols:
            dec = de[:, cs]
            emit_dz(2, cs, dec * cq_ref[:, cs])
            dcq_ext = with_future(dec * zc(2, cs), dcq_c, cs)
            zcc, zh = zc(3, cs), zc(4, cs)
            qv = zcc * zh
            dq = jnp.zeros_like(qv)
            for j in range(SCONV_K):
                adv = _future(dcq_ext, SCONV_K - 1 - j, ts)
                acc(A_SW + j, cs, adv * qv)
                dq = dq + vrow(V_SW + j, cs) * adv
            emit_dz(3, cs, dq * zh)
            emit_dz(4, cs, dq * zcc)
            log_a = -LRU_C * r_ref[:, cs] * sp[:, cs]
            a = jnp.exp(log_a)
            a_keep[:, cs] = a
            m_keep[:, cs] = jnp.sqrt(_one_minus_sq(a, log_a))
            c_scr[:, cs] = _future(with_future(a, acarry, cs), 1, ts)
            b_scr[:, cs] = dh[:, cs]
        _scan_rev(c_scr, b_scr, g_scr, gcarry, ts)

        for hh, cs in enumerate(cols):
            gs, a, mult = g_scr[:, cs], a_keep[:, cs], m_keep[:, cs]
            r, ig, v = r_ref[:, cs], ig_ref[:, cs], v_ref[:, cs]
            hprev = _past(jnp.concatenate([jnp.where(t == 0, 0.0, hh_ref[:, cs]), h_ref[:, cs]], axis=0), 1, ts)
            iv = ig * v
            dlog_a = gs * hprev * a + gs * iv * (-(a * a) / mult)
            div = gs * mult
            acc(A_SP, cs, dlog_a * (-LRU_C) * r)
            dpre_r = dlog_a * (-LRU_C) * sp[:, cs] * r * (1.0 - r)
            dpre_i = div * v * ig * (1.0 - ig)
            acc(A_BR, cs, dpre_r)
            acc(A_BI, cs, dpre_i)
            dv = div * ig + _dot(dpre_r, wrt_ref[hh]) + _dot(dpre_i, wit_ref[hh])
            dwr_ref[hh] += _dot_tn(v, dpre_r)
            dwi_ref[hh] += _dot_tn(v, dpre_i)
            acc(A_CB, cs, dv)
            dv_ext = with_future(dv, dv_c, cs)
            zl = zc(1, cs)
            dzl = jnp.zeros_like(zl)
            for j in range(LRU_CONV):
                adv = _future(dv_ext, LRU_CONV - 1 - j, ts)
                acc(A_CW + j, cs, adv * zl)
                dzl = dzl + vrow(V_CW + j, cs) * adv
            emit_dz(1, cs, dzl)

        tglob = t * ts + lax.broadcasted_iota(jnp.int32, (ts, 1), 0)
        for g, win_len in enumerate(POOL_WINDOWS):
            cs = slice(g * dg, (g + 1) * dg)
            dyps = dyps_scr[:, cs]
            dpw_ref[g] += _dot_tn(p_ref[:, cs], dyps)
            dp = _dot(dyps, pwt_ref[g])
            inv_cnt = 1.0 / jnp.minimum(tglob + 1, win_len).astype(F32)
            sm = with_future(dp * inv_cnt, m_c, cs)
            sh = 1
            while sh < win_len:
                sm = sm + pltpu.roll(sm, ts + HALO - sh, 0)
                sh *= 2
            emit_dz(0, cs, sm[0:ts] - dp)

    def rev(tt):
        return (nt - 1 - tt, 0)

    halo = _halo_index(ts)
    tile = pl.BlockSpec((ts, d), rev)
    hspec = pl.BlockSpec((HALO, d), lambda tt: halo(nt - 1 - tt))
    f32o = jax.ShapeDtypeStruct((s, d), F32)
    bfo = jax.ShapeDtypeStruct((s, d), BF16)
    consts = (pwt, wrt, wit, wlot, wsct, wmixt, vec)
    return _pallas(
        body, name=name, grid=(nt,),
        in_specs=[tile, tile, pl.BlockSpec((ts, 8 * d), rev), tile, hspec] + [tile] * 8
        + [_const_spec(c.shape) for c in consts],
        out_specs=[pl.BlockSpec((ts, 8 * d), rev), tile, tile, tile,
                   _acc_spec((A_ROWS, d)), _acc_spec((1, 8 * d)),
                   _acc_spec(pwt.shape), _acc_spec(wrt.shape), _acc_spec(wit.shape)],
        out_shape=[jax.ShapeDtypeStruct((s, 8 * d), BF16), f32o, bfo, bfo,
                   jax.ShapeDtypeStruct((A_ROWS, d), F32), jax.ShapeDtypeStruct((1, 8 * d), F32),
                   jax.ShapeDtypeStruct(pwt.shape, F32), jax.ShapeDtypeStruct(wrt.shape, F32),
                   jax.ShapeDtypeStruct(wit.shape, F32)],
        scratch_shapes=[pltpu.VMEM((ts, d), F32)] * 6 + [pltpu.VMEM((SUBLANES, d), F32)]
        + [pltpu.VMEM((HALO, d), F32)] * 4,
        semantics=("arbitrary",), args=(dx1, rpre, z, h, h, ypre, yl, yc, pp, vv, rr, ii, cq, *consts), comm=comm)


def _softmax_rows(sc):
    mx = jnp.max(sc, axis=-1, keepdims=True)
    ex = jnp.exp(sc - mx)
    return ex * (1.0 / jnp.sum(ex, axis=-1, keepdims=True))


def _attn_fwd(x1, wq, wo, kt, vv, vec, name):
    s, d = x1.shape
    ts = _tile(s, TS_ATTN)
    hd = d // X_HEADS
    scale = hd ** -0.5

    def body(x_ref, wq_ref, wo_ref, kt_ref, v_ref, vec_ref, x2_ref, rpre_ref, q_ref, o_ref):
        xv = x_ref[...]
        q = _dot(xv, wq_ref[...]).astype(BF16)
        q_ref[...] = q
        for hh in range(X_HEADS):
            cs = slice(hh * hd, (hh + 1) * hd)
            p = _softmax_rows(_dot(q[:, cs], kt_ref[cs, :]) * scale)
            o_ref[:, cs] = _dot(p, v_ref[:, cs]).astype(BF16)
        rpre = ALPHA * xv + _dot(o_ref[...], wo_ref[...])
        rpre_ref[...] = rpre
        x2_ref[...] = _ln_fwd(rpre, vec_ref[V_G + 1:V_G + 2, :], vec_ref[V_B + 1:V_B + 2, :])

    tile = pl.BlockSpec((ts, d), lambda t: (t, 0))
    f32o = jax.ShapeDtypeStruct((s, d), F32)
    bfo = jax.ShapeDtypeStruct((s, d), BF16)
    consts = (wq, wo, kt, vv, vec)
    return pl.pallas_call(
        body, name=name, grid=(s // ts,),
        in_specs=[tile] + [_const_spec(c.shape) for c in consts],
        out_specs=[tile] * 4, out_shape=[f32o, f32o, bfo, bfo],
        compiler_params=_cparams(("parallel",)),
    )(x1, *consts)


def _attn_bwd(dx2, rpre, q, wqt, wot, kk, kt, vt, vec, name):
    s, d = dx2.shape
    ts = _tile(s, TS_ATTN)
    nm = kk.shape[0]
    hd = d // X_HEADS
    scale = hd ** -0.5

    def body(dx2_ref, rpre_ref, q_ref, wqt_ref, wot_ref, k_ref, kt_ref, vt_ref, vec_ref,
             dx1_ref, dq_ref, dr_ref, dk_ref, dv_ref, ln_ref):
        @pl.when(pl.program_id(0) == 0)
        def _():
            for ref in (dk_ref, dv_ref, ln_ref):
                ref[...] = jnp.zeros_like(ref)

        dyv = dx2_ref[...]
        dr, dyy = _ln_bwd(dyv, rpre_ref[...], vec_ref[V_G + 1:V_G + 2, :])
        ln_ref[0:1, :] += _colsum(dyy)
        ln_ref[1:2, :] += _colsum(dyv)
        dr_ref[...] = dr.astype(BF16)
        do = _dot(dr, wot_ref[...])
        q = q_ref[...]
        for hh in range(X_HEADS):
            cs = slice(hh * hd, (hh + 1) * hd)
            p = _softmax_rows(_dot(q[:, cs], kt_ref[cs, :]) * scale)
            dp = _dot(do[:, cs], vt_ref[cs, :])
            ds = p * (dp - jnp.sum(dp * p, axis=-1, keepdims=True)) * scale
            dq_ref[:, cs] = _dot(ds, k_ref[:, cs]).astype(BF16)
            dk_ref[:, cs] += _dot_tn(ds, q[:, cs])
            dv_ref[:, cs] += _dot_tn(p, do[:, cs])
        dx1_ref[...] = ALPHA * dr + _dot(dq_ref[...], wqt_ref[...])

    tile = pl.BlockSpec((ts, d), lambda t: (t, 0))
    consts = (wqt, wot, kk, kt, vt, vec)
    return pl.pallas_call(
        body, name=name, grid=(s // ts,),
        in_specs=[tile, tile, tile] + [_const_spec(c.shape) for c in consts],
        out_specs=[tile, tile, tile, _acc_spec((nm, d)), _acc_spec((nm, d)), _acc_spec((2, d))],
        out_shape=[jax.ShapeDtypeStruct((s, d), F32), jax.ShapeDtypeStruct((s, d), BF16),
                   jax.ShapeDtypeStruct((s, d), BF16), jax.ShapeDtypeStruct((nm, d), F32),
                   jax.ShapeDtypeStruct((nm, d), F32), jax.ShapeDtypeStruct((2, d), F32)],
        compiler_params=_cparams(("arbitrary",)),
    )(dx2, rpre, q, *consts)


def _ffn_out(x2, hgu, wd, vec, name, target=None):
    s, d = x2.shape
    ff = wd.shape[0]
    ts = _tile(s, TS_FFN)
    chunk = 2 * 128 if ff % (2 * 128) == 0 else ff

    def body(*refs):
        if target is None:
            x_ref, hgu_ref, wd_ref, vec_ref, out_ref, rpre_ref, act_ref = refs
        else:
            x_ref, hgu_ref, wd_ref, vec_ref, t_ref, out_ref, rpre_ref, act_ref, loss_ref = refs
        for c in range(0, ff, chunk):
            hg = hgu_ref[:, c:c + chunk]
            act_ref[:, c:c + chunk] = (hg * _sigmoid(hg) * hgu_ref[:, ff + c:ff + c + chunk]).astype(BF16)
        rpre = ALPHA * x_ref[...] + _dot(act_ref[...], wd_ref[...])
        rpre_ref[...] = rpre
        x3 = _ln_fwd(rpre, vec_ref[V_G + 2:V_G + 3, :], vec_ref[V_B + 2:V_B + 3, :])
        if target is None:
            out_ref[...] = x3
            return

        @pl.when(pl.program_id(0) == 0)
        def _():
            loss_ref[...] = jnp.zeros_like(loss_ref)

        err = x3 - t_ref[...]
        out_ref[...] = err / d
        per_token = jnp.mean(err * err, axis=-1, keepdims=True)
        loss_ref[...] += 0.5 * jnp.sum(per_token, axis=0, keepdims=True)

    tile = pl.BlockSpec((ts, d), lambda t: (t, 0))
    f32o = jax.ShapeDtypeStruct((s, d), F32)
    in_specs = [tile, pl.BlockSpec((ts, 2 * ff), lambda t: (t, 0)), _const_spec(wd.shape), _const_spec(vec.shape)]
    out_specs = [tile, tile, pl.BlockSpec((ts, ff), lambda t: (t, 0))]
    out_shape = [f32o, f32o, jax.ShapeDtypeStruct((s, ff), BF16)]
    args = [x2, hgu, wd, vec]
    if target is not None:
        in_specs.append(tile)
        args.append(target)
        out_specs.append(_acc_spec((1, 1)))
        out_shape.append(jax.ShapeDtypeStruct((1, 1), F32))
    return pl.pallas_call(
        body, name=name, grid=(s // ts,), in_specs=in_specs, out_specs=out_specs, out_shape=out_shape,
        compiler_params=_cparams(("parallel",) if target is None else ("arbitrary",)),
    )(*args)


def _ffn_bwd(dy, rpre, hgu, wdt, wgt, wut, vec, name, comm=()):
    s, d = dy.shape
    ff = wgt.shape[0]
    ts = _tile(s, TS_FFN)
    chunk = 2 * 128 if ff % (2 * 128) == 0 else ff

    def body(dy_ref, rpre_ref, hgu_ref, wdt_ref, wgt_ref, wut_ref, vec_ref, dx_ref, dr_ref, dhgu_ref, ln_ref):
        @pl.when(pl.program_id(0) == 0)
        def _():
            ln_ref[...] = jnp.zeros_like(ln_ref)

        dyv = dy_ref[...]
        dr, dyy = _ln_bwd(dyv, rpre_ref[...], vec_ref[V_G + 2:V_G + 3, :])
        ln_ref[0:1, :] += _colsum(dyy)
        ln_ref[1:2, :] += _colsum(dyv)
        dr_ref[...] = dr.astype(BF16)
        dact = _dot(dr, wdt_ref[...])
        for c in range(0, ff, chunk):
            hg = hgu_ref[:, c:c + chunk]
            hu = hgu_ref[:, ff + c:ff + c + chunk]
            da = dact[:, c:c + chunk]
            sg = _sigmoid(hg)
            dhgu_ref[:, c:c + chunk] = (da * hu * (sg * (1.0 + hg * (1.0 - sg)))).astype(BF16)
            dhgu_ref[:, ff + c:ff + c + chunk] = (da * hg * sg).astype(BF16)
        dx_ref[...] = (ALPHA * dr + _dot(dhgu_ref[:, 0:ff], wgt_ref[...])
                       + _dot(dhgu_ref[:, ff:2 * ff], wut_ref[...]))

    tile = pl.BlockSpec((ts, d), lambda t: (t, 0))
    wide = pl.BlockSpec((ts, 2 * ff), lambda t: (t, 0))
    consts = (wdt, wgt, wut, vec)
    return _pallas(
        body, name=name, grid=(s // ts,),
        in_specs=[tile, tile, wide] + [_const_spec(c.shape) for c in consts],
        out_specs=[tile, tile, wide, _acc_spec((2, d))],
        out_shape=[jax.ShapeDtypeStruct((s, d), F32), jax.ShapeDtypeStruct((s, d), BF16),
                   jax.ShapeDtypeStruct((s, 2 * ff), BF16), jax.ShapeDtypeStruct((2, d), F32)],
        semantics=("arbitrary",), args=(dy, rpre, hgu, *consts), comm=comm)


SHARD_AXIS = {"w_in": 1, "pool_w": 1, "lru_w_out": 0, "sconv_w_out": 0, "w_mix_out": 0,
              "xa_w_q": 0, "xa_w_k": 0, "xa_w_v": 0, "xa_w_o": 0,
              "ffn_w_gate": 0, "ffn_w_up": 0, "ffn_w_down": 0,
              "lru_conv_w": 1, "sconv_w": 1, "ln_g": 1, "ln_b": 1}
STORED_TRANSPOSED = ("ffn_w_gate", "ffn_w_up")
GROUP_IN = ("w_in",)
GROUP_MIXER = ("pool_w", "lru_w_out", "sconv_w_out", "w_mix_out")
GROUP_ATTN = ("xa_w_q", "xa_w_k", "xa_w_v", "xa_w_o")
GROUP_FFN = ("ffn_w_gate", "ffn_w_up", "ffn_w_down")
GROUP_VECTORS = ("lru_conv_w", "sconv_w", "ln_g", "ln_b")
REPLICATED = ("b_in", "pool_scale", "lru_conv_b", "lru_w_r", "lru_b_r", "lru_w_i", "lru_b_i", "lru_lambda")
WEIGHTS = ("w_in", "b_in", "pool_w", "pool_scale", "lru_conv_w", "lru_conv_b", "lru_w_r", "lru_b_r", "lru_w_i",
           "lru_b_i", "lru_lambda", "lru_w_out", "sconv_w", "sconv_w_out", "w_mix_out", "xa_w_q", "xa_w_k",
           "xa_w_v", "xa_w_o", "ffn_w_gate", "ffn_w_up", "ffn_w_down", "ln_g", "ln_b")


def _pack(arrs, width, lead=0, row_multiple=ROW_PAD):
    head = arrs[0].shape[:lead]
    flat = jnp.concatenate([a.reshape(head + (-1,)) for a in arrs], axis=lead)
    n = flat.shape[-1]
    chunk = width * row_multiple
    total = -(-n // chunk) * chunk
    if total != n:
        flat = jnp.pad(flat, [(0, 0)] * lead + [(0, total - n)])
    return flat.reshape(head + (total // width, width))


def _unpack(buf, shapes, lead=0):
    head = buf.shape[:lead]
    flat = buf.reshape(head + (-1,))
    out, off = [], 0
    for shp in shapes:
        n = math.prod(shp)
        out.append(flat[..., off:off + n].reshape(head + tuple(shp)))
        off += n
    return out


def _split8(a, axis):
    shp = a.shape
    a = a.reshape(shp[:axis] + (N_DEV, shp[axis] // N_DEV) + shp[axis + 1:])
    return jnp.moveaxis(a, axis, 0)


def _join8(a, axis):
    a = jnp.moveaxis(a, 0, axis)
    shp = a.shape
    return a.reshape(shp[:axis] + (shp[axis] * shp[axis + 1],) + shp[axis + 2:])


def _t(a):
    return jnp.swapaxes(a, -1, -2)


def _stored(name, a):
    return _t(a) if name in STORED_TRANSPOSED else a


def kernel(x, mem, w_in, b_in, pool_w, pool_scale, lru_conv_w, lru_conv_b, lru_w_r, lru_b_r, lru_w_i, lru_b_i, lru_lambda, lru_w_out, sconv_w, sconv_w_out, w_mix_out, xa_w_q, xa_w_k, xa_w_v, xa_w_o, ffn_w_gate, ffn_w_up, ffn_w_down, ln_g, ln_b, loss_target, m_w_in, m_b_in, m_pool_w, m_pool_scale, m_lru_conv_w, m_lru_conv_b, m_lru_w_r, m_lru_b_r, m_lru_w_i, m_lru_b_i, m_lru_lambda, m_lru_w_out, m_sconv_w, m_sconv_w_out, m_w_mix_out, m_xa_w_q, m_xa_w_k, m_xa_w_v, m_xa_w_o, m_ffn_w_gate, m_ffn_w_up, m_ffn_w_down, m_ln_g, m_ln_b, v_w_in, v_b_in, v_pool_w, v_pool_scale, v_lru_conv_w, v_lru_conv_b, v_lru_w_r, v_lru_b_r, v_lru_w_i, v_lru_b_i, v_lru_lambda, v_lru_w_out, v_sconv_w, v_sconv_w_out, v_w_mix_out, v_xa_w_q, v_xa_w_k, v_xa_w_v, v_xa_w_o, v_ffn_w_gate, v_ffn_w_up, v_ffn_w_down, v_ln_g, v_ln_b):
    args = dict(locals())
    w = {n: args[n] for n in WEIGHTS}
    mom_m = {n: args["m_" + n] for n in WEIGHTS}
    mom_v = {n: args["v_" + n] for n in WEIGHTS}
    depth = w_in.shape[0]
    s, d = x.shape[1], x.shape[2]
    nm = mem.shape[1]
    ff = ffn_w_gate.shape[2] * N_DEV
    xs = x.reshape(s, d)
    mems = mem.reshape(nm, d)
    target = loss_target.reshape(s, d)

    def shard(t, n, l):
        return _stored(n, t[n][l])

    def pack_shards(t, names, l, dtype=None):
        arrs = [shard(t, n, l) for n in names]
        return _pack([a if dtype is None else a.astype(dtype) for a in arrs], d)

    def unpack_gathered(buf, names):
        pieces = _unpack(buf, [shard(w, n, 0).shape for n in names], lead=1)
        return {n: (p if n == "w_in" else _join8(p, SHARD_AXIS[n])) for n, p in zip(names, pieces)}

    def layer_vec(l, fw):
        vec = jnp.zeros((V_ROWS, d), F32)
        vec = vec.at[V_PSCALE].set(pool_scale[l]).at[V_CW:V_CW + LRU_CONV].set(fw["lru_conv_w"])
        vec = vec.at[V_CB].set(lru_conv_b[l]).at[V_BR].set(lru_b_r[l]).at[V_BI].set(lru_b_i[l])
        vec = vec.at[V_LAM].set(lru_lambda[l]).at[V_SW:V_SW + SCONV_K].set(fw["sconv_w"])
        return vec.at[V_G:V_G + 3].set(fw["ln_g"]).at[V_B:V_B + 3].set(fw["ln_b"])

    def layer_params(l, fw):
        return dict(
            vec=layer_vec(l, fw), wint=_t(fw["w_in"]).reshape(1, 8 * d, d),
            pw=fw["pool_w"], pwt=_t(fw["pool_w"]),
            wr=lru_w_r[l].astype(BF16), wi=lru_w_i[l].astype(BF16),
            wrt=_t(lru_w_r[l]).astype(BF16), wit=_t(lru_w_i[l]).astype(BF16),
            wlo=fw["lru_w_out"], wlot=_t(fw["lru_w_out"]),
            wsc=fw["sconv_w_out"], wsct=_t(fw["sconv_w_out"]),
            wmix=fw["w_mix_out"], wmixt=_t(fw["w_mix_out"]),
            wq=fw["xa_w_q"], wqt=_t(fw["xa_w_q"]), wo=fw["xa_w_o"], wot=_t(fw["xa_w_o"]),
            wkv=jnp.stack([fw["xa_w_k"], fw["xa_w_v"]]),
            wgu=jnp.stack([_t(fw["ffn_w_gate"]), _t(fw["ffn_w_up"])]),
            wgt=fw["ffn_w_gate"], wut=fw["ffn_w_up"],
            wd=fw["ffn_w_down"], wdt=_t(fw["ffn_w_down"]))

    later = GROUP_ATTN + GROUP_FFN
    fw0 = unpack_gathered(_all_gather(pack_shards(w, GROUP_IN + GROUP_MIXER, 0, BF16), "gather_mixer_0"),
                          GROUP_IN + GROUP_MIXER)
    vectors = _all_gather(_pack([shard(w, n, l) for l in range(depth) for n in GROUP_VECTORS], d), "gather_vectors")
    vec_pieces = _unpack(vectors, [shard(w, n, l).shape for l in range(depth) for n in GROUP_VECTORS], lead=1)
    fvec = [{n: _join8(vec_pieces[l * len(GROUP_VECTORS) + k], SHARD_AXIS[n]) for k, n in enumerate(GROUP_VECTORS)}
            for l in range(depth)]

    layers, saved = [], []
    cur = xs
    fw_next = None
    for l in range(depth):
        fw = dict(fw0 if l == 0 else fw_next)
        fw.update(fvec[l])
        comm = []
        if l == 0:
            comm.append(("gather", pack_shards(w, later, 0, BF16)))
        if l + 1 < depth:
            comm.append(("gather", pack_shards(w, GROUP_IN + GROUP_MIXER + later, l + 1, BF16)))
        z, x1, rpre1, h, ypre, yl, yc, merged, e, pp, vb, rb, ib, cq, *got = _mixer_fwd(
            cur, fw["w_in"], b_in[l].reshape(1, 8 * d), fw["pool_w"], lru_w_r[l].astype(BF16),
            lru_w_i[l].astype(BF16), fw["lru_w_out"], fw["sconv_w_out"], fw["w_mix_out"],
            layer_vec(l, fw), f"mixer_fwd_{l}", comm=comm)
        if l == 0:
            fw.update(unpack_gathered(got.pop(0), later))
        if l + 1 < depth:
            fw_next = unpack_gathered(got.pop(0), GROUP_IN + GROUP_MIXER + later)
        p = layer_params(l, fw)
        kv = _mm(mems, p["wkv"], f"kv_{l}")[0]
        kk = kv[:, :d].astype(BF16)
        vv = kv[:, d:].astype(BF16)
        x2, rpre2, q, o = _attn_fwd(x1, p["wq"], p["wo"], _t(kk), vv, p["vec"], f"attn_fwd_{l}")
        hgu = _mm(x2, p["wgu"], f"ffn_in_{l}", tm=TS_MM // 2)[0]
        if l + 1 < depth:
            cur_next, rpre3, act = _ffn_out(x2, hgu, p["wd"], p["vec"], f"ffn_out_{l}")
        else:
            dcur, rpre3, act, loss_part = _ffn_out(x2, hgu, p["wd"], p["vec"], f"ffn_out_{l}", target=target)
            cur_next = None
        layers.append(p)
        saved.append(dict(x0=cur, z=z, x1=x1, rpre1=rpre1, h=h, ypre=ypre, yl=yl, yc=yc, merged=merged, e=e,
                          pp=pp, vb=vb, rb=rb, ib=ib, cq=cq,
                          kk=kk, vv=vv, x2=x2, rpre2=rpre2, q=q, o=o, hgu=hgu, rpre3=rpre3, act=act))
        cur = cur_next

    loss = lax.psum(loss_part[0, 0], ("x", "y", "c"))

    res = {}

    def slots_of(g, names):
        whole = [n for n in names if n not in GROUP_VECTORS]
        parts = [(g[n] if n == "w_in" else _split8(g[n], SHARD_AXIS[n])).reshape(N_DEV, -1, d) for n in whole]
        if len(whole) < len(names):
            assert tuple(names[len(whole):]) == GROUP_VECTORS
            parts.append(_pack([_split8(g[n], SHARD_AXIS[n]) for n in GROUP_VECTORS], d, lead=1))
        return parts

    def update(received, names, l, tag):
        outs = _adamw_sum(received, *[pack_shards(t, names, l) for t in (w, mom_m, mom_v)], f"adamw_{tag}_{l}")
        shapes = [shard(w, n, l).shape for n in names]
        for n, *parts in zip(names, *[_unpack(o, shapes) for o in outs]):
            res[(n, l)] = [_stored(n, a) for a in parts]

    def settle(exchanges, got):
        for (names, l, tag, _), received in zip(exchanges, got):
            update(received, names, l, tag)

    grads = [None] * depth
    for l in reversed(range(depth)):
        p, sv = layers[l], saved[l]
        g = {}
        dx2, dr3, dhgu, ln3 = _ffn_bwd(dcur, sv["rpre3"], sv["hgu"], p["wdt"], p["wgt"], p["wut"], p["vec"],
                                       f"ffn_bwd_{l}")
        g["ffn_w_down"] = _mm_tn(sv["act"], dr3, d, f"g_wd_{l}")[0][0]
        dwgu = _mm_tn(dhgu, sv["x2"], d, f"g_wgu_{l}", tk=ff)[0][0]
        g["ffn_w_gate"], g["ffn_w_up"] = dwgu[:ff], dwgu[ff:]
        dx1, dq, dr2, dk, dv, ln2 = _attn_bwd(dx2, sv["rpre2"], sv["q"], p["wqt"], p["wot"], sv["kk"], _t(sv["kk"]),
                                              _t(sv["vv"]), p["vec"], f"attn_bwd_{l}")
        g["xa_w_o"] = _mm_tn(sv["o"], dr2, d, f"g_wo_{l}")[0][0]
        g["xa_w_q"] = _mm_tn(sv["x1"], dq, d, f"g_wq_{l}")[0][0]
        dwkv = _mm_tn(mems, jnp.concatenate([dk, dv], axis=1), d, f"g_wkv_{l}")[0]
        g["xa_w_k"], g["xa_w_v"] = dwkv[0], dwkv[1]
        ffn_slots = slots_of(g, GROUP_FFN)
        (dz, dr1, dyl, dyc, accs, dbin, g["pool_w"], g["lru_w_r"], g["lru_w_i"], received) = _mixer_bwd(
            dx1, sv["rpre1"], sv["z"], sv["h"], sv["ypre"], sv["yl"], sv["yc"], sv["pp"], sv["vb"], sv["rb"],
            sv["ib"], sv["cq"], p["pwt"], p["wrt"], p["wit"], p["wlot"], p["wsct"], p["wmixt"], p["vec"],
            f"mixer_bwd_{l}",
            comm=[("scatter", ffn_slots)])
        update(received, GROUP_FFN, l, "ffn")
        g["w_mix_out"] = _mm_tn(sv["merged"], dr1, d, f"g_wmix_{l}")[0][0]
        g["lru_w_out"] = _mm_tn(sv["h"], dyl, d, f"g_wlo_{l}")[0][0]
        g["sconv_w_out"] = _mm_tn(sv["e"], dyc, d, f"g_wsc_{l}")[0][0]
        g["b_in"] = dbin[0]
        g["pool_scale"] = accs[A_PSCALE]
        g["lru_conv_w"] = accs[A_CW:A_CW + LRU_CONV]
        g["lru_conv_b"] = accs[A_CB]
        g["lru_b_r"] = accs[A_BR]
        g["lru_b_i"] = accs[A_BI]
        g["lru_lambda"] = accs[A_SP] * (-_sigmoid(-lru_lambda[l]))
        g["sconv_w"] = accs[A_SW:A_SW + SCONV_K]
        g["ln_g"] = jnp.stack([accs[A_G], ln2[0], ln3[0]])
        g["ln_b"] = jnp.stack([accs[A_B], ln2[1], ln3[1]])
        grads[l] = g
        behind_win = [(GROUP_ATTN, l, "attn", slots_of(g, GROUP_ATTN)),
                      (GROUP_MIXER + GROUP_VECTORS, l, "mixer", slots_of(g, GROUP_MIXER + GROUP_VECTORS))]
        g["w_in"], *got = _mm_tn(sv["x0"], dz, d, f"g_win_{l}", comm=[("scatter", t[3]) for t in behind_win])
        settle(behind_win, got)
        behind_dx = [(GROUP_IN, l, "w_in", [a.astype(BF16) for a in slots_of(g, GROUP_IN)])]
        comm = [("scatter", behind_dx[0][3])]
        if l == 0:
            comm.append(("gather", _pack([jnp.stack([grads[k][n] for k in range(depth)]) for n in REPLICATED], d)))
        dcur, *got = _mm(dz, p["wint"], f"dx_{l}", add=dr1, add_scale=ALPHA, comm=comm)
        settle(behind_dx, got)
        if l == 0:
            outs = _adamw_sum(got[1], *[_pack([t[n] for n in REPLICATED], d) for t in (w, mom_m, mom_v)],
                              "adamw_replicated")
            rep_shapes = [w[n].shape for n in REPLICATED]
            final = {n: parts for n, *parts in zip(REPLICATED, *[_unpack(o, rep_shapes) for o in outs])}
    grad_x = dcur.reshape(x.shape)

    for n in WEIGHTS:
        if n not in final:
            final[n] = [jnp.stack([res[(n, l)][k] for l in range(depth)]) for k in range(4)]
    return (loss, grad_x, *[final[n][0] for n in WEIGHTS], *[final[n][1] for n in WEIGHTS],
            *[final[n][2] for n in WEIGHTS], *[final[n][3] for n in WEIGHTS])
```

```python
import functools
import math

import jax
import jax.numpy as jnp
from jax import lax
from jax.experimental import pallas as pl
from jax.experimental.pallas import tpu as pltpu

F32 = jnp.float32
BF16 = jnp.bfloat16
MESH = pl.DeviceIdType.MESH

N_DEV = 8
LRU_HEADS = 8
LRU_CONV = 4
LRU_C = 8.0
SCONV_K = 3
POOL_WINDOWS = (2, 4, 8, 16)
X_HEADS = 4
DEPTH = 2
ALPHA = (2 * DEPTH) ** 0.25
LN_EPS = 1e-5
ADAM_LR = 0.001
ADAM_B1 = 0.9
ADAM_B2 = 0.999
ADAM_EPS = 1e-08
ADAM_WD = 0.01
ADAM_STEP = 10

HALO = 16
SUBLANES = 8
VMEM_LIMIT = 56 * 1024 * 1024
TS_MIXER = 128
TS_ATTN = 512
TS_FFN = 256
TS_MM = 1024
TK_MM = 2048
TR_ADAM = 256
ROW_PAD = 8
TS_MM_TN = 1024

V_PSCALE, V_CW, V_CB, V_BR, V_BI, V_LAM, V_SW, V_G, V_B = 0, 1, 5, 6, 7, 8, 9, 12, 15
V_ROWS = 24
A_PSCALE, A_CW, A_CB, A_BR, A_BI, A_SP, A_SW, A_G, A_B = 0, 1, 5, 6, 7, 8, 9, 12, 13
A_ROWS = 16


def _cparams(sem):
    return pltpu.CompilerParams(dimension_semantics=sem, vmem_limit_bytes=VMEM_LIMIT)


def _tile(n, pref):
    if n <= pref:
        return n
    assert n % pref == 0, (n, pref)
    return pref


def _const_spec(shape):
    nd = len(shape)
    return pl.BlockSpec(shape, lambda *_: (0,) * nd, pipeline_mode=pl.Buffered(1))


def _acc_spec(shape):
    nd = len(shape)
    return pl.BlockSpec(shape, lambda *_: (0,) * nd)


def _dot(a, b):
    return jnp.dot(a.astype(BF16), b.astype(BF16), preferred_element_type=F32)


def _dot_tn(a, b):
    return lax.dot_general(a.astype(BF16), b.astype(BF16), (((0,), (0,)), ((), ())),
                           preferred_element_type=F32)


def _sigmoid(x):
    return 0.5 * jnp.tanh(0.5 * x) + 0.5


def _softplus(y):
    e = jnp.exp(-jnp.abs(y))
    log1p = jnp.where(e < 1e-4, e * (1.0 - e * (0.5 - e * (1.0 / 3.0))), jnp.log(1.0 + e))
    return jnp.maximum(y, 0.0) + log1p


def _ln_fwd(r, g, b):
    mu = jnp.mean(r, axis=-1, keepdims=True)
    xc = r - mu
    var = jnp.mean(xc * xc, axis=-1, keepdims=True)
    return xc * lax.rsqrt(var + LN_EPS) * g + b


def _ln_bwd(dy, r, g):
    mu = jnp.mean(r, axis=-1, keepdims=True)
    xc = r - mu
    var = jnp.mean(xc * xc, axis=-1, keepdims=True)
    rstd = lax.rsqrt(var + LN_EPS)
    yhat = xc * rstd
    dyh = dy * g
    m1 = jnp.mean(dyh, axis=-1, keepdims=True)
    m2 = jnp.mean(dyh * yhat, axis=-1, keepdims=True)
    return rstd * (dyh - m1 - yhat * m2), dy * yhat


def _colsum(a):
    return jnp.sum(a, axis=0, keepdims=True)


def _position():
    return lax.axis_index("x"), lax.axis_index("y"), lax.axis_index("c")


def _gather_copies(x_ref, out_ref, send_sems, recv_sems, local_sem):
    x, y, c = _position()
    me, sibling = (x, y, c), (x, y, 1 - c)
    chips = [(1 - x, y), (x, 1 - y), (1 - x, 1 - y)]

    def slot(px, py, pc):
        return out_ref.at[4 * px + 2 * py + pc]

    def copy(k, block, to, src=None):
        return pltpu.make_async_remote_copy(
            src_ref=slot(*block) if src is None else src, dst_ref=slot(*block),
            send_sem=send_sems.at[k], recv_sem=recv_sems.at[k], device_id=to, device_id_type=MESH)

    mine = pltpu.make_async_copy(x_ref, slot(*me), local_sem)
    first = [copy(0, me, sibling, src=x_ref)]
    first += [copy(1 + j, me, (*chip, c), src=x_ref) for j, chip in enumerate(chips)]
    passed = [copy(4 + j, (*chip, c), sibling) for j, chip in enumerate(chips)]
    over_ici = [copy(1 + j, (*chip, c), me) for j, chip in enumerate(chips)]
    from_sibling = copy(0, sibling, me)
    forwarded = [copy(4 + j, (*chip, 1 - c), me) for j, chip in enumerate(chips)]
    return mine, first, passed, over_ici, from_sibling, forwarded


def _scatter_copies(g_refs, out_ref, send_sems, recv_sems, local_sem):
    x, y, c = _position()
    me = 4 * x + 2 * y + c
    offsets, rows = [], 0
    for g_ref in g_refs:
        offsets.append(rows)
        rows += g_ref.shape[1]

    def landing(g_ref, off):
        return out_ref.at[me, pl.ds(off, g_ref.shape[1])]

    mine = [pltpu.make_async_copy(g_ref.at[me], landing(g_ref, off), local_sem)
            for g_ref, off in zip(g_refs, offsets)]
    mine_all = pltpu.make_async_copy(out_ref.at[me], out_ref.at[me], local_sem)
    copies, waits = [], []
    for k in range(1, N_DEV):
        px = 1 - x if k & 4 else x
        py = 1 - y if k & 2 else y
        pc = 1 - c if k & 1 else c
        sems = dict(send_sem=send_sems.at[k - 1], recv_sem=recv_sems.at[k - 1],
                    device_id=(px, py, pc), device_id_type=MESH)
        copies += [pltpu.make_async_remote_copy(src_ref=g_ref.at[4 * px + 2 * py + pc], dst_ref=landing(g_ref, off),
                                                **sems) for g_ref, off in zip(g_refs, offsets)]
        waits.append(pltpu.make_async_remote_copy(src_ref=out_ref.at[me], dst_ref=out_ref.at[me], **sems))
    return mine, copies, mine_all, waits


def _comm_start(kind, srcs, *refs):
    if kind == "gather":
        mine, first, _, _, _, _ = _gather_copies(srcs[0], *refs)
        mine.start()
        for cp in first:
            cp.start()
    else:
        mine, copies, _, _ = _scatter_copies(srcs, *refs)
        for cp in mine + copies:
            cp.start()


def _comm_finish(kind, srcs, *refs):
    if kind == "gather":
        mine, first, passed, over_ici, from_sibling, forwarded = _gather_copies(srcs[0], *refs)
        for arrival, forward in zip(over_ici, passed):
            arrival.wait_recv()
            forward.start()
        from_sibling.wait_recv()
        for arrival in forwarded:
            arrival.wait_recv()
        for cp in first + passed:
            cp.wait_send()
        mine.wait()
    else:
        _, _, mine_all, waits = _scatter_copies(srcs, *refs)
        for cp in waits:
            cp.wait_recv()
        for cp in waits:
            cp.wait_send()
        mine_all.wait()


def _comm_sources(kind, payload):
    return [payload] if kind == "gather" else list(payload)


def _comm_out_shape(kind, payload):
    srcs = _comm_sources(kind, payload)
    if kind == "gather":
        return jax.ShapeDtypeStruct((N_DEV,) + srcs[0].shape, srcs[0].dtype)
    return jax.ShapeDtypeStruct((N_DEV, sum(a.shape[1] for a in srcs), srcs[0].shape[2]), srcs[0].dtype)


COMM_SEMAPHORES = [pltpu.SemaphoreType.DMA((7,)), pltpu.SemaphoreType.DMA((7,)), pltpu.SemaphoreType.DMA]


def _pallas(body, *, name, grid, in_specs, out_specs, out_shape, semantics, args, scratch_shapes=(), comm=()):
    in_specs, out_specs, out_shape = list(in_specs), list(out_specs), list(out_shape)
    scratch_shapes = list(scratch_shapes)
    n_in, n_out, n_scr, nc = len(in_specs), len(out_specs), len(scratch_shapes), len(comm)
    if not comm:
        return pl.pallas_call(body, name=name, grid=grid, in_specs=in_specs, out_specs=out_specs, out_shape=out_shape,
                              scratch_shapes=scratch_shapes, compiler_params=_cparams(semantics))(*args)
    kinds = [kind for kind, _ in comm]
    sources = [_comm_sources(kind, payload) for kind, payload in comm]
    n_src = sum(len(srcs) for srcs in sources)

    def carrying(*refs):
        ins, rest = refs[:n_in], refs[n_in:]
        cin, rest = list(rest[:n_src]), rest[n_src:]
        outs, rest = rest[:n_out], rest[n_out:]
        cout, rest = rest[:nc], rest[nc:]
        scr, sems = rest[:n_scr], rest[n_scr:]
        ids = [pl.program_id(ax) for ax in range(len(grid))]
        first = functools.reduce(jnp.logical_and, [i == 0 for i in ids])
        last = functools.reduce(jnp.logical_and, [i == g - 1 for i, g in zip(ids, grid)])
        plans = []
        for k in range(nc):
            mine, cin = cin[:len(sources[k])], cin[len(sources[k]):]
            plans.append((kinds[k], mine, cout[k], *sems[3 * k:3 * k + 3]))

        @pl.when(first)
        def _():
            for plan in plans:
                _comm_start(*plan)

        body(*ins, *outs, *scr)

        @pl.when(last)
        def _():
            for plan in plans:
                _comm_finish(*plan)

    hbm = pl.BlockSpec(memory_space=pl.ANY)
    return pl.pallas_call(
        carrying, name=name, grid=grid,
        in_specs=in_specs + [hbm] * n_src, out_specs=out_specs + [hbm] * nc,
        out_shape=out_shape + [_comm_out_shape(kind, payload) for kind, payload in comm],
        scratch_shapes=scratch_shapes + COMM_SEMAPHORES * nc,
        compiler_params=_cparams(("arbitrary",) * len(grid)),
    )(*args, *[a for srcs in sources for a in srcs])


def _all_gather(xs, name):
    def body(x_ref, out_ref, send_sems, recv_sems, local_sem):
        _comm_start("gather", [x_ref], out_ref, send_sems, recv_sems, local_sem)
        _comm_finish("gather", [x_ref], out_ref, send_sems, recv_sems, local_sem)

    return pl.pallas_call(
        body, name=name, out_shape=_comm_out_shape("gather", xs),
        in_specs=[pl.BlockSpec(memory_space=pl.ANY)], out_specs=pl.BlockSpec(memory_space=pl.ANY),
        scratch_shapes=COMM_SEMAPHORES,
    )(xs)


def _adamw_sum(parts, w, m, v, name):
    _, rows, width = parts.shape
    tr = max(t for t in range(SUBLANES, min(rows, TR_ADAM) + 1, SUBLANES) if rows % t == 0)
    c1 = 1.0 - ADAM_B1 ** ADAM_STEP
    c2 = 1.0 - ADAM_B2 ** ADAM_STEP

    def body(p_ref, w_ref, m_ref, v_ref, g_ref, d_ref, nm_ref, nv_ref):
        g = p_ref[0].astype(F32)
        for k in range(1, N_DEV):
            g = g + p_ref[k].astype(F32)
        nm = ADAM_B1 * m_ref[...] + (1.0 - ADAM_B1) * g
        nv = ADAM_B2 * v_ref[...] + (1.0 - ADAM_B2) * (g * g)
        m_hat = nm / c1
        v_hat = nv / c2
        g_ref[...] = g
        d_ref[...] = -ADAM_LR * (m_hat / (jnp.sqrt(v_hat) + ADAM_EPS) + ADAM_WD * w_ref[...])
        nm_ref[...] = nm
        nv_ref[...] = nv

    spec = pl.BlockSpec((tr, width), lambda i: (i, 0))
    out = jax.ShapeDtypeStruct((rows, width), F32)
    return pl.pallas_call(
        body, name=name, grid=(rows // tr,),
        in_specs=[pl.BlockSpec((N_DEV, tr, width), lambda i: (0, i, 0)), spec, spec, spec],
        out_specs=[spec, spec, spec, spec], out_shape=[out, out, out, out],
        compiler_params=_cparams(("parallel",)),
    )(parts, w, m, v)


def _mm(a, wb, name, bias=None, add=None, add_scale=1.0, out_dtype=F32, tm=None, comm=()):
    m, k = a.shape
    nb, k2, tn = wb.shape
    assert k == k2
    tm = _tile(m, TS_MM if tm is None else tm)
    tk = _tile(k, TK_MM)
    nk = k // tk

    def body(*refs):
        a_ref, w_ref = refs[0], refs[1]
        pos = 2
        b_ref = add_ref = None
        if bias is not None:
            b_ref = refs[pos]
            pos += 1
        if add is not None:
            add_ref = refs[pos]
            pos += 1
        o_ref = refs[pos]

        def finish(r):
            if b_ref is not None:
                r = r + b_ref[...]
            if add_ref is not None:
                r = r + add_scale * add_ref[...]
            o_ref[...] = r.astype(o_ref.dtype)

        if nk == 1:
            finish(_dot(a_ref[...], w_ref[...]))
            return
        acc_ref = refs[pos + 1]
        kk = pl.program_id(2)

        @pl.when(kk == 0)
        def _():
            acc_ref[...] = jnp.zeros_like(acc_ref)

        acc_ref[...] += _dot(a_ref[...], w_ref[...])

        @pl.when(kk == nk - 1)
        def _():
            finish(acc_ref[...])

    in_specs = [pl.BlockSpec((tm, tk), lambda j, i, kk: (i, kk)),
                pl.BlockSpec((None, tk, tn), lambda j, i, kk: (j, kk, 0))]
    args = [a, wb]
    if bias is not None:
        in_specs.append(pl.BlockSpec((1, tn), lambda j, i, kk: (0, j)))
        args.append(bias)
    if add is not None:
        in_specs.append(pl.BlockSpec((tm, tn), lambda j, i, kk: (i, j)))
        args.append(add)
    return _pallas(
        body, name=name, grid=(nb, m // tm, nk),
        in_specs=in_specs,
        out_specs=[pl.BlockSpec((tm, tn), lambda j, i, kk: (i, j))],
        out_shape=[jax.ShapeDtypeStruct((m, nb * tn), out_dtype)],
        scratch_shapes=[pltpu.VMEM((tm, tn), F32)] if nk > 1 else [],
        semantics=("parallel", "parallel", "arbitrary"), args=args, comm=comm)


def _mm_tn(a, b, tn, name, tk=None, comm=()):
    s, k = a.shape
    s2, n = b.shape
    assert s == s2 and n % tn == 0
    nb = n // tn
    ts = _tile(s, TS_MM_TN)
    tk = k if tk is None else tk
    assert k % tk == 0

    def body(a_ref, b_ref, o_ref):
        @pl.when(pl.program_id(2) == 0)
        def _():
            o_ref[...] = jnp.zeros_like(o_ref)

        o_ref[...] += _dot_tn(a_ref[...], b_ref[...])

    return _pallas(
        body, name=name, grid=(nb, k // tk, s // ts),
        in_specs=[pl.BlockSpec((ts, tk), lambda j, kb, i: (i, kb)),
                  pl.BlockSpec((ts, tn), lambda j, kb, i: (i, j))],
        out_specs=[pl.BlockSpec((None, tk, tn), lambda j, kb, i: (j, kb, 0))],
        out_shape=[jax.ShapeDtypeStruct((nb, k, tn), F32)],
        semantics=("parallel", "parallel", "arbitrary"), args=(a, b), comm=comm)


def _scan_fwd(a_ref, b_ref, h_ref, carry_ref, ts):
    rowid = lax.broadcasted_iota(jnp.int32, (SUBLANES, 1), 0)

    def group(gi, hprev):
        r0 = pl.multiple_of(gi * SUBLANES, SUBLANES)
        a = a_ref[pl.ds(r0, SUBLANES), :]
        b = b_ref[pl.ds(r0, SUBLANES), :]
        for d in (1, 2, 4):
            a_sh = jnp.where(rowid >= d, pltpu.roll(a, d, 0), 1.0)
            b_sh = jnp.where(rowid >= d, pltpu.roll(b, d, 0), 0.0)
            b = a * b_sh + b
            a = a * a_sh
        hh = a * hprev + b
        h_ref[pl.ds(r0, SUBLANES), :] = hh
        return hh[SUBLANES - 1:SUBLANES, :]

    last = lax.fori_loop(0, ts // SUBLANES, group, carry_ref[0:1, :])
    carry_ref[0:1, :] = last


def _scan_rev(c_ref, b_ref, g_ref, carry_ref, ts):
    rowid = lax.broadcasted_iota(jnp.int32, (SUBLANES, 1), 0)
    ng = ts // SUBLANES

    def group(gi, gnext):
        r0 = pl.multiple_of((ng - 1 - gi) * SUBLANES, SUBLANES)
        c = c_ref[pl.ds(r0, SUBLANES), :]
        b = b_ref[pl.ds(r0, SUBLANES), :]
        for d in (1, 2, 4):
            keep = rowid < SUBLANES - d
            c_sh = jnp.where(keep, pltpu.roll(c, SUBLANES - d, 0), 1.0)
            b_sh = jnp.where(keep, pltpu.roll(b, SUBLANES - d, 0), 0.0)
            b = c * b_sh + b
            c = c * c_sh
        gg = c * gnext + b
        g_ref[pl.ds(r0, SUBLANES), :] = gg
        return gg[0:1, :]

    first = lax.fori_loop(0, ng, group, carry_ref[0:1, :])
    carry_ref[0:1, :] = first


def _past(ext, sh, ts):
    if sh == 0:
        return ext[HALO:HALO + ts]
    return pltpu.roll(ext, sh, 0)[HALO:HALO + ts]


def _future(ext, sh, ts):
    if sh == 0:
        return ext[0:ts]
    return pltpu.roll(ext, ts + HALO - sh, 0)[0:ts]


def _one_minus_sq(a, log_a):
    x = 2.0 * log_a
    series = -x * (1.0 + x * (0.5 + x * (1.0 / 6.0 + x * (1.0 / 24.0))))
    return jnp.where(x > -0.02, series, 1.0 - a * a)


def _halo_index(ts):
    blocks = ts // HALO
    return lambda t: (jnp.maximum(t * blocks - 1, 0), 0)


def _head_columns(d):
    cw = d // LRU_HEADS
    return [slice(c * cw, (c + 1) * cw) for c in range(LRU_HEADS)]


def _shift(cs, off):
    return slice(cs.start + off, cs.stop + off)


def _mixer_fwd(x, win, b_in, pw, wr, wi, wlo, wsc, wmix, vec, name, comm=()):
    s, d = x.shape
    ts = _tile(s, TS_MIXER)
    nt = s // ts
    dg = d // len(POOL_WINDOWS)
    nblk = win.shape[0]
    cols = _head_columns(d)

    def body(x_ref, xn_ref, win_ref, bin_ref, pw_ref, wr_ref, wi_ref, wlo_ref, wsc_ref, wmix_ref, vec_ref,
             z_hbm, x1_ref, rpre_ref, h_ref, ypre_ref, yl_ref, yc_ref, mg_ref, e_ref, p_ref, v_ref, r_ref, ig_ref,
             cq_ref, z_even, z_odd, zhist, a_scr, b_scr, hcarry, z_sem):
        i = pl.program_id(0)
        first = i == 0

        def project_block(xb, dst, k):
            dst[:, k * d:(k + 1) * d] = _dot(xb, win_ref[k]) + bin_ref[:, k * d:(k + 1) * d]

        @pl.when(first)
        def _():
            hcarry[...] = jnp.zeros_like(hcarry)
            zhist[...] = jnp.zeros_like(zhist)
            xb = x_ref[...].astype(BF16)
            for k in range(nblk):
                project_block(xb, z_even, k)

        def vrow(k, cs):
            return vec_ref[k:k + 1, cs]

        def step(zc, zn):
            z_out = pltpu.make_async_copy(zc, z_hbm.at[pl.ds(pl.multiple_of(i * ts, ts), ts), :], z_sem)
            z_out.start()
            xb = xn_ref[...].astype(BF16)
            tglob = i * ts + lax.broadcasted_iota(jnp.int32, (ts, 1), 0)
            sp = _softplus(-vec_ref[V_LAM:V_LAM + 1, :])

            def with_history(k, cs):
                kc = _shift(cs, k * d)
                return jnp.concatenate([jnp.where(first, 0.0, zhist[:, kc]), zc[:, kc]], axis=0)

            for hh, cs in enumerate(cols):
                if hh < nblk:
                    project_block(xb, zn, hh)
                win_len = POOL_WINDOWS[cs.start // dg]
                ext = with_history(0, cs)
                sm = ext
                sh = 1
                while sh < win_len:
                    sm = sm + pltpu.roll(sm, sh, 0)
                    sh *= 2
                inv_cnt = 1.0 / jnp.minimum(tglob + 1, win_len).astype(F32)
                p_ref[:, cs] = (sm[HALO:HALO + ts] * inv_cnt - ext[HALO:HALO + ts]).astype(BF16)
                ext = with_history(1, cs)
                v = vrow(V_CB, cs)
                for j in range(LRU_CONV):
                    v = v + vrow(V_CW + j, cs) * _past(ext, LRU_CONV - 1 - j, ts)
                r = _sigmoid(_dot(v, wr_ref[hh]) + vrow(V_BR, cs))
                ig = _sigmoid(_dot(v, wi_ref[hh]) + vrow(V_BI, cs))
                log_a = -LRU_C * r * sp[:, cs]
                a = jnp.exp(log_a)
                a_scr[:, cs] = a
                b_scr[:, cs] = jnp.sqrt(_one_minus_sq(a, log_a)) * (ig * v)
                v_ref[:, cs] = v
                r_ref[:, cs] = r
                ig_ref[:, cs] = ig
                ext = with_history(3, cs) * with_history(4, cs)
                cq = jnp.zeros((ts, cs.stop - cs.start), F32)
                for j in range(SCONV_K):
                    cq = cq + vrow(V_SW + j, cs) * _past(ext, SCONV_K - 1 - j, ts)
                cq_ref[:, cs] = cq
                e_ref[:, cs] = (zc[:, _shift(cs, 2 * d)] * cq).astype(BF16)
            for k in range(len(cols), nblk):
                project_block(xb, zn, k)
            _scan_fwd(a_scr, b_scr, h_ref, hcarry, ts)
            ypre = jnp.concatenate([_dot(p_ref[:, g * dg:(g + 1) * dg], pw_ref[g])
                                    for g in range(len(POOL_WINDOWS))], axis=1)
            yl = _dot(h_ref[...], wlo_ref[...])
            yc = _dot(e_ref[...], wsc_ref[...])
            ypre_ref[...] = ypre
            yl_ref[...] = yl
            yc_ref[...] = yc
            for cs in cols:
                merged = (_sigmoid(zc[:, _shift(cs, 5 * d)]) * (ypre[:, cs] * vrow(V_PSCALE, cs))
                          + _sigmoid(zc[:, _shift(cs, 6 * d)]) * yl[:, cs]
                          + _sigmoid(zc[:, _shift(cs, 7 * d)]) * yc[:, cs])
                mg_ref[:, cs] = merged.astype(BF16)
            rpre = ALPHA * x_ref[...] + _dot(mg_ref[...], wmix_ref[...])
            x1_ref[...] = _ln_fwd(rpre, vec_ref[V_G:V_G + 1, :], vec_ref[V_B:V_B + 1, :])
            rpre_ref[...] = rpre
            zhist[...] = zc[ts - HALO:ts, :]
            z_out.wait()

        parity = lax.rem(i, 2)

        @pl.when(parity == 0)
        def _():
            step(z_even, z_odd)

        @pl.when(parity == 1)
        def _():
            step(z_odd, z_even)

    tile = pl.BlockSpec((ts, d), lambda t: (t, 0))
    f32o = jax.ShapeDtypeStruct((s, d), F32)
    bfo = jax.ShapeDtypeStruct((s, d), BF16)
    consts = (win, b_in, pw, wr, wi, wlo, wsc, wmix, vec)
    return _pallas(
        body, name=name, grid=(nt,),
        in_specs=[tile, pl.BlockSpec((ts, d), lambda t: (jnp.minimum(t + 1, nt - 1), 0))]
        + [_const_spec(c.shape) for c in consts],
        out_specs=[pl.BlockSpec(memory_space=pl.ANY)] + [tile] * 13,
        out_shape=[jax.ShapeDtypeStruct((s, nblk * d), F32), f32o, f32o, f32o, f32o, f32o, f32o, bfo, bfo, bfo,
                   f32o, f32o, f32o, f32o],
        scratch_shapes=[pltpu.VMEM((ts, nblk * d), F32)] * 2 + [pltpu.VMEM((HALO, nblk * d), F32)]
        + [pltpu.VMEM((ts, d), F32)] * 2 + [pltpu.VMEM((SUBLANES, d), F32), pltpu.SemaphoreType.DMA],
        semantics=("arbitrary",), args=(x, x, *consts), comm=comm)


def _mixer_bwd(dx1, rpre, z, h, ypre, yl, yc, pp, vv, rr, ii, cq, pwt, wrt, wit, wlot, wsct, wmixt, vec, name,
               comm=()):
    s, d = dx1.shape
    ts = _tile(s, TS_MIXER)
    nt = s // ts
    dg = d // len(POOL_WINDOWS)
    cols = _head_columns(d)

    def body(dx1_ref, rpre_ref, z_ref, h_ref, hh_ref, ypre_ref, yl_ref, yc_ref, p_ref, v_ref, r_ref, ig_ref, cq_ref,
             pwt_ref, wrt_ref, wit_ref, wlot_ref, wsct_ref, wmixt_ref, vec_ref,
             dz_ref, dr_ref, dyl_ref, dyc_ref, acc_ref, dbin_ref, dpw_ref, dwr_ref, dwi_ref,
             c_scr, b_scr, g_scr, dyps_scr, a_keep, m_keep, gcarry, acarry, dcq_c, dv_c, m_c):
        i = pl.program_id(0)
        t = nt - 1 - i

        @pl.when(i == 0)
        def _():
            for ref in (gcarry, acarry, dcq_c, dv_c, m_c, acc_ref, dbin_ref, dpw_ref, dwr_ref, dwi_ref):
                ref[...] = jnp.zeros_like(ref)

        def vrow(k, cs):
            return vec_ref[k:k + 1, cs]

        def zc(k, cs):
            return z_ref[:, _shift(cs, k * d)]

        def acc(row, cs, val):
            acc_ref[row:row + 1, cs] += _colsum(val)

        def emit_dz(k, cs, val):
            kc = _shift(cs, k * d)
            dz_ref[:, kc] = val.astype(BF16)
            dbin_ref[:, kc] += _colsum(val)

        def with_future(tile_val, carry_ref, cs):
            ext = jnp.concatenate([tile_val, carry_ref[:, cs]], axis=0)
            carry_ref[:, cs] = tile_val[0:HALO, :]
            return ext

        dx1v = dx1_ref[...]
        dr, dyy = _ln_bwd(dx1v, rpre_ref[...], vec_ref[V_G:V_G + 1, :])
        acc_ref[A_G:A_G + 1, :] += _colsum(dyy)
        acc_ref[A_B:A_B + 1, :] += _colsum(dx1v)
        dr_ref[...] = dr
        dmg = _dot(dr, wmixt_ref[...])
        for cs in cols:
            dm = dmg[:, cs]
            ypre = ypre_ref[:, cs]
            ys = (ypre * vrow(V_PSCALE, cs), yl_ref[:, cs], yc_ref[:, cs])
            dys = []
            for k in range(3):
                gk = _sigmoid(zc(5 + k, cs))
                emit_dz(5 + k, cs, dm * ys[k] * gk * (1.0 - gk))
                dys.append(dm * gk)
            acc(A_PSCALE, cs, dys[0] * ypre)
            dyps_scr[:, cs] = dys[0] * vrow(V_PSCALE, cs)
            dyl_ref[:, cs] = dys[1].astype(BF16)
            dyc_ref[:, cs] = dys[2].astype(BF16)
        de = _dot(dyc_ref[...], wsct_ref[...])
        dh = _dot(dyl_ref[...], wlot_ref[...])

        sp = _softplus(-vec_ref[V_LAM:V_LAM + 1, :])
        for cs in cols:
            dec = de[:, cs]
            emit_dz(2, cs, dec * cq_ref[:, cs])
            dcq_ext = with_future(dec * zc(2, cs), dcq_c, cs)
            zcc, zh = zc(3, cs), zc(4, cs)
            qv = zcc * zh
            dq = jnp.zeros_like(qv)
            for j in range(SCONV_K):
                adv = _future(dcq_ext, SCONV_K - 1 - j, ts)
                acc(A_SW + j, cs, adv * qv)
                dq = dq + vrow(V_SW + j, cs) * adv
            emit_dz(3, cs, dq * zh)
            emit_dz(4, cs, dq * zcc)
            log_a = -LRU_C * r_ref[:, cs] * sp[:, cs]
            a = jnp.exp(log_a)
            a_keep[:, cs] = a
            m_keep[:, cs] = jnp.sqrt(_one_minus_sq(a, log_a))
            c_scr[:, cs] = _future(with_future(a, acarry, cs), 1, ts)
            b_scr[:, cs] = dh[:, cs]
        _scan_rev(c_scr, b_scr, g_scr, gcarry, ts)

        for hh, cs in enumerate(cols):
            gs, a, mult = g_scr[:, cs], a_keep[:, cs], m_keep[:, cs]
            r, ig, v = r_ref[:, cs], ig_ref[:, cs], v_ref[:, cs]
            hprev = _past(jnp.concatenate([jnp.where(t == 0, 0.0, hh_ref[:, cs]), h_ref[:, cs]], axis=0), 1, ts)
            iv = ig * v
            dlog_a = gs * hprev * a + gs * iv * (-(a * a) / mult)
            div = gs * mult
            acc(A_SP, cs, dlog_a * (-LRU_C) * r)
            dpre_r = dlog_a * (-LRU_C) * sp[:, cs] * r * (1.0 - r)
            dpre_i = div * v * ig * (1.0 - ig)
            acc(A_BR, cs, dpre_r)
            acc(A_BI, cs, dpre_i)
            dv = div * ig + _dot(dpre_r, wrt_ref[hh]) + _dot(dpre_i, wit_ref[hh])
            dwr_ref[hh] += _dot_tn(v, dpre_r)
            dwi_ref[hh] += _dot_tn(v, dpre_i)
            acc(A_CB, cs, dv)
            dv_ext = with_future(dv, dv_c, cs)
            zl = zc(1, cs)
            dzl = jnp.zeros_like(zl)
            for j in range(LRU_CONV):
                adv = _future(dv_ext, LRU_CONV - 1 - j, ts)
                acc(A_CW + j, cs, adv * zl)
                dzl = dzl + vrow(V_CW + j, cs) * adv
            emit_dz(1, cs, dzl)

        tglob = t * ts + lax.broadcasted_iota(jnp.int32, (ts, 1), 0)
        for g, win_len in enumerate(POOL_WINDOWS):
            cs = slice(g * dg, (g + 1) * dg)
            dyps = dyps_scr[:, cs]
            dpw_ref[g] += _dot_tn(p_ref[:, cs], dyps)
            dp = _dot(dyps, pwt_ref[g])
            inv_cnt = 1.0 / jnp.minimum(tglob + 1, win_len).astype(F32)
            sm = with_future(dp * inv_cnt, m_c, cs)
            sh = 1
            while sh < win_len:
                sm = sm + pltpu.roll(sm, ts + HALO - sh, 0)
                sh *= 2
            emit_dz(0, cs, sm[0:ts] - dp)

    def rev(tt):
        return (nt - 1 - tt, 0)

    halo = _halo_index(ts)
    tile = pl.BlockSpec((ts, d), rev)
    hspec = pl.BlockSpec((HALO, d), lambda tt: halo(nt - 1 - tt))
    f32o = jax.ShapeDtypeStruct((s, d), F32)
    bfo = jax.ShapeDtypeStruct((s, d), BF16)
    consts = (pwt, wrt, wit, wlot, wsct, wmixt, vec)
    return _pallas(
        body, name=name, grid=(nt,),
        in_specs=[tile, tile, pl.BlockSpec((ts, 8 * d), rev), tile, hspec] + [tile] * 8
        + [_const_spec(c.shape) for c in consts],
        out_specs=[pl.BlockSpec((ts, 8 * d), rev), tile, tile, tile,
                   _acc_spec((A_ROWS, d)), _acc_spec((1, 8 * d)),
                   _acc_spec(pwt.shape), _acc_spec(wrt.shape), _acc_spec(wit.shape)],
        out_shape=[jax.ShapeDtypeStruct((s, 8 * d), BF16), f32o, bfo, bfo,
                   jax.ShapeDtypeStruct((A_ROWS, d), F32), jax.ShapeDtypeStruct((1, 8 * d), F32),
                   jax.ShapeDtypeStruct(pwt.shape, F32), jax.ShapeDtypeStruct(wrt.shape, F32),
                   jax.ShapeDtypeStruct(wit.shape, F32)],
        scratch_shapes=[pltpu.VMEM((ts, d), F32)] * 6 + [pltpu.VMEM((SUBLANES, d), F32)]
        + [pltpu.VMEM((HALO, d), F32)] * 4,
        semantics=("arbitrary",), args=(dx1, rpre, z, h, h, ypre, yl, yc, pp, vv, rr, ii, cq, *consts), comm=comm)


def _softmax_rows(sc):
    mx = jnp.max(sc, axis=-1, keepdims=True)
    ex = jnp.exp(sc - mx)
    return ex * (1.0 / jnp.sum(ex, axis=-1, keepdims=True))


def _attn_fwd(x1, wq, wo, kt, vv, vec, name):
    s, d = x1.shape
    ts = _tile(s, TS_ATTN)
    hd = d // X_HEADS
    scale = hd ** -0.5

    def body(x_ref, wq_ref, wo_ref, kt_ref, v_ref, vec_ref, x2_ref, rpre_ref, q_ref, o_ref):
        xv = x_ref[...]
        q = _dot(xv, wq_ref[...]).astype(BF16)
        q_ref[...] = q
        for hh in range(X_HEADS):
            cs = slice(hh * hd, (hh + 1) * hd)
            p = _softmax_rows(_dot(q[:, cs], kt_ref[cs, :]) * scale)
            o_ref[:, cs] = _dot(p, v_ref[:, cs]).astype(BF16)
        rpre = ALPHA * xv + _dot(o_ref[...], wo_ref[...])
        rpre_ref[...] = rpre
        x2_ref[...] = _ln_fwd(rpre, vec_ref[V_G + 1:V_G + 2, :], vec_ref[V_B + 1:V_B + 2, :])

    tile = pl.BlockSpec((ts, d), lambda t: (t, 0))
    f32o = jax.ShapeDtypeStruct((s, d), F32)
    bfo = jax.ShapeDtypeStruct((s, d), BF16)
    consts = (wq, wo, kt, vv, vec)
    return pl.pallas_call(
        body, name=name, grid=(s // ts,),
        in_specs=[tile] + [_const_spec(c.shape) for c in consts],
        out_specs=[tile] * 4, out_shape=[f32o, f32o, bfo, bfo],
        compiler_params=_cparams(("parallel",)),
    )(x1, *consts)


def _attn_bwd(dx2, rpre, q, wqt, wot, kk, kt, vt, vec, name):
    s, d = dx2.shape
    ts = _tile(s, TS_ATTN)
    nm = kk.shape[0]
    hd = d // X_HEADS
    scale = hd ** -0.5

    def body(dx2_ref, rpre_ref, q_ref, wqt_ref, wot_ref, k_ref, kt_ref, vt_ref, vec_ref,
             dx1_ref, dq_ref, dr_ref, dk_ref, dv_ref, ln_ref):
        @pl.when(pl.program_id(0) == 0)
        def _():
            for ref in (dk_ref, dv_ref, ln_ref):
                ref[...] = jnp.zeros_like(ref)

        dyv = dx2_ref[...]
        dr, dyy = _ln_bwd(dyv, rpre_ref[...], vec_ref[V_G + 1:V_G + 2, :])
        ln_ref[0:1, :] += _colsum(dyy)
        ln_ref[1:2, :] += _colsum(dyv)
        dr_ref[...] = dr.astype(BF16)
        do = _dot(dr, wot_ref[...])
        q = q_ref[...]
        for hh in range(X_HEADS):
            cs = slice(hh * hd, (hh + 1) * hd)
            p = _softmax_rows(_dot(q[:, cs], kt_ref[cs, :]) * scale)
            dp = _dot(do[:, cs], vt_ref[cs, :])
            ds = p * (dp - jnp.sum(dp * p, axis=-1, keepdims=True)) * scale
            dq_ref[:, cs] = _dot(ds, k_ref[:, cs]).astype(BF16)
            dk_ref[:, cs] += _dot_tn(ds, q[:, cs])
            dv_ref[:, cs] += _dot_tn(p, do[:, cs])
        dx1_ref[...] = ALPHA * dr + _dot(dq_ref[...], wqt_ref[...])

    tile = pl.BlockSpec((ts, d), lambda t: (t, 0))
    consts = (wqt, wot, kk, kt, vt, vec)
    return pl.pallas_call(
        body, name=name, grid=(s // ts,),
        in_specs=[tile, tile, tile] + [_const_spec(c.shape) for c in consts],
        out_specs=[tile, tile, tile, _acc_spec((nm, d)), _acc_spec((nm, d)), _acc_spec((2, d))],
        out_shape=[jax.ShapeDtypeStruct((s, d), F32), jax.ShapeDtypeStruct((s, d), BF16),
                   jax.ShapeDtypeStruct((s, d), BF16), jax.ShapeDtypeStruct((nm, d), F32),
                   jax.ShapeDtypeStruct((nm, d), F32), jax.ShapeDtypeStruct((2, d), F32)],
        compiler_params=_cparams(("arbitrary",)),
    )(dx2, rpre, q, *consts)


def _ffn_out(x2, hgu, wd, vec, name, target=None):
    s, d = x2.shape
    ff = wd.shape[0]
    ts = _tile(s, TS_FFN)
    chunk = 2 * 128 if ff % (2 * 128) == 0 else ff

    def body(*refs):
        if target is None:
            x_ref, hgu_ref, wd_ref, vec_ref, out_ref, rpre_ref, act_ref = refs
        else:
            x_ref, hgu_ref, wd_ref, vec_ref, t_ref, out_ref, rpre_ref, act_ref, loss_ref = refs
        for c in range(0, ff, chunk):
            hg = hgu_ref[:, c:c + chunk]
            act_ref[:, c:c + chunk] = (hg * _sigmoid(hg) * hgu_ref[:, ff + c:ff + c + chunk]).astype(BF16)
        rpre = ALPHA * x_ref[...] + _dot(act_ref[...], wd_ref[...])
        rpre_ref[...] = rpre
        x3 = _ln_fwd(rpre, vec_ref[V_G + 2:V_G + 3, :], vec_ref[V_B + 2:V_B + 3, :])
        if target is None:
            out_ref[...] = x3
            return

        @pl.when(pl.program_id(0) == 0)
        def _():
            loss_ref[...] = jnp.zeros_like(loss_ref)

        err = x3 - t_ref[...]
        out_ref[...] = err / d
        per_token = jnp.mean(err * err, axis=-1, keepdims=True)
        loss_ref[...] += 0.5 * jnp.sum(per_token, axis=0, keepdims=True)

    tile = pl.BlockSpec((ts, d), lambda t: (t, 0))
    f32o = jax.ShapeDtypeStruct((s, d), F32)
    in_specs = [tile, pl.BlockSpec((ts, 2 * ff), lambda t: (t, 0)), _const_spec(wd.shape), _const_spec(vec.shape)]
    out_specs = [tile, tile, pl.BlockSpec((ts, ff), lambda t: (t, 0))]
    out_shape = [f32o, f32o, jax.ShapeDtypeStruct((s, ff), BF16)]
    args = [x2, hgu, wd, vec]
    if target is not None:
        in_specs.append(tile)
        args.append(target)
        out_specs.append(_acc_spec((1, 1)))
        out_shape.append(jax.ShapeDtypeStruct((1, 1), F32))
    return pl.pallas_call(
        body, name=name, grid=(s // ts,), in_specs=in_specs, out_specs=out_specs, out_shape=out_shape,
        compiler_params=_cparams(("parallel",) if target is None else ("arbitrary",)),
    )(*args)


def _ffn_bwd(dy, rpre, hgu, wdt, wgt, wut, vec, name, comm=()):
    s, d = dy.shape
    ff = wgt.shape[0]
    ts = _tile(s, TS_FFN)
    chunk = 2 * 128 if ff % (2 * 128) == 0 else ff

    def body(dy_ref, rpre_ref, hgu_ref, wdt_ref, wgt_ref, wut_ref, vec_ref, dx_ref, dr_ref, dhgu_ref, ln_ref):
        @pl.when(pl.program_id(0) == 0)
        def _():
            ln_ref[...] = jnp.zeros_like(ln_ref)

        dyv = dy_ref[...]
        dr, dyy = _ln_bwd(dyv, rpre_ref[...], vec_ref[V_G + 2:V_G + 3, :])
        ln_ref[0:1, :] += _colsum(dyy)
        ln_ref[1:2, :] += _colsum(dyv)
        dr_ref[...] = dr.astype(BF16)
        dact = _dot(dr, wdt_ref[...])
        for c in range(0, ff, chunk):
            hg = hgu_ref[:, c:c + chunk]
            hu = hgu_ref[:, ff + c:ff + c + chunk]
            da = dact[:, c:c + chunk]
            sg = _sigmoid(hg)
            dhgu_ref[:, c:c + chunk] = (da * hu * (sg * (1.0 + hg * (1.0 - sg)))).astype(BF16)
            dhgu_ref[:, ff + c:ff + c + chunk] = (da * hg * sg).astype(BF16)
        dx_ref[...] = (ALPHA * dr + _dot(dhgu_ref[:, 0:ff], wgt_ref[...])
                       + _dot(dhgu_ref[:, ff:2 * ff], wut_ref[...]))

    tile = pl.BlockSpec((ts, d), lambda t: (t, 0))
    wide = pl.BlockSpec((ts, 2 * ff), lambda t: (t, 0))
    consts = (wdt, wgt, wut, vec)
    return _pallas(
        body, name=name, grid=(s // ts,),
        in_specs=[tile, tile, wide] + [_const_spec(c.shape) for c in consts],
        out_specs=[tile, tile, wide, _acc_spec((2, d))],
        out_shape=[jax.ShapeDtypeStruct((s, d), F32), jax.ShapeDtypeStruct((s, d), BF16),
                   jax.ShapeDtypeStruct((s, 2 * ff), BF16), jax.ShapeDtypeStruct((2, d), F32)],
        semantics=("arbitrary",), args=(dy, rpre, hgu, *consts), comm=comm)


SHARD_AXIS = {"w_in": 1, "pool_w": 1, "lru_w_out": 0, "sconv_w_out": 0, "w_mix_out": 0,
              "xa_w_q": 0, "xa_w_k": 0, "xa_w_v": 0, "xa_w_o": 0,
              "ffn_w_gate": 0, "ffn_w_up": 0, "ffn_w_down": 0,
              "lru_conv_w": 1, "sconv_w": 1, "ln_g": 1, "ln_b": 1}
STORED_TRANSPOSED = ("ffn_w_gate", "ffn_w_up")
GROUP_IN = ("w_in",)
GROUP_MIXER = ("pool_w", "lru_w_out", "sconv_w_out", "w_mix_out")
GROUP_ATTN = ("xa_w_q", "xa_w_k", "xa_w_v", "xa_w_o")
GROUP_FFN = ("ffn_w_gate", "ffn_w_up", "ffn_w_down")
GROUP_VECTORS = ("lru_conv_w", "sconv_w", "ln_g", "ln_b")
REPLICATED = ("b_in", "pool_scale", "lru_conv_b", "lru_w_r", "lru_b_r", "lru_w_i", "lru_b_i", "lru_lambda")
WEIGHTS = ("w_in", "b_in", "pool_w", "pool_scale", "lru_conv_w", "lru_conv_b", "lru_w_r", "lru_b_r", "lru_w_i",
           "lru_b_i", "lru_lambda", "lru_w_out", "sconv_w", "sconv_w_out", "w_mix_out", "xa_w_q", "xa_w_k",
           "xa_w_v", "xa_w_o", "ffn_w_gate", "ffn_w_up", "ffn_w_down", "ln_g", "ln_b")


def _pack(arrs, width, lead=0, row_multiple=ROW_PAD):
    head = arrs[0].shape[:lead]
    flat = jnp.concatenate([a.reshape(head + (-1,)) for a in arrs], axis=lead)
    n = flat.shape[-1]
    chunk = width * row_multiple
    total = -(-n // chunk) * chunk
    if total != n:
        flat = jnp.pad(flat, [(0, 0)] * lead + [(0, total - n)])
    return flat.reshape(head + (total // width, width))


def _unpack(buf, shapes, lead=0):
    head = buf.shape[:lead]
    flat = buf.reshape(head + (-1,))
    out, off = [], 0
    for shp in shapes:
        n = math.prod(shp)
        out.append(flat[..., off:off + n].reshape(head + tuple(shp)))
        off += n
    return out


def _split8(a, axis):
    shp = a.shape
    a = a.reshape(shp[:axis] + (N_DEV, shp[axis] // N_DEV) + shp[axis + 1:])
    return jnp.moveaxis(a, axis, 0)


def _join8(a, axis):
    a = jnp.moveaxis(a, 0, axis)
    shp = a.shape
    return a.reshape(shp[:axis] + (shp[axis] * shp[axis + 1],) + shp[axis + 2:])


def _t(a):
    return jnp.swapaxes(a, -1, -2)


def _stored(name, a):
    return _t(a) if name in STORED_TRANSPOSED else a


def kernel(x, mem, w_in, b_in, pool_w, pool_scale, lru_conv_w, lru_conv_b, lru_w_r, lru_b_r, lru_w_i, lru_b_i, lru_lambda, lru_w_out, sconv_w, sconv_w_out, w_mix_out, xa_w_q, xa_w_k, xa_w_v, xa_w_o, ffn_w_gate, ffn_w_up, ffn_w_down, ln_g, ln_b, loss_target, m_w_in, m_b_in, m_pool_w, m_pool_scale, m_lru_conv_w, m_lru_conv_b, m_lru_w_r, m_lru_b_r, m_lru_w_i, m_lru_b_i, m_lru_lambda, m_lru_w_out, m_sconv_w, m_sconv_w_out, m_w_mix_out, m_xa_w_q, m_xa_w_k, m_xa_w_v, m_xa_w_o, m_ffn_w_gate, m_ffn_w_up, m_ffn_w_down, m_ln_g, m_ln_b, v_w_in, v_b_in, v_pool_w, v_pool_scale, v_lru_conv_w, v_lru_conv_b, v_lru_w_r, v_lru_b_r, v_lru_w_i, v_lru_b_i, v_lru_lambda, v_lru_w_out, v_sconv_w, v_sconv_w_out, v_w_mix_out, v_xa_w_q, v_xa_w_k, v_xa_w_v, v_xa_w_o, v_ffn_w_gate, v_ffn_w_up, v_ffn_w_down, v_ln_g, v_ln_b):
    args = dict(locals())
    w = {n: args[n] for n in WEIGHTS}
    mom_m = {n: args["m_" + n] for n in WEIGHTS}
    mom_v = {n: args["v_" + n] for n in WEIGHTS}
    depth = w_in.shape[0]
    s, d = x.shape[1], x.shape[2]
    nm = mem.shape[1]
    ff = ffn_w_gate.shape[2] * N_DEV
    xs = x.reshape(s, d)
    mems = mem.reshape(nm, d)
    target = loss_target.reshape(s, d)

    def shard(t, n, l):
        return _stored(n, t[n][l])

    def pack_shards(t, names, l, dtype=None):
        arrs = [shard(t, n, l) for n in names]
        return _pack([a if dtype is None else a.astype(dtype) for a in arrs], d)

    def unpack_gathered(buf, names):
        pieces = _unpack(buf, [shard(w, n, 0).shape for n in names], lead=1)
        return {n: (p if n == "w_in" else _join8(p, SHARD_AXIS[n])) for n, p in zip(names, pieces)}

    def layer_vec(l, fw):
        vec = jnp.zeros((V_ROWS, d), F32)
        vec = vec.at[V_PSCALE].set(pool_scale[l]).at[V_CW:V_CW + LRU_CONV].set(fw["lru_conv_w"])
        vec = vec.at[V_CB].set(lru_conv_b[l]).at[V_BR].set(lru_b_r[l]).at[V_BI].set(lru_b_i[l])
        vec = vec.at[V_LAM].set(lru_lambda[l]).at[V_SW:V_SW + SCONV_K].set(fw["sconv_w"])
        return vec.at[V_G:V_G + 3].set(fw["ln_g"]).at[V_B:V_B + 3].set(fw["ln_b"])

    def layer_params(l, fw):
        return dict(
            vec=layer_vec(l, fw), wint=_t(fw["w_in"]).reshape(1, 8 * d, d),
            pw=fw["pool_w"], pwt=_t(fw["pool_w"]),
            wr=lru_w_r[l].astype(BF16), wi=lru_w_i[l].astype(BF16),
            wrt=_t(lru_w_r[l]).astype(BF16), wit=_t(lru_w_i[l]).astype(BF16),
            wlo=fw["lru_w_out"], wlot=_t(fw["lru_w_out"]),
            wsc=fw["sconv_w_out"], wsct=_t(fw["sconv_w_out"]),
            wmix=fw["w_mix_out"], wmixt=_t(fw["w_mix_out"]),
            wq=fw["xa_w_q"], wqt=_t(fw["xa_w_q"]), wo=fw["xa_w_o"], wot=_t(fw["xa_w_o"]),
            wkv=jnp.stack([fw["xa_w_k"], fw["xa_w_v"]]),
            wgu=jnp.stack([_t(fw["ffn_w_gate"]), _t(fw["ffn_w_up"])]),
            wgt=fw["ffn_w_gate"], wut=fw["ffn_w_up"],
            wd=fw["ffn_w_down"], wdt=_t(fw["ffn_w_down"]))

    later = GROUP_ATTN + GROUP_FFN
    fw0 = unpack_gathered(_all_gather(pack_shards(w, GROUP_IN + GROUP_MIXER, 0, BF16), "gather_mixer_0"),
                          GROUP_IN + GROUP_MIXER)
    vectors = _all_gather(_pack([shard(w, n, l) for l in range(depth) for n in GROUP_VECTORS], d), "gather_vectors")
    vec_pieces = _unpack(vectors, [shard(w, n, l).shape for l in range(depth) for n in GROUP_VECTORS], lead=1)
    fvec = [{n: _join8(vec_pieces[l * len(GROUP_VECTORS) + k], SHARD_AXIS[n]) for k, n in enumerate(GROUP_VECTORS)}
            for l in range(depth)]

    layers, saved = [], []
    cur = xs
    fw_next = None
    for l in range(depth):
        fw = dict(fw0 if l == 0 else fw_next)
        fw.update(fvec[l])
        comm = []
        if l == 0:
            comm.append(("gather", pack_shards(w, later, 0, BF16)))
        if l + 1 < depth:
            comm.append(("gather", pack_shards(w, GROUP_IN + GROUP_MIXER + later, l + 1, BF16)))
        z, x1, rpre1, h, ypre, yl, yc, merged, e, pp, vb, rb, ib, cq, *got = _mixer_fwd(
            cur, fw["w_in"], b_in[l].reshape(1, 8 * d), fw["pool_w"], lru_w_r[l].astype(BF16),
            lru_w_i[l].astype(BF16), fw["lru_w_out"], fw["sconv_w_out"], fw["w_mix_out"],
            layer_vec(l, fw), f"mixer_fwd_{l}", comm=comm)
        if l == 0:
            fw.update(unpack_gathered(got.pop(0), later))
        if l + 1 < depth:
            fw_next = unpack_gathered(got.pop(0), GROUP_IN + GROUP_MIXER + later)
        p = layer_params(l, fw)
        kv = _mm(mems, p["wkv"], f"kv_{l}")[0]
        kk = kv[:, :d].astype(BF16)
        vv = kv[:, d:].astype(BF16)
        x2, rpre2, q, o = _attn_fwd(x1, p["wq"], p["wo"], _t(kk), vv, p["vec"], f"attn_fwd_{l}")
        hgu = _mm(x2, p["wgu"], f"ffn_in_{l}", tm=TS_MM // 2)[0]
        if l + 1 < depth:
            cur_next, rpre3, act = _ffn_out(x2, hgu, p["wd"], p["vec"], f"ffn_out_{l}")
        else:
            dcur, rpre3, act, loss_part = _ffn_out(x2, hgu, p["wd"], p["vec"], f"ffn_out_{l}", target=target)
            cur_next = None
        layers.append(p)
        saved.append(dict(x0=cur, z=z, x1=x1, rpre1=rpre1, h=h, ypre=ypre, yl=yl, yc=yc, merged=merged, e=e,
                          pp=pp, vb=vb, rb=rb, ib=ib, cq=cq,
                          kk=kk, vv=vv, x2=x2, rpre2=rpre2, q=q, o=o, hgu=hgu, rpre3=rpre3, act=act))
        cur = cur_next

    loss = lax.psum(loss_part[0, 0], ("x", "y", "c"))

    res = {}

    def slots_of(g, names):
        if tuple(names) == GROUP_VECTORS:
            return [_pack([_split8(g[n], SHARD_AXIS[n]) for n in names], d, lead=1)]
        return [(g[n] if n == "w_in" else _split8(g[n], SHARD_AXIS[n])).reshape(N_DEV, -1, d).astype(BF16)
                for n in names]

    def update(received, names, l, tag):
        outs = _adamw_sum(received, *[pack_shards(t, names, l) for t in (w, mom_m, mom_v)], f"adamw_{tag}_{l}")
        shapes = [shard(w, n, l).shape for n in names]
        for n, *parts in zip(names, *[_unpack(o, shapes) for o in outs]):
            res[(n, l)] = [_stored(n, a) for a in parts]

    def settle(exchanges, got):
        for (names, l, tag, _), received in zip(exchanges, got):
            update(received, names, l, tag)

    grads = [None] * depth
    for l in reversed(range(depth)):
        p, sv = layers[l], saved[l]
        g = {}
        dx2, dr3, dhgu, ln3 = _ffn_bwd(dcur, sv["rpre3"], sv["hgu"], p["wdt"], p["wgt"], p["wut"], p["vec"],
                                       f"ffn_bwd_{l}")
        g["ffn_w_down"] = _mm_tn(sv["act"], dr3, d, f"g_wd_{l}")[0][0]
        dwgu = _mm_tn(dhgu, sv["x2"], d, f"g_wgu_{l}", tk=ff)[0][0]
        g["ffn_w_gate"], g["ffn_w_up"] = dwgu[:ff], dwgu[ff:]
        dx1, dq, dr2, dk, dv, ln2 = _attn_bwd(dx2, sv["rpre2"], sv["q"], p["wqt"], p["wot"], sv["kk"], _t(sv["kk"]),
                                              _t(sv["vv"]), p["vec"], f"attn_bwd_{l}")
        g["xa_w_o"] = _mm_tn(sv["o"], dr2, d, f"g_wo_{l}")[0][0]
        g["xa_w_q"] = _mm_tn(sv["x1"], dq, d, f"g_wq_{l}")[0][0]
        dwkv = _mm_tn(mems, jnp.concatenate([dk, dv], axis=1), d, f"g_wkv_{l}")[0]
        g["xa_w_k"], g["xa_w_v"] = dwkv[0], dwkv[1]
        ffn_slots = slots_of(g, GROUP_FFN)
        (dz, dr1, dyl, dyc, accs, dbin, g["pool_w"], g["lru_w_r"], g["lru_w_i"], received) = _mixer_bwd(
            dx1, sv["rpre1"], sv["z"], sv["h"], sv["ypre"], sv["yl"], sv["yc"], sv["pp"], sv["vb"], sv["rb"],
            sv["ib"], sv["cq"], p["pwt"], p["wrt"], p["wit"], p["wlot"], p["wsct"], p["wmixt"], p["vec"],
            f"mixer_bwd_{l}",
            comm=[("scatter", ffn_slots)])
        update(received, GROUP_FFN, l, "ffn")
        g["w_mix_out"] = _mm_tn(sv["merged"], dr1, d, f"g_wmix_{l}")[0][0]
        g["lru_w_out"] = _mm_tn(sv["h"], dyl, d, f"g_wlo_{l}")[0][0]
        g["sconv_w_out"] = _mm_tn(sv["e"], dyc, d, f"g_wsc_{l}")[0][0]
        g["b_in"] = dbin[0]
        g["pool_scale"] = accs[A_PSCALE]
        g["lru_conv_w"] = accs[A_CW:A_CW + LRU_CONV]
        g["lru_conv_b"] = accs[A_CB]
        g["lru_b_r"] = accs[A_BR]
        g["lru_b_i"] = accs[A_BI]
        g["lru_lambda"] = accs[A_SP] * (-_sigmoid(-lru_lambda[l]))
        g["sconv_w"] = accs[A_SW:A_SW + SCONV_K]
        g["ln_g"] = jnp.stack([accs[A_G], ln2[0], ln3[0]])
        g["ln_b"] = jnp.stack([accs[A_B], ln2[1], ln3[1]])
        grads[l] = g
        behind_win = [(names, l, tag, slots_of(g, names)) for names, tag in
                      ((GROUP_ATTN, "attn"), (GROUP_MIXER, "mixer"), (GROUP_VECTORS, "vectors"))]
        g["w_in"], *got = _mm_tn(sv["x0"], dz, d, f"g_win_{l}", comm=[("scatter", t[3]) for t in behind_win])
        settle(behind_win, got)
        behind_dx = [(GROUP_IN, l, "w_in", slots_of(g, GROUP_IN))]
        comm = [("scatter", behind_dx[0][3])]
        if l == 0:
            comm.append(("gather", _pack([jnp.stack([grads[k][n] for k in range(depth)]) for n in REPLICATED], d)))
        dcur, *got = _mm(dz, p["wint"], f"dx_{l}", add=dr1, add_scale=ALPHA, comm=comm)
        settle(behind_dx, got)
        if l == 0:
            outs = _adamw_sum(got[1], *[_pack([t[n] for n in REPLICATED], d) for t in (w, mom_m, mom_v)],
                              "adamw_replicated")
            rep_shapes = [w[n].shape for n in REPLICATED]
            final = {n: parts for n, *parts in zip(REPLICATED, *[_unpack(o, rep_shapes) for o in outs])}
    grad_x = dcur.reshape(x.shape)

    for n in WEIGHTS:
        if n not in final:
            final[n] = [jnp.stack([res[(n, l)][k] for l in range(depth)]) for k in range(4)]
    return (loss, grad_x, *[final[n][0] for n in WEIGHTS], *[final[n][1] for n in WEIGHTS],
            *[final[n][2] for n in WEIGHTS], *[final[n][3] for n in WEIGHTS])
```

```python
import functools
import math

import jax
import jax.numpy as jnp
from jax import lax
from jax.experimental import pallas as pl
from jax.experimental.pallas import tpu as pltpu

F32 = jnp.float32
BF16 = jnp.bfloat16
MESH = pl.DeviceIdType.MESH

N_DEV = 8
LRU_HEADS = 8
LRU_CONV = 4
LRU_C = 8.0
SCONV_K = 3
POOL_WINDOWS = (2, 4, 8, 16)
X_HEADS = 4
DEPTH = 2
ALPHA = (2 * DEPTH) ** 0.25
LN_EPS = 1e-5
ADAM_LR = 0.001
ADAM_B1 = 0.9
ADAM_B2 = 0.999
ADAM_EPS = 1e-08
ADAM_WD = 0.01
ADAM_STEP = 10

HALO = 16
SUBLANES = 8
VMEM_LIMIT = 56 * 1024 * 1024
TS_MIXER = 128
TS_ATTN = 512
TS_FFN = 256
TS_MM = 1024
TK_MM = 2048
TR_ADAM = 256
ROW_PAD = 8
TS_MM_TN = 1024

V_PSCALE, V_CW, V_CB, V_BR, V_BI, V_LAM, V_SW, V_G, V_B = 0, 1, 5, 6, 7, 8, 9, 12, 15
V_ROWS = 24
A_PSCALE, A_CW, A_CB, A_BR, A_BI, A_SP, A_SW, A_G, A_B = 0, 1, 5, 6, 7, 8, 9, 12, 13
A_ROWS = 16


def _cparams(sem):
    return pltpu.CompilerParams(dimension_semantics=sem, vmem_limit_bytes=VMEM_LIMIT)


def _tile(n, pref):
    if n <= pref:
        return n
    assert n % pref == 0, (n, pref)
    return pref


def _const_spec(shape):
    nd = len(shape)
    return pl.BlockSpec(shape, lambda *_: (0,) * nd, pipeline_mode=pl.Buffered(1))


def _acc_spec(shape):
    nd = len(shape)
    return pl.BlockSpec(shape, lambda *_: (0,) * nd)


def _dot(a, b):
    return jnp.dot(a.astype(BF16), b.astype(BF16), preferred_element_type=F32)


def _dot_tn(a, b):
    return lax.dot_general(a.astype(BF16), b.astype(BF16), (((0,), (0,)), ((), ())),
                           preferred_element_type=F32)


def _sigmoid(x):
    return 0.5 * jnp.tanh(0.5 * x) + 0.5


def _softplus(y):
    e = jnp.exp(-jnp.abs(y))
    log1p = jnp.where(e < 1e-4, e * (1.0 - e * (0.5 - e * (1.0 / 3.0))), jnp.log(1.0 + e))
    return jnp.maximum(y, 0.0) + log1p


def _ln_fwd(r, g, b):
    mu = jnp.mean(r, axis=-1, keepdims=True)
    xc = r - mu
    var = jnp.mean(xc * xc, axis=-1, keepdims=True)
    return xc * lax.rsqrt(var + LN_EPS) * g + b


def _ln_bwd(dy, r, g):
    mu = jnp.mean(r, axis=-1, keepdims=True)
    xc = r - mu
    var = jnp.mean(xc * xc, axis=-1, keepdims=True)
    rstd = lax.rsqrt(var + LN_EPS)
    yhat = xc * rstd
    dyh = dy * g
    m1 = jnp.mean(dyh, axis=-1, keepdims=True)
    m2 = jnp.mean(dyh * yhat, axis=-1, keepdims=True)
    return rstd * (dyh - m1 - yhat * m2), dy * yhat


def _colsum(a):
    return jnp.sum(a, axis=0, keepdims=True)


def _position():
    return lax.axis_index("x"), lax.axis_index("y"), lax.axis_index("c")


def _gather_copies(x_ref, out_ref, send_sems, recv_sems, local_sem):
    x, y, c = _position()
    me, sibling = (x, y, c), (x, y, 1 - c)
    chips = [(1 - x, y), (x, 1 - y), (1 - x, 1 - y)]

    def slot(px, py, pc):
        return out_ref.at[4 * px + 2 * py + pc]

    def copy(k, block, to, src=None):
        return pltpu.make_async_remote_copy(
            src_ref=slot(*block) if src is None else src, dst_ref=slot(*block),
            send_sem=send_sems.at[k], recv_sem=recv_sems.at[k], device_id=to, device_id_type=MESH)

    mine = pltpu.make_async_copy(x_ref, slot(*me), local_sem)
    first = [copy(0, me, sibling, src=x_ref)]
    first += [copy(1 + j, me, (*chip, c), src=x_ref) for j, chip in enumerate(chips)]
    passed = [copy(4 + j, (*chip, c), sibling) for j, chip in enumerate(chips)]
    over_ici = [copy(1 + j, (*chip, c), me) for j, chip in enumerate(chips)]
    from_sibling = copy(0, sibling, me)
    forwarded = [copy(4 + j, (*chip, 1 - c), me) for j, chip in enumerate(chips)]
    return mine, first, passed, over_ici, from_sibling, forwarded


def _scatter_copies(g_refs, out_ref, send_sems, recv_sems, local_sem):
    x, y, c = _position()
    me = 4 * x + 2 * y + c
    offsets, rows = [], 0
    for g_ref in g_refs:
        offsets.append(rows)
        rows += g_ref.shape[1]

    def landing(g_ref, off):
        return out_ref.at[me, pl.ds(off, g_ref.shape[1])]

    mine = [pltpu.make_async_copy(g_ref.at[me], landing(g_ref, off), local_sem)
            for g_ref, off in zip(g_refs, offsets)]
    mine_all = pltpu.make_async_copy(out_ref.at[me], out_ref.at[me], local_sem)
    copies, waits = [], []
    for k in range(1, N_DEV):
        px = 1 - x if k & 4 else x
        py = 1 - y if k & 2 else y
        pc = 1 - c if k & 1 else c
        sems = dict(send_sem=send_sems.at[k - 1], recv_sem=recv_sems.at[k - 1],
                    device_id=(px, py, pc), device_id_type=MESH)
        copies += [pltpu.make_async_remote_copy(src_ref=g_ref.at[4 * px + 2 * py + pc], dst_ref=landing(g_ref, off),
                                                **sems) for g_ref, off in zip(g_refs, offsets)]
        waits.append(pltpu.make_async_remote_copy(src_ref=out_ref.at[me], dst_ref=out_ref.at[me], **sems))
    return mine, copies, mine_all, waits


def _comm_start(kind, srcs, *refs):
    if kind == "gather":
        mine, first, _, _, _, _ = _gather_copies(srcs[0], *refs)
        mine.start()
        for cp in first:
            cp.start()
    else:
        mine, copies, _, _ = _scatter_copies(srcs, *refs)
        for cp in mine + copies:
            cp.start()


def _gather_pass_on(srcs, *refs):
    _, _, passed, over_ici, _, _ = _gather_copies(srcs[0], *refs)
    for arrival, forward in zip(over_ici, passed):
        arrival.wait_recv()
        forward.start()


def _comm_finish(kind, srcs, *refs, passed_on=False):
    if kind == "gather":
        if not passed_on:
            _gather_pass_on(srcs, *refs)
        mine, first, passed, _, from_sibling, forwarded = _gather_copies(srcs[0], *refs)
        from_sibling.wait_recv()
        for arrival in forwarded:
            arrival.wait_recv()
        for cp in first + passed:
            cp.wait_send()
        mine.wait()
    else:
        _, _, mine_all, waits = _scatter_copies(srcs, *refs)
        for cp in waits:
            cp.wait_recv()
        for cp in waits:
            cp.wait_send()
        mine_all.wait()


def _comm_sources(kind, payload):
    return [payload] if kind == "gather" else list(payload)


def _comm_out_shape(kind, payload):
    srcs = _comm_sources(kind, payload)
    if kind == "gather":
        return jax.ShapeDtypeStruct((N_DEV,) + srcs[0].shape, srcs[0].dtype)
    return jax.ShapeDtypeStruct((N_DEV, sum(a.shape[1] for a in srcs), srcs[0].shape[2]), srcs[0].dtype)


COMM_SEMAPHORES = [pltpu.SemaphoreType.DMA((7,)), pltpu.SemaphoreType.DMA((7,)), pltpu.SemaphoreType.DMA]


def _pallas(body, *, name, grid, in_specs, out_specs, out_shape, semantics, args, scratch_shapes=(), comm=()):
    in_specs, out_specs, out_shape = list(in_specs), list(out_specs), list(out_shape)
    scratch_shapes = list(scratch_shapes)
    n_in, n_out, n_scr, nc = len(in_specs), len(out_specs), len(scratch_shapes), len(comm)
    if not comm:
        return pl.pallas_call(body, name=name, grid=grid, in_specs=in_specs, out_specs=out_specs, out_shape=out_shape,
                              scratch_shapes=scratch_shapes, compiler_params=_cparams(semantics))(*args)
    kinds = [kind for kind, _ in comm]
    sources = [_comm_sources(kind, payload) for kind, payload in comm]
    n_src = sum(len(srcs) for srcs in sources)

    def carrying(*refs):
        ins, rest = refs[:n_in], refs[n_in:]
        cin, rest = list(rest[:n_src]), rest[n_src:]
        outs, rest = rest[:n_out], rest[n_out:]
        cout, rest = rest[:nc], rest[nc:]
        scr, sems = rest[:n_scr], rest[n_scr:]
        ids = [pl.program_id(ax) for ax in range(len(grid))]
        first = functools.reduce(jnp.logical_and, [i == 0 for i in ids])
        last = functools.reduce(jnp.logical_and, [i == g - 1 for i, g in zip(ids, grid)])
        plans = []
        for k in range(nc):
            mine, cin = cin[:len(sources[k])], cin[len(sources[k]):]
            plans.append((kinds[k], mine, cout[k], *sems[3 * k:3 * k + 3]))

        @pl.when(first)
        def _():
            for plan in plans:
                _comm_start(*plan)

        halfway = len(grid) == 1 and grid[0] >= 4 and "gather" in kinds
        if halfway:
            @pl.when(ids[0] == grid[0] // 2)
            def _():
                for plan in plans:
                    if plan[0] == "gather":
                        _gather_pass_on(*plan[1:])

        body(*ins, *outs, *scr)

        @pl.when(last)
        def _():
            for plan in plans:
                _comm_finish(*plan, passed_on=halfway)

    hbm = pl.BlockSpec(memory_space=pl.ANY)
    return pl.pallas_call(
        carrying, name=name, grid=grid,
        in_specs=in_specs + [hbm] * n_src, out_specs=out_specs + [hbm] * nc,
        out_shape=out_shape + [_comm_out_shape(kind, payload) for kind, payload in comm],
        scratch_shapes=scratch_shapes + COMM_SEMAPHORES * nc,
        compiler_params=_cparams(("arbitrary",) * len(grid)),
    )(*args, *[a for srcs in sources for a in srcs])


def _all_gather(arrays, name):
    n = len(arrays)

    def body(*refs):
        plans = [([refs[k]], refs[n + k], *refs[2 * n + 3 * k:2 * n + 3 * k + 3]) for k in range(n)]
        for plan in plans:
            _comm_start("gather", *plan)
        for plan in plans:
            _comm_finish("gather", *plan)

    hbm = pl.BlockSpec(memory_space=pl.ANY)
    return pl.pallas_call(
        body, name=name, out_shape=[_comm_out_shape("gather", a) for a in arrays],
        in_specs=[hbm] * n, out_specs=[hbm] * n, scratch_shapes=COMM_SEMAPHORES * n,
    )(*arrays)


def _adamw_sum(parts, w, m, v, name):
    _, rows, width = parts.shape
    tr = max(t for t in range(SUBLANES, min(rows, TR_ADAM) + 1, SUBLANES) if rows % t == 0)
    c1 = 1.0 - ADAM_B1 ** ADAM_STEP
    c2 = 1.0 - ADAM_B2 ** ADAM_STEP

    def body(p_ref, w_ref, m_ref, v_ref, g_ref, d_ref, nm_ref, nv_ref):
        g = p_ref[0].astype(F32)
        for k in range(1, N_DEV):
            g = g + p_ref[k].astype(F32)
        nm = ADAM_B1 * m_ref[...] + (1.0 - ADAM_B1) * g
        nv = ADAM_B2 * v_ref[...] + (1.0 - ADAM_B2) * (g * g)
        m_hat = nm / c1
        v_hat = nv / c2
        g_ref[...] = g
        d_ref[...] = -ADAM_LR * (m_hat / (jnp.sqrt(v_hat) + ADAM_EPS) + ADAM_WD * w_ref[...])
        nm_ref[...] = nm
        nv_ref[...] = nv

    spec = pl.BlockSpec((tr, width), lambda i: (i, 0))
    out = jax.ShapeDtypeStruct((rows, width), F32)
    return pl.pallas_call(
        body, name=name, grid=(rows // tr,),
        in_specs=[pl.BlockSpec((N_DEV, tr, width), lambda i: (0, i, 0)), spec, spec, spec],
        out_specs=[spec, spec, spec, spec], out_shape=[out, out, out, out],
        compiler_params=_cparams(("parallel",)),
    )(parts, w, m, v)


def _mm(a, wb, name, bias=None, add=None, add_scale=1.0, out_dtype=F32, tm=None, comm=()):
    m, k = a.shape
    nb, k2, tn = wb.shape
    assert k == k2
    tm = _tile(m, TS_MM if tm is None else tm)
    tk = _tile(k, TK_MM)
    nk = k // tk

    def body(*refs):
        a_ref, w_ref = refs[0], refs[1]
        pos = 2
        b_ref = add_ref = None
        if bias is not None:
            b_ref = refs[pos]
            pos += 1
        if add is not None:
            add_ref = refs[pos]
            pos += 1
        o_ref = refs[pos]

        def finish(r):
            if b_ref is not None:
                r = r + b_ref[...]
            if add_ref is not None:
                r = r + add_scale * add_ref[...]
            o_ref[...] = r.astype(o_ref.dtype)

        if nk == 1:
            finish(_dot(a_ref[...], w_ref[...]))
            return
        acc_ref = refs[pos + 1]
        kk = pl.program_id(2)

        @pl.when(kk == 0)
        def _():
            acc_ref[...] = jnp.zeros_like(acc_ref)

        acc_ref[...] += _dot(a_ref[...], w_ref[...])

        @pl.when(kk == nk - 1)
        def _():
            finish(acc_ref[...])

    in_specs = [pl.BlockSpec((tm, tk), lambda j, i, kk: (i, kk)),
                pl.BlockSpec((None, tk, tn), lambda j, i, kk: (j, kk, 0))]
    args = [a, wb]
    if bias is not None:
        in_specs.append(pl.BlockSpec((1, tn), lambda j, i, kk: (0, j)))
        args.append(bias)
    if add is not None:
        in_specs.append(pl.BlockSpec((tm, tn), lambda j, i, kk: (i, j)))
        args.append(add)
    return _pallas(
        body, name=name, grid=(nb, m // tm, nk),
        in_specs=in_specs,
        out_specs=[pl.BlockSpec((tm, tn), lambda j, i, kk: (i, j))],
        out_shape=[jax.ShapeDtypeStruct((m, nb * tn), out_dtype)],
        scratch_shapes=[pltpu.VMEM((tm, tn), F32)] if nk > 1 else [],
        semantics=("parallel", "parallel", "arbitrary"), args=args, comm=comm)


def _mm_tn(a, b, tn, name, tk=None, comm=()):
    s, k = a.shape
    s2, n = b.shape
    assert s == s2 and n % tn == 0
    nb = n // tn
    ts = _tile(s, TS_MM_TN)
    tk = k if tk is None else tk
    assert k % tk == 0

    def body(a_ref, b_ref, o_ref):
        @pl.when(pl.program_id(2) == 0)
        def _():
            o_ref[...] = jnp.zeros_like(o_ref)

        o_ref[...] += _dot_tn(a_ref[...], b_ref[...])

    return _pallas(
        body, name=name, grid=(nb, k // tk, s // ts),
        in_specs=[pl.BlockSpec((ts, tk), lambda j, kb, i: (i, kb)),
                  pl.BlockSpec((ts, tn), lambda j, kb, i: (i, j))],
        out_specs=[pl.BlockSpec((None, tk, tn), lambda j, kb, i: (j, kb, 0))],
        out_shape=[jax.ShapeDtypeStruct((nb, k, tn), F32)],
        semantics=("parallel", "parallel", "arbitrary"), args=(a, b), comm=comm)


def _scan_fwd(a_ref, b_ref, h_ref, carry_ref, ts):
    rowid = lax.broadcasted_iota(jnp.int32, (SUBLANES, 1), 0)

    def group(gi, hprev):
        r0 = pl.multiple_of(gi * SUBLANES, SUBLANES)
        a = a_ref[pl.ds(r0, SUBLANES), :]
        b = b_ref[pl.ds(r0, SUBLANES), :]
        for d in (1, 2, 4):
            a_sh = jnp.where(rowid >= d, pltpu.roll(a, d, 0), 1.0)
            b_sh = jnp.where(rowid >= d, pltpu.roll(b, d, 0), 0.0)
            b = a * b_sh + b
            a = a * a_sh
        hh = a * hprev + b
        h_ref[pl.ds(r0, SUBLANES), :] = hh
        return hh[SUBLANES - 1:SUBLANES, :]

    last = lax.fori_loop(0, ts // SUBLANES, group, carry_ref[0:1, :])
    carry_ref[0:1, :] = last


def _scan_rev(c_ref, b_ref, g_ref, carry_ref, ts):
    rowid = lax.broadcasted_iota(jnp.int32, (SUBLANES, 1), 0)
    ng = ts // SUBLANES

    def group(gi, gnext):
        r0 = pl.multiple_of((ng - 1 - gi) * SUBLANES, SUBLANES)
        c = c_ref[pl.ds(r0, SUBLANES), :]
        b = b_ref[pl.ds(r0, SUBLANES), :]
        for d in (1, 2, 4):
            keep = rowid < SUBLANES - d
            c_sh = jnp.where(keep, pltpu.roll(c, SUBLANES - d, 0), 1.0)
            b_sh = jnp.where(keep, pltpu.roll(b, SUBLANES - d, 0), 0.0)
            b = c * b_sh + b
            c = c * c_sh
        gg = c * gnext + b
        g_ref[pl.ds(r0, SUBLANES), :] = gg
        return gg[0:1, :]

    first = lax.fori_loop(0, ng, group, carry_ref[0:1, :])
    carry_ref[0:1, :] = first


def _past(ext, sh, ts):
    if sh == 0:
        return ext[HALO:HALO + ts]
    return pltpu.roll(ext, sh, 0)[HALO:HALO + ts]


def _future(ext, sh, ts):
    if sh == 0:
        return ext[0:ts]
    return pltpu.roll(ext, ts + HALO - sh, 0)[0:ts]


def _one_minus_sq(a, log_a):
    x = 2.0 * log_a
    series = -x * (1.0 + x * (0.5 + x * (1.0 / 6.0 + x * (1.0 / 24.0))))
    return jnp.where(x > -0.02, series, 1.0 - a * a)


def _halo_index(ts):
    blocks = ts // HALO
    return lambda t: (jnp.maximum(t * blocks - 1, 0), 0)


def _head_columns(d):
    cw = d // LRU_HEADS
    return [slice(c * cw, (c + 1) * cw) for c in range(LRU_HEADS)]


def _shift(cs, off):
    return slice(cs.start + off, cs.stop + off)


def _mixer_fwd(x, win, b_in, pw, wr, wi, wlo, wsc, wmix, vec, name, comm=()):
    s, d = x.shape
    ts = _tile(s, TS_MIXER)
    nt = s // ts
    dg = d // len(POOL_WINDOWS)
    nblk = win.shape[0]
    cols = _head_columns(d)

    def body(x_ref, xn_ref, win_ref, bin_ref, pw_ref, wr_ref, wi_ref, wlo_ref, wsc_ref, wmix_ref, vec_ref,
             z_hbm, x1_ref, rpre_ref, h_ref, ypre_ref, yl_ref, yc_ref, mg_ref, e_ref, p_ref, v_ref, r_ref, ig_ref,
             cq_ref, z_even, z_odd, zhist, a_scr, b_scr, hcarry, z_sem):
        i = pl.program_id(0)
        first = i == 0

        def project_block(xb, dst, k):
            dst[:, k * d:(k + 1) * d] = _dot(xb, win_ref[k]) + bin_ref[:, k * d:(k + 1) * d]

        @pl.when(first)
        def _():
            hcarry[...] = jnp.zeros_like(hcarry)
            zhist[...] = jnp.zeros_like(zhist)
            xb = x_ref[...].astype(BF16)
            for k in range(nblk):
                project_block(xb, z_even, k)

        def vrow(k, cs):
            return vec_ref[k:k + 1, cs]

        def step(zc, zn):
            z_out = pltpu.make_async_copy(zc, z_hbm.at[pl.ds(pl.multiple_of(i * ts, ts), ts), :], z_sem)
            z_out.start()
            xb = xn_ref[...].astype(BF16)
            tglob = i * ts + lax.broadcasted_iota(jnp.int32, (ts, 1), 0)
            sp = _softplus(-vec_ref[V_LAM:V_LAM + 1, :])

            def with_history(k, cs):
                kc = _shift(cs, k * d)
                return jnp.concatenate([jnp.where(first, 0.0, zhist[:, kc]), zc[:, kc]], axis=0)

            for hh, cs in enumerate(cols):
                if hh < nblk:
                    project_block(xb, zn, hh)
                win_len = POOL_WINDOWS[cs.start // dg]
                ext = with_history(0, cs)
                sm = ext
                sh = 1
                while sh < win_len:
                    sm = sm + pltpu.roll(sm, sh, 0)
                    sh *= 2
                inv_cnt = 1.0 / jnp.minimum(tglob + 1, win_len).astype(F32)
                p_ref[:, cs] = (sm[HALO:HALO + ts] * inv_cnt - ext[HALO:HALO + ts]).astype(BF16)
                ext = with_history(1, cs)
                v = vrow(V_CB, cs)
                for j in range(LRU_CONV):
                    v = v + vrow(V_CW + j, cs) * _past(ext, LRU_CONV - 1 - j, ts)
                r = _sigmoid(_dot(v, wr_ref[hh]) + vrow(V_BR, cs))
                ig = _sigmoid(_dot(v, wi_ref[hh]) + vrow(V_BI, cs))
                log_a = -LRU_C * r * sp[:, cs]
                a = jnp.exp(log_a)
                a_scr[:, cs] = a
                b_scr[:, cs] = jnp.sqrt(_one_minus_sq(a, log_a)) * (ig * v)
                v_ref[:, cs] = v
                r_ref[:, cs] = r
                ig_ref[:, cs] = ig
                ext = with_history(3, cs) * with_history(4, cs)
                cq = jnp.zeros((ts, cs.stop - cs.start), F32)
                for j in range(SCONV_K):
                    cq = cq + vrow(V_SW + j, cs) * _past(ext, SCONV_K - 1 - j, ts)
                cq_ref[:, cs] = cq
                e_ref[:, cs] = (zc[:, _shift(cs, 2 * d)] * cq).astype(BF16)
            for k in range(len(cols), nblk):
                project_block(xb, zn, k)
            _scan_fwd(a_scr, b_scr, h_ref, hcarry, ts)
            ypre = jnp.concatenate([_dot(p_ref[:, g * dg:(g + 1) * dg], pw_ref[g])
                                    for g in range(len(POOL_WINDOWS))], axis=1)
            yl = _dot(h_ref[...], wlo_ref[...])
            yc = _dot(e_ref[...], wsc_ref[...])
            ypre_ref[...] = ypre
            yl_ref[...] = yl
            yc_ref[...] = yc
            for cs in cols:
                merged = (_sigmoid(zc[:, _shift(cs, 5 * d)]) * (ypre[:, cs] * vrow(V_PSCALE, cs))
                          + _sigmoid(zc[:, _shift(cs, 6 * d)]) * yl[:, cs]
                          + _sigmoid(zc[:, _shift(cs, 7 * d)]) * yc[:, cs])
                mg_ref[:, cs] = merged.astype(BF16)
            rpre = ALPHA * x_ref[...] + _dot(mg_ref[...], wmix_ref[...])
            x1_ref[...] = _ln_fwd(rpre, vec_ref[V_G:V_G + 1, :], vec_ref[V_B:V_B + 1, :])
            rpre_ref[...] = rpre
            zhist[...] = zc[ts - HALO:ts, :]
            z_out.wait()

        parity = lax.rem(i, 2)

        @pl.when(parity == 0)
        def _():
            step(z_even, z_odd)

        @pl.when(parity == 1)
        def _():
            step(z_odd, z_even)

    tile = pl.BlockSpec((ts, d), lambda t: (t, 0))
    f32o = jax.ShapeDtypeStruct((s, d), F32)
    bfo = jax.ShapeDtypeStruct((s, d), BF16)
    consts = (win, b_in, pw, wr, wi, wlo, wsc, wmix, vec)
    return _pallas(
        body, name=name, grid=(nt,),
        in_specs=[tile, pl.BlockSpec((ts, d), lambda t: (jnp.minimum(t + 1, nt - 1), 0))]
        + [_const_spec(c.shape) for c in consts],
        out_specs=[pl.BlockSpec(memory_space=pl.ANY)] + [tile] * 13,
        out_shape=[jax.ShapeDtypeStruct((s, nblk * d), F32), f32o, f32o, f32o, f32o, f32o, f32o, bfo, bfo, bfo,
                   f32o, f32o, f32o, f32o],
        scratch_shapes=[pltpu.VMEM((ts, nblk * d), F32)] * 2 + [pltpu.VMEM((HALO, nblk * d), F32)]
        + [pltpu.VMEM((ts, d), F32)] * 2 + [pltpu.VMEM((SUBLANES, d), F32), pltpu.SemaphoreType.DMA],
        semantics=("arbitrary",), args=(x, x, *consts), comm=comm)


def _mixer_bwd(dx1, rpre, z, h, ypre, yl, yc, pp, vv, rr, ii, cq, pwt, wrt, wit, wlot, wsct, wmixt, vec, name,
               comm=()):
    s, d = dx1.shape
    ts = _tile(s, TS_MIXER)
    nt = s // ts
    dg = d // len(POOL_WINDOWS)
    cols = _head_columns(d)

    def body(dx1_ref, rpre_ref, z_ref, h_ref, hh_ref, ypre_ref, yl_ref, yc_ref, p_ref, v_ref, r_ref, ig_ref, cq_ref,
             pwt_ref, wrt_ref, wit_ref, wlot_ref, wsct_ref, wmixt_ref, vec_ref,
             dz_ref, dr_ref, dyl_ref, dyc_ref, acc_ref, dbin_ref, dpw_ref, dwr_ref, dwi_ref,
             c_scr, b_scr, g_scr, dyps_scr, a_keep, m_keep, gcarry, acarry, dcq_c, dv_c, m_c):
        i = pl.program_id(0)
        t = nt - 1 - i

        @pl.when(i == 0)
        def _():
            for ref in (gcarry, acarry, dcq_c, dv_c, m_c, acc_ref, dbin_ref, dpw_ref, dwr_ref, dwi_ref):
                ref[...] = jnp.zeros_like(ref)

        def vrow(k, cs):
            return vec_ref[k:k + 1, cs]

        def zc(k, cs):
            return z_ref[:, _shift(cs, k * d)]

        def acc(row, cs, val):
            acc_ref[row:row + 1, cs] += _colsum(val)

        def emit_dz(k, cs, val):
            kc = _shift(cs, k * d)
            dz_ref[:, kc] = val.astype(BF16)
            dbin_ref[:, kc] += _colsum(val)

        def with_future(tile_val, carry_ref, cs):
            ext = jnp.concatenate([tile_val, carry_ref[:, cs]], axis=0)
            carry_ref[:, cs] = tile_val[0:HALO, :]
            return ext

        dx1v = dx1_ref[...]
        dr, dyy = _ln_bwd(dx1v, rpre_ref[...], vec_ref[V_G:V_G + 1, :])
        acc_ref[A_G:A_G + 1, :] += _colsum(dyy)
        acc_ref[A_B:A_B + 1, :] += _colsum(dx1v)
        dr_ref[...] = dr
        dmg = _dot(dr, wmixt_ref[...])
        for cs in cols:
            dm = dmg[:, cs]
            ypre = ypre_ref[:, cs]
            ys = (ypre * vrow(V_PSCALE, cs), yl_ref[:, cs], yc_ref[:, cs])
            dys = []
            for k in range(3):
                gk = _sigmoid(zc(5 + k, cs))
                emit_dz(5 + k, cs, dm * ys[k] * gk * (1.0 - gk))
                dys.append(dm * gk)
            acc(A_PSCALE, cs, dys[0] * ypre)
            dyps_scr[:, cs] = dys[0] * vrow(V_PSCALE, cs)
            dyl_ref[:, cs] = dys[1].astype(BF16)
            dyc_ref[:, cs] = dys[2].astype(BF16)
        de = _dot(dyc_ref[...], wsct_ref[...])
        dh = _dot(dyl_ref[...], wlot_ref[...])

        sp = _softplus(-vec_ref[V_LAM:V_LAM + 1, :])
        for cs in cols:
            dec = de[:, cs]
            emit_dz(2, cs, dec * cq_ref[:, cs])
            dcq_ext = with_future(dec * zc(2, cs), dcq_c, cs)
            zcc, zh = zc(3, cs), zc(4, cs)
            qv = zcc * zh
            dq = jnp.zeros_like(qv)
            for j in range(SCONV_K):
                adv = _future(dcq_ext, SCONV_K - 1 - j, ts)
                acc(A_SW + j, cs, adv * qv)
                dq = dq + vrow(V_SW + j, cs) * adv
            emit_dz(3, cs, dq * zh)
            emit_dz(4, cs, dq * zcc)
            log_a = -LRU_C * r_ref[:, cs] * sp[:, cs]
            a = jnp.exp(log_a)
            a_keep[:, cs] = a
            m_keep[:, cs] = jnp.sqrt(_one_minus_sq(a, log_a))
            c_scr[:, cs] = _future(with_future(a, acarry, cs), 1, ts)
            b_scr[:, cs] = dh[:, cs]
        _scan_rev(c_scr, b_scr, g_scr, gcarry, ts)

        for hh, cs in enumerate(cols):
            gs, a, mult = g_scr[:, cs], a_keep[:, cs], m_keep[:, cs]
            r, ig, v = r_ref[:, cs], ig_ref[:, cs], v_ref[:, cs]
            hprev = _past(jnp.concatenate([jnp.where(t == 0, 0.0, hh_ref[:, cs]), h_ref[:, cs]], axis=0), 1, ts)
            iv = ig * v
            dlog_a = gs * hprev * a + gs * iv * (-(a * a) / mult)
            div = gs * mult
            acc(A_SP, cs, dlog_a * (-LRU_C) * r)
            dpre_r = dlog_a * (-LRU_C) * sp[:, cs] * r * (1.0 - r)
            dpre_i = div * v * ig * (1.0 - ig)
            acc(A_BR, cs, dpre_r)
            acc(A_BI, cs, dpre_i)
            dv = div * ig + _dot(dpre_r, wrt_ref[hh]) + _dot(dpre_i, wit_ref[hh])
            dwr_ref[hh] += _dot_tn(v, dpre_r)
            dwi_ref[hh] += _dot_tn(v, dpre_i)
            acc(A_CB, cs, dv)
            dv_ext = with_future(dv, dv_c, cs)
            zl = zc(1, cs)
            dzl = jnp.zeros_like(zl)
            for j in range(LRU_CONV):
                adv = _future(dv_ext, LRU_CONV - 1 - j, ts)
                acc(A_CW + j, cs, adv * zl)
                dzl = dzl + vrow(V_CW + j, cs) * adv
            emit_dz(1, cs, dzl)

        tglob = t * ts + lax.broadcasted_iota(jnp.int32, (ts, 1), 0)
        for g, win_len in enumerate(POOL_WINDOWS):
            cs = slice(g * dg, (g + 1) * dg)
            dyps = dyps_scr[:, cs]
            dpw_ref[g] += _dot_tn(p_ref[:, cs], dyps)
            dp = _dot(dyps, pwt_ref[g])
            inv_cnt = 1.0 / jnp.minimum(tglob + 1, win_len).astype(F32)
            sm = with_future(dp * inv_cnt, m_c, cs)
            sh = 1
            while sh < win_len:
                sm = sm + pltpu.roll(sm, ts + HALO - sh, 0)
                sh *= 2
            emit_dz(0, cs, sm[0:ts] - dp)

    def rev(tt):
        return (nt - 1 - tt, 0)

    halo = _halo_index(ts)
    tile = pl.BlockSpec((ts, d), rev)
    hspec = pl.BlockSpec((HALO, d), lambda tt: halo(nt - 1 - tt))
    f32o = jax.ShapeDtypeStruct((s, d), F32)
    bfo = jax.ShapeDtypeStruct((s, d), BF16)
    consts = (pwt, wrt, wit, wlot, wsct, wmixt, vec)
    return _pallas(
        body, name=name, grid=(nt,),
        in_specs=[tile, tile, pl.BlockSpec((ts, 8 * d), rev), tile, hspec] + [tile] * 8
        + [_const_spec(c.shape) for c in consts],
        out_specs=[pl.BlockSpec((ts, 8 * d), rev), tile, tile, tile,
                   _acc_spec((A_ROWS, d)), _acc_spec((1, 8 * d)),
                   _acc_spec(pwt.shape), _acc_spec(wrt.shape), _acc_spec(wit.shape)],
        out_shape=[jax.ShapeDtypeStruct((s, 8 * d), BF16), f32o, bfo, bfo,
                   jax.ShapeDtypeStruct((A_ROWS, d), F32), jax.ShapeDtypeStruct((1, 8 * d), F32),
                   jax.ShapeDtypeStruct(pwt.shape, F32), jax.ShapeDtypeStruct(wrt.shape, F32),
                   jax.ShapeDtypeStruct(wit.shape, F32)],
        scratch_shapes=[pltpu.VMEM((ts, d), F32)] * 6 + [pltpu.VMEM((SUBLANES, d), F32)]
        + [pltpu.VMEM((HALO, d), F32)] * 4,
        semantics=("arbitrary",), args=(dx1, rpre, z, h, h, ypre, yl, yc, pp, vv, rr, ii, cq, *consts), comm=comm)


def _softmax_rows(sc):
    mx = jnp.max(sc, axis=-1, keepdims=True)
    ex = jnp.exp(sc - mx)
    return ex * (1.0 / jnp.sum(ex, axis=-1, keepdims=True))


def _attn_fwd(x1, wq, wo, kt, vv, vec, name):
    s, d = x1.shape
    ts = _tile(s, TS_ATTN)
    hd = d // X_HEADS
    scale = hd ** -0.5

    def body(x_ref, wq_ref, wo_ref, kt_ref, v_ref, vec_ref, x2_ref, rpre_ref, q_ref, o_ref):
        xv = x_ref[...]
        q = _dot(xv, wq_ref[...]).astype(BF16)
        q_ref[...] = q
        for hh in range(X_HEADS):
            cs = slice(hh * hd, (hh + 1) * hd)
            p = _softmax_rows(_dot(q[:, cs], kt_ref[cs, :]) * scale)
            o_ref[:, cs] = _dot(p, v_ref[:, cs]).astype(BF16)
        rpre = ALPHA * xv + _dot(o_ref[...], wo_ref[...])
        rpre_ref[...] = rpre
        x2_ref[...] = _ln_fwd(rpre, vec_ref[V_G + 1:V_G + 2, :], vec_ref[V_B + 1:V_B + 2, :])

    tile = pl.BlockSpec((ts, d), lambda t: (t, 0))
    f32o = jax.ShapeDtypeStruct((s, d), F32)
    bfo = jax.ShapeDtypeStruct((s, d), BF16)
    consts = (wq, wo, kt, vv, vec)
    return pl.pallas_call(
        body, name=name, grid=(s // ts,),
        in_specs=[tile] + [_const_spec(c.shape) for c in consts],
        out_specs=[tile] * 4, out_shape=[f32o, f32o, bfo, bfo],
        compiler_params=_cparams(("parallel",)),
    )(x1, *consts)


def _attn_bwd(dx2, rpre, q, wqt, wot, kk, kt, vt, vec, name):
    s, d = dx2.shape
    ts = _tile(s, TS_ATTN)
    nm = kk.shape[0]
    hd = d // X_HEADS
    scale = hd ** -0.5

    def body(dx2_ref, rpre_ref, q_ref, wqt_ref, wot_ref, k_ref, kt_ref, vt_ref, vec_ref,
             dx1_ref, dq_ref, dr_ref, dk_ref, dv_ref, ln_ref):
        @pl.when(pl.program_id(0) == 0)
        def _():
            for ref in (dk_ref, dv_ref, ln_ref):
                ref[...] = jnp.zeros_like(ref)

        dyv = dx2_ref[...]
        dr, dyy = _ln_bwd(dyv, rpre_ref[...], vec_ref[V_G + 1:V_G + 2, :])
        ln_ref[0:1, :] += _colsum(dyy)
        ln_ref[1:2, :] += _colsum(dyv)
        dr_ref[...] = dr.astype(BF16)
        do = _dot(dr, wot_ref[...])
        q = q_ref[...]
        for hh in range(X_HEADS):
            cs = slice(hh * hd, (hh + 1) * hd)
            p = _softmax_rows(_dot(q[:, cs], kt_ref[cs, :]) * scale)
            dp = _dot(do[:, cs], vt_ref[cs, :])
            ds = p * (dp - jnp.sum(dp * p, axis=-1, keepdims=True)) * scale
            dq_ref[:, cs] = _dot(ds, k_ref[:, cs]).astype(BF16)
            dk_ref[:, cs] += _dot_tn(ds, q[:, cs])
            dv_ref[:, cs] += _dot_tn(p, do[:, cs])
        dx1_ref[...] = ALPHA * dr + _dot(dq_ref[...], wqt_ref[...])

    tile = pl.BlockSpec((ts, d), lambda t: (t, 0))
    consts = (wqt, wot, kk, kt, vt, vec)
    return pl.pallas_call(
        body, name=name, grid=(s // ts,),
        in_specs=[tile, tile, tile] + [_const_spec(c.shape) for c in consts],
        out_specs=[tile, tile, tile, _acc_spec((nm, d)), _acc_spec((nm, d)), _acc_spec((2, d))],
        out_shape=[jax.ShapeDtypeStruct((s, d), F32), jax.ShapeDtypeStruct((s, d), BF16),
                   jax.ShapeDtypeStruct((s, d), BF16), jax.ShapeDtypeStruct((nm, d), F32),
                   jax.ShapeDtypeStruct((nm, d), F32), jax.ShapeDtypeStruct((2, d), F32)],
        compiler_params=_cparams(("arbitrary",)),
    )(dx2, rpre, q, *consts)


def _ffn_out(x2, hgu, wd, vec, name, target=None):
    s, d = x2.shape
    ff = wd.shape[0]
    ts = _tile(s, TS_FFN)
    chunk = 2 * 128 if ff % (2 * 128) == 0 else ff

    def body(*refs):
        if target is None:
            x_ref, hgu_ref, wd_ref, vec_ref, out_ref, rpre_ref, act_ref = refs
        else:
            x_ref, hgu_ref, wd_ref, vec_ref, t_ref, out_ref, rpre_ref, act_ref, loss_ref = refs
        for c in range(0, ff, chunk):
            hg = hgu_ref[:, c:c + chunk]
            act_ref[:, c:c + chunk] = (hg * _sigmoid(hg) * hgu_ref[:, ff + c:ff + c + chunk]).astype(BF16)
        rpre = ALPHA * x_ref[...] + _dot(act_ref[...], wd_ref[...])
        rpre_ref[...] = rpre
        x3 = _ln_fwd(rpre, vec_ref[V_G + 2:V_G + 3, :], vec_ref[V_B + 2:V_B + 3, :])
        if target is None:
            out_ref[...] = x3
            return

        @pl.when(pl.program_id(0) == 0)
        def _():
            loss_ref[...] = jnp.zeros_like(loss_ref)

        err = x3 - t_ref[...]
        out_ref[...] = err / d
        per_token = jnp.mean(err * err, axis=-1, keepdims=True)
        loss_ref[...] += 0.5 * jnp.sum(per_token, axis=0, keepdims=True)

    tile = pl.BlockSpec((ts, d), lambda t: (t, 0))
    f32o = jax.ShapeDtypeStruct((s, d), F32)
    in_specs = [tile, pl.BlockSpec((ts, 2 * ff), lambda t: (t, 0)), _const_spec(wd.shape), _const_spec(vec.shape)]
    out_specs = [tile, tile, pl.BlockSpec((ts, ff), lambda t: (t, 0))]
    out_shape = [f32o, f32o, jax.ShapeDtypeStruct((s, ff), BF16)]
    args = [x2, hgu, wd, vec]
    if target is not None:
        in_specs.append(tile)
        args.append(target)
        out_specs.append(_acc_spec((1, 1)))
        out_shape.append(jax.ShapeDtypeStruct((1, 1), F32))
    return pl.pallas_call(
        body, name=name, grid=(s // ts,), in_specs=in_specs, out_specs=out_specs, out_shape=out_shape,
        compiler_params=_cparams(("parallel",) if target is None else ("arbitrary",)),
    )(*args)


def _ffn_bwd(dy, rpre, hgu, wdt, wgt, wut, vec, name, comm=()):
    s, d = dy.shape
    ff = wgt.shape[0]
    ts = _tile(s, TS_FFN)
    chunk = 2 * 128 if ff % (2 * 128) == 0 else ff

    def body(dy_ref, rpre_ref, hgu_ref, wdt_ref, wgt_ref, wut_ref, vec_ref, dx_ref, dr_ref, dhgu_ref, ln_ref):
        @pl.when(pl.program_id(0) == 0)
        def _():
            ln_ref[...] = jnp.zeros_like(ln_ref)

        dyv = dy_ref[...]
        dr, dyy = _ln_bwd(dyv, rpre_ref[...], vec_ref[V_G + 2:V_G + 3, :])
        ln_ref[0:1, :] += _colsum(dyy)
        ln_ref[1:2, :] += _colsum(dyv)
        dr_ref[...] = dr.astype(BF16)
        dact = _dot(dr, wdt_ref[...])
        for c in range(0, ff, chunk):
            hg = hgu_ref[:, c:c + chunk]
            hu = hgu_ref[:, ff + c:ff + c + chunk]
            da = dact[:, c:c + chunk]
            sg = _sigmoid(hg)
            dhgu_ref[:, c:c + chunk] = (da * hu * (sg * (1.0 + hg * (1.0 - sg)))).astype(BF16)
            dhgu_ref[:, ff + c:ff + c + chunk] = (da * hg * sg).astype(BF16)
        dx_ref[...] = (ALPHA * dr + _dot(dhgu_ref[:, 0:ff], wgt_ref[...])
                       + _dot(dhgu_ref[:, ff:2 * ff], wut_ref[...]))

    tile = pl.BlockSpec((ts, d), lambda t: (t, 0))
    wide = pl.BlockSpec((ts, 2 * ff), lambda t: (t, 0))
    consts = (wdt, wgt, wut, vec)
    return _pallas(
        body, name=name, grid=(s // ts,),
        in_specs=[tile, tile, wide] + [_const_spec(c.shape) for c in consts],
        out_specs=[tile, tile, wide, _acc_spec((2, d))],
        out_shape=[jax.ShapeDtypeStruct((s, d), F32), jax.ShapeDtypeStruct((s, d), BF16),
                   jax.ShapeDtypeStruct((s, 2 * ff), BF16), jax.ShapeDtypeStruct((2, d), F32)],
        semantics=("arbitrary",), args=(dy, rpre, hgu, *consts), comm=comm)


SHARD_AXIS = {"w_in": 1, "pool_w": 1, "lru_w_out": 0, "sconv_w_out": 0, "w_mix_out": 0,
              "xa_w_q": 0, "xa_w_k": 0, "xa_w_v": 0, "xa_w_o": 0,
              "ffn_w_gate": 0, "ffn_w_up": 0, "ffn_w_down": 0,
              "lru_conv_w": 1, "sconv_w": 1, "ln_g": 1, "ln_b": 1}
STORED_TRANSPOSED = ("ffn_w_gate", "ffn_w_up")
GROUP_IN = ("w_in",)
GROUP_MIXER = ("pool_w", "lru_w_out", "sconv_w_out", "w_mix_out")
GROUP_ATTN = ("xa_w_q", "xa_w_k", "xa_w_v", "xa_w_o")
GROUP_FFN = ("ffn_w_gate", "ffn_w_up", "ffn_w_down")
GROUP_VECTORS = ("lru_conv_w", "sconv_w", "ln_g", "ln_b")
REPLICATED = ("b_in", "pool_scale", "lru_conv_b", "lru_w_r", "lru_b_r", "lru_w_i", "lru_b_i", "lru_lambda")
WEIGHTS = ("w_in", "b_in", "pool_w", "pool_scale", "lru_conv_w", "lru_conv_b", "lru_w_r", "lru_b_r", "lru_w_i",
           "lru_b_i", "lru_lambda", "lru_w_out", "sconv_w", "sconv_w_out", "w_mix_out", "xa_w_q", "xa_w_k",
           "xa_w_v", "xa_w_o", "ffn_w_gate", "ffn_w_up", "ffn_w_down", "ln_g", "ln_b")


def _pack(arrs, width, lead=0, row_multiple=ROW_PAD):
    head = arrs[0].shape[:lead]
    flat = jnp.concatenate([a.reshape(head + (-1,)) for a in arrs], axis=lead)
    n = flat.shape[-1]
    chunk = width * row_multiple
    total = -(-n // chunk) * chunk
    if total != n:
        flat = jnp.pad(flat, [(0, 0)] * lead + [(0, total - n)])
    return flat.reshape(head + (total // width, width))


def _unpack(buf, shapes, lead=0):
    head = buf.shape[:lead]
    flat = buf.reshape(head + (-1,))
    out, off = [], 0
    for shp in shapes:
        n = math.prod(shp)
        out.append(flat[..., off:off + n].reshape(head + tuple(shp)))
        off += n
    return out


def _split8(a, axis):
    shp = a.shape
    a = a.reshape(shp[:axis] + (N_DEV, shp[axis] // N_DEV) + shp[axis + 1:])
    return jnp.moveaxis(a, axis, 0)


def _join8(a, axis):
    a = jnp.moveaxis(a, 0, axis)
    shp = a.shape
    return a.reshape(shp[:axis] + (shp[axis] * shp[axis + 1],) + shp[axis + 2:])


def _t(a):
    return jnp.swapaxes(a, -1, -2)


def _stored(name, a):
    return _t(a) if name in STORED_TRANSPOSED else a


def kernel(x, mem, w_in, b_in, pool_w, pool_scale, lru_conv_w, lru_conv_b, lru_w_r, lru_b_r, lru_w_i, lru_b_i, lru_lambda, lru_w_out, sconv_w, sconv_w_out, w_mix_out, xa_w_q, xa_w_k, xa_w_v, xa_w_o, ffn_w_gate, ffn_w_up, ffn_w_down, ln_g, ln_b, loss_target, m_w_in, m_b_in, m_pool_w, m_pool_scale, m_lru_conv_w, m_lru_conv_b, m_lru_w_r, m_lru_b_r, m_lru_w_i, m_lru_b_i, m_lru_lambda, m_lru_w_out, m_sconv_w, m_sconv_w_out, m_w_mix_out, m_xa_w_q, m_xa_w_k, m_xa_w_v, m_xa_w_o, m_ffn_w_gate, m_ffn_w_up, m_ffn_w_down, m_ln_g, m_ln_b, v_w_in, v_b_in, v_pool_w, v_pool_scale, v_lru_conv_w, v_lru_conv_b, v_lru_w_r, v_lru_b_r, v_lru_w_i, v_lru_b_i, v_lru_lambda, v_lru_w_out, v_sconv_w, v_sconv_w_out, v_w_mix_out, v_xa_w_q, v_xa_w_k, v_xa_w_v, v_xa_w_o, v_ffn_w_gate, v_ffn_w_up, v_ffn_w_down, v_ln_g, v_ln_b):
    args = dict(locals())
    w = {n: args[n] for n in WEIGHTS}
    mom_m = {n: args["m_" + n] for n in WEIGHTS}
    mom_v = {n: args["v_" + n] for n in WEIGHTS}
    depth = w_in.shape[0]
    s, d = x.shape[1], x.shape[2]
    nm = mem.shape[1]
    ff = ffn_w_gate.shape[2] * N_DEV
    xs = x.reshape(s, d)
    mems = mem.reshape(nm, d)
    target = loss_target.reshape(s, d)

    def shard(t, n, l):
        return _stored(n, t[n][l])

    def pack_shards(t, names, l, dtype=None):
        arrs = [shard(t, n, l) for n in names]
        return _pack([a if dtype is None else a.astype(dtype) for a in arrs], d)

    def unpack_gathered(buf, names):
        pieces = _unpack(buf, [shard(w, n, 0).shape for n in names], lead=1)
        return {n: (p if n == "w_in" else _join8(p, SHARD_AXIS[n])) for n, p in zip(names, pieces)}

    def layer_vec(l, fw):
        vec = jnp.zeros((V_ROWS, d), F32)
        vec = vec.at[V_PSCALE].set(pool_scale[l]).at[V_CW:V_CW + LRU_CONV].set(fw["lru_conv_w"])
        vec = vec.at[V_CB].set(lru_conv_b[l]).at[V_BR].set(lru_b_r[l]).at[V_BI].set(lru_b_i[l])
        vec = vec.at[V_LAM].set(lru_lambda[l]).at[V_SW:V_SW + SCONV_K].set(fw["sconv_w"])
        return vec.at[V_G:V_G + 3].set(fw["ln_g"]).at[V_B:V_B + 3].set(fw["ln_b"])

    def layer_params(l, fw):
        return dict(
            vec=layer_vec(l, fw), wint=_t(fw["w_in"]).reshape(1, 8 * d, d),
            pw=fw["pool_w"], pwt=_t(fw["pool_w"]),
            wr=lru_w_r[l].astype(BF16), wi=lru_w_i[l].astype(BF16),
            wrt=_t(lru_w_r[l]).astype(BF16), wit=_t(lru_w_i[l]).astype(BF16),
            wlo=fw["lru_w_out"], wlot=_t(fw["lru_w_out"]),
            wsc=fw["sconv_w_out"], wsct=_t(fw["sconv_w_out"]),
            wmix=fw["w_mix_out"], wmixt=_t(fw["w_mix_out"]),
            wq=fw["xa_w_q"], wqt=_t(fw["xa_w_q"]), wo=fw["xa_w_o"], wot=_t(fw["xa_w_o"]),
            wkv=jnp.stack([fw["xa_w_k"], fw["xa_w_v"]]),
            wgu=jnp.stack([_t(fw["ffn_w_gate"]), _t(fw["ffn_w_up"])]),
            wgt=fw["ffn_w_gate"], wut=fw["ffn_w_up"],
            wd=fw["ffn_w_down"], wdt=_t(fw["ffn_w_down"]))

    later = GROUP_ATTN + GROUP_FFN
    first_weights, vectors = _all_gather(
        [pack_shards(w, GROUP_IN + GROUP_MIXER, 0, BF16),
         _pack([shard(w, n, l) for l in range(depth) for n in GROUP_VECTORS], d)], "gather_first")
    fw0 = unpack_gathered(first_weights, GROUP_IN + GROUP_MIXER)
    vec_pieces = _unpack(vectors, [shard(w, n, l).shape for l in range(depth) for n in GROUP_VECTORS], lead=1)
    fvec = [{n: _join8(vec_pieces[l * len(GROUP_VECTORS) + k], SHARD_AXIS[n]) for k, n in enumerate(GROUP_VECTORS)}
            for l in range(depth)]

    layers, saved = [], []
    cur = xs
    fw_next = None
    for l in range(depth):
        fw = dict(fw0 if l == 0 else fw_next)
        fw.update(fvec[l])
        comm = []
        if l == 0:
            comm.append(("gather", pack_shards(w, later, 0, BF16)))
        if l + 1 < depth:
            comm.append(("gather", pack_shards(w, GROUP_IN + GROUP_MIXER + later, l + 1, BF16)))
        z, x1, rpre1, h, ypre, yl, yc, merged, e, pp, vb, rb, ib, cq, *got = _mixer_fwd(
            cur, fw["w_in"], b_in[l].reshape(1, 8 * d), fw["pool_w"], lru_w_r[l].astype(BF16),
            lru_w_i[l].astype(BF16), fw["lru_w_out"], fw["sconv_w_out"], fw["w_mix_out"],
            layer_vec(l, fw), f"mixer_fwd_{l}", comm=comm)
        if l == 0:
            fw.update(unpack_gathered(got.pop(0), later))
        if l + 1 < depth:
            fw_next = unpack_gathered(got.pop(0), GROUP_IN + GROUP_MIXER + later)
        p = layer_params(l, fw)
        kv = _mm(mems, p["wkv"], f"kv_{l}")[0]
        kk = kv[:, :d].astype(BF16)
        vv = kv[:, d:].astype(BF16)
        x2, rpre2, q, o = _attn_fwd(x1, p["wq"], p["wo"], _t(kk), vv, p["vec"], f"attn_fwd_{l}")
        hgu = _mm(x2, p["wgu"], f"ffn_in_{l}", tm=TS_MM // 2)[0]
        if l + 1 < depth:
            cur_next, rpre3, act = _ffn_out(x2, hgu, p["wd"], p["vec"], f"ffn_out_{l}")
        else:
            dcur, rpre3, act, loss_part = _ffn_out(x2, hgu, p["wd"], p["vec"], f"ffn_out_{l}", target=target)
            cur_next = None
        layers.append(p)
        saved.append(dict(x0=cur, z=z, x1=x1, rpre1=rpre1, h=h, ypre=ypre, yl=yl, yc=yc, merged=merged, e=e,
                          pp=pp, vb=vb, rb=rb, ib=ib, cq=cq,
                          kk=kk, vv=vv, x2=x2, rpre2=rpre2, q=q, o=o, hgu=hgu, rpre3=rpre3, act=act))
        cur = cur_next

    loss = lax.psum(loss_part[0, 0], ("x", "y", "c"))

    res = {}

    def slots_of(g, names):
        if tuple(names) == GROUP_VECTORS:
            return [_pack([_split8(g[n], SHARD_AXIS[n]) for n in names], d, lead=1)]
        return [(g[n] if n == "w_in" else _split8(g[n], SHARD_AXIS[n])).reshape(N_DEV, -1, d).astype(BF16)
                for n in names]

    def update(received, names, l, tag):
        outs = _adamw_sum(received, *[pack_shards(t, names, l) for t in (w, mom_m, mom_v)], f"adamw_{tag}_{l}")
        shapes = [shard(w, n, l).shape for n in names]
        for n, *parts in zip(names, *[_unpack(o, shapes) for o in outs]):
            res[(n, l)] = [_stored(n, a) for a in parts]

    def settle(exchanges, got):
        for (names, l, tag, _), received in zip(exchanges, got):
            update(received, names, l, tag)

    grads = [None] * depth
    for l in reversed(range(depth)):
        p, sv = layers[l], saved[l]
        g = {}
        dx2, dr3, dhgu, ln3 = _ffn_bwd(dcur, sv["rpre3"], sv["hgu"], p["wdt"], p["wgt"], p["wut"], p["vec"],
                                       f"ffn_bwd_{l}")
        g["ffn_w_down"] = _mm_tn(sv["act"], dr3, d, f"g_wd_{l}")[0][0]
        dwgu = _mm_tn(dhgu, sv["x2"], d, f"g_wgu_{l}", tk=ff)[0][0]
        g["ffn_w_gate"], g["ffn_w_up"] = dwgu[:ff], dwgu[ff:]
        dx1, dq, dr2, dk, dv, ln2 = _attn_bwd(dx2, sv["rpre2"], sv["q"], p["wqt"], p["wot"], sv["kk"], _t(sv["kk"]),
                                              _t(sv["vv"]), p["vec"], f"attn_bwd_{l}")
        g["xa_w_o"] = _mm_tn(sv["o"], dr2, d, f"g_wo_{l}")[0][0]
        g["xa_w_q"] = _mm_tn(sv["x1"], dq, d, f"g_wq_{l}")[0][0]
        dwkv = _mm_tn(mems, jnp.concatenate([dk, dv], axis=1), d, f"g_wkv_{l}")[0]
        g["xa_w_k"], g["xa_w_v"] = dwkv[0], dwkv[1]
        ffn_slots = slots_of(g, GROUP_FFN)
        (dz, dr1, dyl, dyc, accs, dbin, g["pool_w"], g["lru_w_r"], g["lru_w_i"], received) = _mixer_bwd(
            dx1, sv["rpre1"], sv["z"], sv["h"], sv["ypre"], sv["yl"], sv["yc"], sv["pp"], sv["vb"], sv["rb"],
            sv["ib"], sv["cq"], p["pwt"], p["wrt"], p["wit"], p["wlot"], p["wsct"], p["wmixt"], p["vec"],
            f"mixer_bwd_{l}",
            comm=[("scatter", ffn_slots)])
        update(received, GROUP_FFN, l, "ffn")
        g["w_mix_out"] = _mm_tn(sv["merged"], dr1, d, f"g_wmix_{l}")[0][0]
        g["lru_w_out"] = _mm_tn(sv["h"], dyl, d, f"g_wlo_{l}")[0][0]
        g["sconv_w_out"] = _mm_tn(sv["e"], dyc, d, f"g_wsc_{l}")[0][0]
        g["b_in"] = dbin[0]
        g["pool_scale"] = accs[A_PSCALE]
        g["lru_conv_w"] = accs[A_CW:A_CW + LRU_CONV]
        g["lru_conv_b"] = accs[A_CB]
        g["lru_b_r"] = accs[A_BR]
        g["lru_b_i"] = accs[A_BI]
        g["lru_lambda"] = accs[A_SP] * (-_sigmoid(-lru_lambda[l]))
        g["sconv_w"] = accs[A_SW:A_SW + SCONV_K]
        g["ln_g"] = jnp.stack([accs[A_G], ln2[0], ln3[0]])
        g["ln_b"] = jnp.stack([accs[A_B], ln2[1], ln3[1]])
        grads[l] = g
        behind_win = [(names, l, tag, slots_of(g, names)) for names, tag in
                      ((GROUP_ATTN, "attn"), (GROUP_MIXER, "mixer"), (GROUP_VECTORS, "vectors"))]
        g["w_in"], *got = _mm_tn(sv["x0"], dz, d, f"g_win_{l}", comm=[("scatter", t[3]) for t in behind_win])
        settle(behind_win, got)
        behind_dx = [(GROUP_IN, l, "w_in", slots_of(g, GROUP_IN))]
        comm = [("scatter", behind_dx[0][3])]
        if l == 0:
            comm.append(("gather", _pack([jnp.stack([grads[k][n] for k in range(depth)]) for n in REPLICATED], d)))
        dcur, *got = _mm(dz, p["wint"], f"dx_{l}", add=dr1, add_scale=ALPHA, comm=comm)
        settle(behind_dx, got)
        if l == 0:
            outs = _adamw_sum(got[1], *[_pack([t[n] for n in REPLICATED], d) for t in (w, mom_m, mom_v)],
                              "adamw_replicated")
            rep_shapes = [w[n].shape for n in REPLICATED]
            final = {n: parts for n, *parts in zip(REPLICATED, *[_unpack(o, rep_shapes) for o in outs])}
    grad_x = dcur.reshape(x.shape)

    for n in WEIGHTS:
        if n not in final:
            final[n] = [jnp.stack([res[(n, l)][k] for l in range(depth)]) for k in range(4)]
    return (loss, grad_x, *[final[n][0] for n in WEIGHTS], *[final[n][1] for n in WEIGHTS],
            *[final[n][2] for n in WEIGHTS], *[final[n][3] for n in WEIGHTS])
```

```python
import functools
import math

import jax
import jax.numpy as jnp
from jax import lax
from jax.experimental import pallas as pl
from jax.experimental.pallas import tpu as pltpu

F32 = jnp.float32
BF16 = jnp.bfloat16
MESH = pl.DeviceIdType.MESH

N_DEV = 8
LRU_HEADS = 8
LRU_CONV = 4
LRU_C = 8.0
SCONV_K = 3
POOL_WINDOWS = (2, 4, 8, 16)
X_HEADS = 4
DEPTH = 2
ALPHA = (2 * DEPTH) ** 0.25
LN_EPS = 1e-5
ADAM_LR = 0.001
ADAM_B1 = 0.9
ADAM_B2 = 0.999
ADAM_EPS = 1e-08
ADAM_WD = 0.01
ADAM_STEP = 10

HALO = 16
SUBLANES = 8
VMEM_LIMIT = 56 * 1024 * 1024
TS_MIXER = 128
TS_ATTN = 1024
TS_FFN = 256
TS_MM = 1024
TK_MM = 2048
TR_ADAM = 256
ROW_PAD = 8
TS_MM_TN = 1024

V_PSCALE, V_CW, V_CB, V_BR, V_BI, V_LAM, V_SW, V_G, V_B = 0, 1, 5, 6, 7, 8, 9, 12, 15
V_ROWS = 24
A_PSCALE, A_CW, A_CB, A_BR, A_BI, A_SP, A_SW, A_G, A_B = 0, 1, 5, 6, 7, 8, 9, 12, 13
A_ROWS = 16


def _cparams(sem):
    return pltpu.CompilerParams(dimension_semantics=sem, vmem_limit_bytes=VMEM_LIMIT)


def _tile(n, pref):
    if n <= pref:
        return n
    assert n % pref == 0, (n, pref)
    return pref


def _const_spec(shape):
    nd = len(shape)
    return pl.BlockSpec(shape, lambda *_: (0,) * nd, pipeline_mode=pl.Buffered(1))


def _acc_spec(shape):
    nd = len(shape)
    return pl.BlockSpec(shape, lambda *_: (0,) * nd)


def _dot(a, b):
    return jnp.dot(a.astype(BF16), b.astype(BF16), preferred_element_type=F32)


def _dot_tn(a, b):
    return lax.dot_general(a.astype(BF16), b.astype(BF16), (((0,), (0,)), ((), ())),
                           preferred_element_type=F32)


def _sigmoid(x):
    return 0.5 * jnp.tanh(0.5 * x) + 0.5


def _softplus(y):
    e = jnp.exp(-jnp.abs(y))
    log1p = jnp.where(e < 1e-4, e * (1.0 - e * (0.5 - e * (1.0 / 3.0))), jnp.log(1.0 + e))
    return jnp.maximum(y, 0.0) + log1p


def _ln_fwd(r, g, b):
    mu = jnp.mean(r, axis=-1, keepdims=True)
    xc = r - mu
    var = jnp.mean(xc * xc, axis=-1, keepdims=True)
    return xc * lax.rsqrt(var + LN_EPS) * g + b


def _ln_bwd(dy, r, g):
    mu = jnp.mean(r, axis=-1, keepdims=True)
    xc = r - mu
    var = jnp.mean(xc * xc, axis=-1, keepdims=True)
    rstd = lax.rsqrt(var + LN_EPS)
    yhat = xc * rstd
    dyh = dy * g
    m1 = jnp.mean(dyh, axis=-1, keepdims=True)
    m2 = jnp.mean(dyh * yhat, axis=-1, keepdims=True)
    return rstd * (dyh - m1 - yhat * m2), dy * yhat


def _colsum(a):
    return jnp.sum(a, axis=0, keepdims=True)


def _position():
    return lax.axis_index("x"), lax.axis_index("y"), lax.axis_index("c")


def _gather_copies(x_ref, out_ref, send_sems, recv_sems, local_sem):
    x, y, c = _position()
    me, sibling = (x, y, c), (x, y, 1 - c)
    chips = [(1 - x, y), (x, 1 - y), (1 - x, 1 - y)]

    def slot(px, py, pc):
        return out_ref.at[4 * px + 2 * py + pc]

    def copy(k, block, to, src=None):
        return pltpu.make_async_remote_copy(
            src_ref=slot(*block) if src is None else src, dst_ref=slot(*block),
            send_sem=send_sems.at[k], recv_sem=recv_sems.at[k], device_id=to, device_id_type=MESH)

    mine = pltpu.make_async_copy(x_ref, slot(*me), local_sem)
    first = [copy(0, me, sibling, src=x_ref)]
    first += [copy(1 + j, me, (*chip, c), src=x_ref) for j, chip in enumerate(chips)]
    passed = [copy(4 + j, (*chip, c), sibling) for j, chip in enumerate(chips)]
    over_ici = [copy(1 + j, (*chip, c), me) for j, chip in enumerate(chips)]
    from_sibling = copy(0, sibling, me)
    forwarded = [copy(4 + j, (*chip, 1 - c), me) for j, chip in enumerate(chips)]
    return mine, first, passed, over_ici, from_sibling, forwarded


def _scatter_copies(g_refs, out_ref, send_sems, recv_sems, local_sem):
    x, y, c = _position()
    me = 4 * x + 2 * y + c
    offsets, rows = [], 0
    for g_ref in g_refs:
        offsets.append(rows)
        rows += g_ref.shape[1]

    def landing(g_ref, off):
        return out_ref.at[me, pl.ds(off, g_ref.shape[1])]

    mine = [pltpu.make_async_copy(g_ref.at[me], landing(g_ref, off), local_sem)
            for g_ref, off in zip(g_refs, offsets)]
    mine_all = pltpu.make_async_copy(out_ref.at[me], out_ref.at[me], local_sem)
    copies, waits = [], []
    for k in range(1, N_DEV):
        px = 1 - x if k & 4 else x
        py = 1 - y if k & 2 else y
        pc = 1 - c if k & 1 else c
        sems = dict(send_sem=send_sems.at[k - 1], recv_sem=recv_sems.at[k - 1],
                    device_id=(px, py, pc), device_id_type=MESH)
        copies += [pltpu.make_async_remote_copy(src_ref=g_ref.at[4 * px + 2 * py + pc], dst_ref=landing(g_ref, off),
                                                **sems) for g_ref, off in zip(g_refs, offsets)]
        waits.append(pltpu.make_async_remote_copy(src_ref=out_ref.at[me], dst_ref=out_ref.at[me], **sems))
    return mine, copies, mine_all, waits


def _comm_start(kind, srcs, *refs):
    if kind == "gather":
        mine, first, _, _, _, _ = _gather_copies(srcs[0], *refs)
        mine.start()
        for cp in first:
            cp.start()
    else:
        mine, copies, _, _ = _scatter_copies(srcs, *refs)
        for cp in mine + copies:
            cp.start()


def _gather_pass_on(srcs, *refs):
    _, _, passed, over_ici, _, _ = _gather_copies(srcs[0], *refs)
    for arrival, forward in zip(over_ici, passed):
        arrival.wait_recv()
        forward.start()


def _comm_finish(kind, srcs, *refs, passed_on=False):
    if kind == "gather":
        if not passed_on:
            _gather_pass_on(srcs, *refs)
        mine, first, passed, _, from_sibling, forwarded = _gather_copies(srcs[0], *refs)
        from_sibling.wait_recv()
        for arrival in forwarded:
            arrival.wait_recv()
        for cp in first + passed:
            cp.wait_send()
        mine.wait()
    else:
        _, _, mine_all, waits = _scatter_copies(srcs, *refs)
        for cp in waits:
            cp.wait_recv()
        for cp in waits:
            cp.wait_send()
        mine_all.wait()


def _comm_sources(kind, payload):
    return [payload] if kind == "gather" else list(payload)


def _comm_out_shape(kind, payload):
    srcs = _comm_sources(kind, payload)
    if kind == "gather":
        return jax.ShapeDtypeStruct((N_DEV,) + srcs[0].shape, srcs[0].dtype)
    return jax.ShapeDtypeStruct((N_DEV, sum(a.shape[1] for a in srcs), srcs[0].shape[2]), srcs[0].dtype)


COMM_SEMAPHORES = [pltpu.SemaphoreType.DMA((7,)), pltpu.SemaphoreType.DMA((7,)), pltpu.SemaphoreType.DMA]


def _pallas(body, *, name, grid, in_specs, out_specs, out_shape, semantics, args, scratch_shapes=(), comm=()):
    in_specs, out_specs, out_shape = list(in_specs), list(out_specs), list(out_shape)
    scratch_shapes = list(scratch_shapes)
    n_in, n_out, n_scr, nc = len(in_specs), len(out_specs), len(scratch_shapes), len(comm)
    if not comm:
        return pl.pallas_call(body, name=name, grid=grid, in_specs=in_specs, out_specs=out_specs, out_shape=out_shape,
                              scratch_shapes=scratch_shapes, compiler_params=_cparams(semantics))(*args)
    kinds = [kind for kind, _ in comm]
    sources = [_comm_sources(kind, payload) for kind, payload in comm]
    n_src = sum(len(srcs) for srcs in sources)

    def carrying(*refs):
        ins, rest = refs[:n_in], refs[n_in:]
        cin, rest = list(rest[:n_src]), rest[n_src:]
        outs, rest = rest[:n_out], rest[n_out:]
        cout, rest = rest[:nc], rest[nc:]
        scr, sems = rest[:n_scr], rest[n_scr:]
        ids = [pl.program_id(ax) for ax in range(len(grid))]
        first = functools.reduce(jnp.logical_and, [i == 0 for i in ids])
        last = functools.reduce(jnp.logical_and, [i == g - 1 for i, g in zip(ids, grid)])
        plans = []
        for k in range(nc):
            mine, cin = cin[:len(sources[k])], cin[len(sources[k]):]
            plans.append((kinds[k], mine, cout[k], *sems[3 * k:3 * k + 3]))

        @pl.when(first)
        def _():
            for plan in plans:
                _comm_start(*plan)

        halfway = len(grid) == 1 and grid[0] >= 4 and "gather" in kinds
        if halfway:
            @pl.when(ids[0] == grid[0] // 2)
            def _():
                for plan in plans:
                    if plan[0] == "gather":
                        _gather_pass_on(*plan[1:])

        body(*ins, *outs, *scr)

        @pl.when(last)
        def _():
            for plan in plans:
                _comm_finish(*plan, passed_on=halfway)

    hbm = pl.BlockSpec(memory_space=pl.ANY)
    return pl.pallas_call(
        carrying, name=name, grid=grid,
        in_specs=in_specs + [hbm] * n_src, out_specs=out_specs + [hbm] * nc,
        out_shape=out_shape + [_comm_out_shape(kind, payload) for kind, payload in comm],
        scratch_shapes=scratch_shapes + COMM_SEMAPHORES * nc,
        compiler_params=_cparams(("arbitrary",) * len(grid)),
    )(*args, *[a for srcs in sources for a in srcs])


def _all_gather(arrays, name):
    n = len(arrays)

    def body(*refs):
        plans = [([refs[k]], refs[n + k], *refs[2 * n + 3 * k:2 * n + 3 * k + 3]) for k in range(n)]
        for plan in plans:
            _comm_start("gather", *plan)
        for plan in plans:
            _comm_finish("gather", *plan)

    hbm = pl.BlockSpec(memory_space=pl.ANY)
    return pl.pallas_call(
        body, name=name, out_shape=[_comm_out_shape("gather", a) for a in arrays],
        in_specs=[hbm] * n, out_specs=[hbm] * n, scratch_shapes=COMM_SEMAPHORES * n,
    )(*arrays)


def _adamw_sum(parts, w, m, v, name):
    _, rows, width = parts.shape
    tr = max(t for t in range(SUBLANES, min(rows, TR_ADAM) + 1, SUBLANES) if rows % t == 0)
    c1 = 1.0 - ADAM_B1 ** ADAM_STEP
    c2 = 1.0 - ADAM_B2 ** ADAM_STEP

    def body(p_ref, w_ref, m_ref, v_ref, g_ref, d_ref, nm_ref, nv_ref):
        g = p_ref[0].astype(F32)
        for k in range(1, N_DEV):
            g = g + p_ref[k].astype(F32)
        nm = ADAM_B1 * m_ref[...] + (1.0 - ADAM_B1) * g
        nv = ADAM_B2 * v_ref[...] + (1.0 - ADAM_B2) * (g * g)
        m_hat = nm / c1
        v_hat = nv / c2
        g_ref[...] = g
        d_ref[...] = -ADAM_LR * (m_hat / (jnp.sqrt(v_hat) + ADAM_EPS) + ADAM_WD * w_ref[...])
        nm_ref[...] = nm
        nv_ref[...] = nv

    spec = pl.BlockSpec((tr, width), lambda i: (i, 0))
    out = jax.ShapeDtypeStruct((rows, width), F32)
    return pl.pallas_call(
        body, name=name, grid=(rows // tr,),
        in_specs=[pl.BlockSpec((N_DEV, tr, width), lambda i: (0, i, 0)), spec, spec, spec],
        out_specs=[spec, spec, spec, spec], out_shape=[out, out, out, out],
        compiler_params=_cparams(("parallel",)),
    )(parts, w, m, v)


def _mm(a, wb, name, bias=None, add=None, add_scale=1.0, out_dtype=F32, tm=None, comm=()):
    m, k = a.shape
    nb, k2, tn = wb.shape
    assert k == k2
    tm = _tile(m, TS_MM if tm is None else tm)
    tk = _tile(k, TK_MM)
    nk = k // tk

    def body(*refs):
        a_ref, w_ref = refs[0], refs[1]
        pos = 2
        b_ref = add_ref = None
        if bias is not None:
            b_ref = refs[pos]
            pos += 1
        if add is not None:
            add_ref = refs[pos]
            pos += 1
        o_ref = refs[pos]

        def finish(r):
            if b_ref is not None:
                r = r + b_ref[...]
            if add_ref is not None:
                r = r + add_scale * add_ref[...]
            o_ref[...] = r.astype(o_ref.dtype)

        if nk == 1:
            finish(_dot(a_ref[...], w_ref[...]))
            return
        acc_ref = refs[pos + 1]
        kk = pl.program_id(2)

        @pl.when(kk == 0)
        def _():
            acc_ref[...] = jnp.zeros_like(acc_ref)

        acc_ref[...] += _dot(a_ref[...], w_ref[...])

        @pl.when(kk == nk - 1)
        def _():
            finish(acc_ref[...])

    in_specs = [pl.BlockSpec((tm, tk), lambda j, i, kk: (i, kk)),
                pl.BlockSpec((None, tk, tn), lambda j, i, kk: (j, kk, 0))]
    args = [a, wb]
    if bias is not None:
        in_specs.append(pl.BlockSpec((1, tn), lambda j, i, kk: (0, j)))
        args.append(bias)
    if add is not None:
        in_specs.append(pl.BlockSpec((tm, tn), lambda j, i, kk: (i, j)))
        args.append(add)
    return _pallas(
        body, name=name, grid=(nb, m // tm, nk),
        in_specs=in_specs,
        out_specs=[pl.BlockSpec((tm, tn), lambda j, i, kk: (i, j))],
        out_shape=[jax.ShapeDtypeStruct((m, nb * tn), out_dtype)],
        scratch_shapes=[pltpu.VMEM((tm, tn), F32)] if nk > 1 else [],
        semantics=("parallel", "parallel", "arbitrary"), args=args, comm=comm)


def _mm_tn(a, b, tn, name, tk=None, comm=()):
    s, k = a.shape
    s2, n = b.shape
    assert s == s2 and n % tn == 0
    nb = n // tn
    ts = _tile(s, TS_MM_TN)
    tk = k if tk is None else tk
    assert k % tk == 0

    def body(a_ref, b_ref, o_ref):
        @pl.when(pl.program_id(2) == 0)
        def _():
            o_ref[...] = jnp.zeros_like(o_ref)

        o_ref[...] += _dot_tn(a_ref[...], b_ref[...])

    return _pallas(
        body, name=name, grid=(nb, k // tk, s // ts),
        in_specs=[pl.BlockSpec((ts, tk), lambda j, kb, i: (i, kb)),
                  pl.BlockSpec((ts, tn), lambda j, kb, i: (i, j))],
        out_specs=[pl.BlockSpec((None, tk, tn), lambda j, kb, i: (j, kb, 0))],
        out_shape=[jax.ShapeDtypeStruct((nb, k, tn), F32)],
        semantics=("parallel", "parallel", "arbitrary"), args=(a, b), comm=comm)


def _scan_fwd(a_ref, b_ref, h_ref, carry_ref, ts):
    rowid = lax.broadcasted_iota(jnp.int32, (SUBLANES, 1), 0)

    def group(gi, hprev):
        r0 = pl.multiple_of(gi * SUBLANES, SUBLANES)
        a = a_ref[pl.ds(r0, SUBLANES), :]
        b = b_ref[pl.ds(r0, SUBLANES), :]
        for d in (1, 2, 4):
            a_sh = jnp.where(rowid >= d, pltpu.roll(a, d, 0), 1.0)
            b_sh = jnp.where(rowid >= d, pltpu.roll(b, d, 0), 0.0)
            b = a * b_sh + b
            a = a * a_sh
        hh = a * hprev + b
        h_ref[pl.ds(r0, SUBLANES), :] = hh
        return hh[SUBLANES - 1:SUBLANES, :]

    last = lax.fori_loop(0, ts // SUBLANES, group, carry_ref[0:1, :])
    carry_ref[0:1, :] = last


def _scan_rev(c_ref, b_ref, g_ref, carry_ref, ts):
    rowid = lax.broadcasted_iota(jnp.int32, (SUBLANES, 1), 0)
    ng = ts // SUBLANES

    def group(gi, gnext):
        r0 = pl.multiple_of((ng - 1 - gi) * SUBLANES, SUBLANES)
        c = c_ref[pl.ds(r0, SUBLANES), :]
        b = b_ref[pl.ds(r0, SUBLANES), :]
        for d in (1, 2, 4):
            keep = rowid < SUBLANES - d
            c_sh = jnp.where(keep, pltpu.roll(c, SUBLANES - d, 0), 1.0)
            b_sh = jnp.where(keep, pltpu.roll(b, SUBLANES - d, 0), 0.0)
            b = c * b_sh + b
            c = c * c_sh
        gg = c * gnext + b
        g_ref[pl.ds(r0, SUBLANES), :] = gg
        return gg[0:1, :]

    first = lax.fori_loop(0, ng, group, carry_ref[0:1, :])
    carry_ref[0:1, :] = first


def _past(ext, sh, ts):
    if sh == 0:
        return ext[HALO:HALO + ts]
    return pltpu.roll(ext, sh, 0)[HALO:HALO + ts]


def _future(ext, sh, ts):
    if sh == 0:
        return ext[0:ts]
    return pltpu.roll(ext, ts + HALO - sh, 0)[0:ts]


def _one_minus_sq(a, log_a):
    x = 2.0 * log_a
    series = -x * (1.0 + x * (0.5 + x * (1.0 / 6.0 + x * (1.0 / 24.0))))
    return jnp.where(x > -0.02, series, 1.0 - a * a)


def _halo_index(ts):
    blocks = ts // HALO
    return lambda t: (jnp.maximum(t * blocks - 1, 0), 0)


def _head_columns(d):
    cw = d // LRU_HEADS
    return [slice(c * cw, (c + 1) * cw) for c in range(LRU_HEADS)]


def _shift(cs, off):
    return slice(cs.start + off, cs.stop + off)


def _mixer_fwd(x, win, b_in, pw, wr, wi, wlo, wsc, wmix, vec, name, comm=()):
    s, d = x.shape
    ts = _tile(s, TS_MIXER)
    nt = s // ts
    dg = d // len(POOL_WINDOWS)
    nblk = win.shape[0]
    cols = _head_columns(d)

    def body(x_ref, xn_ref, win_ref, bin_ref, pw_ref, wr_ref, wi_ref, wlo_ref, wsc_ref, wmix_ref, vec_ref,
             z_hbm, x1_ref, rpre_ref, h_ref, ypre_ref, yl_ref, yc_ref, mg_ref, e_ref, p_ref, v_ref, r_ref, ig_ref,
             cq_ref, z_even, z_odd, zhist, a_scr, b_scr, hcarry, z_sem):
        i = pl.program_id(0)
        first = i == 0

        def project_block(xb, dst, k):
            dst[:, k * d:(k + 1) * d] = _dot(xb, win_ref[k]) + bin_ref[:, k * d:(k + 1) * d]

        @pl.when(first)
        def _():
            hcarry[...] = jnp.zeros_like(hcarry)
            zhist[...] = jnp.zeros_like(zhist)
            xb = x_ref[...].astype(BF16)
            for k in range(nblk):
                project_block(xb, z_even, k)

        def vrow(k, cs):
            return vec_ref[k:k + 1, cs]

        def step(zc, zn):
            z_out = pltpu.make_async_copy(zc, z_hbm.at[pl.ds(pl.multiple_of(i * ts, ts), ts), :], z_sem)
            z_out.start()
            xb = xn_ref[...].astype(BF16)
            tglob = i * ts + lax.broadcasted_iota(jnp.int32, (ts, 1), 0)
            sp = _softplus(-vec_ref[V_LAM:V_LAM + 1, :])

            def with_history(k, cs):
                kc = _shift(cs, k * d)
                return jnp.concatenate([jnp.where(first, 0.0, zhist[:, kc]), zc[:, kc]], axis=0)

            for hh, cs in enumerate(cols):
                if hh < nblk:
                    project_block(xb, zn, hh)
                win_len = POOL_WINDOWS[cs.start // dg]
                ext = with_history(0, cs)
                sm = ext
                sh = 1
                while sh < win_len:
                    sm = sm + pltpu.roll(sm, sh, 0)
                    sh *= 2
                inv_cnt = 1.0 / jnp.minimum(tglob + 1, win_len).astype(F32)
                p_ref[:, cs] = (sm[HALO:HALO + ts] * inv_cnt - ext[HALO:HALO + ts]).astype(BF16)
                ext = with_history(1, cs)
                v = vrow(V_CB, cs)
                for j in range(LRU_CONV):
                    v = v + vrow(V_CW + j, cs) * _past(ext, LRU_CONV - 1 - j, ts)
                r = _sigmoid(_dot(v, wr_ref[hh]) + vrow(V_BR, cs))
                ig = _sigmoid(_dot(v, wi_ref[hh]) + vrow(V_BI, cs))
                log_a = -LRU_C * r * sp[:, cs]
                a = jnp.exp(log_a)
                a_scr[:, cs] = a
                b_scr[:, cs] = jnp.sqrt(_one_minus_sq(a, log_a)) * (ig * v)
                v_ref[:, cs] = v
                r_ref[:, cs] = r
                ig_ref[:, cs] = ig
                ext = with_history(3, cs) * with_history(4, cs)
                cq = jnp.zeros((ts, cs.stop - cs.start), F32)
                for j in range(SCONV_K):
                    cq = cq + vrow(V_SW + j, cs) * _past(ext, SCONV_K - 1 - j, ts)
                cq_ref[:, cs] = cq
                e_ref[:, cs] = (zc[:, _shift(cs, 2 * d)] * cq).astype(BF16)
            for k in range(len(cols), nblk):
                project_block(xb, zn, k)
            _scan_fwd(a_scr, b_scr, h_ref, hcarry, ts)
            ypre = jnp.concatenate([_dot(p_ref[:, g * dg:(g + 1) * dg], pw_ref[g])
                                    for g in range(len(POOL_WINDOWS))], axis=1)
            yl = _dot(h_ref[...], wlo_ref[...])
            yc = _dot(e_ref[...], wsc_ref[...])
            ypre_ref[...] = ypre
            yl_ref[...] = yl
            yc_ref[...] = yc
            for cs in cols:
                merged = (_sigmoid(zc[:, _shift(cs, 5 * d)]) * (ypre[:, cs] * vrow(V_PSCALE, cs))
                          + _sigmoid(zc[:, _shift(cs, 6 * d)]) * yl[:, cs]
                          + _sigmoid(zc[:, _shift(cs, 7 * d)]) * yc[:, cs])
                mg_ref[:, cs] = merged.astype(BF16)
            rpre = ALPHA * x_ref[...] + _dot(mg_ref[...], wmix_ref[...])
            x1_ref[...] = _ln_fwd(rpre, vec_ref[V_G:V_G + 1, :], vec_ref[V_B:V_B + 1, :])
            rpre_ref[...] = rpre
            zhist[...] = zc[ts - HALO:ts, :]
            z_out.wait()

        parity = lax.rem(i, 2)

        @pl.when(parity == 0)
        def _():
            step(z_even, z_odd)

        @pl.when(parity == 1)
        def _():
            step(z_odd, z_even)

    tile = pl.BlockSpec((ts, d), lambda t: (t, 0))
    f32o = jax.ShapeDtypeStruct((s, d), F32)
    bfo = jax.ShapeDtypeStruct((s, d), BF16)
    consts = (win, b_in, pw, wr, wi, wlo, wsc, wmix, vec)
    return _pallas(
        body, name=name, grid=(nt,),
        in_specs=[tile, pl.BlockSpec((ts, d), lambda t: (jnp.minimum(t + 1, nt - 1), 0))]
        + [_const_spec(c.shape) for c in consts],
        out_specs=[pl.BlockSpec(memory_space=pl.ANY)] + [tile] * 13,
        out_shape=[jax.ShapeDtypeStruct((s, nblk * d), F32), f32o, f32o, f32o, f32o, f32o, f32o, bfo, bfo, bfo,
                   f32o, f32o, f32o, f32o],
        scratch_shapes=[pltpu.VMEM((ts, nblk * d), F32)] * 2 + [pltpu.VMEM((HALO, nblk * d), F32)]
        + [pltpu.VMEM((ts, d), F32)] * 2 + [pltpu.VMEM((SUBLANES, d), F32), pltpu.SemaphoreType.DMA],
        semantics=("arbitrary",), args=(x, x, *consts), comm=comm)


def _mixer_bwd(dx1, rpre, z, h, ypre, yl, yc, pp, vv, rr, ii, cq, pwt, wrt, wit, wlot, wsct, wmixt, vec, name,
               comm=()):
    s, d = dx1.shape
    ts = _tile(s, TS_MIXER)
    nt = s // ts
    dg = d // len(POOL_WINDOWS)
    cols = _head_columns(d)

    def body(dx1_ref, rpre_ref, z_ref, h_ref, hh_ref, ypre_ref, yl_ref, yc_ref, p_ref, v_ref, r_ref, ig_ref, cq_ref,
             pwt_ref, wrt_ref, wit_ref, wlot_ref, wsct_ref, wmixt_ref, vec_ref,
             dz_ref, dr_ref, dyl_ref, dyc_ref, acc_ref, dbin_ref, dpw_ref, dwr_ref, dwi_ref,
             c_scr, b_scr, g_scr, dyps_scr, a_keep, m_keep, gcarry, acarry, dcq_c, dv_c, m_c):
        i = pl.program_id(0)
        t = nt - 1 - i

        @pl.when(i == 0)
        def _():
            for ref in (gcarry, acarry, dcq_c, dv_c, m_c, acc_ref, dbin_ref, dpw_ref, dwr_ref, dwi_ref):
                ref[...] = jnp.zeros_like(ref)

        def vrow(k, cs):
            return vec_ref[k:k + 1, cs]

        def zc(k, cs):
            return z_ref[:, _shift(cs, k * d)]

        def acc(row, cs, val):
            acc_ref[row:row + 1, cs] += _colsum(val)

        def emit_dz(k, cs, val):
            kc = _shift(cs, k * d)
            dz_ref[:, kc] = val.astype(BF16)
            dbin_ref[:, kc] += _colsum(val)

        def with_future(tile_val, carry_ref, cs):
            ext = jnp.concatenate([tile_val, carry_ref[:, cs]], axis=0)
            carry_ref[:, cs] = tile_val[0:HALO, :]
            return ext

        dx1v = dx1_ref[...]
        dr, dyy = _ln_bwd(dx1v, rpre_ref[...], vec_ref[V_G:V_G + 1, :])
        acc_ref[A_G:A_G + 1, :] += _colsum(dyy)
        acc_ref[A_B:A_B + 1, :] += _colsum(dx1v)
        dr_ref[...] = dr
        dmg = _dot(dr, wmixt_ref[...])
        for cs in cols:
            dm = dmg[:, cs]
            ypre = ypre_ref[:, cs]
            ys = (ypre * vrow(V_PSCALE, cs), yl_ref[:, cs], yc_ref[:, cs])
            dys = []
            for k in range(3):
                gk = _sigmoid(zc(5 + k, cs))
                emit_dz(5 + k, cs, dm * ys[k] * gk * (1.0 - gk))
                dys.append(dm * gk)
            acc(A_PSCALE, cs, dys[0] * ypre)
            dyps_scr[:, cs] = dys[0] * vrow(V_PSCALE, cs)
            dyl_ref[:, cs] = dys[1].astype(BF16)
            dyc_ref[:, cs] = dys[2].astype(BF16)
        de = _dot(dyc_ref[...], wsct_ref[...])
        dh = _dot(dyl_ref[...], wlot_ref[...])

        sp = _softplus(-vec_ref[V_LAM:V_LAM + 1, :])
        for cs in cols:
            dec = de[:, cs]
            emit_dz(2, cs, dec * cq_ref[:, cs])
            dcq_ext = with_future(dec * zc(2, cs), dcq_c, cs)
            zcc, zh = zc(3, cs), zc(4, cs)
            qv = zcc * zh
            dq = jnp.zeros_like(qv)
            for j in range(SCONV_K):
                adv = _future(dcq_ext, SCONV_K - 1 - j, ts)
                acc(A_SW + j, cs, adv * qv)
                dq = dq + vrow(V_SW + j, cs) * adv
            emit_dz(3, cs, dq * zh)
            emit_dz(4, cs, dq * zcc)
            log_a = -LRU_C * r_ref[:, cs] * sp[:, cs]
            a = jnp.exp(log_a)
            a_keep[:, cs] = a
            m_keep[:, cs] = jnp.sqrt(_one_minus_sq(a, log_a))
            c_scr[:, cs] = _future(with_future(a, acarry, cs), 1, ts)
            b_scr[:, cs] = dh[:, cs]
        _scan_rev(c_scr, b_scr, g_scr, gcarry, ts)

        for hh, cs in enumerate(cols):
            gs, a, mult = g_scr[:, cs], a_keep[:, cs], m_keep[:, cs]
            r, ig, v = r_ref[:, cs], ig_ref[:, cs], v_ref[:, cs]
            hprev = _past(jnp.concatenate([jnp.where(t == 0, 0.0, hh_ref[:, cs]), h_ref[:, cs]], axis=0), 1, ts)
            iv = ig * v
            dlog_a = gs * hprev * a + gs * iv * (-(a * a) / mult)
            div = gs * mult
            acc(A_SP, cs, dlog_a * (-LRU_C) * r)
            dpre_r = dlog_a * (-LRU_C) * sp[:, cs] * r * (1.0 - r)
            dpre_i = div * v * ig * (1.0 - ig)
            acc(A_BR, cs, dpre_r)
            acc(A_BI, cs, dpre_i)
            dv = div * ig + _dot(dpre_r, wrt_ref[hh]) + _dot(dpre_i, wit_ref[hh])
            dwr_ref[hh] += _dot_tn(v, dpre_r)
            dwi_ref[hh] += _dot_tn(v, dpre_i)
            acc(A_CB, cs, dv)
            dv_ext = with_future(dv, dv_c, cs)
            zl = zc(1, cs)
            dzl = jnp.zeros_like(zl)
            for j in range(LRU_CONV):
                adv = _future(dv_ext, LRU_CONV - 1 - j, ts)
                acc(A_CW + j, cs, adv * zl)
                dzl = dzl + vrow(V_CW + j, cs) * adv
            emit_dz(1, cs, dzl)

        tglob = t * ts + lax.broadcasted_iota(jnp.int32, (ts, 1), 0)
        for g, win_len in enumerate(POOL_WINDOWS):
            cs = slice(g * dg, (g + 1) * dg)
            dyps = dyps_scr[:, cs]
            dpw_ref[g] += _dot_tn(p_ref[:, cs], dyps)
            dp = _dot(dyps, pwt_ref[g])
            inv_cnt = 1.0 / jnp.minimum(tglob + 1, win_len).astype(F32)
            sm = with_future(dp * inv_cnt, m_c, cs)
            sh = 1
            while sh < win_len:
                sm = sm + pltpu.roll(sm, ts + HALO - sh, 0)
                sh *= 2
            emit_dz(0, cs, sm[0:ts] - dp)

    def rev(tt):
        return (nt - 1 - tt, 0)

    halo = _halo_index(ts)
    tile = pl.BlockSpec((ts, d), rev)
    hspec = pl.BlockSpec((HALO, d), lambda tt: halo(nt - 1 - tt))
    f32o = jax.ShapeDtypeStruct((s, d), F32)
    bfo = jax.ShapeDtypeStruct((s, d), BF16)
    consts = (pwt, wrt, wit, wlot, wsct, wmixt, vec)
    return _pallas(
        body, name=name, grid=(nt,),
        in_specs=[tile, tile, pl.BlockSpec((ts, 8 * d), rev), tile, hspec] + [tile] * 8
        + [_const_spec(c.shape) for c in consts],
        out_specs=[pl.BlockSpec((ts, 8 * d), rev), tile, tile, tile,
                   _acc_spec((A_ROWS, d)), _acc_spec((1, 8 * d)),
                   _acc_spec(pwt.shape), _acc_spec(wrt.shape), _acc_spec(wit.shape)],
        out_shape=[jax.ShapeDtypeStruct((s, 8 * d), BF16), f32o, bfo, bfo,
                   jax.ShapeDtypeStruct((A_ROWS, d), F32), jax.ShapeDtypeStruct((1, 8 * d), F32),
                   jax.ShapeDtypeStruct(pwt.shape, F32), jax.ShapeDtypeStruct(wrt.shape, F32),
                   jax.ShapeDtypeStruct(wit.shape, F32)],
        scratch_shapes=[pltpu.VMEM((ts, d), F32)] * 6 + [pltpu.VMEM((SUBLANES, d), F32)]
        + [pltpu.VMEM((HALO, d), F32)] * 4,
        semantics=("arbitrary",), args=(dx1, rpre, z, h, h, ypre, yl, yc, pp, vv, rr, ii, cq, *consts), comm=comm)


def _softmax_rows(sc):
    mx = jnp.max(sc, axis=-1, keepdims=True)
    ex = jnp.exp(sc - mx)
    return ex * (1.0 / jnp.sum(ex, axis=-1, keepdims=True))


def _attn_fwd(x1, wq, wo, kt, vv, vec, name):
    s, d = x1.shape
    ts = _tile(s, TS_ATTN)
    hd = d // X_HEADS
    scale = hd ** -0.5

    def body(x_ref, wq_ref, wo_ref, kt_ref, v_ref, vec_ref, x2_ref, rpre_ref, q_ref, o_ref):
        xv = x_ref[...]
        q = _dot(xv, wq_ref[...]).astype(BF16)
        q_ref[...] = q
        for hh in range(X_HEADS):
            cs = slice(hh * hd, (hh + 1) * hd)
            p = _softmax_rows(_dot(q[:, cs], kt_ref[cs, :]) * scale)
            o_ref[:, cs] = _dot(p, v_ref[:, cs]).astype(BF16)
        rpre = ALPHA * xv + _dot(o_ref[...], wo_ref[...])
        rpre_ref[...] = rpre
        x2_ref[...] = _ln_fwd(rpre, vec_ref[V_G + 1:V_G + 2, :], vec_ref[V_B + 1:V_B + 2, :])

    tile = pl.BlockSpec((ts, d), lambda t: (t, 0))
    f32o = jax.ShapeDtypeStruct((s, d), F32)
    bfo = jax.ShapeDtypeStruct((s, d), BF16)
    consts = (wq, wo, kt, vv, vec)
    return pl.pallas_call(
        body, name=name, grid=(s // ts,),
        in_specs=[tile] + [_const_spec(c.shape) for c in consts],
        out_specs=[tile] * 4, out_shape=[f32o, f32o, bfo, bfo],
        compiler_params=_cparams(("parallel",)),
    )(x1, *consts)


def _attn_bwd(dx2, rpre, q, wqt, wot, kk, kt, vt, vec, name):
    s, d = dx2.shape
    ts = _tile(s, TS_ATTN)
    nm = kk.shape[0]
    hd = d // X_HEADS
    scale = hd ** -0.5

    def body(dx2_ref, rpre_ref, q_ref, wqt_ref, wot_ref, k_ref, kt_ref, vt_ref, vec_ref,
             dx1_ref, dq_ref, dr_ref, dk_ref, dv_ref, ln_ref):
        @pl.when(pl.program_id(0) == 0)
        def _():
            for ref in (dk_ref, dv_ref, ln_ref):
                ref[...] = jnp.zeros_like(ref)

        dyv = dx2_ref[...]
        dr, dyy = _ln_bwd(dyv, rpre_ref[...], vec_ref[V_G + 1:V_G + 2, :])
        ln_ref[0:1, :] += _colsum(dyy)
        ln_ref[1:2, :] += _colsum(dyv)
        dr_ref[...] = dr.astype(BF16)
        do = _dot(dr, wot_ref[...])
        q = q_ref[...]
        for hh in range(X_HEADS):
            cs = slice(hh * hd, (hh + 1) * hd)
            p = _softmax_rows(_dot(q[:, cs], kt_ref[cs, :]) * scale)
            dp = _dot(do[:, cs], vt_ref[cs, :])
            ds = p * (dp - jnp.sum(dp * p, axis=-1, keepdims=True)) * scale
            dq_ref[:, cs] = _dot(ds, k_ref[:, cs]).astype(BF16)
            dk_ref[:, cs] += _dot_tn(ds, q[:, cs])
            dv_ref[:, cs] += _dot_tn(p, do[:, cs])
        dx1_ref[...] = ALPHA * dr + _dot(dq_ref[...], wqt_ref[...])

    tile = pl.BlockSpec((ts, d), lambda t: (t, 0))
    consts = (wqt, wot, kk, kt, vt, vec)
    return pl.pallas_call(
        body, name=name, grid=(s // ts,),
        in_specs=[tile, tile, tile] + [_const_spec(c.shape) for c in consts],
        out_specs=[tile, tile, tile, _acc_spec((nm, d)), _acc_spec((nm, d)), _acc_spec((2, d))],
        out_shape=[jax.ShapeDtypeStruct((s, d), F32), jax.ShapeDtypeStruct((s, d), BF16),
                   jax.ShapeDtypeStruct((s, d), BF16), jax.ShapeDtypeStruct((nm, d), F32),
                   jax.ShapeDtypeStruct((nm, d), F32), jax.ShapeDtypeStruct((2, d), F32)],
        compiler_params=_cparams(("arbitrary",)),
    )(dx2, rpre, q, *consts)


def _ffn_out(x2, hgu, wd, vec, name, target=None):
    s, d = x2.shape
    ff = wd.shape[0]
    ts = _tile(s, TS_FFN)
    chunk = 2 * 128 if ff % (2 * 128) == 0 else ff

    def body(*refs):
        if target is None:
            x_ref, hgu_ref, wd_ref, vec_ref, out_ref, rpre_ref, act_ref = refs
        else:
            x_ref, hgu_ref, wd_ref, vec_ref, t_ref, out_ref, rpre_ref, act_ref, loss_ref = refs
        for c in range(0, ff, chunk):
            hg = hgu_ref[:, c:c + chunk]
            act_ref[:, c:c + chunk] = (hg * _sigmoid(hg) * hgu_ref[:, ff + c:ff + c + chunk]).astype(BF16)
        rpre = ALPHA * x_ref[...] + _dot(act_ref[...], wd_ref[...])
        rpre_ref[...] = rpre
        x3 = _ln_fwd(rpre, vec_ref[V_G + 2:V_G + 3, :], vec_ref[V_B + 2:V_B + 3, :])
        if target is None:
            out_ref[...] = x3
            return

        @pl.when(pl.program_id(0) == 0)
        def _():
            loss_ref[...] = jnp.zeros_like(loss_ref)

        err = x3 - t_ref[...]
        out_ref[...] = err / d
        per_token = jnp.mean(err * err, axis=-1, keepdims=True)
        loss_ref[...] += 0.5 * jnp.sum(per_token, axis=0, keepdims=True)

    tile = pl.BlockSpec((ts, d), lambda t: (t, 0))
    f32o = jax.ShapeDtypeStruct((s, d), F32)
    in_specs = [tile, pl.BlockSpec((ts, 2 * ff), lambda t: (t, 0)), _const_spec(wd.shape), _const_spec(vec.shape)]
    out_specs = [tile, tile, pl.BlockSpec((ts, ff), lambda t: (t, 0))]
    out_shape = [f32o, f32o, jax.ShapeDtypeStruct((s, ff), BF16)]
    args = [x2, hgu, wd, vec]
    if target is not None:
        in_specs.append(tile)
        args.append(target)
        out_specs.append(_acc_spec((1, 1)))
        out_shape.append(jax.ShapeDtypeStruct((1, 1), F32))
    return pl.pallas_call(
        body, name=name, grid=(s // ts,), in_specs=in_specs, out_specs=out_specs, out_shape=out_shape,
        compiler_params=_cparams(("parallel",) if target is None else ("arbitrary",)),
    )(*args)


def _ffn_bwd(dy, rpre, hgu, wdt, wgt, wut, vec, name, comm=()):
    s, d = dy.shape
    ff = wgt.shape[0]
    ts = _tile(s, TS_FFN)
    chunk = 2 * 128 if ff % (2 * 128) == 0 else ff

    def body(dy_ref, rpre_ref, hgu_ref, wdt_ref, wgt_ref, wut_ref, vec_ref, dx_ref, dr_ref, dhgu_ref, ln_ref):
        @pl.when(pl.program_id(0) == 0)
        def _():
            ln_ref[...] = jnp.zeros_like(ln_ref)

        dyv = dy_ref[...]
        dr, dyy = _ln_bwd(dyv, rpre_ref[...], vec_ref[V_G + 2:V_G + 3, :])
        ln_ref[0:1, :] += _colsum(dyy)
        ln_ref[1:2, :] += _colsum(dyv)
        dr_ref[...] = dr.astype(BF16)
        dact = _dot(dr, wdt_ref[...])
        for c in range(0, ff, chunk):
            hg = hgu_ref[:, c:c + chunk]
            hu = hgu_ref[:, ff + c:ff + c + chunk]
            da = dact[:, c:c + chunk]
            sg = _sigmoid(hg)
            dhgu_ref[:, c:c + chunk] = (da * hu * (sg * (1.0 + hg * (1.0 - sg)))).astype(BF16)
            dhgu_ref[:, ff + c:ff + c + chunk] = (da * hg * sg).astype(BF16)
        dx_ref[...] = (ALPHA * dr + _dot(dhgu_ref[:, 0:ff], wgt_ref[...])
                       + _dot(dhgu_ref[:, ff:2 * ff], wut_ref[...]))

    tile = pl.BlockSpec((ts, d), lambda t: (t, 0))
    wide = pl.BlockSpec((ts, 2 * ff), lambda t: (t, 0))
    consts = (wdt, wgt, wut, vec)
    return _pallas(
        body, name=name, grid=(s // ts,),
        in_specs=[tile, tile, wide] + [_const_spec(c.shape) for c in consts],
        out_specs=[tile, tile, wide, _acc_spec((2, d))],
        out_shape=[jax.ShapeDtypeStruct((s, d), F32), jax.ShapeDtypeStruct((s, d), BF16),
                   jax.ShapeDtypeStruct((s, 2 * ff), BF16), jax.ShapeDtypeStruct((2, d), F32)],
        semantics=("arbitrary",), args=(dy, rpre, hgu, *consts), comm=comm)


SHARD_AXIS = {"w_in": 1, "pool_w": 1, "lru_w_out": 0, "sconv_w_out": 0, "w_mix_out": 0,
              "xa_w_q": 0, "xa_w_k": 0, "xa_w_v": 0, "xa_w_o": 0,
              "ffn_w_gate": 0, "ffn_w_up": 0, "ffn_w_down": 0,
              "lru_conv_w": 1, "sconv_w": 1, "ln_g": 1, "ln_b": 1}
STORED_TRANSPOSED = ("ffn_w_gate", "ffn_w_up")
GROUP_IN = ("w_in",)
GROUP_MIXER = ("pool_w", "lru_w_out", "sconv_w_out", "w_mix_out")
GROUP_ATTN = ("xa_w_q", "xa_w_k", "xa_w_v", "xa_w_o")
GROUP_FFN = ("ffn_w_gate", "ffn_w_up", "ffn_w_down")
GROUP_VECTORS = ("lru_conv_w", "sconv_w", "ln_g", "ln_b")
REPLICATED = ("b_in", "pool_scale", "lru_conv_b", "lru_w_r", "lru_b_r", "lru_w_i", "lru_b_i", "lru_lambda")
WEIGHTS = ("w_in", "b_in", "pool_w", "pool_scale", "lru_conv_w", "lru_conv_b", "lru_w_r", "lru_b_r", "lru_w_i",
           "lru_b_i", "lru_lambda", "lru_w_out", "sconv_w", "sconv_w_out", "w_mix_out", "xa_w_q", "xa_w_k",
           "xa_w_v", "xa_w_o", "ffn_w_gate", "ffn_w_up", "ffn_w_down", "ln_g", "ln_b")


def _pack(arrs, width, lead=0, row_multiple=ROW_PAD):
    head = arrs[0].shape[:lead]
    flat = jnp.concatenate([a.reshape(head + (-1,)) for a in arrs], axis=lead)
    n = flat.shape[-1]
    chunk = width * row_multiple
    total = -(-n // chunk) * chunk
    if total != n:
        flat = jnp.pad(flat, [(0, 0)] * lead + [(0, total - n)])
    return flat.reshape(head + (total // width, width))


def _unpack(buf, shapes, lead=0):
    head = buf.shape[:lead]
    flat = buf.reshape(head + (-1,))
    out, off = [], 0
    for shp in shapes:
        n = math.prod(shp)
        out.append(flat[..., off:off + n].reshape(head + tuple(shp)))
        off += n
    return out


def _split8(a, axis):
    shp = a.shape
    a = a.reshape(shp[:axis] + (N_DEV, shp[axis] // N_DEV) + shp[axis + 1:])
    return jnp.moveaxis(a, axis, 0)


def _join8(a, axis):
    a = jnp.moveaxis(a, 0, axis)
    shp = a.shape
    return a.reshape(shp[:axis] + (shp[axis] * shp[axis + 1],) + shp[axis + 2:])


def _t(a):
    return jnp.swapaxes(a, -1, -2)


def _stored(name, a):
    return _t(a) if name in STORED_TRANSPOSED else a


def kernel(x, mem, w_in, b_in, pool_w, pool_scale, lru_conv_w, lru_conv_b, lru_w_r, lru_b_r, lru_w_i, lru_b_i, lru_lambda, lru_w_out, sconv_w, sconv_w_out, w_mix_out, xa_w_q, xa_w_k, xa_w_v, xa_w_o, ffn_w_gate, ffn_w_up, ffn_w_down, ln_g, ln_b, loss_target, m_w_in, m_b_in, m_pool_w, m_pool_scale, m_lru_conv_w, m_lru_conv_b, m_lru_w_r, m_lru_b_r, m_lru_w_i, m_lru_b_i, m_lru_lambda, m_lru_w_out, m_sconv_w, m_sconv_w_out, m_w_mix_out, m_xa_w_q, m_xa_w_k, m_xa_w_v, m_xa_w_o, m_ffn_w_gate, m_ffn_w_up, m_ffn_w_down, m_ln_g, m_ln_b, v_w_in, v_b_in, v_pool_w, v_pool_scale, v_lru_conv_w, v_lru_conv_b, v_lru_w_r, v_lru_b_r, v_lru_w_i, v_lru_b_i, v_lru_lambda, v_lru_w_out, v_sconv_w, v_sconv_w_out, v_w_mix_out, v_xa_w_q, v_xa_w_k, v_xa_w_v, v_xa_w_o, v_ffn_w_gate, v_ffn_w_up, v_ffn_w_down, v_ln_g, v_ln_b):
    args = dict(locals())
    w = {n: args[n] for n in WEIGHTS}
    mom_m = {n: args["m_" + n] for n in WEIGHTS}
    mom_v = {n: args["v_" + n] for n in WEIGHTS}
    depth = w_in.shape[0]
    s, d = x.shape[1], x.shape[2]
    nm = mem.shape[1]
    ff = ffn_w_gate.shape[2] * N_DEV
    xs = x.reshape(s, d)
    mems = mem.reshape(nm, d)
    target = loss_target.reshape(s, d)

    def shard(t, n, l):
        return _stored(n, t[n][l])

    def pack_shards(t, names, l, dtype=None):
        arrs = [shard(t, n, l) for n in names]
        return _pack([a if dtype is None else a.astype(dtype) for a in arrs], d)

    def unpack_gathered(buf, names):
        pieces = _unpack(buf, [shard(w, n, 0).shape for n in names], lead=1)
        return {n: (p if n == "w_in" else _join8(p, SHARD_AXIS[n])) for n, p in zip(names, pieces)}

    def layer_vec(l, fw):
        vec = jnp.zeros((V_ROWS, d), F32)
        vec = vec.at[V_PSCALE].set(pool_scale[l]).at[V_CW:V_CW + LRU_CONV].set(fw["lru_conv_w"])
        vec = vec.at[V_CB].set(lru_conv_b[l]).at[V_BR].set(lru_b_r[l]).at[V_BI].set(lru_b_i[l])
        vec = vec.at[V_LAM].set(lru_lambda[l]).at[V_SW:V_SW + SCONV_K].set(fw["sconv_w"])
        return vec.at[V_G:V_G + 3].set(fw["ln_g"]).at[V_B:V_B + 3].set(fw["ln_b"])

    def layer_params(l, fw):
        return dict(
            vec=layer_vec(l, fw), wint=_t(fw["w_in"]).reshape(1, 8 * d, d),
            pw=fw["pool_w"], pwt=_t(fw["pool_w"]),
            wr=lru_w_r[l].astype(BF16), wi=lru_w_i[l].astype(BF16),
            wrt=_t(lru_w_r[l]).astype(BF16), wit=_t(lru_w_i[l]).astype(BF16),
            wlo=fw["lru_w_out"], wlot=_t(fw["lru_w_out"]),
            wsc=fw["sconv_w_out"], wsct=_t(fw["sconv_w_out"]),
            wmix=fw["w_mix_out"], wmixt=_t(fw["w_mix_out"]),
            wq=fw["xa_w_q"], wqt=_t(fw["xa_w_q"]), wo=fw["xa_w_o"], wot=_t(fw["xa_w_o"]),
            wkv=jnp.stack([fw["xa_w_k"], fw["xa_w_v"]]),
            wgu=jnp.stack([_t(fw["ffn_w_gate"]), _t(fw["ffn_w_up"])]),
            wgt=fw["ffn_w_gate"], wut=fw["ffn_w_up"],
            wd=fw["ffn_w_down"], wdt=_t(fw["ffn_w_down"]))

    later = GROUP_ATTN + GROUP_FFN
    first_weights, vectors = _all_gather(
        [pack_shards(w, GROUP_IN + GROUP_MIXER, 0, BF16),
         _pack([shard(w, n, l) for l in range(depth) for n in GROUP_VECTORS], d)], "gather_first")
    fw0 = unpack_gathered(first_weights, GROUP_IN + GROUP_MIXER)
    vec_pieces = _unpack(vectors, [shard(w, n, l).shape for l in range(depth) for n in GROUP_VECTORS], lead=1)
    fvec = [{n: _join8(vec_pieces[l * len(GROUP_VECTORS) + k], SHARD_AXIS[n]) for k, n in enumerate(GROUP_VECTORS)}
            for l in range(depth)]

    layers, saved = [], []
    cur = xs
    fw_next = None
    for l in range(depth):
        fw = dict(fw0 if l == 0 else fw_next)
        fw.update(fvec[l])
        comm = []
        if l == 0:
            comm.append(("gather", pack_shards(w, later, 0, BF16)))
        if l + 1 < depth:
            comm.append(("gather", pack_shards(w, GROUP_IN + GROUP_MIXER + later, l + 1, BF16)))
        z, x1, rpre1, h, ypre, yl, yc, merged, e, pp, vb, rb, ib, cq, *got = _mixer_fwd(
            cur, fw["w_in"], b_in[l].reshape(1, 8 * d), fw["pool_w"], lru_w_r[l].astype(BF16),
            lru_w_i[l].astype(BF16), fw["lru_w_out"], fw["sconv_w_out"], fw["w_mix_out"],
            layer_vec(l, fw), f"mixer_fwd_{l}", comm=comm)
        if l == 0:
            fw.update(unpack_gathered(got.pop(0), later))
        if l + 1 < depth:
            fw_next = unpack_gathered(got.pop(0), GROUP_IN + GROUP_MIXER + later)
        p = layer_params(l, fw)
        kv = _mm(mems, p["wkv"], f"kv_{l}")[0]
        kk = kv[:, :d].astype(BF16)
        vv = kv[:, d:].astype(BF16)
        x2, rpre2, q, o = _attn_fwd(x1, p["wq"], p["wo"], _t(kk), vv, p["vec"], f"attn_fwd_{l}")
        hgu = _mm(x2, p["wgu"], f"ffn_in_{l}", tm=TS_MM // 2)[0]
        if l + 1 < depth:
            cur_next, rpre3, act = _ffn_out(x2, hgu, p["wd"], p["vec"], f"ffn_out_{l}")
        else:
            dcur, rpre3, act, loss_part = _ffn_out(x2, hgu, p["wd"], p["vec"], f"ffn_out_{l}", target=target)
            cur_next = None
        layers.append(p)
        saved.append(dict(x0=cur, z=z, x1=x1, rpre1=rpre1, h=h, ypre=ypre, yl=yl, yc=yc, merged=merged, e=e,
                          pp=pp, vb=vb, rb=rb, ib=ib, cq=cq,
                          kk=kk, vv=vv, x2=x2, rpre2=rpre2, q=q, o=o, hgu=hgu, rpre3=rpre3, act=act))
        cur = cur_next

    loss = lax.psum(loss_part[0, 0], ("x", "y", "c"))

    res = {}

    def slots_of(g, names):
        if tuple(names) == GROUP_VECTORS:
            return [_pack([_split8(g[n], SHARD_AXIS[n]) for n in names], d, lead=1)]
        return [(g[n] if n == "w_in" else _split8(g[n], SHARD_AXIS[n])).reshape(N_DEV, -1, d).astype(BF16)
                for n in names]

    def update(received, names, l, tag):
        outs = _adamw_sum(received, *[pack_shards(t, names, l) for t in (w, mom_m, mom_v)], f"adamw_{tag}_{l}")
        shapes = [shard(w, n, l).shape for n in names]
        for n, *parts in zip(names, *[_unpack(o, shapes) for o in outs]):
            res[(n, l)] = [_stored(n, a) for a in parts]

    def settle(exchanges, got):
        for (names, l, tag, _), received in zip(exchanges, got):
            update(received, names, l, tag)

    grads = [None] * depth
    for l in reversed(range(depth)):
        p, sv = layers[l], saved[l]
        g = {}
        dx2, dr3, dhgu, ln3 = _ffn_bwd(dcur, sv["rpre3"], sv["hgu"], p["wdt"], p["wgt"], p["wut"], p["vec"],
                                       f"ffn_bwd_{l}")
        g["ffn_w_down"] = _mm_tn(sv["act"], dr3, d, f"g_wd_{l}")[0][0]
        dwgu = _mm_tn(dhgu, sv["x2"], d, f"g_wgu_{l}", tk=ff)[0][0]
        g["ffn_w_gate"], g["ffn_w_up"] = dwgu[:ff], dwgu[ff:]
        dx1, dq, dr2, dk, dv, ln2 = _attn_bwd(dx2, sv["rpre2"], sv["q"], p["wqt"], p["wot"], sv["kk"], _t(sv["kk"]),
                                              _t(sv["vv"]), p["vec"], f"attn_bwd_{l}")
        g["xa_w_o"] = _mm_tn(sv["o"], dr2, d, f"g_wo_{l}")[0][0]
        g["xa_w_q"] = _mm_tn(sv["x1"], dq, d, f"g_wq_{l}")[0][0]
        dwkv = _mm_tn(mems, jnp.concatenate([dk, dv], axis=1), d, f"g_wkv_{l}")[0]
        g["xa_w_k"], g["xa_w_v"] = dwkv[0], dwkv[1]
        ffn_slots = slots_of(g, GROUP_FFN)
        (dz, dr1, dyl, dyc, accs, dbin, g["pool_w"], g["lru_w_r"], g["lru_w_i"], received) = _mixer_bwd(
            dx1, sv["rpre1"], sv["z"], sv["h"], sv["ypre"], sv["yl"], sv["yc"], sv["pp"], sv["vb"], sv["rb"],
            sv["ib"], sv["cq"], p["pwt"], p["wrt"], p["wit"], p["wlot"], p["wsct"], p["wmixt"], p["vec"],
            f"mixer_bwd_{l}",
            comm=[("scatter", ffn_slots)])
        update(received, GROUP_FFN, l, "ffn")
        g["w_mix_out"] = _mm_tn(sv["merged"], dr1, d, f"g_wmix_{l}")[0][0]
        g["lru_w_out"] = _mm_tn(sv["h"], dyl, d, f"g_wlo_{l}")[0][0]
        g["sconv_w_out"] = _mm_tn(sv["e"], dyc, d, f"g_wsc_{l}")[0][0]
        g["b_in"] = dbin[0]
        g["pool_scale"] = accs[A_PSCALE]
        g["lru_conv_w"] = accs[A_CW:A_CW + LRU_CONV]
        g["lru_conv_b"] = accs[A_CB]
        g["lru_b_r"] = accs[A_BR]
        g["lru_b_i"] = accs[A_BI]
        g["lru_lambda"] = accs[A_SP] * (-_sigmoid(-lru_lambda[l]))
        g["sconv_w"] = accs[A_SW:A_SW + SCONV_K]
        g["ln_g"] = jnp.stack([accs[A_G], ln2[0], ln3[0]])
        g["ln_b"] = jnp.stack([accs[A_B], ln2[1], ln3[1]])
        grads[l] = g
        behind_win = [(names, l, tag, slots_of(g, names)) for names, tag in
                      ((GROUP_ATTN, "attn"), (GROUP_MIXER, "mixer"), (GROUP_VECTORS, "vectors"))]
        g["w_in"], *got = _mm_tn(sv["x0"], dz, d, f"g_win_{l}", comm=[("scatter", t[3]) for t in behind_win])
        settle(behind_win, got)
        behind_dx = [(GROUP_IN, l, "w_in", slots_of(g, GROUP_IN))]
        comm = [("scatter", behind_dx[0][3])]
        if l == 0:
            comm.append(("gather", _pack([jnp.stack([grads[k][n] for k in range(depth)]) for n in REPLICATED], d)))
        dcur, *got = _mm(dz, p["wint"], f"dx_{l}", add=dr1, add_scale=ALPHA, comm=comm)
        settle(behind_dx, got)
        if l == 0:
            outs = _adamw_sum(got[1], *[_pack([t[n] for n in REPLICATED], d) for t in (w, mom_m, mom_v)],
                              "adamw_replicated")
            rep_shapes = [w[n].shape for n in REPLICATED]
            final = {n: parts for n, *parts in zip(REPLICATED, *[_unpack(o, rep_shapes) for o in outs])}
    grad_x = dcur.reshape(x.shape)

    for n in WEIGHTS:
        if n not in final:
            final[n] = [jnp.stack([res[(n, l)][k] for l in range(depth)]) for k in range(4)]
    return (loss, grad_x, *[final[n][0] for n in WEIGHTS], *[final[n][1] for n in WEIGHTS],
            *[final[n][2] for n in WEIGHTS], *[final[n][3] for n in WEIGHTS])
```

```python
import functools
import math

import jax
import jax.numpy as jnp
from jax import lax
from jax.experimental import pallas as pl
from jax.experimental.pallas import tpu as pltpu

F32 = jnp.float32
BF16 = jnp.bfloat16
MESH = pl.DeviceIdType.MESH

N_DEV = 8
LRU_HEADS = 8
LRU_CONV = 4
LRU_C = 8.0
SCONV_K = 3
POOL_WINDOWS = (2, 4, 8, 16)
X_HEADS = 4
DEPTH = 2
ALPHA = (2 * DEPTH) ** 0.25
LN_EPS = 1e-5
ADAM_LR = 0.001
ADAM_B1 = 0.9
ADAM_B2 = 0.999
ADAM_EPS = 1e-08
ADAM_WD = 0.01
ADAM_STEP = 10

HALO = 16
SUBLANES = 8
VMEM_LIMIT = 56 * 1024 * 1024
TS_MIXER = 128
TS_ATTN = 1024
TS_FFN = 256
TS_MM = 1024
TK_MM = 2048
TR_ADAM = 256
ROW_PAD = 8
TS_MM_TN = 1024

V_PSCALE, V_CW, V_CB, V_BR, V_BI, V_LAM, V_SW, V_G, V_B = 0, 1, 5, 6, 7, 8, 9, 12, 15
V_ROWS = 24
A_PSCALE, A_CW, A_CB, A_BR, A_BI, A_SP, A_SW, A_G, A_B = 0, 1, 5, 6, 7, 8, 9, 12, 13
A_ROWS = 16


def _cparams(sem):
    return pltpu.CompilerParams(dimension_semantics=sem, vmem_limit_bytes=VMEM_LIMIT)


def _tile(n, pref):
    if n <= pref:
        return n
    assert n % pref == 0, (n, pref)
    return pref


def _const_spec(shape):
    nd = len(shape)
    return pl.BlockSpec(shape, lambda *_: (0,) * nd, pipeline_mode=pl.Buffered(1))


def _acc_spec(shape):
    nd = len(shape)
    return pl.BlockSpec(shape, lambda *_: (0,) * nd)


def _dot(a, b):
    return jnp.dot(a.astype(BF16), b.astype(BF16), preferred_element_type=F32)


def _dot_tn(a, b):
    return lax.dot_general(a.astype(BF16), b.astype(BF16), (((0,), (0,)), ((), ())),
                           preferred_element_type=F32)


def _sigmoid(x):
    return 0.5 * jnp.tanh(0.5 * x) + 0.5


def _softplus(y):
    e = jnp.exp(-jnp.abs(y))
    log1p = jnp.where(e < 1e-4, e * (1.0 - e * (0.5 - e * (1.0 / 3.0))), jnp.log(1.0 + e))
    return jnp.maximum(y, 0.0) + log1p


def _ln_fwd(r, g, b):
    mu = jnp.mean(r, axis=-1, keepdims=True)
    xc = r - mu
    var = jnp.mean(xc * xc, axis=-1, keepdims=True)
    return xc * lax.rsqrt(var + LN_EPS) * g + b


def _ln_bwd(dy, r, g):
    mu = jnp.mean(r, axis=-1, keepdims=True)
    xc = r - mu
    var = jnp.mean(xc * xc, axis=-1, keepdims=True)
    rstd = lax.rsqrt(var + LN_EPS)
    yhat = xc * rstd
    dyh = dy * g
    m1 = jnp.mean(dyh, axis=-1, keepdims=True)
    m2 = jnp.mean(dyh * yhat, axis=-1, keepdims=True)
    return rstd * (dyh - m1 - yhat * m2), dy * yhat


def _colsum(a):
    return jnp.sum(a, axis=0, keepdims=True)


def _position():
    return lax.axis_index("x"), lax.axis_index("y"), lax.axis_index("c")


def _gather_copies(x_ref, out_ref, send_sems, recv_sems, local_sem):
    x, y, c = _position()
    me, sibling = (x, y, c), (x, y, 1 - c)
    chips = [(1 - x, y), (x, 1 - y), (1 - x, 1 - y)]

    def slot(px, py, pc):
        return out_ref.at[4 * px + 2 * py + pc]

    def copy(k, block, to, src=None):
        return pltpu.make_async_remote_copy(
            src_ref=slot(*block) if src is None else src, dst_ref=slot(*block),
            send_sem=send_sems.at[k], recv_sem=recv_sems.at[k], device_id=to, device_id_type=MESH)

    mine = pltpu.make_async_copy(x_ref, slot(*me), local_sem)
    first = [copy(0, me, sibling, src=x_ref)]
    first += [copy(1 + j, me, (*chip, c), src=x_ref) for j, chip in enumerate(chips)]
    passed = [copy(4 + j, (*chip, c), sibling) for j, chip in enumerate(chips)]
    over_ici = [copy(1 + j, (*chip, c), me) for j, chip in enumerate(chips)]
    from_sibling = copy(0, sibling, me)
    forwarded = [copy(4 + j, (*chip, 1 - c), me) for j, chip in enumerate(chips)]
    return mine, first, passed, over_ici, from_sibling, forwarded


def _scatter_copies(g_refs, out_ref, send_sems, recv_sems, local_sem):
    x, y, c = _position()
    me = 4 * x + 2 * y + c
    offsets, rows = [], 0
    for g_ref in g_refs:
        offsets.append(rows)
        rows += g_ref.shape[1]

    def landing(g_ref, off):
        return out_ref.at[me, pl.ds(off, g_ref.shape[1])]

    mine = [pltpu.make_async_copy(g_ref.at[me], landing(g_ref, off), local_sem)
            for g_ref, off in zip(g_refs, offsets)]
    mine_all = pltpu.make_async_copy(out_ref.at[me], out_ref.at[me], local_sem)
    copies, waits = [], []
    for k in range(1, N_DEV):
        px = 1 - x if k & 4 else x
        py = 1 - y if k & 2 else y
        pc = 1 - c if k & 1 else c
        sems = dict(send_sem=send_sems.at[k - 1], recv_sem=recv_sems.at[k - 1],
                    device_id=(px, py, pc), device_id_type=MESH)
        copies += [pltpu.make_async_remote_copy(src_ref=g_ref.at[4 * px + 2 * py + pc], dst_ref=landing(g_ref, off),
                                                **sems) for g_ref, off in zip(g_refs, offsets)]
        waits.append(pltpu.make_async_remote_copy(src_ref=out_ref.at[me], dst_ref=out_ref.at[me], **sems))
    return mine, copies, mine_all, waits


def _comm_start(kind, srcs, *refs):
    if kind == "gather":
        mine, first, _, _, _, _ = _gather_copies(srcs[0], *refs)
        mine.start()
        for cp in first:
            cp.start()
    else:
        mine, copies, _, _ = _scatter_copies(srcs, *refs)
        for cp in mine + copies:
            cp.start()


def _gather_pass_on(srcs, *refs):
    _, _, passed, over_ici, _, _ = _gather_copies(srcs[0], *refs)
    for arrival, forward in zip(over_ici, passed):
        arrival.wait_recv()
        forward.start()


def _comm_finish(kind, srcs, *refs, passed_on=False):
    if kind == "gather":
        if not passed_on:
            _gather_pass_on(srcs, *refs)
        mine, first, passed, _, from_sibling, forwarded = _gather_copies(srcs[0], *refs)
        from_sibling.wait_recv()
        for arrival in forwarded:
            arrival.wait_recv()
        for cp in first + passed:
            cp.wait_send()
        mine.wait()
    else:
        _, _, mine_all, waits = _scatter_copies(srcs, *refs)
        for cp in waits:
            cp.wait_recv()
        for cp in waits:
            cp.wait_send()
        mine_all.wait()


def _comm_sources(kind, payload):
    return [payload] if kind == "gather" else list(payload)


def _comm_out_shape(kind, payload):
    srcs = _comm_sources(kind, payload)
    if kind == "gather":
        return jax.ShapeDtypeStruct((N_DEV,) + srcs[0].shape, srcs[0].dtype)
    return jax.ShapeDtypeStruct((N_DEV, sum(a.shape[1] for a in srcs), srcs[0].shape[2]), srcs[0].dtype)


COMM_SEMAPHORES = [pltpu.SemaphoreType.DMA((7,)), pltpu.SemaphoreType.DMA((7,)), pltpu.SemaphoreType.DMA]


def _pallas(body, *, name, grid, in_specs, out_specs, out_shape, semantics, args, scratch_shapes=(), comm=()):
    in_specs, out_specs, out_shape = list(in_specs), list(out_specs), list(out_shape)
    scratch_shapes = list(scratch_shapes)
    n_in, n_out, n_scr, nc = len(in_specs), len(out_specs), len(scratch_shapes), len(comm)
    if not comm:
        return pl.pallas_call(body, name=name, grid=grid, in_specs=in_specs, out_specs=out_specs, out_shape=out_shape,
                              scratch_shapes=scratch_shapes, compiler_params=_cparams(semantics))(*args)
    kinds = [kind for kind, _ in comm]
    sources = [_comm_sources(kind, payload) for kind, payload in comm]
    n_src = sum(len(srcs) for srcs in sources)

    def carrying(*refs):
        ins, rest = refs[:n_in], refs[n_in:]
        cin, rest = list(rest[:n_src]), rest[n_src:]
        outs, rest = rest[:n_out], rest[n_out:]
        cout, rest = rest[:nc], rest[nc:]
        scr, sems = rest[:n_scr], rest[n_scr:]
        ids = [pl.program_id(ax) for ax in range(len(grid))]
        first = functools.reduce(jnp.logical_and, [i == 0 for i in ids])
        last = functools.reduce(jnp.logical_and, [i == g - 1 for i, g in zip(ids, grid)])
        plans = []
        for k in range(nc):
            mine, cin = cin[:len(sources[k])], cin[len(sources[k]):]
            plans.append((kinds[k], mine, cout[k], *sems[3 * k:3 * k + 3]))

        @pl.when(first)
        def _():
            for plan in plans:
                _comm_start(*plan)

        halfway = len(grid) == 1 and grid[0] >= 4 and "gather" in kinds
        if halfway:
            @pl.when(ids[0] == grid[0] // 2)
            def _():
                for plan in plans:
                    if plan[0] == "gather":
                        _gather_pass_on(*plan[1:])

        body(*ins, *outs, *scr)

        @pl.when(last)
        def _():
            for plan in plans:
                _comm_finish(*plan, passed_on=halfway)

    hbm = pl.BlockSpec(memory_space=pl.ANY)
    return pl.pallas_call(
        carrying, name=name, grid=grid,
        in_specs=in_specs + [hbm] * n_src, out_specs=out_specs + [hbm] * nc,
        out_shape=out_shape + [_comm_out_shape(kind, payload) for kind, payload in comm],
        scratch_shapes=scratch_shapes + COMM_SEMAPHORES * nc,
        compiler_params=_cparams(("arbitrary",) * len(grid)),
    )(*args, *[a for srcs in sources for a in srcs])


def _all_gather(arrays, name):
    n = len(arrays)

    def body(*refs):
        plans = [([refs[k]], refs[n + k], *refs[2 * n + 3 * k:2 * n + 3 * k + 3]) for k in range(n)]
        for plan in plans:
            _comm_start("gather", *plan)
        for plan in plans:
            _comm_finish("gather", *plan)

    hbm = pl.BlockSpec(memory_space=pl.ANY)
    return pl.pallas_call(
        body, name=name, out_shape=[_comm_out_shape("gather", a) for a in arrays],
        in_specs=[hbm] * n, out_specs=[hbm] * n, scratch_shapes=COMM_SEMAPHORES * n,
    )(*arrays)


def _adamw_sum(parts, w, m, v, name):
    _, rows, width = parts.shape
    tr = max(t for t in range(SUBLANES, min(rows, TR_ADAM) + 1, SUBLANES) if rows % t == 0)
    c1 = 1.0 - ADAM_B1 ** ADAM_STEP
    c2 = 1.0 - ADAM_B2 ** ADAM_STEP

    def body(p_ref, w_ref, m_ref, v_ref, g_ref, d_ref, nm_ref, nv_ref):
        g = p_ref[0].astype(F32)
        for k in range(1, N_DEV):
            g = g + p_ref[k].astype(F32)
        nm = ADAM_B1 * m_ref[...] + (1.0 - ADAM_B1) * g
        nv = ADAM_B2 * v_ref[...] + (1.0 - ADAM_B2) * (g * g)
        m_hat = nm / c1
        v_hat = nv / c2
        g_ref[...] = g
        d_ref[...] = -ADAM_LR * (m_hat / (jnp.sqrt(v_hat) + ADAM_EPS) + ADAM_WD * w_ref[...])
        nm_ref[...] = nm
        nv_ref[...] = nv

    spec = pl.BlockSpec((tr, width), lambda i: (i, 0))
    out = jax.ShapeDtypeStruct((rows, width), F32)
    return pl.pallas_call(
        body, name=name, grid=(rows // tr,),
        in_specs=[pl.BlockSpec((N_DEV, tr, width), lambda i: (0, i, 0)), spec, spec, spec],
        out_specs=[spec, spec, spec, spec], out_shape=[out, out, out, out],
        compiler_params=_cparams(("parallel",)),
    )(parts, w, m, v)


def _mm(a, wb, name, bias=None, add=None, add_scale=1.0, out_dtype=F32, tm=None, comm=()):
    m, k = a.shape
    nb, k2, tn = wb.shape
    assert k == k2
    tm = _tile(m, TS_MM if tm is None else tm)
    tk = _tile(k, TK_MM)
    nk = k // tk

    def body(*refs):
        a_ref, w_ref = refs[0], refs[1]
        pos = 2
        b_ref = add_ref = None
        if bias is not None:
            b_ref = refs[pos]
            pos += 1
        if add is not None:
            add_ref = refs[pos]
            pos += 1
        o_ref = refs[pos]

        def finish(r):
            if b_ref is not None:
                r = r + b_ref[...]
            if add_ref is not None:
                r = r + add_scale * add_ref[...]
            o_ref[...] = r.astype(o_ref.dtype)

        if nk == 1:
            finish(_dot(a_ref[...], w_ref[...]))
            return
        acc_ref = refs[pos + 1]
        kk = pl.program_id(2)

        @pl.when(kk == 0)
        def _():
            acc_ref[...] = jnp.zeros_like(acc_ref)

        acc_ref[...] += _dot(a_ref[...], w_ref[...])

        @pl.when(kk == nk - 1)
        def _():
            finish(acc_ref[...])

    in_specs = [pl.BlockSpec((tm, tk), lambda j, i, kk: (i, kk)),
                pl.BlockSpec((None, tk, tn), lambda j, i, kk: (j, kk, 0))]
    args = [a, wb]
    if bias is not None:
        in_specs.append(pl.BlockSpec((1, tn), lambda j, i, kk: (0, j)))
        args.append(bias)
    if add is not None:
        in_specs.append(pl.BlockSpec((tm, tn), lambda j, i, kk: (i, j)))
        args.append(add)
    return _pallas(
        body, name=name, grid=(nb, m // tm, nk),
        in_specs=in_specs,
        out_specs=[pl.BlockSpec((tm, tn), lambda j, i, kk: (i, j))],
        out_shape=[jax.ShapeDtypeStruct((m, nb * tn), out_dtype)],
        scratch_shapes=[pltpu.VMEM((tm, tn), F32)] if nk > 1 else [],
        semantics=("parallel", "parallel", "arbitrary"), args=args, comm=comm)


def _mm_tn(a, b, tn, name, tk=None, comm=()):
    s, k = a.shape
    s2, n = b.shape
    assert s == s2 and n % tn == 0
    nb = n // tn
    ts = _tile(s, TS_MM_TN)
    tk = k if tk is None else tk
    assert k % tk == 0

    def body(a_ref, b_ref, o_ref):
        @pl.when(pl.program_id(2) == 0)
        def _():
            o_ref[...] = jnp.zeros_like(o_ref)

        o_ref[...] += _dot_tn(a_ref[...], b_ref[...])

    return _pallas(
        body, name=name, grid=(nb, k // tk, s // ts),
        in_specs=[pl.BlockSpec((ts, tk), lambda j, kb, i: (i, kb)),
                  pl.BlockSpec((ts, tn), lambda j, kb, i: (i, j))],
        out_specs=[pl.BlockSpec((None, tk, tn), lambda j, kb, i: (j, kb, 0))],
        out_shape=[jax.ShapeDtypeStruct((nb, k, tn), F32)],
        semantics=("parallel", "parallel", "arbitrary"), args=(a, b), comm=comm)


def _scan_fwd(a_ref, b_ref, h_ref, carry_ref, ts):
    rowid = lax.broadcasted_iota(jnp.int32, (SUBLANES, 1), 0)

    def group(gi, hprev):
        r0 = pl.multiple_of(gi * SUBLANES, SUBLANES)
        a = a_ref[pl.ds(r0, SUBLANES), :]
        b = b_ref[pl.ds(r0, SUBLANES), :]
        for d in (1, 2, 4):
            a_sh = jnp.where(rowid >= d, pltpu.roll(a, d, 0), 1.0)
            b_sh = jnp.where(rowid >= d, pltpu.roll(b, d, 0), 0.0)
            b = a * b_sh + b
            a = a * a_sh
        hh = a * hprev + b
        h_ref[pl.ds(r0, SUBLANES), :] = hh
        return hh[SUBLANES - 1:SUBLANES, :]

    last = lax.fori_loop(0, ts // SUBLANES, group, carry_ref[0:1, :])
    carry_ref[0:1, :] = last


def _scan_rev(c_ref, b_ref, g_ref, carry_ref, ts):
    rowid = lax.broadcasted_iota(jnp.int32, (SUBLANES, 1), 0)
    ng = ts // SUBLANES

    def group(gi, gnext):
        r0 = pl.multiple_of((ng - 1 - gi) * SUBLANES, SUBLANES)
        c = c_ref[pl.ds(r0, SUBLANES), :]
        b = b_ref[pl.ds(r0, SUBLANES), :]
        for d in (1, 2, 4):
            keep = rowid < SUBLANES - d
            c_sh = jnp.where(keep, pltpu.roll(c, SUBLANES - d, 0), 1.0)
            b_sh = jnp.where(keep, pltpu.roll(b, SUBLANES - d, 0), 0.0)
            b = c * b_sh + b
            c = c * c_sh
        gg = c * gnext + b
        g_ref[pl.ds(r0, SUBLANES), :] = gg
        return gg[0:1, :]

    first = lax.fori_loop(0, ng, group, carry_ref[0:1, :])
    carry_ref[0:1, :] = first


def _past(ext, sh, ts):
    if sh == 0:
        return ext[HALO:HALO + ts]
    return pltpu.roll(ext, sh, 0)[HALO:HALO + ts]


def _future(ext, sh, ts):
    if sh == 0:
        return ext[0:ts]
    return pltpu.roll(ext, ts + HALO - sh, 0)[0:ts]


def _one_minus_sq(a, log_a):
    x = 2.0 * log_a
    series = -x * (1.0 + x * (0.5 + x * (1.0 / 6.0 + x * (1.0 / 24.0))))
    return jnp.where(x > -0.02, series, 1.0 - a * a)


def _halo_index(ts):
    blocks = ts // HALO
    return lambda t: (jnp.maximum(t * blocks - 1, 0), 0)


def _head_columns(d):
    cw = d // LRU_HEADS
    return [slice(c * cw, (c + 1) * cw) for c in range(LRU_HEADS)]


def _shift(cs, off):
    return slice(cs.start + off, cs.stop + off)


def _mixer_fwd(x, win, b_in, pw, wr, wi, wlo, wsc, wmix, vec, name, comm=()):
    s, d = x.shape
    ts = _tile(s, TS_MIXER)
    nt = s // ts
    dg = d // len(POOL_WINDOWS)
    nblk = win.shape[0]
    cols = _head_columns(d)

    def body(x_ref, xn_ref, win_ref, bin_ref, pw_ref, wr_ref, wi_ref, wlo_ref, wsc_ref, wmix_ref, vec_ref,
             z_hbm, x1_ref, rpre_ref, h_ref, ypre_ref, yl_ref, yc_ref, mg_ref, e_ref, p_ref, v_ref, r_ref, ig_ref,
             cq_ref, z_even, z_odd, zhist, a_scr, b_scr, hcarry, z_sem):
        i = pl.program_id(0)
        first = i == 0

        def project_block(xb, dst, k):
            dst[:, k * d:(k + 1) * d] = _dot(xb, win_ref[k]) + bin_ref[:, k * d:(k + 1) * d]

        @pl.when(first)
        def _():
            hcarry[...] = jnp.zeros_like(hcarry)
            zhist[...] = jnp.zeros_like(zhist)
            xb = x_ref[...].astype(BF16)
            for k in range(nblk):
                project_block(xb, z_even, k)

        def vrow(k, cs):
            return vec_ref[k:k + 1, cs]

        def step(zc, zn):
            z_out = pltpu.make_async_copy(zc, z_hbm.at[pl.ds(pl.multiple_of(i * ts, ts), ts), :], z_sem)
            z_out.start()
            xb = xn_ref[...].astype(BF16)
            tglob = i * ts + lax.broadcasted_iota(jnp.int32, (ts, 1), 0)
            sp = _softplus(-vec_ref[V_LAM:V_LAM + 1, :])

            def with_history(k, cs):
                kc = _shift(cs, k * d)
                return jnp.concatenate([jnp.where(first, 0.0, zhist[:, kc]), zc[:, kc]], axis=0)

            for hh, cs in enumerate(cols):
                if hh < nblk:
                    project_block(xb, zn, hh)
                win_len = POOL_WINDOWS[cs.start // dg]
                ext = with_history(0, cs)
                sm = ext
                sh = 1
                while sh < win_len:
                    sm = sm + pltpu.roll(sm, sh, 0)
                    sh *= 2
                inv_cnt = 1.0 / jnp.minimum(tglob + 1, win_len).astype(F32)
                p_ref[:, cs] = (sm[HALO:HALO + ts] * inv_cnt - ext[HALO:HALO + ts]).astype(BF16)
                ext = with_history(1, cs)
                v = vrow(V_CB, cs)
                for j in range(LRU_CONV):
                    v = v + vrow(V_CW + j, cs) * _past(ext, LRU_CONV - 1 - j, ts)
                r = _sigmoid(_dot(v, wr_ref[hh]) + vrow(V_BR, cs))
                ig = _sigmoid(_dot(v, wi_ref[hh]) + vrow(V_BI, cs))
                log_a = -LRU_C * r * sp[:, cs]
                a = jnp.exp(log_a)
                a_scr[:, cs] = a
                b_scr[:, cs] = jnp.sqrt(_one_minus_sq(a, log_a)) * (ig * v)
                v_ref[:, cs] = v
                r_ref[:, cs] = r
                ig_ref[:, cs] = ig
                ext = with_history(3, cs) * with_history(4, cs)
                cq = jnp.zeros((ts, cs.stop - cs.start), F32)
                for j in range(SCONV_K):
                    cq = cq + vrow(V_SW + j, cs) * _past(ext, SCONV_K - 1 - j, ts)
                cq_ref[:, cs] = cq
                e_ref[:, cs] = (zc[:, _shift(cs, 2 * d)] * cq).astype(BF16)
            for k in range(len(cols), nblk):
                project_block(xb, zn, k)
            _scan_fwd(a_scr, b_scr, h_ref, hcarry, ts)
            ypre = jnp.concatenate([_dot(p_ref[:, g * dg:(g + 1) * dg], pw_ref[g])
                                    for g in range(len(POOL_WINDOWS))], axis=1)
            yl = _dot(h_ref[...], wlo_ref[...])
            yc = _dot(e_ref[...], wsc_ref[...])
            ypre_ref[...] = ypre
            yl_ref[...] = yl
            yc_ref[...] = yc
            for cs in cols:
                merged = (_sigmoid(zc[:, _shift(cs, 5 * d)]) * (ypre[:, cs] * vrow(V_PSCALE, cs))
                          + _sigmoid(zc[:, _shift(cs, 6 * d)]) * yl[:, cs]
                          + _sigmoid(zc[:, _shift(cs, 7 * d)]) * yc[:, cs])
                mg_ref[:, cs] = merged.astype(BF16)
            rpre = ALPHA * x_ref[...] + _dot(mg_ref[...], wmix_ref[...])
            x1_ref[...] = _ln_fwd(rpre, vec_ref[V_G:V_G + 1, :], vec_ref[V_B:V_B + 1, :])
            rpre_ref[...] = rpre
            zhist[...] = zc[ts - HALO:ts, :]
            z_out.wait()

        parity = lax.rem(i, 2)

        @pl.when(parity == 0)
        def _():
            step(z_even, z_odd)

        @pl.when(parity == 1)
        def _():
            step(z_odd, z_even)

    tile = pl.BlockSpec((ts, d), lambda t: (t, 0))
    f32o = jax.ShapeDtypeStruct((s, d), F32)
    bfo = jax.ShapeDtypeStruct((s, d), BF16)
    consts = (win, b_in, pw, wr, wi, wlo, wsc, wmix, vec)
    return _pallas(
        body, name=name, grid=(nt,),
        in_specs=[tile, pl.BlockSpec((ts, d), lambda t: (jnp.minimum(t + 1, nt - 1), 0))]
        + [_const_spec(c.shape) for c in consts],
        out_specs=[pl.BlockSpec(memory_space=pl.ANY)] + [tile] * 13,
        out_shape=[jax.ShapeDtypeStruct((s, nblk * d), F32), f32o, f32o, f32o, f32o, f32o, f32o, bfo, bfo, bfo,
                   f32o, f32o, f32o, f32o],
        scratch_shapes=[pltpu.VMEM((ts, nblk * d), F32)] * 2 + [pltpu.VMEM((HALO, nblk * d), F32)]
        + [pltpu.VMEM((ts, d), F32)] * 2 + [pltpu.VMEM((SUBLANES, d), F32), pltpu.SemaphoreType.DMA],
        semantics=("arbitrary",), args=(x, x, *consts), comm=comm)


def _mixer_bwd(dx1, rpre, z, h, ypre, yl, yc, pp, vv, rr, ii, cq, pwt, wrt, wit, wlot, wsct, wmixt, vec, name,
               comm=()):
    s, d = dx1.shape
    ts = _tile(s, TS_MIXER)
    nt = s // ts
    dg = d // len(POOL_WINDOWS)
    cols = _head_columns(d)

    def body(dx1_ref, rpre_ref, z_ref, h_ref, hh_ref, ypre_ref, yl_ref, yc_ref, p_ref, v_ref, r_ref, ig_ref, cq_ref,
             pwt_ref, wrt_ref, wit_ref, wlot_ref, wsct_ref, wmixt_ref, vec_ref,
             dz_ref, dr_ref, dyl_ref, dyc_ref, acc_ref, dbin_ref, dpw_ref, dwr_ref, dwi_ref,
             c_scr, b_scr, g_scr, dyps_scr, a_keep, m_keep, gcarry, acarry, dcq_c, dv_c, m_c):
        i = pl.program_id(0)
        t = nt - 1 - i

        @pl.when(i == 0)
        def _():
            for ref in (gcarry, acarry, dcq_c, dv_c, m_c, acc_ref, dbin_ref, dpw_ref, dwr_ref, dwi_ref):
                ref[...] = jnp.zeros_like(ref)

        def vrow(k, cs):
            return vec_ref[k:k + 1, cs]

        def zc(k, cs):
            return z_ref[:, _shift(cs, k * d)]

        def acc(row, cs, val):
            acc_ref[row:row + 1, cs] += _colsum(val)

        def emit_dz(k, cs, val):
            kc = _shift(cs, k * d)
            dz_ref[:, kc] = val.astype(BF16)

        def with_future(tile_val, carry_ref, cs):
            ext = jnp.concatenate([tile_val, carry_ref[:, cs]], axis=0)
            carry_ref[:, cs] = tile_val[0:HALO, :]
            return ext

        dx1v = dx1_ref[...]
        dr, dyy = _ln_bwd(dx1v, rpre_ref[...], vec_ref[V_G:V_G + 1, :])
        acc_ref[A_G:A_G + 1, :] += _colsum(dyy)
        acc_ref[A_B:A_B + 1, :] += _colsum(dx1v)
        dr_ref[...] = dr
        dmg = _dot(dr, wmixt_ref[...])
        for cs in cols:
            dm = dmg[:, cs]
            ypre = ypre_ref[:, cs]
            ys = (ypre * vrow(V_PSCALE, cs), yl_ref[:, cs], yc_ref[:, cs])
            dys = []
            for k in range(3):
                gk = _sigmoid(zc(5 + k, cs))
                emit_dz(5 + k, cs, dm * ys[k] * gk * (1.0 - gk))
                dys.append(dm * gk)
            acc(A_PSCALE, cs, dys[0] * ypre)
            dyps_scr[:, cs] = dys[0] * vrow(V_PSCALE, cs)
            dyl_ref[:, cs] = dys[1].astype(BF16)
            dyc_ref[:, cs] = dys[2].astype(BF16)
        de = _dot(dyc_ref[...], wsct_ref[...])
        dh = _dot(dyl_ref[...], wlot_ref[...])

        sp = _softplus(-vec_ref[V_LAM:V_LAM + 1, :])
        for cs in cols:
            dec = de[:, cs]
            emit_dz(2, cs, dec * cq_ref[:, cs])
            dcq_ext = with_future(dec * zc(2, cs), dcq_c, cs)
            zcc, zh = zc(3, cs), zc(4, cs)
            qv = zcc * zh
            dq = jnp.zeros_like(qv)
            for j in range(SCONV_K):
                adv = _future(dcq_ext, SCONV_K - 1 - j, ts)
                acc(A_SW + j, cs, adv * qv)
                dq = dq + vrow(V_SW + j, cs) * adv
            emit_dz(3, cs, dq * zh)
            emit_dz(4, cs, dq * zcc)
            log_a = -LRU_C * r_ref[:, cs] * sp[:, cs]
            a = jnp.exp(log_a)
            a_keep[:, cs] = a
            m_keep[:, cs] = jnp.sqrt(_one_minus_sq(a, log_a))
            c_scr[:, cs] = _future(with_future(a, acarry, cs), 1, ts)
            b_scr[:, cs] = dh[:, cs]
        _scan_rev(c_scr, b_scr, g_scr, gcarry, ts)

        for hh, cs in enumerate(cols):
            gs, a, mult = g_scr[:, cs], a_keep[:, cs], m_keep[:, cs]
            r, ig, v = r_ref[:, cs], ig_ref[:, cs], v_ref[:, cs]
            hprev = _past(jnp.concatenate([jnp.where(t == 0, 0.0, hh_ref[:, cs]), h_ref[:, cs]], axis=0), 1, ts)
            iv = ig * v
            dlog_a = gs * hprev * a + gs * iv * (-(a * a) / mult)
            div = gs * mult
            acc(A_SP, cs, dlog_a * (-LRU_C) * r)
            dpre_r = dlog_a * (-LRU_C) * sp[:, cs] * r * (1.0 - r)
            dpre_i = div * v * ig * (1.0 - ig)
            acc(A_BR, cs, dpre_r)
            acc(A_BI, cs, dpre_i)
            dv = div * ig + _dot(dpre_r, wrt_ref[hh]) + _dot(dpre_i, wit_ref[hh])
            dwr_ref[hh] += _dot_tn(v, dpre_r)
            dwi_ref[hh] += _dot_tn(v, dpre_i)
            acc(A_CB, cs, dv)
            dv_ext = with_future(dv, dv_c, cs)
            zl = zc(1, cs)
            dzl = jnp.zeros_like(zl)
            for j in range(LRU_CONV):
                adv = _future(dv_ext, LRU_CONV - 1 - j, ts)
                acc(A_CW + j, cs, adv * zl)
                dzl = dzl + vrow(V_CW + j, cs) * adv
            emit_dz(1, cs, dzl)

        tglob = t * ts + lax.broadcasted_iota(jnp.int32, (ts, 1), 0)
        for g, win_len in enumerate(POOL_WINDOWS):
            cs = slice(g * dg, (g + 1) * dg)
            dyps = dyps_scr[:, cs]
            dpw_ref[g] += _dot_tn(p_ref[:, cs], dyps)
            dp = _dot(dyps, pwt_ref[g])
            inv_cnt = 1.0 / jnp.minimum(tglob + 1, win_len).astype(F32)
            sm = with_future(dp * inv_cnt, m_c, cs)
            sh = 1
            while sh < win_len:
                sm = sm + pltpu.roll(sm, ts + HALO - sh, 0)
                sh *= 2
            emit_dz(0, cs, sm[0:ts] - dp)

        dbin_ref[...] += _dot(jnp.ones((SUBLANES, ts), BF16), dz_ref[...])[0:1, :]

    def rev(tt):
        return (nt - 1 - tt, 0)

    halo = _halo_index(ts)
    tile = pl.BlockSpec((ts, d), rev)
    hspec = pl.BlockSpec((HALO, d), lambda tt: halo(nt - 1 - tt))
    f32o = jax.ShapeDtypeStruct((s, d), F32)
    bfo = jax.ShapeDtypeStruct((s, d), BF16)
    consts = (pwt, wrt, wit, wlot, wsct, wmixt, vec)
    return _pallas(
        body, name=name, grid=(nt,),
        in_specs=[tile, tile, pl.BlockSpec((ts, 8 * d), rev), tile, hspec] + [tile] * 8
        + [_const_spec(c.shape) for c in consts],
        out_specs=[pl.BlockSpec((ts, 8 * d), rev), tile, tile, tile,
                   _acc_spec((A_ROWS, d)), _acc_spec((1, 8 * d)),
                   _acc_spec(pwt.shape), _acc_spec(wrt.shape), _acc_spec(wit.shape)],
        out_shape=[jax.ShapeDtypeStruct((s, 8 * d), BF16), f32o, bfo, bfo,
                   jax.ShapeDtypeStruct((A_ROWS, d), F32), jax.ShapeDtypeStruct((1, 8 * d), F32),
                   jax.ShapeDtypeStruct(pwt.shape, F32), jax.ShapeDtypeStruct(wrt.shape, F32),
                   jax.ShapeDtypeStruct(wit.shape, F32)],
        scratch_shapes=[pltpu.VMEM((ts, d), F32)] * 6 + [pltpu.VMEM((SUBLANES, d), F32)]
        + [pltpu.VMEM((HALO, d), F32)] * 4,
        semantics=("arbitrary",), args=(dx1, rpre, z, h, h, ypre, yl, yc, pp, vv, rr, ii, cq, *consts), comm=comm)


def _softmax_rows(sc):
    mx = jnp.max(sc, axis=-1, keepdims=True)
    ex = jnp.exp(sc - mx)
    return ex * (1.0 / jnp.sum(ex, axis=-1, keepdims=True))


def _attn_fwd(x1, wq, wo, kt, vv, vec, name):
    s, d = x1.shape
    ts = _tile(s, TS_ATTN)
    hd = d // X_HEADS
    scale = hd ** -0.5

    def body(x_ref, wq_ref, wo_ref, kt_ref, v_ref, vec_ref, x2_ref, rpre_ref, q_ref, o_ref):
        xv = x_ref[...]
        q = _dot(xv, wq_ref[...]).astype(BF16)
        q_ref[...] = q
        for hh in range(X_HEADS):
            cs = slice(hh * hd, (hh + 1) * hd)
            p = _softmax_rows(_dot(q[:, cs], kt_ref[cs, :]) * scale)
            o_ref[:, cs] = _dot(p, v_ref[:, cs]).astype(BF16)
        rpre = ALPHA * xv + _dot(o_ref[...], wo_ref[...])
        rpre_ref[...] = rpre
        x2_ref[...] = _ln_fwd(rpre, vec_ref[V_G + 1:V_G + 2, :], vec_ref[V_B + 1:V_B + 2, :])

    tile = pl.BlockSpec((ts, d), lambda t: (t, 0))
    f32o = jax.ShapeDtypeStruct((s, d), F32)
    bfo = jax.ShapeDtypeStruct((s, d), BF16)
    consts = (wq, wo, kt, vv, vec)
    return pl.pallas_call(
        body, name=name, grid=(s // ts,),
        in_specs=[tile] + [_const_spec(c.shape) for c in consts],
        out_specs=[tile] * 4, out_shape=[f32o, f32o, bfo, bfo],
        compiler_params=_cparams(("parallel",)),
    )(x1, *consts)


def _attn_bwd(dx2, rpre, q, wqt, wot, kk, kt, vt, vec, name):
    s, d = dx2.shape
    ts = _tile(s, TS_ATTN)
    nm = kk.shape[0]
    hd = d // X_HEADS
    scale = hd ** -0.5

    def body(dx2_ref, rpre_ref, q_ref, wqt_ref, wot_ref, k_ref, kt_ref, vt_ref, vec_ref,
             dx1_ref, dq_ref, dr_ref, dk_ref, dv_ref, ln_ref):
        @pl.when(pl.program_id(0) == 0)
        def _():
            for ref in (dk_ref, dv_ref, ln_ref):
                ref[...] = jnp.zeros_like(ref)

        dyv = dx2_ref[...]
        dr, dyy = _ln_bwd(dyv, rpre_ref[...], vec_ref[V_G + 1:V_G + 2, :])
        ln_ref[0:1, :] += _colsum(dyy)
        ln_ref[1:2, :] += _colsum(dyv)
        dr_ref[...] = dr.astype(BF16)
        do = _dot(dr, wot_ref[...])
        q = q_ref[...]
        for hh in range(X_HEADS):
            cs = slice(hh * hd, (hh + 1) * hd)
            p = _softmax_rows(_dot(q[:, cs], kt_ref[cs, :]) * scale)
            dp = _dot(do[:, cs], vt_ref[cs, :])
            ds = p * (dp - jnp.sum(dp * p, axis=-1, keepdims=True)) * scale
            dq_ref[:, cs] = _dot(ds, k_ref[:, cs]).astype(BF16)
            dk_ref[:, cs] += _dot_tn(ds, q[:, cs])
            dv_ref[:, cs] += _dot_tn(p, do[:, cs])
        dx1_ref[...] = ALPHA * dr + _dot(dq_ref[...], wqt_ref[...])

    tile = pl.BlockSpec((ts, d), lambda t: (t, 0))
    consts = (wqt, wot, kk, kt, vt, vec)
    return pl.pallas_call(
        body, name=name, grid=(s // ts,),
        in_specs=[tile, tile, tile] + [_const_spec(c.shape) for c in consts],
        out_specs=[tile, tile, tile, _acc_spec((nm, d)), _acc_spec((nm, d)), _acc_spec((2, d))],
        out_shape=[jax.ShapeDtypeStruct((s, d), F32), jax.ShapeDtypeStruct((s, d), BF16),
                   jax.ShapeDtypeStruct((s, d), BF16), jax.ShapeDtypeStruct((nm, d), F32),
                   jax.ShapeDtypeStruct((nm, d), F32), jax.ShapeDtypeStruct((2, d), F32)],
        compiler_params=_cparams(("arbitrary",)),
    )(dx2, rpre, q, *consts)


def _ffn_out(x2, hgu, wd, vec, name, target=None):
    s, d = x2.shape
    ff = wd.shape[0]
    ts = _tile(s, TS_FFN)
    chunk = 2 * 128 if ff % (2 * 128) == 0 else ff

    def body(*refs):
        if target is None:
            x_ref, hgu_ref, wd_ref, vec_ref, out_ref, rpre_ref, act_ref = refs
        else:
            x_ref, hgu_ref, wd_ref, vec_ref, t_ref, out_ref, rpre_ref, act_ref, loss_ref = refs
        for c in range(0, ff, chunk):
            hg = hgu_ref[:, c:c + chunk]
            act_ref[:, c:c + chunk] = (hg * _sigmoid(hg) * hgu_ref[:, ff + c:ff + c + chunk]).astype(BF16)
        rpre = ALPHA * x_ref[...] + _dot(act_ref[...], wd_ref[...])
        rpre_ref[...] = rpre
        x3 = _ln_fwd(rpre, vec_ref[V_G + 2:V_G + 3, :], vec_ref[V_B + 2:V_B + 3, :])
        if target is None:
            out_ref[...] = x3
            return

        @pl.when(pl.program_id(0) == 0)
        def _():
            loss_ref[...] = jnp.zeros_like(loss_ref)

        err = x3 - t_ref[...]
        out_ref[...] = err / d
        per_token = jnp.mean(err * err, axis=-1, keepdims=True)
        loss_ref[...] += 0.5 * jnp.sum(per_token, axis=0, keepdims=True)

    tile = pl.BlockSpec((ts, d), lambda t: (t, 0))
    f32o = jax.ShapeDtypeStruct((s, d), F32)
    in_specs = [tile, pl.BlockSpec((ts, 2 * ff), lambda t: (t, 0)), _const_spec(wd.shape), _const_spec(vec.shape)]
    out_specs = [tile, tile, pl.BlockSpec((ts, ff), lambda t: (t, 0))]
    out_shape = [f32o, f32o, jax.ShapeDtypeStruct((s, ff), BF16)]
    args = [x2, hgu, wd, vec]
    if target is not None:
        in_specs.append(tile)
        args.append(target)
        out_specs.append(_acc_spec((1, 1)))
        out_shape.append(jax.ShapeDtypeStruct((1, 1), F32))
    return pl.pallas_call(
        body, name=name, grid=(s // ts,), in_specs=in_specs, out_specs=out_specs, out_shape=out_shape,
        compiler_params=_cparams(("parallel",) if target is None else ("arbitrary",)),
    )(*args)


def _ffn_bwd(dy, rpre, hgu, wdt, wgt, wut, vec, name, comm=()):
    s, d = dy.shape
    ff = wgt.shape[0]
    ts = _tile(s, TS_FFN)
    chunk = 2 * 128 if ff % (2 * 128) == 0 else ff

    def body(dy_ref, rpre_ref, hgu_ref, wdt_ref, wgt_ref, wut_ref, vec_ref, dx_ref, dr_ref, dhgu_ref, ln_ref):
        @pl.when(pl.program_id(0) == 0)
        def _():
            ln_ref[...] = jnp.zeros_like(ln_ref)

        dyv = dy_ref[...]
        dr, dyy = _ln_bwd(dyv, rpre_ref[...], vec_ref[V_G + 2:V_G + 3, :])
        ln_ref[0:1, :] += _colsum(dyy)
        ln_ref[1:2, :] += _colsum(dyv)
        dr_ref[...] = dr.astype(BF16)
        dact = _dot(dr, wdt_ref[...])
        for c in range(0, ff, chunk):
            hg = hgu_ref[:, c:c + chunk]
            hu = hgu_ref[:, ff + c:ff + c + chunk]
            da = dact[:, c:c + chunk]
            sg = _sigmoid(hg)
            dhgu_ref[:, c:c + chunk] = (da * hu * (sg * (1.0 + hg * (1.0 - sg)))).astype(BF16)
            dhgu_ref[:, ff + c:ff + c + chunk] = (da * hg * sg).astype(BF16)
        dx_ref[...] = (ALPHA * dr + _dot(dhgu_ref[:, 0:ff], wgt_ref[...])
                       + _dot(dhgu_ref[:, ff:2 * ff], wut_ref[...]))

    tile = pl.BlockSpec((ts, d), lambda t: (t, 0))
    wide = pl.BlockSpec((ts, 2 * ff), lambda t: (t, 0))
    consts = (wdt, wgt, wut, vec)
    return _pallas(
        body, name=name, grid=(s // ts,),
        in_specs=[tile, tile, wide] + [_const_spec(c.shape) for c in consts],
        out_specs=[tile, tile, wide, _acc_spec((2, d))],
        out_shape=[jax.ShapeDtypeStruct((s, d), F32), jax.ShapeDtypeStruct((s, d), BF16),
                   jax.ShapeDtypeStruct((s, 2 * ff), BF16), jax.ShapeDtypeStruct((2, d), F32)],
        semantics=("arbitrary",), args=(dy, rpre, hgu, *consts), comm=comm)


SHARD_AXIS = {"w_in": 1, "pool_w": 1, "lru_w_out": 0, "sconv_w_out": 0, "w_mix_out": 0,
              "xa_w_q": 0, "xa_w_k": 0, "xa_w_v": 0, "xa_w_o": 0,
              "ffn_w_gate": 0, "ffn_w_up": 0, "ffn_w_down": 0,
              "lru_conv_w": 1, "sconv_w": 1, "ln_g": 1, "ln_b": 1}
STORED_TRANSPOSED = ("ffn_w_gate", "ffn_w_up")
GROUP_IN = ("w_in",)
GROUP_MIXER = ("pool_w", "lru_w_out", "sconv_w_out", "w_mix_out")
GROUP_ATTN = ("xa_w_q", "xa_w_k", "xa_w_v", "xa_w_o")
GROUP_FFN = ("ffn_w_gate", "ffn_w_up", "ffn_w_down")
GROUP_VECTORS = ("lru_conv_w", "sconv_w", "ln_g", "ln_b")
REPLICATED = ("b_in", "pool_scale", "lru_conv_b", "lru_w_r", "lru_b_r", "lru_w_i", "lru_b_i", "lru_lambda")
WEIGHTS = ("w_in", "b_in", "pool_w", "pool_scale", "lru_conv_w", "lru_conv_b", "lru_w_r", "lru_b_r", "lru_w_i",
           "lru_b_i", "lru_lambda", "lru_w_out", "sconv_w", "sconv_w_out", "w_mix_out", "xa_w_q", "xa_w_k",
           "xa_w_v", "xa_w_o", "ffn_w_gate", "ffn_w_up", "ffn_w_down", "ln_g", "ln_b")


def _pack(arrs, width, lead=0, row_multiple=ROW_PAD):
    head = arrs[0].shape[:lead]
    flat = jnp.concatenate([a.reshape(head + (-1,)) for a in arrs], axis=lead)
    n = flat.shape[-1]
    chunk = width * row_multiple
    total = -(-n // chunk) * chunk
    if total != n:
        flat = jnp.pad(flat, [(0, 0)] * lead + [(0, total - n)])
    return flat.reshape(head + (total // width, width))


def _unpack(buf, shapes, lead=0):
    head = buf.shape[:lead]
    flat = buf.reshape(head + (-1,))
    out, off = [], 0
    for shp in shapes:
        n = math.prod(shp)
        out.append(flat[..., off:off + n].reshape(head + tuple(shp)))
        off += n
    return out


def _split8(a, axis):
    shp = a.shape
    a = a.reshape(shp[:axis] + (N_DEV, shp[axis] // N_DEV) + shp[axis + 1:])
    return jnp.moveaxis(a, axis, 0)


def _join8(a, axis):
    a = jnp.moveaxis(a, 0, axis)
    shp = a.shape
    return a.reshape(shp[:axis] + (shp[axis] * shp[axis + 1],) + shp[axis + 2:])


def _t(a):
    return jnp.swapaxes(a, -1, -2)


def _stored(name, a):
    return _t(a) if name in STORED_TRANSPOSED else a


def kernel(x, mem, w_in, b_in, pool_w, pool_scale, lru_conv_w, lru_conv_b, lru_w_r, lru_b_r, lru_w_i, lru_b_i, lru_lambda, lru_w_out, sconv_w, sconv_w_out, w_mix_out, xa_w_q, xa_w_k, xa_w_v, xa_w_o, ffn_w_gate, ffn_w_up, ffn_w_down, ln_g, ln_b, loss_target, m_w_in, m_b_in, m_pool_w, m_pool_scale, m_lru_conv_w, m_lru_conv_b, m_lru_w_r, m_lru_b_r, m_lru_w_i, m_lru_b_i, m_lru_lambda, m_lru_w_out, m_sconv_w, m_sconv_w_out, m_w_mix_out, m_xa_w_q, m_xa_w_k, m_xa_w_v, m_xa_w_o, m_ffn_w_gate, m_ffn_w_up, m_ffn_w_down, m_ln_g, m_ln_b, v_w_in, v_b_in, v_pool_w, v_pool_scale, v_lru_conv_w, v_lru_conv_b, v_lru_w_r, v_lru_b_r, v_lru_w_i, v_lru_b_i, v_lru_lambda, v_lru_w_out, v_sconv_w, v_sconv_w_out, v_w_mix_out, v_xa_w_q, v_xa_w_k, v_xa_w_v, v_xa_w_o, v_ffn_w_gate, v_ffn_w_up, v_ffn_w_down, v_ln_g, v_ln_b):
    args = dict(locals())
    w = {n: args[n] for n in WEIGHTS}
    mom_m = {n: args["m_" + n] for n in WEIGHTS}
    mom_v = {n: args["v_" + n] for n in WEIGHTS}
    depth = w_in.shape[0]
    s, d = x.shape[1], x.shape[2]
    nm = mem.shape[1]
    ff = ffn_w_gate.shape[2] * N_DEV
    xs = x.reshape(s, d)
    mems = mem.reshape(nm, d)
    target = loss_target.reshape(s, d)

    def shard(t, n, l):
        return _stored(n, t[n][l])

    def pack_shards(t, names, l, dtype=None):
        arrs = [shard(t, n, l) for n in names]
        return _pack([a if dtype is None else a.astype(dtype) for a in arrs], d)

    def unpack_gathered(buf, names):
        pieces = _unpack(buf, [shard(w, n, 0).shape for n in names], lead=1)
        return {n: (p if n == "w_in" else _join8(p, SHARD_AXIS[n])) for n, p in zip(names, pieces)}

    def layer_vec(l, fw):
        vec = jnp.zeros((V_ROWS, d), F32)
        vec = vec.at[V_PSCALE].set(pool_scale[l]).at[V_CW:V_CW + LRU_CONV].set(fw["lru_conv_w"])
        vec = vec.at[V_CB].set(lru_conv_b[l]).at[V_BR].set(lru_b_r[l]).at[V_BI].set(lru_b_i[l])
        vec = vec.at[V_LAM].set(lru_lambda[l]).at[V_SW:V_SW + SCONV_K].set(fw["sconv_w"])
        return vec.at[V_G:V_G + 3].set(fw["ln_g"]).at[V_B:V_B + 3].set(fw["ln_b"])

    def layer_params(l, fw):
        return dict(
            vec=layer_vec(l, fw), wint=_t(fw["w_in"]).reshape(1, 8 * d, d),
            pw=fw["pool_w"], pwt=_t(fw["pool_w"]),
            wr=lru_w_r[l].astype(BF16), wi=lru_w_i[l].astype(BF16),
            wrt=_t(lru_w_r[l]).astype(BF16), wit=_t(lru_w_i[l]).astype(BF16),
            wlo=fw["lru_w_out"], wlot=_t(fw["lru_w_out"]),
            wsc=fw["sconv_w_out"], wsct=_t(fw["sconv_w_out"]),
            wmix=fw["w_mix_out"], wmixt=_t(fw["w_mix_out"]),
            wq=fw["xa_w_q"], wqt=_t(fw["xa_w_q"]), wo=fw["xa_w_o"], wot=_t(fw["xa_w_o"]),
            wkv=jnp.stack([fw["xa_w_k"], fw["xa_w_v"]]),
            wgu=jnp.stack([_t(fw["ffn_w_gate"]), _t(fw["ffn_w_up"])]),
            wgt=fw["ffn_w_gate"], wut=fw["ffn_w_up"],
            wd=fw["ffn_w_down"], wdt=_t(fw["ffn_w_down"]))

    later = GROUP_ATTN + GROUP_FFN
    first_weights, vectors = _all_gather(
        [pack_shards(w, GROUP_IN + GROUP_MIXER, 0, BF16),
         _pack([shard(w, n, l) for l in range(depth) for n in GROUP_VECTORS], d)], "gather_first")
    fw0 = unpack_gathered(first_weights, GROUP_IN + GROUP_MIXER)
    vec_pieces = _unpack(vectors, [shard(w, n, l).shape for l in range(depth) for n in GROUP_VECTORS], lead=1)
    fvec = [{n: _join8(vec_pieces[l * len(GROUP_VECTORS) + k], SHARD_AXIS[n]) for k, n in enumerate(GROUP_VECTORS)}
            for l in range(depth)]

    layers, saved = [], []
    cur = xs
    fw_next = None
    for l in range(depth):
        fw = dict(fw0 if l == 0 else fw_next)
        fw.update(fvec[l])
        comm = []
        if l == 0:
            comm.append(("gather", pack_shards(w, later, 0, BF16)))
        if l + 1 < depth:
            comm.append(("gather", pack_shards(w, GROUP_IN + GROUP_MIXER + later, l + 1, BF16)))
        z, x1, rpre1, h, ypre, yl, yc, merged, e, pp, vb, rb, ib, cq, *got = _mixer_fwd(
            cur, fw["w_in"], b_in[l].reshape(1, 8 * d), fw["pool_w"], lru_w_r[l].astype(BF16),
            lru_w_i[l].astype(BF16), fw["lru_w_out"], fw["sconv_w_out"], fw["w_mix_out"],
            layer_vec(l, fw), f"mixer_fwd_{l}", comm=comm)
        if l == 0:
            fw.update(unpack_gathered(got.pop(0), later))
        if l + 1 < depth:
            fw_next = unpack_gathered(got.pop(0), GROUP_IN + GROUP_MIXER + later)
        p = layer_params(l, fw)
        kv = _mm(mems, p["wkv"], f"kv_{l}")[0]
        kk = kv[:, :d].astype(BF16)
        vv = kv[:, d:].astype(BF16)
        x2, rpre2, q, o = _attn_fwd(x1, p["wq"], p["wo"], _t(kk), vv, p["vec"], f"attn_fwd_{l}")
        hgu = _mm(x2, p["wgu"], f"ffn_in_{l}", tm=TS_MM // 2)[0]
        if l + 1 < depth:
            cur_next, rpre3, act = _ffn_out(x2, hgu, p["wd"], p["vec"], f"ffn_out_{l}")
        else:
            dcur, rpre3, act, loss_part = _ffn_out(x2, hgu, p["wd"], p["vec"], f"ffn_out_{l}", target=target)
            cur_next = None
        layers.append(p)
        saved.append(dict(x0=cur, z=z, x1=x1, rpre1=rpre1, h=h, ypre=ypre, yl=yl, yc=yc, merged=merged, e=e,
                          pp=pp, vb=vb, rb=rb, ib=ib, cq=cq,
                          kk=kk, vv=vv, x2=x2, rpre2=rpre2, q=q, o=o, hgu=hgu, rpre3=rpre3, act=act))
        cur = cur_next

    loss = lax.psum(loss_part[0, 0], ("x", "y", "c"))

    res = {}

    def slots_of(g, names):
        if tuple(names) == GROUP_VECTORS:
            return [_pack([_split8(g[n], SHARD_AXIS[n]) for n in names], d, lead=1)]
        return [(g[n] if n == "w_in" else _split8(g[n], SHARD_AXIS[n])).reshape(N_DEV, -1, d).astype(BF16)
                for n in names]

    def update(received, names, l, tag):
        outs = _adamw_sum(received, *[pack_shards(t, names, l) for t in (w, mom_m, mom_v)], f"adamw_{tag}_{l}")
        shapes = [shard(w, n, l).shape for n in names]
        for n, *parts in zip(names, *[_unpack(o, shapes) for o in outs]):
            res[(n, l)] = [_stored(n, a) for a in parts]

    def settle(exchanges, got):
        for (names, l, tag, _), received in zip(exchanges, got):
            update(received, names, l, tag)

    grads = [None] * depth
    for l in reversed(range(depth)):
        p, sv = layers[l], saved[l]
        g = {}
        dx2, dr3, dhgu, ln3 = _ffn_bwd(dcur, sv["rpre3"], sv["hgu"], p["wdt"], p["wgt"], p["wut"], p["vec"],
                                       f"ffn_bwd_{l}")
        g["ffn_w_down"] = _mm_tn(sv["act"], dr3, d, f"g_wd_{l}")[0][0]
        dwgu = _mm_tn(dhgu, sv["x2"], d, f"g_wgu_{l}", tk=ff)[0][0]
        g["ffn_w_gate"], g["ffn_w_up"] = dwgu[:ff], dwgu[ff:]
        dx1, dq, dr2, dk, dv, ln2 = _attn_bwd(dx2, sv["rpre2"], sv["q"], p["wqt"], p["wot"], sv["kk"], _t(sv["kk"]),
                                              _t(sv["vv"]), p["vec"], f"attn_bwd_{l}")
        g["xa_w_o"] = _mm_tn(sv["o"], dr2, d, f"g_wo_{l}")[0][0]
        g["xa_w_q"] = _mm_tn(sv["x1"], dq, d, f"g_wq_{l}")[0][0]
        dwkv = _mm_tn(mems, jnp.concatenate([dk, dv], axis=1), d, f"g_wkv_{l}")[0]
        g["xa_w_k"], g["xa_w_v"] = dwkv[0], dwkv[1]
        ffn_slots = slots_of(g, GROUP_FFN)
        (dz, dr1, dyl, dyc, accs, dbin, g["pool_w"], g["lru_w_r"], g["lru_w_i"], received) = _mixer_bwd(
            dx1, sv["rpre1"], sv["z"], sv["h"], sv["ypre"], sv["yl"], sv["yc"], sv["pp"], sv["vb"], sv["rb"],
            sv["ib"], sv["cq"], p["pwt"], p["wrt"], p["wit"], p["wlot"], p["wsct"], p["wmixt"], p["vec"],
            f"mixer_bwd_{l}",
            comm=[("scatter", ffn_slots)])
        update(received, GROUP_FFN, l, "ffn")
        g["w_mix_out"] = _mm_tn(sv["merged"], dr1, d, f"g_wmix_{l}")[0][0]
        g["lru_w_out"] = _mm_tn(sv["h"], dyl, d, f"g_wlo_{l}")[0][0]
        g["sconv_w_out"] = _mm_tn(sv["e"], dyc, d, f"g_wsc_{l}")[0][0]
        g["b_in"] = dbin[0]
        g["pool_scale"] = accs[A_PSCALE]
        g["lru_conv_w"] = accs[A_CW:A_CW + LRU_CONV]
        g["lru_conv_b"] = accs[A_CB]
        g["lru_b_r"] = accs[A_BR]
        g["lru_b_i"] = accs[A_BI]
        g["lru_lambda"] = accs[A_SP] * (-_sigmoid(-lru_lambda[l]))
        g["sconv_w"] = accs[A_SW:A_SW + SCONV_K]
        g["ln_g"] = jnp.stack([accs[A_G], ln2[0], ln3[0]])
        g["ln_b"] = jnp.stack([accs[A_B], ln2[1], ln3[1]])
        grads[l] = g
        behind_win = [(names, l, tag, slots_of(g, names)) for names, tag in
                      ((GROUP_ATTN, "attn"), (GROUP_MIXER, "mixer"), (GROUP_VECTORS, "vectors"))]
        g["w_in"], *got = _mm_tn(sv["x0"], dz, d, f"g_win_{l}", comm=[("scatter", t[3]) for t in behind_win])
        settle(behind_win, got)
        behind_dx = [(GROUP_IN, l, "w_in", slots_of(g, GROUP_IN))]
        comm = [("scatter", behind_dx[0][3])]
        if l == 0:
            comm.append(("gather", _pack([jnp.stack([grads[k][n] for k in range(depth)]) for n in REPLICATED], d)))
        dcur, *got = _mm(dz, p["wint"], f"dx_{l}", add=dr1, add_scale=ALPHA, comm=comm)
        settle(behind_dx, got)
        if l == 0:
            outs = _adamw_sum(got[1], *[_pack([t[n] for n in REPLICATED], d) for t in (w, mom_m, mom_v)],
                              "adamw_replicated")
            rep_shapes = [w[n].shape for n in REPLICATED]
            final = {n: parts for n, *parts in zip(REPLICATED, *[_unpack(o, rep_shapes) for o in outs])}
    grad_x = dcur.reshape(x.shape)

    for n in WEIGHTS:
        if n not in final:
            final[n] = [jnp.stack([res[(n, l)][k] for l in range(depth)]) for k in range(4)]
    return (loss, grad_x, *[final[n][0] for n in WEIGHTS], *[final[n][1] for n in WEIGHTS],
            *[final[n][2] for n in WEIGHTS], *[final[n][3] for n in WEIGHTS])
```
